```python
import math
import jax, jax.numpy as jnp
from jax import lax
import numpy as np

D_MODEL = 1024
BATCH = 2
SEQ = 8192
DEPTH = 2

N_META = 16
CHUNK = 64
PAD = CHUNK - N_META
CONV_K = 4
RET_HEADS = 4
RET_DK = D_MODEL // 8
RET_DV = D_MODEL // 4
RET_QK = RET_HEADS * RET_DK
RET_VW = RET_HEADS * RET_DV
ROPE_BASE = 10000.0
GDN_HEADS = 4
GDN_DK = D_MODEL // 8
GDN_DV = D_MODEL // 4
GDN_QK = GDN_HEADS * GDN_DK
GDN_VW = GDN_HEADS * GDN_DV
AB_SIZES = (RET_QK, RET_QK, RET_VW, RET_VW, GDN_QK, GDN_QK, GDN_VW, GDN_HEADS, GDN_HEADS, GDN_VW)
AB_IN = 2 * RET_QK + 2 * RET_VW + 2 * GDN_QK + 2 * GDN_VW + 2 * GDN_HEADS
AB_OUT = RET_VW + GDN_VW
SSD_DINNER = 2 * D_MODEL
SSD_HEADDIM = 64
SSD_HEADS = SSD_DINNER // SSD_HEADDIM
SSD_GROUPS = 4
SSD_HPG = SSD_HEADS // SSD_GROUPS
SSD_DSTATE = 128
SSD_GN = SSD_GROUPS * SSD_DSTATE
SSD_CONV_CH = SSD_DINNER + 2 * SSD_GN
SSD_IN = SSD_DINNER + SSD_CONV_CH + SSD_HEADS
D_FF = 4 * D_MODEL
DN_ALPHA = (2 * DEPTH) ** 0.25
DN_BETA = (8 * DEPTH) ** -0.25
N_AB = (DEPTH + 1) // 2
N_C = DEPTH // 2
LN_EPS = 1e-5
GN_EPS = 1e-5
RMS_EPS = 1e-6

kernel_name = 'hybrid_retnet_gdn_mamba2_deepnorm'


def split_cols(a, sizes):
    offs, acc = [], 0
    for s in sizes[:-1]:
        acc += s
        offs.append(acc)
    return jnp.split(a, offs, axis=-1)


def _standardize(x, eps):
    xf = x.astype(jnp.float32)
    xc = xf - jnp.mean(xf, -1, keepdims=True)
    return xc * lax.rsqrt(jnp.mean(xc * xc, -1, keepdims=True) + eps)


def _rms(x, eps):
    xf = x.astype(jnp.float32)
    return xf * lax.rsqrt(jnp.mean(xf * xf, -1, keepdims=True) + eps)


def layer_norm(x, w, b):
    return (_standardize(x, LN_EPS) * w + b).astype(x.dtype)


def l2norm(x):
    xf = x.astype(jnp.float32)
    return xf * lax.rsqrt(jnp.sum(xf * xf, -1, keepdims=True) + 1e-6)


def rope(x, cos, sin):
    x1, x2 = jnp.split(x.astype(jnp.float32), 2, axis=-1)
    return jnp.concatenate([x1 * cos - x2 * sin, x1 * sin + x2 * cos], axis=-1)


def causal_dwconv(x, w):
    L = x.shape[1]
    xp = jnp.pad(x, ((0, 0), (CONV_K - 1, 0), (0, 0)))
    return sum(xp[:, k:k + L] * w[k] for k in range(CONV_K))


def to_chunks(a):
    a = jnp.pad(a, [(0, 0), (PAD, 0)] + [(0, 0)] * (a.ndim - 2))
    nc = a.shape[1] // CHUNK
    a = a.reshape((a.shape[0], nc, CHUNK) + a.shape[2:])
    return jnp.moveaxis(a, 1, 0)


def from_chunks(o):
    o = jnp.moveaxis(o, 0, 1)
    o = o.reshape((o.shape[0], -1) + o.shape[3:])
    return o[:, PAD:]


def chunked_decay_attention(q, k, v, g):
    B, _, G, dk = q.shape
    Hg, dv = v.shape[-2:]
    xs = tuple(to_chunks(a.astype(jnp.float32)) for a in (q, k, v, g))
    incl = jnp.tril(jnp.ones((CHUNK, CHUNK), bool))[None, :, :, None, None]

    def body(S, inp):
        qc, kc, vc, gc = inp
        gcum = jnp.cumsum(gc, axis=1)
        seg = gcum[:, :, None] - gcum[:, None]
        att = jnp.einsum('btgd,bsgd->btsg', qc, kc)[..., None] * jnp.exp(jnp.where(incl, seg, -jnp.inf))
        o = jnp.einsum('btsgh,bsghv->btghv', att, vc)
        o = o + jnp.einsum('btgd,bghdv->btghv', qc, S) * jnp.exp(gcum)[..., None]
        g_last = gcum[:, -1]
        S = S * jnp.exp(g_last)[..., None, None] + jnp.einsum(
            'bsgd,bsghv->bghdv', kc, vc * jnp.exp(g_last[:, None] - gcum)[..., None])
        return S, o

    S0 = jnp.zeros((B, G, Hg, dk, dv), jnp.float32)
    _, o = lax.scan(body, S0, xs)
    return from_chunks(o)


def chunked_gated_delta_rule(q, k, v, beta, g):
    B, _, H, dk = q.shape
    dv = v.shape[-1]
    xs = tuple(to_chunks(a.astype(jnp.float32)) for a in (q, k, v, beta, g))
    eye = jnp.eye(CHUNK, dtype=jnp.float32)
    strict = jnp.tril(jnp.ones((CHUNK, CHUNK), bool), -1)
    incl = jnp.tril(jnp.ones((CHUNK, CHUNK), bool))

    def body(S, inp):
        qc, kc, vc, bc, gc = (jnp.moveaxis(a, 1, 2) for a in inp)
        gcum = jnp.cumsum(gc, axis=-1)
        seg = gcum[..., :, None] - gcum[..., None, :]
        kb = kc * bc[..., None]
        A = jnp.einsum('bhtd,bhsd->bhts', kb, kc) * jnp.exp(jnp.where(strict, seg, -jnp.inf))
        rhs = jnp.concatenate([vc * bc[..., None], kb * jnp.exp(gcum)[..., None]], axis=-1)
        u = lax.linalg.triangular_solve(eye + A, rhs, left_side=True, lower=True)
        w_v, w_k = u[..., :dv], u[..., dv:]
        v_new = w_v - jnp.einsum('bhcd,bhdv->bhcv', w_k, S)
        att = jnp.einsum('bhtd,bhsd->bhts', qc, kc) * jnp.exp(jnp.where(incl, seg, -jnp.inf))
        o = jnp.einsum('bhtd,bhdv->bhtv', qc * jnp.exp(gcum)[..., None], S) + jnp.einsum('bhts,bhsv->bhtv', att, v_new)
        g_last = gcum[..., -1:]
        S = S * jnp.exp(g_last)[..., None] + jnp.einsum(
            'bhsd,bhsv->bhdv', kc * jnp.exp(g_last - gcum)[..., None], v_new)
        return S, jnp.moveaxis(o, 1, 2)

    S0 = jnp.zeros((B, H, dk, dv), jnp.float32)
    _, o = lax.scan(body, S0, xs)
    return from_chunks(o)


def retention_gdn_mixer(h, w_in, ret_gn_w, conv_q, conv_k, conv_v, A_log, dt_bias, gdn_norm_w, w_out, cos, sin):
    B, L, _ = h.shape
    rq, rk, rv, rg, gq, gk, gv, gb, ga, gg = split_cols(h @ w_in, AB_SIZES)
    rq = rope(rq.reshape(B, L, RET_HEADS, RET_DK), cos, sin) * RET_DK ** -0.5
    rk = rope(rk.reshape(B, L, RET_HEADS, RET_DK), cos, sin)
    rv = rv.reshape(B, L, RET_HEADS, 1, RET_DV)
    log_gamma = jnp.log1p(-jnp.exp2(-5.0 - jnp.arange(RET_HEADS, dtype=jnp.float32)))
    rgam = jnp.broadcast_to(log_gamma[:, None], (B, L, RET_HEADS, 1))
    ro = chunked_decay_attention(rq, rk, rv, rgam).reshape(B, L, RET_HEADS, RET_DV)
    ro = _standardize(ro, GN_EPS).reshape(B, L, RET_VW) * ret_gn_w
    ret_out = jax.nn.silu(rg.astype(jnp.float32)) * ro
    gq = l2norm(jax.nn.silu(causal_dwconv(gq, conv_q)).reshape(B, L, GDN_HEADS, GDN_DK)) * GDN_DK ** -0.5
    gk = l2norm(jax.nn.silu(causal_dwconv(gk, conv_k)).reshape(B, L, GDN_HEADS, GDN_DK))
    gv = jax.nn.silu(causal_dwconv(gv, conv_v)).reshape(B, L, GDN_HEADS, GDN_DV)
    beta = jax.nn.sigmoid(gb.astype(jnp.float32))
    gdec = -jnp.exp(A_log.astype(jnp.float32)) * jax.nn.softplus(ga.astype(jnp.float32) + dt_bias)
    go = chunked_gated_delta_rule(gq, gk, gv, beta, gdec)
    go = (_rms(go, RMS_EPS) * gdn_norm_w).reshape(B, L, GDN_VW) * jax.nn.silu(gg.astype(jnp.float32))
    y = jnp.concatenate([ret_out, go], axis=-1).astype(h.dtype)
    return y @ w_out


def mamba2_mixer(h, w_in, conv_w, conv_b, A_log, dt_bias, D, norm_w, w_out):
    B, L, _ = h.shape
    z, xbc, dt = split_cols(h @ w_in, (SSD_DINNER, SSD_CONV_CH, SSD_HEADS))
    xbc = jax.nn.silu(causal_dwconv(xbc, conv_w) + conv_b)
    xs, Bm, Cm = split_cols(xbc, (SSD_DINNER, SSD_GN, SSD_GN))
    xs = xs.astype(jnp.float32).reshape(B, L, SSD_GROUPS, SSD_HPG, SSD_HEADDIM)
    Bm = Bm.reshape(B, L, SSD_GROUPS, SSD_DSTATE)
    Cm = Cm.reshape(B, L, SSD_GROUPS, SSD_DSTATE)
    dt = jax.nn.softplus(dt.astype(jnp.float32) + dt_bias).reshape(B, L, SSD_GROUPS, SSD_HPG)
    A = -jnp.exp(A_log.astype(jnp.float32)).reshape(SSD_GROUPS, SSD_HPG)
    y = chunked_decay_attention(Cm, Bm, xs * dt[..., None], dt * A)
    y = y + xs * D.reshape(SSD_GROUPS, SSD_HPG, 1)
    y = y.reshape(B, L, SSD_DINNER) * jax.nn.silu(z.astype(jnp.float32))
    y = _rms(y.reshape(B, L, SSD_GROUPS, SSD_DINNER // SSD_GROUPS), RMS_EPS).reshape(B, L, SSD_DINNER) * norm_w
    return y.astype(h.dtype) @ w_out


def sq_relu_mlp(h, w1, w2):
    return jnp.square(jax.nn.relu(h @ w1)) @ w2


def setup_inputs(seed: int = 0) -> dict:
    key = jax.random.key(seed)
    ks = iter(jax.random.split(key, 32))

    def nrm(shape, scale):
        return jax.random.normal(next(ks), shape, jnp.float32) * scale

    def gain(shape):
        return 1.0 + nrm(shape, 0.02)

    def dt_bias(shape):
        dt = jnp.exp(jax.random.uniform(next(ks), shape, jnp.float32, math.log(1e-3), math.log(1e-1)))
        return dt + jnp.log(-jnp.expm1(-dt))

    def a_log(shape):
        return jnp.log(jax.random.uniform(next(ks), shape, jnp.float32, 1.0, 16.0))

    return {
        'x': nrm((BATCH, SEQ, D_MODEL), 1.0),
        'meta_tokens': nrm((N_META, D_MODEL), 1.0),
        'ab_w_in': nrm((N_AB, D_MODEL, AB_IN), D_MODEL ** -0.5),
        'ab_ret_gn_w': gain((N_AB, RET_VW)),
        'ab_conv_q': nrm((N_AB, CONV_K, GDN_QK), CONV_K ** -0.5),
        'ab_conv_k': nrm((N_AB, CONV_K, GDN_QK), CONV_K ** -0.5),
        'ab_conv_v': nrm((N_AB, CONV_K, GDN_VW), CONV_K ** -0.5),
        'ab_A_log': a_log((N_AB, GDN_HEADS)),
        'ab_dt_bias': dt_bias((N_AB, GDN_HEADS)),
        'ab_gdn_norm_w': gain((N_AB, GDN_DV)),
        'ab_w_out': nrm((N_AB, AB_OUT, D_MODEL), AB_OUT ** -0.5 * DN_BETA),
        'c_w_in': nrm((N_C, D_MODEL, SSD_IN), D_MODEL ** -0.5),
        'c_conv_w': nrm((N_C, CONV_K, SSD_CONV_CH), CONV_K ** -0.5),
        'c_conv_b': nrm((N_C, SSD_CONV_CH), 0.02),
        'c_A_log': a_log((N_C, SSD_HEADS)),
        'c_dt_bias': dt_bias((N_C, SSD_HEADS)),
        'c_D': gain((N_C, SSD_HEADS)),
        'c_norm_w': gain((N_C, SSD_DINNER)),
        'c_w_out': nrm((N_C, SSD_DINNER, D_MODEL), SSD_DINNER ** -0.5 * DN_BETA),
        'mlp_w1': nrm((DEPTH, D_MODEL, D_FF), D_MODEL ** -0.5),
        'mlp_w2': nrm((DEPTH, D_FF, D_MODEL), D_FF ** -0.5 * DN_BETA),
        'ln1_w': gain((DEPTH, D_MODEL)),
        'ln1_b': nrm((DEPTH, D_MODEL), 0.02),
        'ln2_w': gain((DEPTH, D_MODEL)),
        'ln2_b': nrm((DEPTH, D_MODEL), 0.02),
    }


def reference(x, meta_tokens, ab_w_in, ab_ret_gn_w, ab_conv_q, ab_conv_k, ab_conv_v, ab_A_log, ab_dt_bias,
              ab_gdn_norm_w, ab_w_out, c_w_in, c_conv_w, c_conv_b, c_A_log, c_dt_bias, c_D, c_norm_w, c_w_out,
              mlp_w1, mlp_w2, ln1_w, ln1_b, ln2_w, ln2_b):
    B = x.shape[0]
    meta = jnp.broadcast_to(meta_tokens[None].astype(x.dtype), (B, N_META, D_MODEL))
    h = jnp.concatenate([meta, x], axis=1)
    L = h.shape[1]
    pos = jnp.arange(L, dtype=jnp.float32)
    inv_freq = 1.0 / (ROPE_BASE ** jnp.linspace(0.0, 1.0, RET_DK // 2, dtype=jnp.float32))
    ang = pos[:, None] * inv_freq[None]
    cos = jnp.cos(ang)[None, :, None, :]
    sin = jnp.sin(ang)[None, :, None, :]
    for i in range(DEPTH):
        j = i // 2
        if i % 2 == 0:
            mix = retention_gdn_mixer(h, ab_w_in[j], ab_ret_gn_w[j], ab_conv_q[j], ab_conv_k[j], ab_conv_v[j],
                                      ab_A_log[j], ab_dt_bias[j], ab_gdn_norm_w[j], ab_w_out[j], cos, sin)
        else:
            mix = mamba2_mixer(h, c_w_in[j], c_conv_w[j], c_conv_b[j], c_A_log[j], c_dt_bias[j], c_D[j],
                               c_norm_w[j], c_w_out[j])
        h = layer_norm(DN_ALPHA * h + mix, ln1_w[i], ln1_b[i])
        h = layer_norm(DN_ALPHA * h + sq_relu_mlp(h, mlp_w1[i], mlp_w2[i]), ln2_w[i], ln2_b[i])
    return h[:, N_META:]
```

```python
import functools

import jax
import jax.numpy as jnp
from jax import lax
from jax.experimental import pallas as pl
from jax.experimental.pallas import tpu as pltpu

F32 = jnp.float32
BF16 = jnp.bfloat16
HI = lax.Precision.HIGHEST

D_MODEL = 1024
DEPTH = 2
N_META = 16
CONV_K = 4
RET_HEADS = 4
RET_DK = 128
RET_DV = 256
ROPE_BASE = 10000.0
GDN_HEADS = 4
GDN_DK = 128
GDN_DV = 256
SSD_DINNER = 2048
SSD_HEADDIM = 64
SSD_HEADS = 32
SSD_GROUPS = 4
SSD_HPG = 8
SSD_DSTATE = 128
SSD_GN = 512
D_FF = 4096
DN_ALPHA = (2 * DEPTH) ** 0.25
LN_EPS = 1e-5
GN_EPS = 1e-5
RMS_EPS = 1e-6

LANES = 128
COL_TILE = 512
CHUNK = 128
GDN_CHUNK = 64
PAD_FRONT = CHUNK - N_META
HALO = 8
VMEM_LIMIT = 56 * 1024 * 1024


def _row_tile(rows_per_batch):
    for tm in (640, 512, 256, 128):
        if rows_per_batch % tm == 0:
            return tm
    raise ValueError(f"unsupported padded sequence length {rows_per_batch}")


def _params(sem):
    return pltpu.CompilerParams(dimension_semantics=sem, vmem_limit_bytes=VMEM_LIMIT)


def _softplus(x):
    return jnp.maximum(x, 0.0) + jnp.log1p(jnp.exp(-jnp.abs(x)))


def _silu(x):
    return x * jax.nn.sigmoid(x)


def _dot(a, b):
    return jnp.dot(a, b, preferred_element_type=F32)


def _dot_nt(a, b):
    return lax.dot_general(a, b, (((1,), (1,)), ((), ())), preferred_element_type=F32)


def _dot_hi(a, b):
    return jnp.dot(a, b, preferred_element_type=F32, precision=HI)


def _causal_conv(xe_ref, cw_ref, tm):
    y = cw_ref[CONV_K - 1:CONV_K, :] * xe_ref[HALO:HALO + tm, :]
    for d in range(1, CONV_K):
        y = y + cw_ref[CONV_K - 1 - d:CONV_K - d, :] * xe_ref[HALO - d:HALO - d + tm, :]
    return y


def _proj_ab_kernel(h_ref, w_ref, cos_ref, sin_ref, cw_ref, o_ref, xe_ref, *, tm, tiles_per_batch):
    j = pl.program_id(0)
    i = pl.program_id(1)
    xe_ref[HALO:HALO + tm, :] = _dot(h_ref[...], w_ref[...])

    @pl.when(j < 2)
    def _rope():
        scale = jnp.where(j == 0, RET_DK ** -0.5, 1.0).astype(F32)
        for hd in range(COL_TILE // RET_DK):
            x = xe_ref[HALO:HALO + tm, RET_DK * hd:RET_DK * (hd + 1)]
            y = x * cos_ref[...] + pltpu.roll(x, RET_DK // 2, 1) * sin_ref[...]
            o_ref[:, RET_DK * hd:RET_DK * (hd + 1)] = (y * scale).astype(BF16)

    @pl.when(jnp.logical_or(jnp.logical_and(j >= 2, j < 6), j >= 10))
    def _plain():
        o_ref[...] = xe_ref[HALO:HALO + tm, :].astype(BF16)

    @pl.when(jnp.logical_and(j >= 6, j < 10))
    def _conv():
        @pl.when(i % tiles_per_batch == 0)
        def _():
            xe_ref[0:HALO, :] = jnp.zeros((HALO, COL_TILE), F32)

        y = _silu(_causal_conv(xe_ref, cw_ref, tm))
        xe_ref[0:HALO, :] = xe_ref[tm:tm + HALO, :]

        @pl.when(j < 8)
        def _l2():
            scale = jnp.where(j == 6, GDN_DK ** -0.5, 1.0).astype(F32)
            for hd in range(COL_TILE // GDN_DK):
                yh = y[:, GDN_DK * hd:GDN_DK * (hd + 1)]
                inv = lax.rsqrt(jnp.sum(yh * yh, axis=-1, keepdims=True) + 1e-6)
                o_ref[:, GDN_DK * hd:GDN_DK * (hd + 1)] = (yh * (inv * scale)).astype(BF16)

        @pl.when(j >= 8)
        def _v():
            o_ref[...] = y.astype(BF16)


def _proj_ab(hb, w, cosf, sinf, cw, rows_per_batch):
    rows = hb.shape[0]
    tm = _row_tile(rows_per_batch)
    tpb = rows_per_batch // tm
    ncol = w.shape[1] // COL_TILE
    return pl.pallas_call(
        functools.partial(_proj_ab_kernel, tm=tm, tiles_per_batch=tpb),
        grid=(ncol, rows // tm),
        in_specs=[
            pl.BlockSpec((tm, D_MODEL), lambda j, i: (i, 0)),
            pl.BlockSpec((D_MODEL, COL_TILE), lambda j, i: (0, j)),
            pl.BlockSpec((tm, LANES), lambda j, i: (i % tpb, 0)),
            pl.BlockSpec((tm, LANES), lambda j, i: (i % tpb, 0)),
            pl.BlockSpec((CONV_K, COL_TILE), lambda j, i: (0, jnp.clip(j - 6, 0, 3))),
        ],
        out_specs=pl.BlockSpec((tm, COL_TILE), lambda j, i: (i, j)),
        out_shape=jax.ShapeDtypeStruct((rows, w.shape[1]), BF16),
        scratch_shapes=[pltpu.VMEM((tm + HALO, COL_TILE), F32)],
        compiler_params=_params(("arbitrary", "arbitrary")),
        name="proj_ab",
    )(hb, w, cosf, sinf, cw)


def _gates_ab_kernel(h_ref, w_ref, alog_ref, bias_ref, o_ref):
    acc = _dot(h_ref[...], w_ref[...])
    lane = lax.broadcasted_iota(jnp.int32, acc.shape, 1)
    decay = -jnp.exp(alog_ref[...]) * _softplus(acc + bias_ref[...])
    o_ref[...] = jnp.where(lane < GDN_HEADS, jax.nn.sigmoid(acc), decay)


def _gates_c_kernel(h_ref, w_ref, alog_ref, bias_ref, o_ref):
    acc = _dot(h_ref[...], w_ref[...])
    lane = lax.broadcasted_iota(jnp.int32, acc.shape, 1)
    dt = _softplus(acc + bias_ref[...])
    o_ref[...] = jnp.where(lane < SSD_HEADS, dt, dt * -jnp.exp(alog_ref[...]))


def _gates(kernel_fn, hb, w, alog, bias, rows_per_batch, name):
    rows = hb.shape[0]
    tm = _row_tile(rows_per_batch)
    vec = pl.BlockSpec((1, LANES), lambda i: (0, 0))
    return pl.pallas_call(
        kernel_fn,
        grid=(rows // tm,),
        in_specs=[
            pl.BlockSpec((tm, D_MODEL), lambda i: (i, 0)),
            pl.BlockSpec((D_MODEL, LANES), lambda i: (0, 0)),
            vec, vec,
        ],
        out_specs=pl.BlockSpec((tm, LANES), lambda i: (i, 0)),
        out_shape=jax.ShapeDtypeStruct((rows, LANES), F32),
        compiler_params=_params(("arbitrary",)),
        name=name,
    )(hb, w, alog, bias)


def _retention_kernel(q_ref, k_ref, v_ref, g_ref, gnw_ref, o_ref, s_ref):
    c = pl.program_id(1)

    @pl.when(c == 0)
    def _():
        s_ref[...] = jnp.zeros(s_ref.shape, F32)

    t_idx = lax.broadcasted_iota(jnp.int32, (CHUNK, CHUNK), 0)
    s_idx = lax.broadcasted_iota(jnp.int32, (CHUNK, CHUNK), 1)
    t_col = lax.broadcasted_iota(jnp.int32, (CHUNK, 1), 0).astype(F32)
    for hd in range(RET_HEADS):
        log_gamma = jnp.log1p(-jnp.exp2(jnp.full((1, 1), -5.0 - hd, F32)))
        q = q_ref[:, RET_DK * hd:RET_DK * (hd + 1)]
        k = k_ref[:, RET_DK * hd:RET_DK * (hd + 1)]
        v = v_ref[:, RET_DV * hd:RET_DV * (hd + 1)]
        decay = jnp.exp(jnp.where(t_idx >= s_idx, (t_idx - s_idx).astype(F32) * log_gamma, -jnp.inf))
        att = _dot_nt(q, k) * decay
        q_in = q.astype(F32) * jnp.exp((t_col + 1.0) * log_gamma)
        state = s_ref[hd]
        o = _dot(att.astype(BF16), v) + _dot(q_in.astype(BF16), state.astype(BF16))
        k_out = k.astype(F32) * jnp.exp((CHUNK - 1.0 - t_col) * log_gamma)
        s_ref[hd] = state * jnp.exp(CHUNK * log_gamma) + _dot(k_out.T.astype(BF16), v)
        oc = o - jnp.mean(o, axis=-1, keepdims=True)
        on = oc * lax.rsqrt(jnp.mean(oc * oc, axis=-1, keepdims=True) + GN_EPS)
        gate = _silu(g_ref[:, RET_DV * hd:RET_DV * (hd + 1)].astype(F32))
        o_ref[:, RET_DV * hd:RET_DV * (hd + 1)] = (
            on * gnw_ref[:, RET_DV * hd:RET_DV * (hd + 1)] * gate).astype(BF16)


def _retention(p0, gn_w, batch, rows_per_batch):
    nc = rows_per_batch // CHUNK
    row = lambda b, c: b * nc + c
    return pl.pallas_call(
        _retention_kernel,
        grid=(batch, nc),
        in_specs=[
            pl.BlockSpec((CHUNK, 512), lambda b, c: (row(b, c), 0)),
            pl.BlockSpec((CHUNK, 512), lambda b, c: (row(b, c), 1)),
            pl.BlockSpec((CHUNK, 1024), lambda b, c: (row(b, c), 1)),
            pl.BlockSpec((CHUNK, 1024), lambda b, c: (row(b, c), 2)),
            pl.BlockSpec((1, 1024), lambda b, c: (0, 0)),
        ],
        out_specs=pl.BlockSpec((CHUNK, 1024), lambda b, c: (row(b, c), 0)),
        out_shape=jax.ShapeDtypeStruct((p0.shape[0], 1024), BF16),
        scratch_shapes=[pltpu.VMEM((RET_HEADS, RET_DK, RET_DV), F32)],
        compiler_params=_params(("arbitrary", "arbitrary")),
        name="retention",
    )(p0, p0, p0, p0, gn_w)


def _inv_unit_lower(a):
    n = a.shape[0]
    eye = (lax.broadcasted_iota(jnp.int32, (n, n), 0) == lax.broadcasted_iota(jnp.int32, (n, n), 1)).astype(F32)
    inv = eye - a
    power = a
    span = 2
    while span < n:
        power = _dot_hi(power, power)
        inv = inv + _dot_hi(inv, power)
        span *= 2
    return inv


def _gdn_kernel(q_ref, k_ref, v_ref, g_ref, gcol_ref, grow_ref, nw_ref, o_ref, s_ref):
    c = pl.program_id(1)
    n = GDN_CHUNK

    @pl.when(c == 0)
    def _():
        s_ref[...] = jnp.zeros(s_ref.shape, F32)

    t_idx = lax.broadcasted_iota(jnp.int32, (n, n), 0)
    s_idx = lax.broadcasted_iota(jnp.int32, (n, n), 1)
    lower = (t_idx >= s_idx).astype(F32)
    upper = (t_idx <= s_idx).astype(F32)
    gcol = gcol_ref[...]
    gcum_col = _dot_hi(lower, gcol)
    gcum_row = _dot_hi(grow_ref[0], upper)
    for hd in range(GDN_HEADS):
        q = q_ref[:, GDN_DK * hd:GDN_DK * (hd + 1)].astype(F32)
        k = k_ref[:, GDN_DK * hd:GDN_DK * (hd + 1)].astype(F32)
        v = v_ref[:, GDN_DV * hd:GDN_DV * (hd + 1)].astype(F32)
        beta = gcol[:, hd:hd + 1]
        gc = gcum_col[:, GDN_HEADS + hd:GDN_HEADS + hd + 1]
        gr = gcum_row[GDN_HEADS + hd:GDN_HEADS + hd + 1, :]
        g_last = gc[n - 1:n, :]
        seg = gc - gr
        dec_strict = jnp.exp(jnp.where(t_idx > s_idx, seg, -jnp.inf))
        dec_incl = jnp.exp(jnp.where(t_idx >= s_idx, seg, -jnp.inf))
        kb = k * beta
        kb16 = kb.astype(BF16)
        k16 = k.astype(BF16)
        a = _dot_nt(kb16, k16) * dec_strict
        inv = _inv_unit_lower(a)
        e_gc = jnp.exp(gc)
        rhs = jnp.concatenate([v * beta, kb * e_gc], axis=-1)
        u = _dot_hi(inv, rhs)
        w_v = u[:, :GDN_DV]
        w_k = u[:, GDN_DV:]
        state = s_ref[hd]
        s16 = state.astype(BF16)
        v_new = w_v - _dot(w_k.astype(BF16), s16)
        v_new16 = v_new.astype(BF16)
        att = _dot_nt(q.astype(BF16), k16) * dec_incl
        o = _dot((q * e_gc).astype(BF16), s16) + _dot(att.astype(BF16), v_new16)
        k_out = k * jnp.exp(g_last - gc)
        s_ref[hd] = state * jnp.exp(g_last) + _dot(k_out.T.astype(BF16), v_new16)
        on = o * lax.rsqrt(jnp.mean(o * o, axis=-1, keepdims=True) + RMS_EPS)
        gate = _silu(g_ref[:, GDN_DV * hd:GDN_DV * (hd + 1)].astype(F32))
        o_ref[:, GDN_DV * hd:GDN_DV * (hd + 1)] = (on * nw_ref[...] * gate).astype(BF16)


def _gdn(p0, gates, gates_row, norm_w, batch, rows_per_batch):
    n = GDN_CHUNK
    nc = rows_per_batch // n
    row = lambda b, c: b * nc + c
    return pl.pallas_call(
        _gdn_kernel,
        grid=(batch, nc),
        in_specs=[
            pl.BlockSpec((n, 512), lambda b, c: (row(b, c), 6)),
            pl.BlockSpec((n, 512), lambda b, c: (row(b, c), 7)),
            pl.BlockSpec((n, 1024), lambda b, c: (row(b, c), 4)),
            pl.BlockSpec((n, 1024), lambda b, c: (row(b, c), 5)),
            pl.BlockSpec((n, LANES), lambda b, c: (row(b, c), 0)),
            pl.BlockSpec((1, 8, n), lambda b, c: (row(b, c), 0, 0)),
            pl.BlockSpec((1, GDN_DV), lambda b, c: (0, 0)),
        ],
        out_specs=pl.BlockSpec((n, 1024), lambda b, c: (row(b, c), 0)),
        out_shape=jax.ShapeDtypeStruct((p0.shape[0], 1024), BF16),
        scratch_shapes=[pltpu.VMEM((GDN_HEADS, GDN_DK, GDN_DV), F32)],
        compiler_params=_params(("arbitrary", "arbitrary")),
        name="gated_delta",
    )(p0, p0, p0, p0, gates, gates_row, norm_w)


def _layer_norm(x, w, b):
    xc = x - jnp.mean(x, axis=-1, keepdims=True)
    return xc * lax.rsqrt(jnp.mean(xc * xc, axis=-1, keepdims=True) + LN_EPS) * w + b


def _outproj_ln_kernel(ya_ref, yb_ref, w_ref, h_ref, lw_ref, lb_ref, o_ref):
    half = ya_ref.shape[1]
    acc = _dot(ya_ref[...], w_ref[0:half, :]) + _dot(yb_ref[...], w_ref[half:2 * half, :])
    o_ref[...] = _layer_norm(DN_ALPHA * h_ref[...] + acc, lw_ref[...], lb_ref[...])


def _outproj_ln(ya, yb, col_a, col_b, w, h, lw, lb, rows_per_batch):
    rows = h.shape[0]
    tm = _row_tile(rows_per_batch)
    half = w.shape[0] // 2
    vec = pl.BlockSpec((1, D_MODEL), lambda i: (0, 0))
    return pl.pallas_call(
        _outproj_ln_kernel,
        grid=(rows // tm,),
        in_specs=[
            pl.BlockSpec((tm, half), lambda i: (i, col_a)),
            pl.BlockSpec((tm, half), lambda i: (i, col_b)),
            pl.BlockSpec((2 * half, D_MODEL), lambda i: (0, 0), pipeline_mode=pl.Buffered(1)),
            pl.BlockSpec((tm, D_MODEL), lambda i: (i, 0)),
            vec, vec,
        ],
        out_specs=pl.BlockSpec((tm, D_MODEL), lambda i: (i, 0)),
        out_shape=jax.ShapeDtypeStruct((rows, D_MODEL), F32),
        compiler_params=_params(("arbitrary",)),
        name="outproj_ln",
    )(ya, yb, w, h, lw, lb)


def _mlp_ln_kernel(h_ref, w1_ref, w2_ref, lw_ref, lb_ref, o_ref, ob_ref):
    h = h_ref[...]
    hb = h.astype(BF16)
    acc = jnp.zeros(h.shape, F32)
    step = 1024
    for f in range(0, D_FF, step):
        a = _dot(hb, w1_ref[:, f:f + step])
        a = jnp.square(jnp.maximum(a, 0.0)).astype(BF16)
        acc = acc + _dot(a, w2_ref[f:f + step, :])
    out = _layer_norm(DN_ALPHA * h + acc, lw_ref[...], lb_ref[...])
    o_ref[...] = out
    ob_ref[...] = out.astype(BF16)


def _mlp_ln(h, w1, w2, lw, lb, rows_per_batch):
    rows = h.shape[0]
    tm = _row_tile(rows_per_batch)
    vec = pl.BlockSpec((1, D_MODEL), lambda i: (0, 0))
    tile = pl.BlockSpec((tm, D_MODEL), lambda i: (i, 0))
    return pl.pallas_call(
        _mlp_ln_kernel,
        grid=(rows // tm,),
        in_specs=[
            tile,
            pl.BlockSpec((D_MODEL, D_FF), lambda i: (0, 0), pipeline_mode=pl.Buffered(1)),
            pl.BlockSpec((D_FF, D_MODEL), lambda i: (0, 0), pipeline_mode=pl.Buffered(1)),
            vec, vec,
        ],
        out_specs=[tile, tile],
        out_shape=[jax.ShapeDtypeStruct((rows, D_MODEL), F32), jax.ShapeDtypeStruct((rows, D_MODEL), BF16)],
        compiler_params=_params(("arbitrary",)),
        name="mlp_ln",
    )(h, w1, w2, lw, lb)


def _proj_c_kernel(h_ref, w_ref, cw_ref, cb_ref, o_ref, xe_ref, *, tm, tiles_per_batch):
    j = pl.program_id(0)
    i = pl.program_id(1)
    acc = _dot(h_ref[...], w_ref[...])
    row = (i % tiles_per_batch) * tm + lax.broadcasted_iota(jnp.int32, (tm, 1), 0)
    live = row >= PAD_FRONT
    xe_ref[HALO:HALO + tm, :] = jnp.where(live, acc, 0.0)

    @pl.when(j < 4)
    def _plain():
        o_ref[...] = xe_ref[HALO:HALO + tm, :].astype(BF16)

    @pl.when(j >= 4)
    def _conv():
        @pl.when(i % tiles_per_batch == 0)
        def _():
            xe_ref[0:HALO, :] = jnp.zeros((HALO, COL_TILE), F32)

        y = _silu(_causal_conv(xe_ref, cw_ref, tm) + cb_ref[...])
        xe_ref[0:HALO, :] = xe_ref[tm:tm + HALO, :]
        o_ref[...] = jnp.where(live, y, 0.0).astype(BF16)


def _proj_c(hb, w, cw, cb, rows_per_batch):
    rows = hb.shape[0]
    tm = _row_tile(rows_per_batch)
    tpb = rows_per_batch // tm
    ncol = w.shape[1] // COL_TILE
    nconv = cw.shape[1] // COL_TILE
    return pl.pallas_call(
        functools.partial(_proj_c_kernel, tm=tm, tiles_per_batch=tpb),
        grid=(ncol, rows // tm),
        in_specs=[
            pl.BlockSpec((tm, D_MODEL), lambda j, i: (i, 0)),
            pl.BlockSpec((D_MODEL, COL_TILE), lambda j, i: (0, j)),
            pl.BlockSpec((CONV_K, COL_TILE), lambda j, i: (0, jnp.clip(j - 4, 0, nconv - 1))),
            pl.BlockSpec((1, COL_TILE), lambda j, i: (0, jnp.clip(j - 4, 0, nconv - 1))),
        ],
        out_specs=pl.BlockSpec((tm, COL_TILE), lambda j, i: (i, j)),
        out_shape=jax.ShapeDtypeStruct((rows, w.shape[1]), BF16),
        scratch_shapes=[pltpu.VMEM((tm + HALO, COL_TILE), F32)],
        compiler_params=_params(("arbitrary", "arbitrary")),
        name="proj_c",
    )(hb, w, cw, cb)


def _ssd_kernel(z_ref, x_ref, b_ref, c_ref, gcol_ref, grow_ref, dskip_ref, nw_ref, o_ref, s_ref):
    cidx = pl.program_id(1)
    n = CHUNK
    pair_w = 2 * SSD_HEADDIM
    pairs_per_group = SSD_HPG // 2
    group_w = SSD_HPG * SSD_HEADDIM

    @pl.when(cidx == 0)
    def _():
        s_ref[...] = jnp.zeros(s_ref.shape, F32)

    t_idx = lax.broadcasted_iota(jnp.int32, (n, n), 0)
    s_idx = lax.broadcasted_iota(jnp.int32, (n, n), 1)
    causal = t_idx >= s_idx
    lower = causal.astype(F32)
    upper = (t_idx <= s_idx).astype(F32)
    lane_v = lax.broadcasted_iota(jnp.int32, (n, pair_w), 1)
    lane_s = lax.broadcasted_iota(jnp.int32, (SSD_DSTATE, pair_w), 1)
    lane_1 = lax.broadcasted_iota(jnp.int32, (1, pair_w), 1)
    grow = grow_ref[0]
    gcum_col = _dot_hi(lower, gcol_ref[...])
    gcum_row = _dot_hi(grow, upper)
    for g in range(SSD_GROUPS):
        q = c_ref[:, SSD_DSTATE * g:SSD_DSTATE * (g + 1)]
        k = b_ref[:, SSD_DSTATE * g:SSD_DSTATE * (g + 1)]
        qf = q.astype(F32)
        kt = k.astype(F32).T
        cb = _dot_nt(q, k)
        ys = []
        for p in range(pairs_per_group):
            pair = g * pairs_per_group + p
            col = group_w * g + pair_w * p
            xv = x_ref[:, col:col + pair_w]
            state = s_ref[pair]
            o = jnp.zeros((n, pair_w), F32)
            upd = jnp.zeros((SSD_DSTATE, pair_w), F32)
            last = []
            for side in range(2):
                hd = 2 * pair + side
                gc = gcum_col[:, SSD_HEADS + hd:SSD_HEADS + hd + 1]
                gr = gcum_row[SSD_HEADS + hd:SSD_HEADS + hd + 1, :]
                dt_row = grow[hd:hd + 1, :]
                g_last = gr[:, n - 1:n]
                dec = jnp.exp(jnp.where(causal, gc - gr, -jnp.inf))
                att = cb * dec * dt_row
                q_in = qf * jnp.exp(gc)
                mine_v = (lane_v >= SSD_HEADDIM) if side else (lane_v < SSD_HEADDIM)
                mine_s = (lane_s >= SSD_HEADDIM) if side else (lane_s < SSD_HEADDIM)
                xm = jnp.where(mine_v, xv, jnp.zeros_like(xv))
                sm = jnp.where(mine_s, state, 0.0).astype(BF16)
                o = o + _dot(att.astype(BF16), xm) + _dot(q_in.astype(BF16), sm)
                k_out = kt * (jnp.exp(g_last - gr) * dt_row)
                upd = upd + _dot(k_out.astype(BF16), xm)
                last.append(jnp.exp(g_last))
            s_ref[pair] = state * jnp.where(lane_1 < SSD_HEADDIM, last[0], last[1]) + upd
            ys.append(o + xv.astype(F32) * dskip_ref[:, col:col + pair_w])
        y = jnp.concatenate(ys, axis=-1)
        y = y * _silu(z_ref[:, group_w * g:group_w * (g + 1)].astype(F32))
        y = y * lax.rsqrt(jnp.mean(y * y, axis=-1, keepdims=True) + RMS_EPS)
        o_ref[:, group_w * g:group_w * (g + 1)] = (y * nw_ref[:, group_w * g:group_w * (g + 1)]).astype(BF16)


def _ssd(p1, gates, gates_row, dskip, norm_w, batch, rows_per_batch):
    n = CHUNK
    nc = rows_per_batch // n
    row = lambda b, c: b * nc + c
    return pl.pallas_call(
        _ssd_kernel,
        grid=(batch, nc),
        in_specs=[
            pl.BlockSpec((n, SSD_DINNER), lambda b, c: (row(b, c), 0)),
            pl.BlockSpec((n, SSD_DINNER), lambda b, c: (row(b, c), 1)),
            pl.BlockSpec((n, SSD_GN), lambda b, c: (row(b, c), 8)),
            pl.BlockSpec((n, SSD_GN), lambda b, c: (row(b, c), 9)),
            pl.BlockSpec((n, LANES), lambda b, c: (row(b, c), 0)),
            pl.BlockSpec((1, 2 * SSD_HEADS, n), lambda b, c: (b, 0, c)),
            pl.BlockSpec((1, SSD_DINNER), lambda b, c: (0, 0)),
            pl.BlockSpec((1, SSD_DINNER), lambda b, c: (0, 0)),
        ],
        out_specs=pl.BlockSpec((n, SSD_DINNER), lambda b, c: (row(b, c), 0)),
        out_shape=jax.ShapeDtypeStruct((p1.shape[0], SSD_DINNER), BF16),
        scratch_shapes=[pltpu.VMEM((SSD_HEADS // 2, SSD_DSTATE, 2 * SSD_HEADDIM), F32)],
        compiler_params=_params(("arbitrary", "arbitrary")),
        name="ssd",
    )(p1, p1, p1, p1, gates, gates_row, dskip, norm_w)


def _pad_lanes(v):
    v = v.reshape(1, -1).astype(F32)
    return jnp.pad(v, ((0, 0), (0, LANES - v.shape[1])))


def kernel(x, meta_tokens, ab_w_in, ab_ret_gn_w, ab_conv_q, ab_conv_k, ab_conv_v, ab_A_log, ab_dt_bias, ab_gdn_norm_w, ab_w_out, c_w_in, c_conv_w, c_conv_b, c_A_log, c_dt_bias, c_D, c_norm_w, c_w_out, mlp_w1, mlp_w2, ln1_w, ln1_b, ln2_w, ln2_b):
    batch, seq, d = x.shape
    assert d == D_MODEL and meta_tokens.shape == (N_META, D_MODEL)
    lp = PAD_FRONT + N_META + seq
    rows = batch * lp

    meta = jnp.broadcast_to(meta_tokens[None].astype(x.dtype), (batch, N_META, d))
    h = jnp.concatenate([jnp.zeros((batch, PAD_FRONT, d), x.dtype), meta, x], axis=1).reshape(rows, d)
    hb = h.astype(BF16)

    pos = jnp.arange(lp, dtype=F32) - PAD_FRONT
    inv_freq = 1.0 / (ROPE_BASE ** jnp.linspace(0.0, 1.0, RET_DK // 2, dtype=F32))
    ang = pos[:, None] * inv_freq[None]
    cosf = jnp.concatenate([jnp.cos(ang), jnp.cos(ang)], axis=-1)
    sinf = jnp.concatenate([-jnp.sin(ang), jnp.sin(ang)], axis=-1)

    w_in = ab_w_in[0]
    w_main = jnp.concatenate([w_in[:, :5120], w_in[:, 5128:]], axis=1).astype(BF16)
    w_gate = jnp.pad(w_in[:, 5120:5128], ((0, 0), (0, LANES - 8))).astype(BF16)
    conv_w = jnp.concatenate([ab_conv_q[0], ab_conv_k[0], ab_conv_v[0]], axis=1)
    p0 = _proj_ab(hb, w_main, cosf, sinf, conv_w, lp)
    alog = _pad_lanes(jnp.concatenate([jnp.zeros((GDN_HEADS,), F32), ab_A_log[0]]))
    bias = _pad_lanes(jnp.concatenate([jnp.zeros((GDN_HEADS,), F32), ab_dt_bias[0]]))
    g0 = _gates(_gates_ab_kernel, hb, w_gate, alog, bias, lp, "gates_ab")
    g0_row = jnp.swapaxes(g0[:, :8].reshape(rows // GDN_CHUNK, GDN_CHUNK, 8), 1, 2)
    y_ret = _retention(p0, ab_ret_gn_w[0].reshape(1, -1), batch, lp)
    y_gdn = _gdn(p0, g0, g0_row, ab_gdn_norm_w[0].reshape(1, -1), batch, lp)
    h = _outproj_ln(y_ret, y_gdn, 0, 0, ab_w_out[0].astype(BF16), h,
                    ln1_w[0].reshape(1, -1), ln1_b[0].reshape(1, -1), lp)
    h, hb = _mlp_ln(h, mlp_w1[0].astype(BF16), mlp_w2[0].astype(BF16),
                    ln2_w[0].reshape(1, -1), ln2_b[0].reshape(1, -1), lp)

    w_in = c_w_in[0]
    w_dt = w_in[:, 5120:]
    w_gate = jnp.pad(jnp.concatenate([w_dt, w_dt], axis=1), ((0, 0), (0, LANES - 2 * SSD_HEADS))).astype(BF16)
    p1 = _proj_c(hb, w_in[:, :5120].astype(BF16), c_conv_w[0], c_conv_b[0].reshape(1, -1), lp)
    alog = _pad_lanes(jnp.concatenate([jnp.zeros((SSD_HEADS,), F32), c_A_log[0]]))
    bias = _pad_lanes(jnp.concatenate([c_dt_bias[0], c_dt_bias[0]]))
    g1 = _gates(_gates_c_kernel, hb, w_gate, alog, bias, lp, "gates_c")
    g1_row = jnp.swapaxes(g1[:, :2 * SSD_HEADS].reshape(batch, lp, 2 * SSD_HEADS), 1, 2)
    dskip = jnp.repeat(c_D[0].astype(F32), SSD_HEADDIM).reshape(1, -1)
    y_ssd = _ssd(p1, g1, g1_row, dskip, c_norm_w[0].reshape(1, -1), batch, lp)
    h = _outproj_ln(y_ssd, y_ssd, 0, 1, c_w_out[0].astype(BF16), h,
                    ln1_w[1].reshape(1, -1), ln1_b[1].reshape(1, -1), lp)
    h, _ = _mlp_ln(h, mlp_w1[1].astype(BF16), mlp_w2[1].astype(BF16),
                   ln2_w[1].reshape(1, -1), ln2_b[1].reshape(1, -1), lp)
    return h.reshape(batch, lp, d)[:, PAD_FRONT + N_META:]
```

```python
import functools

import jax
import jax.numpy as jnp
from jax import lax
from jax.experimental import pallas as pl
from jax.experimental.pallas import tpu as pltpu

F32 = jnp.float32
BF16 = jnp.bfloat16
HI = lax.Precision.HIGHEST

D_MODEL = 1024
DEPTH = 2
N_META = 16
CONV_K = 4
RET_HEADS = 4
RET_DK = 128
RET_DV = 256
ROPE_BASE = 10000.0
GDN_HEADS = 4
GDN_DK = 128
GDN_DV = 256
SSD_DINNER = 2048
SSD_HEADDIM = 64
SSD_HEADS = 32
SSD_GROUPS = 4
SSD_HPG = 8
SSD_DSTATE = 128
SSD_GN = 512
D_FF = 4096
DN_ALPHA = (2 * DEPTH) ** 0.25
LN_EPS = 1e-5
GN_EPS = 1e-5
RMS_EPS = 1e-6

LANES = 128
COL_TILE = 512
CHUNK = 128
GDN_CHUNK = 64
PAD_FRONT = CHUNK - N_META
HALO = 8
VMEM_LIMIT = 56 * 1024 * 1024


def _row_tile(rows_per_batch):
    for tm in (640, 512, 256, 128):
        if rows_per_batch % tm == 0:
            return tm
    raise ValueError(f"unsupported padded sequence length {rows_per_batch}")


def _params(sem):
    return pltpu.CompilerParams(dimension_semantics=sem, vmem_limit_bytes=VMEM_LIMIT)


def _softplus(x):
    return jnp.maximum(x, 0.0) + jnp.log1p(jnp.exp(-jnp.abs(x)))


def _silu(x):
    return x * jax.nn.sigmoid(x)


def _dot(a, b):
    return jnp.dot(a, b, preferred_element_type=F32)


def _dot_nt(a, b):
    return lax.dot_general(a, b, (((1,), (1,)), ((), ())), preferred_element_type=F32)


def _dot_hi(a, b):
    return jnp.dot(a, b, preferred_element_type=F32, precision=HI)


def _split(a):
    hi = a.astype(BF16)
    return hi, (a - hi.astype(F32)).astype(BF16)


def _dot_split(a, b):
    return _dot(a[0], b[0]) + (_dot(a[1], b[0]) + _dot(a[0], b[1]))


def _causal_conv(xe_ref, cw_ref, tm):
    y = cw_ref[CONV_K - 1:CONV_K, :] * xe_ref[HALO:HALO + tm, :]
    for d in range(1, CONV_K):
        y = y + cw_ref[CONV_K - 1 - d:CONV_K - d, :] * xe_ref[HALO - d:HALO - d + tm, :]
    return y


def _proj_ab_kernel(h_ref, w_ref, cos_ref, sin_ref, cw_ref, o_ref, xe_ref, *, tm, tiles_per_batch):
    j = pl.program_id(0)
    i = pl.program_id(1)
    xe_ref[HALO:HALO + tm, :] = _dot(h_ref[...], w_ref[...])

    @pl.when(j < 2)
    def _rope():
        scale = jnp.where(j == 0, RET_DK ** -0.5, 1.0).astype(F32)
        for hd in range(COL_TILE // RET_DK):
            x = xe_ref[HALO:HALO + tm, RET_DK * hd:RET_DK * (hd + 1)]
            y = x * cos_ref[...] + pltpu.roll(x, RET_DK // 2, 1) * sin_ref[...]
            o_ref[:, RET_DK * hd:RET_DK * (hd + 1)] = (y * scale).astype(BF16)

    @pl.when(jnp.logical_or(jnp.logical_and(j >= 2, j < 6), j >= 10))
    def _plain():
        o_ref[...] = xe_ref[HALO:HALO + tm, :].astype(BF16)

    @pl.when(jnp.logical_and(j >= 6, j < 10))
    def _conv():
        @pl.when(i % tiles_per_batch == 0)
        def _():
            xe_ref[0:HALO, :] = jnp.zeros((HALO, COL_TILE), F32)

        y = _silu(_causal_conv(xe_ref, cw_ref, tm))
        xe_ref[0:HALO, :] = xe_ref[tm:tm + HALO, :]

        @pl.when(j < 8)
        def _l2():
            scale = jnp.where(j == 6, GDN_DK ** -0.5, 1.0).astype(F32)
            for hd in range(COL_TILE // GDN_DK):
                yh = y[:, GDN_DK * hd:GDN_DK * (hd + 1)]
                inv = lax.rsqrt(jnp.sum(yh * yh, axis=-1, keepdims=True) + 1e-6)
                o_ref[:, GDN_DK * hd:GDN_DK * (hd + 1)] = (yh * (inv * scale)).astype(BF16)

        @pl.when(j >= 8)
        def _v():
            o_ref[...] = y.astype(BF16)


def _proj_ab(hb, w, cosf, sinf, cw, rows_per_batch):
    rows = hb.shape[0]
    tm = _row_tile(rows_per_batch)
    tpb = rows_per_batch // tm
    ncol = w.shape[1] // COL_TILE
    return pl.pallas_call(
        functools.partial(_proj_ab_kernel, tm=tm, tiles_per_batch=tpb),
        grid=(ncol, rows // tm),
        in_specs=[
            pl.BlockSpec((tm, D_MODEL), lambda j, i: (i, 0)),
            pl.BlockSpec((D_MODEL, COL_TILE), lambda j, i: (0, j)),
            pl.BlockSpec((tm, LANES), lambda j, i: (i % tpb, 0)),
            pl.BlockSpec((tm, LANES), lambda j, i: (i % tpb, 0)),
            pl.BlockSpec((CONV_K, COL_TILE), lambda j, i: (0, jnp.clip(j - 6, 0, 3))),
        ],
        out_specs=pl.BlockSpec((tm, COL_TILE), lambda j, i: (i, j)),
        out_shape=jax.ShapeDtypeStruct((rows, w.shape[1]), BF16),
        scratch_shapes=[pltpu.VMEM((tm + HALO, COL_TILE), F32)],
        compiler_params=_params(("arbitrary", "arbitrary")),
        name="proj_ab",
    )(hb, w, cosf, sinf, cw)


def _gates_ab_kernel(h_ref, w_ref, alog_ref, bias_ref, o_ref):
    acc = _dot(h_ref[...], w_ref[...])
    lane = lax.broadcasted_iota(jnp.int32, acc.shape, 1)
    decay = -jnp.exp(alog_ref[...]) * _softplus(acc + bias_ref[...])
    o_ref[...] = jnp.where(lane < GDN_HEADS, jax.nn.sigmoid(acc), decay)


def _gates_c_kernel(h_ref, w_ref, alog_ref, bias_ref, o_ref):
    acc = _dot(h_ref[...], w_ref[...])
    lane = lax.broadcasted_iota(jnp.int32, acc.shape, 1)
    dt = _softplus(acc + bias_ref[...])
    o_ref[...] = jnp.where(lane < SSD_HEADS, dt, dt * -jnp.exp(alog_ref[...]))


def _gates(kernel_fn, hb, w, alog, bias, rows_per_batch, name):
    rows = hb.shape[0]
    tm = _row_tile(rows_per_batch)
    vec = pl.BlockSpec((1, LANES), lambda i: (0, 0))
    return pl.pallas_call(
        kernel_fn,
        grid=(rows // tm,),
        in_specs=[
            pl.BlockSpec((tm, D_MODEL), lambda i: (i, 0)),
            pl.BlockSpec((D_MODEL, LANES), lambda i: (0, 0)),
            vec, vec,
        ],
        out_specs=pl.BlockSpec((tm, LANES), lambda i: (i, 0)),
        out_shape=jax.ShapeDtypeStruct((rows, LANES), F32),
        compiler_params=_params(("arbitrary",)),
        name=name,
    )(hb, w, alog, bias)


def _retention_kernel(q_ref, k_ref, v_ref, g_ref, gnw_ref, o_ref, s_ref):
    c = pl.program_id(1)

    @pl.when(c == 0)
    def _():
        s_ref[...] = jnp.zeros(s_ref.shape, F32)

    t_idx = lax.broadcasted_iota(jnp.int32, (CHUNK, CHUNK), 0)
    s_idx = lax.broadcasted_iota(jnp.int32, (CHUNK, CHUNK), 1)
    t_col = lax.broadcasted_iota(jnp.int32, (CHUNK, 1), 0).astype(F32)
    for hd in range(RET_HEADS):
        log_gamma = jnp.log1p(-jnp.exp2(jnp.full((1, 1), -5.0 - hd, F32)))
        q = q_ref[:, RET_DK * hd:RET_DK * (hd + 1)]
        k = k_ref[:, RET_DK * hd:RET_DK * (hd + 1)]
        v = v_ref[:, RET_DV * hd:RET_DV * (hd + 1)]
        decay = jnp.exp(jnp.where(t_idx >= s_idx, (t_idx - s_idx).astype(F32) * log_gamma, -jnp.inf))
        att = _dot_nt(q, k) * decay
        q_in = q.astype(F32) * jnp.exp((t_col + 1.0) * log_gamma)
        state = s_ref[hd]
        o = _dot(att.astype(BF16), v) + _dot(q_in.astype(BF16), state.astype(BF16))
        k_out = k.astype(F32) * jnp.exp((CHUNK - 1.0 - t_col) * log_gamma)
        s_ref[hd] = state * jnp.exp(CHUNK * log_gamma) + _dot(k_out.T.astype(BF16), v)
        oc = o - jnp.mean(o, axis=-1, keepdims=True)
        on = oc * lax.rsqrt(jnp.mean(oc * oc, axis=-1, keepdims=True) + GN_EPS)
        gate = _silu(g_ref[:, RET_DV * hd:RET_DV * (hd + 1)].astype(F32))
        o_ref[:, RET_DV * hd:RET_DV * (hd + 1)] = (
            on * gnw_ref[:, RET_DV * hd:RET_DV * (hd + 1)] * gate).astype(BF16)


def _retention(p0, gn_w, batch, rows_per_batch):
    nc = rows_per_batch // CHUNK
    row = lambda b, c: b * nc + c
    return pl.pallas_call(
        _retention_kernel,
        grid=(batch, nc),
        in_specs=[
            pl.BlockSpec((CHUNK, 512), lambda b, c: (row(b, c), 0)),
            pl.BlockSpec((CHUNK, 512), lambda b, c: (row(b, c), 1)),
            pl.BlockSpec((CHUNK, 1024), lambda b, c: (row(b, c), 1)),
            pl.BlockSpec((CHUNK, 1024), lambda b, c: (row(b, c), 2)),
            pl.BlockSpec((1, 1024), lambda b, c: (0, 0)),
        ],
        out_specs=pl.BlockSpec((CHUNK, 1024), lambda b, c: (row(b, c), 0)),
        out_shape=jax.ShapeDtypeStruct((p0.shape[0], 1024), BF16),
        scratch_shapes=[pltpu.VMEM((RET_HEADS, RET_DK, RET_DV), F32)],
        compiler_params=_params(("arbitrary", "arbitrary")),
        name="retention",
    )(p0, p0, p0, p0, gn_w)


def _gdn_kernel(q_ref, k_ref, v_ref, g_ref, gcol_ref, grow_ref, nw_ref, o_ref, s_ref, *, batch):
    n = GDN_CHUNK

    @pl.when(pl.program_id(0) == 0)
    def _():
        s_ref[...] = jnp.zeros(s_ref.shape, F32)

    t_idx = lax.broadcasted_iota(jnp.int32, (n, n), 0)
    s_idx = lax.broadcasted_iota(jnp.int32, (n, n), 1)
    lower = (t_idx >= s_idx).astype(F32)
    upper = (t_idx <= s_idx).astype(F32)
    eye = (t_idx == s_idx).astype(F32)
    chains = [(b, hd) for b in range(batch) for hd in range(GDN_HEADS)]
    cs = range(len(chains))
    gcol = [gcol_ref[b] for b in range(batch)]
    gcum_col = [_dot_hi(lower, gcol[b]) for b in range(batch)]
    gcum_row = [_dot_hi(grow_ref[b, 0], upper) for b in range(batch)]
    states = [s_ref[c] for c in cs]
    q = [q_ref[b, :, GDN_DK * hd:GDN_DK * (hd + 1)].astype(F32) for b, hd in chains]
    k = [k_ref[b, :, GDN_DK * hd:GDN_DK * (hd + 1)].astype(F32) for b, hd in chains]
    v = [v_ref[b, :, GDN_DV * hd:GDN_DV * (hd + 1)].astype(F32) for b, hd in chains]
    beta = [gcol[b][:, hd:hd + 1] for b, hd in chains]
    gc = [gcum_col[b][:, GDN_HEADS + hd:GDN_HEADS + hd + 1] for b, hd in chains]
    gr = [gcum_row[b][GDN_HEADS + hd:GDN_HEADS + hd + 1, :] for b, hd in chains]
    g_last = [gc[c][n - 1:n, :] for c in cs]
    seg = [gc[c] - gr[c] for c in cs]
    dec_strict = [jnp.exp(jnp.where(t_idx > s_idx, seg[c], -jnp.inf)) for c in cs]
    dec_incl = [jnp.exp(jnp.where(t_idx >= s_idx, seg[c], -jnp.inf)) for c in cs]
    kb = [k[c] * beta[c] for c in cs]
    k16 = [k[c].astype(BF16) for c in cs]
    a = [_dot_nt(kb[c].astype(BF16), k16[c]) * dec_strict[c] for c in cs]
    inv = [eye - a[c] for c in cs]
    power = [_split(a[c]) for c in cs]
    span = 2
    while span < n:
        power = [_split(_dot_split(power[c], power[c])) for c in cs]
        inv = [inv[c] + _dot_split(_split(inv[c]), power[c]) for c in cs]
        span *= 2
    e_gc = [jnp.exp(gc[c]) for c in cs]
    rhs = [jnp.concatenate([v[c] * beta[c], kb[c] * e_gc[c]], axis=-1) for c in cs]
    u = [_dot_split(_split(inv[c]), _split(rhs[c])) for c in cs]
    s16 = [states[c].astype(BF16) for c in cs]
    v_new = [u[c][:, :GDN_DV] - _dot(u[c][:, GDN_DV:].astype(BF16), s16[c]) for c in cs]
    v_new16 = [v_new[c].astype(BF16) for c in cs]
    att = [_dot_nt(q[c].astype(BF16), k16[c]) * dec_incl[c] for c in cs]
    o = [_dot((q[c] * e_gc[c]).astype(BF16), s16[c]) + _dot(att[c].astype(BF16), v_new16[c]) for c in cs]
    k_out = [k[c] * jnp.exp(g_last[c] - gc[c]) for c in cs]
    for c in cs:
        s_ref[c] = states[c] * jnp.exp(g_last[c]) + _dot(k_out[c].T.astype(BF16), v_new16[c])
    for c, (b, hd) in enumerate(chains):
        on = o[c] * lax.rsqrt(jnp.mean(o[c] * o[c], axis=-1, keepdims=True) + RMS_EPS)
        gate = _silu(g_ref[b, :, GDN_DV * hd:GDN_DV * (hd + 1)].astype(F32))
        o_ref[b, :, GDN_DV * hd:GDN_DV * (hd + 1)] = (on * nw_ref[...] * gate).astype(BF16)


def _gdn(p0, gates, norm_w, batch, rows_per_batch):
    n = GDN_CHUNK
    nc = rows_per_batch // n
    p0 = p0.reshape(batch, rows_per_batch, p0.shape[1])
    gates = gates.reshape(batch, rows_per_batch, LANES)
    gates_row = jnp.swapaxes(gates[:, :, :8].reshape(batch, nc, n, 8), 2, 3)
    out = pl.pallas_call(
        functools.partial(_gdn_kernel, batch=batch),
        grid=(nc,),
        in_specs=[
            pl.BlockSpec((batch, n, 512), lambda c: (0, c, 6)),
            pl.BlockSpec((batch, n, 512), lambda c: (0, c, 7)),
            pl.BlockSpec((batch, n, 1024), lambda c: (0, c, 4)),
            pl.BlockSpec((batch, n, 1024), lambda c: (0, c, 5)),
            pl.BlockSpec((batch, n, LANES), lambda c: (0, c, 0)),
            pl.BlockSpec((batch, 1, 8, n), lambda c: (0, c, 0, 0)),
            pl.BlockSpec((1, GDN_DV), lambda c: (0, 0)),
        ],
        out_specs=pl.BlockSpec((batch, n, 1024), lambda c: (0, c, 0)),
        out_shape=jax.ShapeDtypeStruct((batch, rows_per_batch, 1024), BF16),
        scratch_shapes=[pltpu.VMEM((batch * GDN_HEADS, GDN_DK, GDN_DV), F32)],
        compiler_params=_params(("arbitrary",)),
        name="gated_delta",
    )(p0, p0, p0, p0, gates, gates_row, norm_w)
    return out.reshape(batch * rows_per_batch, 1024)


def _layer_norm(x, w, b):
    xc = x - jnp.mean(x, axis=-1, keepdims=True)
    return xc * lax.rsqrt(jnp.mean(xc * xc, axis=-1, keepdims=True) + LN_EPS) * w + b


def _outproj_ln_kernel(ya_ref, yb_ref, w_ref, h_ref, lw_ref, lb_ref, o_ref):
    half = ya_ref.shape[1]
    acc = _dot(ya_ref[...], w_ref[0:half, :]) + _dot(yb_ref[...], w_ref[half:2 * half, :])
    o_ref[...] = _layer_norm(DN_ALPHA * h_ref[...] + acc, lw_ref[...], lb_ref[...])


def _outproj_ln(ya, yb, col_a, col_b, w, h, lw, lb, rows_per_batch):
    rows = h.shape[0]
    tm = _row_tile(rows_per_batch)
    half = w.shape[0] // 2
    vec = pl.BlockSpec((1, D_MODEL), lambda i: (0, 0))
    return pl.pallas_call(
        _outproj_ln_kernel,
        grid=(rows // tm,),
        in_specs=[
            pl.BlockSpec((tm, half), lambda i: (i, col_a)),
            pl.BlockSpec((tm, half), lambda i: (i, col_b)),
            pl.BlockSpec((2 * half, D_MODEL), lambda i: (0, 0), pipeline_mode=pl.Buffered(1)),
            pl.BlockSpec((tm, D_MODEL), lambda i: (i, 0)),
            vec, vec,
        ],
        out_specs=pl.BlockSpec((tm, D_MODEL), lambda i: (i, 0)),
        out_shape=jax.ShapeDtypeStruct((rows, D_MODEL), F32),
        compiler_params=_params(("arbitrary",)),
        name="outproj_ln",
    )(ya, yb, w, h, lw, lb)


def _mlp_ln_kernel(h_ref, w1_ref, w2_ref, lw_ref, lb_ref, o_ref, ob_ref):
    h = h_ref[...]
    hb = h.astype(BF16)
    acc = jnp.zeros(h.shape, F32)
    step = 1024
    for f in range(0, D_FF, step):
        a = _dot(hb, w1_ref[:, f:f + step])
        a = jnp.square(jnp.maximum(a, 0.0)).astype(BF16)
        acc = acc + _dot(a, w2_ref[f:f + step, :])
    out = _layer_norm(DN_ALPHA * h + acc, lw_ref[...], lb_ref[...])
    o_ref[...] = out
    ob_ref[...] = out.astype(BF16)


def _mlp_ln(h, w1, w2, lw, lb, rows_per_batch):
    rows = h.shape[0]
    tm = _row_tile(rows_per_batch)
    vec = pl.BlockSpec((1, D_MODEL), lambda i: (0, 0))
    tile = pl.BlockSpec((tm, D_MODEL), lambda i: (i, 0))
    return pl.pallas_call(
        _mlp_ln_kernel,
        grid=(rows // tm,),
        in_specs=[
            tile,
            pl.BlockSpec((D_MODEL, D_FF), lambda i: (0, 0), pipeline_mode=pl.Buffered(1)),
            pl.BlockSpec((D_FF, D_MODEL), lambda i: (0, 0), pipeline_mode=pl.Buffered(1)),
            vec, vec,
        ],
        out_specs=[tile, tile],
        out_shape=[jax.ShapeDtypeStruct((rows, D_MODEL), F32), jax.ShapeDtypeStruct((rows, D_MODEL), BF16)],
        compiler_params=_params(("arbitrary",)),
        name="mlp_ln",
    )(h, w1, w2, lw, lb)


def _proj_c_kernel(h_ref, w_ref, cw_ref, cb_ref, o_ref, xe_ref, *, tm, tiles_per_batch):
    j = pl.program_id(0)
    i = pl.program_id(1)
    acc = _dot(h_ref[...], w_ref[...])
    row = (i % tiles_per_batch) * tm + lax.broadcasted_iota(jnp.int32, (tm, 1), 0)
    live = row >= PAD_FRONT
    xe_ref[HALO:HALO + tm, :] = jnp.where(live, acc, 0.0)

    @pl.when(j < 4)
    def _plain():
        o_ref[...] = xe_ref[HALO:HALO + tm, :].astype(BF16)

    @pl.when(j >= 4)
    def _conv():
        @pl.when(i % tiles_per_batch == 0)
        def _():
            xe_ref[0:HALO, :] = jnp.zeros((HALO, COL_TILE), F32)

        y = _silu(_causal_conv(xe_ref, cw_ref, tm) + cb_ref[...])
        xe_ref[0:HALO, :] = xe_ref[tm:tm + HALO, :]
        o_ref[...] = jnp.where(live, y, 0.0).astype(BF16)


def _proj_c(hb, w, cw, cb, rows_per_batch):
    rows = hb.shape[0]
    tm = _row_tile(rows_per_batch)
    tpb = rows_per_batch // tm
    ncol = w.shape[1] // COL_TILE
    nconv = cw.shape[1] // COL_TILE
    return pl.pallas_call(
        functools.partial(_proj_c_kernel, tm=tm, tiles_per_batch=tpb),
        grid=(ncol, rows // tm),
        in_specs=[
            pl.BlockSpec((tm, D_MODEL), lambda j, i: (i, 0)),
            pl.BlockSpec((D_MODEL, COL_TILE), lambda j, i: (0, j)),
            pl.BlockSpec((CONV_K, COL_TILE), lambda j, i: (0, jnp.clip(j - 4, 0, nconv - 1))),
            pl.BlockSpec((1, COL_TILE), lambda j, i: (0, jnp.clip(j - 4, 0, nconv - 1))),
        ],
        out_specs=pl.BlockSpec((tm, COL_TILE), lambda j, i: (i, j)),
        out_shape=jax.ShapeDtypeStruct((rows, w.shape[1]), BF16),
        scratch_shapes=[pltpu.VMEM((tm + HALO, COL_TILE), F32)],
        compiler_params=_params(("arbitrary", "arbitrary")),
        name="proj_c",
    )(hb, w, cw, cb)


def _ssd_kernel(z_ref, x_ref, b_ref, c_ref, gcol_ref, grow_ref, dskip_ref, nw_ref, o_ref, s_ref):
    cidx = pl.program_id(1)
    n = CHUNK
    pair_w = 2 * SSD_HEADDIM
    pairs_per_group = SSD_HPG // 2
    group_w = SSD_HPG * SSD_HEADDIM

    @pl.when(cidx == 0)
    def _():
        s_ref[...] = jnp.zeros(s_ref.shape, F32)

    t_idx = lax.broadcasted_iota(jnp.int32, (n, n), 0)
    s_idx = lax.broadcasted_iota(jnp.int32, (n, n), 1)
    causal = t_idx >= s_idx
    lower = causal.astype(F32)
    upper = (t_idx <= s_idx).astype(F32)
    lane_v = lax.broadcasted_iota(jnp.int32, (n, pair_w), 1)
    lane_s = lax.broadcasted_iota(jnp.int32, (SSD_DSTATE, pair_w), 1)
    lane_1 = lax.broadcasted_iota(jnp.int32, (1, pair_w), 1)
    grow = grow_ref[0]
    gcum_col = _dot_hi(lower, gcol_ref[...])
    gcum_row = _dot_hi(grow, upper)
    for g in range(SSD_GROUPS):
        q = c_ref[:, SSD_DSTATE * g:SSD_DSTATE * (g + 1)]
        k = b_ref[:, SSD_DSTATE * g:SSD_DSTATE * (g + 1)]
        qf = q.astype(F32)
        kt = k.astype(F32).T
        cb = _dot_nt(q, k)
        ys = []
        for p in range(pairs_per_group):
            pair = g * pairs_per_group + p
            col = group_w * g + pair_w * p
            xv = x_ref[:, col:col + pair_w]
            state = s_ref[pair]
            o = jnp.zeros((n, pair_w), F32)
            upd = jnp.zeros((SSD_DSTATE, pair_w), F32)
            last = []
            for side in range(2):
                hd = 2 * pair + side
                gc = gcum_col[:, SSD_HEADS + hd:SSD_HEADS + hd + 1]
                gr = gcum_row[SSD_HEADS + hd:SSD_HEADS + hd + 1, :]
                dt_row = grow[hd:hd + 1, :]
                g_last = gr[:, n - 1:n]
                dec = jnp.exp(jnp.where(causal, gc - gr, -jnp.inf))
                att = cb * dec * dt_row
                q_in = qf * jnp.exp(gc)
                mine_v = (lane_v >= SSD_HEADDIM) if side else (lane_v < SSD_HEADDIM)
                mine_s = (lane_s >= SSD_HEADDIM) if side else (lane_s < SSD_HEADDIM)
                xm = jnp.where(mine_v, xv, jnp.zeros_like(xv))
                sm = jnp.where(mine_s, state, 0.0).astype(BF16)
                o = o + _dot(att.astype(BF16), xm) + _dot(q_in.astype(BF16), sm)
                k_out = kt * (jnp.exp(g_last - gr) * dt_row)
                upd = upd + _dot(k_out.astype(BF16), xm)
                last.append(jnp.exp(g_last))
            s_ref[pair] = state * jnp.where(lane_1 < SSD_HEADDIM, last[0], last[1]) + upd
            ys.append(o + xv.astype(F32) * dskip_ref[:, col:col + pair_w])
        y = jnp.concatenate(ys, axis=-1)
        y = y * _silu(z_ref[:, group_w * g:group_w * (g + 1)].astype(F32))
        y = y * lax.rsqrt(jnp.mean(y * y, axis=-1, keepdims=True) + RMS_EPS)
        o_ref[:, group_w * g:group_w * (g + 1)] = (y * nw_ref[:, group_w * g:group_w * (g + 1)]).astype(BF16)


def _ssd(p1, gates, gates_row, dskip, norm_w, batch, rows_per_batch):
    n = CHUNK
    nc = rows_per_batch // n
    row = lambda b, c: b * nc + c
    return pl.pallas_call(
        _ssd_kernel,
        grid=(batch, nc),
        in_specs=[
            pl.BlockSpec((n, SSD_DINNER), lambda b, c: (row(b, c), 0)),
            pl.BlockSpec((n, SSD_DINNER), lambda b, c: (row(b, c), 1)),
            pl.BlockSpec((n, SSD_GN), lambda b, c: (row(b, c), 8)),
            pl.BlockSpec((n, SSD_GN), lambda b, c: (row(b, c), 9)),
            pl.BlockSpec((n, LANES), lambda b, c: (row(b, c), 0)),
            pl.BlockSpec((1, 2 * SSD_HEADS, n), lambda b, c: (b, 0, c)),
            pl.BlockSpec((1, SSD_DINNER), lambda b, c: (0, 0)),
            pl.BlockSpec((1, SSD_DINNER), lambda b, c: (0, 0)),
        ],
        out_specs=pl.BlockSpec((n, SSD_DINNER), lambda b, c: (row(b, c), 0)),
        out_shape=jax.ShapeDtypeStruct((p1.shape[0], SSD_DINNER), BF16),
        scratch_shapes=[pltpu.VMEM((SSD_HEADS // 2, SSD_DSTATE, 2 * SSD_HEADDIM), F32)],
        compiler_params=_params(("arbitrary", "arbitrary")),
        name="ssd",
    )(p1, p1, p1, p1, gates, gates_row, dskip, norm_w)


def _pad_lanes(v):
    v = v.reshape(1, -1).astype(F32)
    return jnp.pad(v, ((0, 0), (0, LANES - v.shape[1])))


def kernel(x, meta_tokens, ab_w_in, ab_ret_gn_w, ab_conv_q, ab_conv_k, ab_conv_v, ab_A_log, ab_dt_bias, ab_gdn_norm_w, ab_w_out, c_w_in, c_conv_w, c_conv_b, c_A_log, c_dt_bias, c_D, c_norm_w, c_w_out, mlp_w1, mlp_w2, ln1_w, ln1_b, ln2_w, ln2_b):
    batch, seq, d = x.shape
    assert d == D_MODEL and meta_tokens.shape == (N_META, D_MODEL)
    lp = PAD_FRONT + N_META + seq
    rows = batch * lp

    meta = jnp.broadcast_to(meta_tokens[None].astype(x.dtype), (batch, N_META, d))
    h = jnp.concatenate([jnp.zeros((batch, PAD_FRONT, d), x.dtype), meta, x], axis=1).reshape(rows, d)
    hb = h.astype(BF16)

    pos = jnp.arange(lp, dtype=F32) - PAD_FRONT
    inv_freq = 1.0 / (ROPE_BASE ** jnp.linspace(0.0, 1.0, RET_DK // 2, dtype=F32))
    ang = pos[:, None] * inv_freq[None]
    cosf = jnp.concatenate([jnp.cos(ang), jnp.cos(ang)], axis=-1)
    sinf = jnp.concatenate([-jnp.sin(ang), jnp.sin(ang)], axis=-1)

    w_in = ab_w_in[0]
    w_main = jnp.concatenate([w_in[:, :5120], w_in[:, 5128:]], axis=1).astype(BF16)
    w_gate = jnp.pad(w_in[:, 5120:5128], ((0, 0), (0, LANES - 8))).astype(BF16)
    conv_w = jnp.concatenate([ab_conv_q[0], ab_conv_k[0], ab_conv_v[0]], axis=1)
    p0 = _proj_ab(hb, w_main, cosf, sinf, conv_w, lp)
    alog = _pad_lanes(jnp.concatenate([jnp.zeros((GDN_HEADS,), F32), ab_A_log[0]]))
    bias = _pad_lanes(jnp.concatenate([jnp.zeros((GDN_HEADS,), F32), ab_dt_bias[0]]))
    g0 = _gates(_gates_ab_kernel, hb, w_gate, alog, bias, lp, "gates_ab")
    y_ret = _retention(p0, ab_ret_gn_w[0].reshape(1, -1), batch, lp)
    y_gdn = _gdn(p0, g0, ab_gdn_norm_w[0].reshape(1, -1), batch, lp)
    h = _outproj_ln(y_ret, y_gdn, 0, 0, ab_w_out[0].astype(BF16), h,
                    ln1_w[0].reshape(1, -1), ln1_b[0].reshape(1, -1), lp)
    h, hb = _mlp_ln(h, mlp_w1[0].astype(BF16), mlp_w2[0].astype(BF16),
                    ln2_w[0].reshape(1, -1), ln2_b[0].reshape(1, -1), lp)

    w_in = c_w_in[0]
    w_dt = w_in[:, 5120:]
    w_gate = jnp.pad(jnp.concatenate([w_dt, w_dt], axis=1), ((0, 0), (0, LANES - 2 * SSD_HEADS))).astype(BF16)
    p1 = _proj_c(hb, w_in[:, :5120].astype(BF16), c_conv_w[0], c_conv_b[0].reshape(1, -1), lp)
    alog = _pad_lanes(jnp.concatenate([jnp.zeros((SSD_HEADS,), F32), c_A_log[0]]))
    bias = _pad_lanes(jnp.concatenate([c_dt_bias[0], c_dt_bias[0]]))
    g1 = _gates(_gates_c_kernel, hb, w_gate, alog, bias, lp, "gates_c")
    g1_row = jnp.swapaxes(g1[:, :2 * SSD_HEADS].reshape(batch, lp, 2 * SSD_HEADS), 1, 2)
    dskip = jnp.repeat(c_D[0].astype(F32), SSD_HEADDIM).reshape(1, -1)
    y_ssd = _ssd(p1, g1, g1_row, dskip, c_norm_w[0].reshape(1, -1), batch, lp)
    h = _outproj_ln(y_ssd, y_ssd, 0, 1, c_w_out[0].astype(BF16), h,
                    ln1_w[1].reshape(1, -1), ln1_b[1].reshape(1, -1), lp)
    h, _ = _mlp_ln(h, mlp_w1[1].astype(BF16), mlp_w2[1].astype(BF16),
                   ln2_w[1].reshape(1, -1), ln2_b[1].reshape(1, -1), lp)
    return h.reshape(batch, lp, d)[:, PAD_FRONT + N_META:]
```

```python
import functools

import jax
import jax.numpy as jnp
from jax import lax
from jax.experimental import pallas as pl
from jax.experimental.pallas import tpu as pltpu

F32 = jnp.float32
BF16 = jnp.bfloat16
HI = lax.Precision.HIGHEST

D_MODEL = 1024
DEPTH = 2
N_META = 16
CONV_K = 4
RET_HEADS = 4
RET_DK = 128
RET_DV = 256
ROPE_BASE = 10000.0
GDN_HEADS = 4
GDN_DK = 128
GDN_DV = 256
SSD_DINNER = 2048
SSD_HEADDIM = 64
SSD_HEADS = 32
SSD_GROUPS = 4
SSD_HPG = 8
SSD_DSTATE = 128
SSD_GN = 512
D_FF = 4096
DN_ALPHA = (2 * DEPTH) ** 0.25
LN_EPS = 1e-5
GN_EPS = 1e-5
RMS_EPS = 1e-6

LANES = 128
COL_TILE = 512
CHUNK = 128
GDN_CHUNK = 64
PAD_FRONT = CHUNK - N_META
HALO = 8
CONV_ROWS = 64
VMEM_LIMIT = 56 * 1024 * 1024


def _row_tile(rows_per_batch):
    for tm in (640, 512, 256, 128):
        if rows_per_batch % tm == 0:
            return tm
    raise ValueError(f"unsupported padded sequence length {rows_per_batch}")


def _params(sem):
    return pltpu.CompilerParams(dimension_semantics=sem, vmem_limit_bytes=VMEM_LIMIT)


def _softplus(x):
    return jnp.maximum(x, 0.0) + jnp.log1p(jnp.exp(-jnp.abs(x)))


def _silu(x):
    return x * jax.nn.sigmoid(x)


def _dot(a, b):
    return jnp.dot(a, b, preferred_element_type=F32)


def _dot_nt(a, b):
    return lax.dot_general(a, b, (((1,), (1,)), ((), ())), preferred_element_type=F32)


def _dot_hi(a, b):
    return jnp.dot(a, b, preferred_element_type=F32, precision=HI)


def _split(a):
    hi = a.astype(BF16)
    return hi, (a - hi.astype(F32)).astype(BF16)


def _dot_split(a, b):
    return _dot(a[0], b[0]) + (_dot(a[1], b[0]) + _dot(a[0], b[1]))


def _epilogue_plain(acc_ref, o_ref, *, tm):
    o_ref[...] = acc_ref[HALO:HALO + tm, :].astype(BF16)


def _epilogue_rope(acc_ref, o_ref, cos_ref, sin_ref, scale_ref, *, tm):
    for hd in range(COL_TILE // RET_DK):
        cols = slice(RET_DK * hd, RET_DK * (hd + 1))
        x = acc_ref[HALO:HALO + tm, cols]
        y = x * cos_ref[...] + pltpu.roll(x, RET_DK // 2, 1) * sin_ref[...]
        o_ref[:, cols] = (y * scale_ref[:, cols]).astype(BF16)


def _epilogue_conv(acc_ref, o_ref, cw_ref, cb_ref, scale_ref, *, tm, l2norm):
    for r in range(0, tm, CONV_ROWS):
        rows = slice(r, r + CONV_ROWS)
        for c in range(0, COL_TILE, LANES):
            cols = slice(c, c + LANES)
            w = [cw_ref[tap:tap + 1, cols] for tap in range(CONV_K)]
            xw = acc_ref[r:r + CONV_ROWS + HALO, cols]
            s1 = pltpu.roll(xw, 1, 0)
            y = w[3] * xw + w[2] * s1 + pltpu.roll(w[1] * xw + w[0] * s1, 2, 0)
            y = _silu(y[HALO:] + cb_ref[:, cols])
            if l2norm:
                inv = lax.rsqrt(jnp.sum(y * y, axis=-1, keepdims=True) + 1e-6)
                y = y * (inv * scale_ref[:, cols])
            o_ref[rows, cols] = y.astype(BF16)


def _proj_kernel(*refs, tm, tiles_per_batch, n_extra, epilogue):
    h_ref, w_ref = refs[0], refs[1]
    extra = refs[2:2 + n_extra]
    o_ref, acc_a, acc_b = refs[2 + n_extra:]
    i = pl.program_id(1)
    starts_batch = (i % tiles_per_batch) == 0

    @pl.when(i == 0)
    def _():
        acc_b[...] = jnp.zeros(acc_b.shape, F32)

    def step(cur, prev):
        cur[HALO:HALO + tm, :] = _dot(h_ref[...], w_ref[...])
        cur[0:HALO, :] = jnp.where(starts_batch, 0.0, prev[tm:tm + HALO, :])
        epilogue(prev, o_ref, *extra, tm=tm)

    @pl.when(i % 2 == 0)
    def _():
        step(acc_a, acc_b)

    @pl.when(i % 2 == 1)
    def _():
        step(acc_b, acc_a)


def _proj(hb, w, col_tiles, epilogue, extra, extra_specs, rows_per_batch, name):
    rows = hb.shape[0]
    tm = _row_tile(rows_per_batch)
    tpb = rows_per_batch // tm
    n_tiles = rows // tm
    col_fn, ncol = col_tiles
    prev = lambda i: jnp.maximum(i - 1, 0)
    return pl.pallas_call(
        functools.partial(_proj_kernel, tm=tm, tiles_per_batch=tpb, n_extra=len(extra), epilogue=epilogue),
        grid=(ncol, n_tiles + 1),
        in_specs=[
            pl.BlockSpec((tm, D_MODEL), lambda j, i: (jnp.minimum(i, n_tiles - 1), 0)),
            pl.BlockSpec((D_MODEL, COL_TILE), lambda j, i: (0, col_fn(j))),
        ] + [spec(tm, tpb, prev) for spec in extra_specs],
        out_specs=pl.BlockSpec((tm, COL_TILE), lambda j, i: (prev(i), j)),
        out_shape=jax.ShapeDtypeStruct((rows, ncol * COL_TILE), BF16),
        scratch_shapes=[pltpu.VMEM((tm + HALO, COL_TILE), F32), pltpu.VMEM((tm + HALO, COL_TILE), F32)],
        compiler_params=_params(("arbitrary", "arbitrary")),
        name=name,
    )(hb, w, *extra)


def _per_col(width=COL_TILE, rows=1):
    return lambda tm, tpb, prev: pl.BlockSpec((rows, width), lambda j, i: (0, j))


def _per_row_in_batch(width):
    return lambda tm, tpb, prev: pl.BlockSpec((tm, width), lambda j, i: (prev(i) % tpb, 0))


def _gates_ab_kernel(h_ref, w_ref, alog_ref, bias_ref, o_ref, *, tm, tiles_per_batch):
    acc = _dot(h_ref[...], w_ref[...])
    lane = lax.broadcasted_iota(jnp.int32, acc.shape, 1)
    decay = -jnp.exp(alog_ref[...]) * _softplus(acc + bias_ref[...])
    o_ref[...] = jnp.where(lane < GDN_HEADS, jax.nn.sigmoid(acc), decay)


def _gates_c_kernel(h_ref, w_ref, alog_ref, bias_ref, o_ref, *, tm, tiles_per_batch):
    acc = _dot(h_ref[...], w_ref[...])
    lane = lax.broadcasted_iota(jnp.int32, acc.shape, 1)
    row = (pl.program_id(0) % tiles_per_batch) * tm + lax.broadcasted_iota(jnp.int32, (tm, 1), 0)
    dt = jnp.where(row >= PAD_FRONT, _softplus(acc + bias_ref[...]), 0.0)
    o_ref[...] = jnp.where(lane < SSD_HEADS, dt, dt * -jnp.exp(alog_ref[...]))


def _gates(kernel_fn, hb, w, alog, bias, rows_per_batch, name):
    rows = hb.shape[0]
    tm = _row_tile(rows_per_batch)
    vec = pl.BlockSpec((1, LANES), lambda i: (0, 0))
    return pl.pallas_call(
        functools.partial(kernel_fn, tm=tm, tiles_per_batch=rows_per_batch // tm),
        grid=(rows // tm,),
        in_specs=[
            pl.BlockSpec((tm, D_MODEL), lambda i: (i, 0)),
            pl.BlockSpec((D_MODEL, LANES), lambda i: (0, 0)),
            vec, vec,
        ],
        out_specs=pl.BlockSpec((tm, LANES), lambda i: (i, 0)),
        out_shape=jax.ShapeDtypeStruct((rows, LANES), F32),
        compiler_params=_params(("arbitrary",)),
        name=name,
    )(hb, w, alog, bias)


def _retention_kernel(q_ref, k_ref, v_ref, g_ref, gnw_ref, o_ref, s_ref, *, batch):
    n = CHUNK

    @pl.when(pl.program_id(0) == 0)
    def _():
        s_ref[...] = jnp.zeros(s_ref.shape, F32)

    t_idx = lax.broadcasted_iota(jnp.int32, (n, n), 0)
    s_idx = lax.broadcasted_iota(jnp.int32, (n, n), 1)
    t_col = lax.broadcasted_iota(jnp.int32, (n, 1), 0).astype(F32)
    gap = (t_idx - s_idx).astype(F32)
    chains = [(b, hd) for b in range(batch) for hd in range(RET_HEADS)]
    cs = range(len(chains))
    log_gamma = [jnp.log1p(-jnp.exp2(jnp.full((1, 1), -5.0 - hd, F32))) for hd in range(RET_HEADS)]
    decay = [jnp.exp(jnp.where(t_idx >= s_idx, gap * lg, -jnp.inf)) for lg in log_gamma]
    e_in = [jnp.exp((t_col + 1.0) * lg) for lg in log_gamma]
    e_out = [jnp.exp((n - 1.0 - t_col) * lg) for lg in log_gamma]
    e_all = [jnp.exp(n * lg) for lg in log_gamma]
    q = [q_ref[b, :, RET_DK * hd:RET_DK * (hd + 1)] for b, hd in chains]
    k = [k_ref[b, :, RET_DK * hd:RET_DK * (hd + 1)] for b, hd in chains]
    v = [v_ref[b, :, RET_DV * hd:RET_DV * (hd + 1)] for b, hd in chains]
    states = [s_ref[c] for c in cs]
    att = [(_dot_nt(q[c], k[c]) * decay[chains[c][1]]).astype(BF16) for c in cs]
    q_in = [(q[c].astype(F32) * e_in[chains[c][1]]).astype(BF16) for c in cs]
    o = [_dot(att[c], v[c]) + _dot(q_in[c], states[c].astype(BF16)) for c in cs]
    k_out = [(k[c].astype(F32) * e_out[chains[c][1]]).T.astype(BF16) for c in cs]
    for c in cs:
        s_ref[c] = states[c] * e_all[chains[c][1]] + _dot(k_out[c], v[c])
    for c, (b, hd) in enumerate(chains):
        cols = slice(RET_DV * hd, RET_DV * (hd + 1))
        oc = o[c] - jnp.mean(o[c], axis=-1, keepdims=True)
        on = oc * lax.rsqrt(jnp.mean(oc * oc, axis=-1, keepdims=True) + GN_EPS)
        gate = _silu(g_ref[b, :, cols].astype(F32))
        o_ref[b, :, cols] = (on * gnw_ref[:, cols] * gate).astype(BF16)


def _retention(qk, vg, gn_w, batch, rows_per_batch):
    n = CHUNK
    nc = rows_per_batch // n
    qk = qk.reshape(batch, rows_per_batch, qk.shape[1])
    vg = vg.reshape(batch, rows_per_batch, vg.shape[1])
    out = pl.pallas_call(
        functools.partial(_retention_kernel, batch=batch),
        grid=(nc,),
        in_specs=[
            pl.BlockSpec((batch, n, 512), lambda c: (0, c, 0)),
            pl.BlockSpec((batch, n, 512), lambda c: (0, c, 1)),
            pl.BlockSpec((batch, n, 1024), lambda c: (0, c, 0)),
            pl.BlockSpec((batch, n, 1024), lambda c: (0, c, 1)),
            pl.BlockSpec((1, 1024), lambda c: (0, 0)),
        ],
        out_specs=pl.BlockSpec((batch, n, 1024), lambda c: (0, c, 0)),
        out_shape=jax.ShapeDtypeStruct((batch, rows_per_batch, 1024), BF16),
        scratch_shapes=[pltpu.VMEM((batch * RET_HEADS, RET_DK, RET_DV), F32)],
        compiler_params=_params(("arbitrary",)),
        name="retention",
    )(qk, qk, vg, vg, gn_w)
    return out.reshape(batch * rows_per_batch, 1024)


def _gdn_kernel(q_ref, k_ref, v_ref, g_ref, gcol_ref, grow_ref, nw_ref, o_ref, s_ref, *, batch):
    n = GDN_CHUNK

    @pl.when(pl.program_id(0) == 0)
    def _():
        s_ref[...] = jnp.zeros(s_ref.shape, F32)

    t_idx = lax.broadcasted_iota(jnp.int32, (n, n), 0)
    s_idx = lax.broadcasted_iota(jnp.int32, (n, n), 1)
    lower = (t_idx >= s_idx).astype(F32)
    upper = (t_idx <= s_idx).astype(F32)
    eye = (t_idx == s_idx).astype(F32)
    chains = [(b, hd) for b in range(batch) for hd in range(GDN_HEADS)]
    cs = range(len(chains))
    gcol = [gcol_ref[b] for b in range(batch)]
    gcum_col = [_dot_hi(lower, gcol[b]) for b in range(batch)]
    gcum_row = [_dot_hi(grow_ref[b, 0], upper) for b in range(batch)]
    states = [s_ref[c] for c in cs]
    q = [q_ref[b, :, GDN_DK * hd:GDN_DK * (hd + 1)].astype(F32) for b, hd in chains]
    k = [k_ref[b, :, GDN_DK * hd:GDN_DK * (hd + 1)].astype(F32) for b, hd in chains]
    v = [v_ref[b, :, GDN_DV * hd:GDN_DV * (hd + 1)].astype(F32) for b, hd in chains]
    beta = [gcol[b][:, hd:hd + 1] for b, hd in chains]
    gc = [gcum_col[b][:, GDN_HEADS + hd:GDN_HEADS + hd + 1] for b, hd in chains]
    gr = [gcum_row[b][GDN_HEADS + hd:GDN_HEADS + hd + 1, :] for b, hd in chains]
    g_last = [gc[c][n - 1:n, :] for c in cs]
    seg = [gc[c] - gr[c] for c in cs]
    dec_strict = [jnp.exp(jnp.where(t_idx > s_idx, seg[c], -jnp.inf)) for c in cs]
    dec_incl = [jnp.exp(jnp.where(t_idx >= s_idx, seg[c], -jnp.inf)) for c in cs]
    kb = [k[c] * beta[c] for c in cs]
    k16 = [k[c].astype(BF16) for c in cs]
    a = [_dot_nt(kb[c].astype(BF16), k16[c]) * dec_strict[c] for c in cs]
    inv = [eye - a[c] for c in cs]
    power = [_split(a[c]) for c in cs]
    span = 2
    while span < n:
        power = [_split(_dot_split(power[c], power[c])) for c in cs]
        inv = [inv[c] + _dot_split(_split(inv[c]), power[c]) for c in cs]
        span *= 2
    e_gc = [jnp.exp(gc[c]) for c in cs]
    rhs = [jnp.concatenate([v[c] * beta[c], kb[c] * e_gc[c]], axis=-1) for c in cs]
    u = [_dot_split(_split(inv[c]), _split(rhs[c])) for c in cs]
    s16 = [states[c].astype(BF16) for c in cs]
    v_new = [u[c][:, :GDN_DV] - _dot(u[c][:, GDN_DV:].astype(BF16), s16[c]) for c in cs]
    v_new16 = [v_new[c].astype(BF16) for c in cs]
    att = [_dot_nt(q[c].astype(BF16), k16[c]) * dec_incl[c] for c in cs]
    o = [_dot((q[c] * e_gc[c]).astype(BF16), s16[c]) + _dot(att[c].astype(BF16), v_new16[c]) for c in cs]
    k_out = [k[c] * jnp.exp(g_last[c] - gc[c]) for c in cs]
    for c in cs:
        s_ref[c] = states[c] * jnp.exp(g_last[c]) + _dot(k_out[c].T.astype(BF16), v_new16[c])
    for c, (b, hd) in enumerate(chains):
        on = o[c] * lax.rsqrt(jnp.mean(o[c] * o[c], axis=-1, keepdims=True) + RMS_EPS)
        gate = _silu(g_ref[b, :, GDN_DV * hd:GDN_DV * (hd + 1)].astype(F32))
        o_ref[b, :, GDN_DV * hd:GDN_DV * (hd + 1)] = (on * nw_ref[...] * gate).astype(BF16)


def _gdn(qk, v, plain, gates, norm_w, batch, rows_per_batch):
    n = GDN_CHUNK
    nc = rows_per_batch // n
    qk = qk.reshape(batch, rows_per_batch, qk.shape[1])
    v = v.reshape(batch, rows_per_batch, v.shape[1])
    plain = plain.reshape(batch, rows_per_batch, plain.shape[1])
    gates = gates.reshape(batch, rows_per_batch, LANES)
    gates_row = jnp.swapaxes(gates[:, :, :8].reshape(batch, nc, n, 8), 2, 3)
    out = pl.pallas_call(
        functools.partial(_gdn_kernel, batch=batch),
        grid=(nc,),
        in_specs=[
            pl.BlockSpec((batch, n, 512), lambda c: (0, c, 0)),
            pl.BlockSpec((batch, n, 512), lambda c: (0, c, 1)),
            pl.BlockSpec((batch, n, 1024), lambda c: (0, c, 0)),
            pl.BlockSpec((batch, n, 1024), lambda c: (0, c, 2)),
            pl.BlockSpec((batch, n, LANES), lambda c: (0, c, 0)),
            pl.BlockSpec((batch, 1, 8, n), lambda c: (0, c, 0, 0)),
            pl.BlockSpec((1, GDN_DV), lambda c: (0, 0)),
        ],
        out_specs=pl.BlockSpec((batch, n, 1024), lambda c: (0, c, 0)),
        out_shape=jax.ShapeDtypeStruct((batch, rows_per_batch, 1024), BF16),
        scratch_shapes=[pltpu.VMEM((batch * GDN_HEADS, GDN_DK, GDN_DV), F32)],
        compiler_params=_params(("arbitrary",)),
        name="gated_delta",
    )(qk, qk, v, plain, gates, gates_row, norm_w)
    return out.reshape(batch * rows_per_batch, 1024)


def _layer_norm(x, w, b):
    xc = x - jnp.mean(x, axis=-1, keepdims=True)
    return xc * lax.rsqrt(jnp.mean(xc * xc, axis=-1, keepdims=True) + LN_EPS) * w + b


def _outproj_ln_kernel(ya_ref, yb_ref, w_ref, h_ref, lw_ref, lb_ref, o_ref):
    half = ya_ref.shape[1]
    acc = _dot(ya_ref[...], w_ref[0:half, :]) + _dot(yb_ref[...], w_ref[half:2 * half, :])
    o_ref[...] = _layer_norm(DN_ALPHA * h_ref[...] + acc, lw_ref[...], lb_ref[...])


def _outproj_ln(ya, yb, col_a, col_b, w, h, lw, lb, rows_per_batch):
    rows = h.shape[0]
    tm = _row_tile(rows_per_batch)
    half = w.shape[0] // 2
    vec = pl.BlockSpec((1, D_MODEL), lambda i: (0, 0))
    return pl.pallas_call(
        _outproj_ln_kernel,
        grid=(rows // tm,),
        in_specs=[
            pl.BlockSpec((tm, half), lambda i: (i, col_a)),
            pl.BlockSpec((tm, half), lambda i: (i, col_b)),
            pl.BlockSpec((2 * half, D_MODEL), lambda i: (0, 0), pipeline_mode=pl.Buffered(1)),
            pl.BlockSpec((tm, D_MODEL), lambda i: (i, 0)),
            vec, vec,
        ],
        out_specs=pl.BlockSpec((tm, D_MODEL), lambda i: (i, 0)),
        out_shape=jax.ShapeDtypeStruct((rows, D_MODEL), F32),
        compiler_params=_params(("arbitrary",)),
        name="outproj_ln",
    )(ya, yb, w, h, lw, lb)


def _mlp_ln_kernel(h_ref, w1_ref, w2_ref, lw_ref, lb_ref, o_ref, ob_ref, *, tm, tiles_per_batch):
    h = h_ref[...]
    hb = h.astype(BF16)
    acc = jnp.zeros(h.shape, F32)
    step = 1024
    for f in range(0, D_FF, step):
        a = _dot(hb, w1_ref[:, f:f + step])
        a = jnp.square(jnp.maximum(a, 0.0)).astype(BF16)
        acc = acc + _dot(a, w2_ref[f:f + step, :])
    out = _layer_norm(DN_ALPHA * h + acc, lw_ref[...], lb_ref[...])
    o_ref[...] = out
    row = (pl.program_id(0) % tiles_per_batch) * tm + lax.broadcasted_iota(jnp.int32, (tm, 1), 0)
    ob_ref[...] = jnp.where(row >= PAD_FRONT, out, 0.0).astype(BF16)


def _mlp_ln(h, w1, w2, lw, lb, rows_per_batch):
    rows = h.shape[0]
    tm = _row_tile(rows_per_batch)
    vec = pl.BlockSpec((1, D_MODEL), lambda i: (0, 0))
    tile = pl.BlockSpec((tm, D_MODEL), lambda i: (i, 0))
    return pl.pallas_call(
        functools.partial(_mlp_ln_kernel, tm=tm, tiles_per_batch=rows_per_batch // tm),
        grid=(rows // tm,),
        in_specs=[
            tile,
            pl.BlockSpec((D_MODEL, D_FF), lambda i: (0, 0), pipeline_mode=pl.Buffered(1)),
            pl.BlockSpec((D_FF, D_MODEL), lambda i: (0, 0), pipeline_mode=pl.Buffered(1)),
            vec, vec,
        ],
        out_specs=[tile, tile],
        out_shape=[jax.ShapeDtypeStruct((rows, D_MODEL), F32), jax.ShapeDtypeStruct((rows, D_MODEL), BF16)],
        compiler_params=_params(("arbitrary",)),
        name="mlp_ln",
    )(h, w1, w2, lw, lb)


def _ssd_kernel(z_ref, x_ref, b_ref, c_ref, gcol_ref, grow_ref, dskip_ref, nw_ref, o_ref, s_ref):
    cidx = pl.program_id(1)
    n = CHUNK
    pair_w = 2 * SSD_HEADDIM
    pairs_per_group = SSD_HPG // 2
    group_w = SSD_HPG * SSD_HEADDIM

    @pl.when(cidx == 0)
    def _():
        s_ref[...] = jnp.zeros(s_ref.shape, F32)

    t_idx = lax.broadcasted_iota(jnp.int32, (n, n), 0)
    s_idx = lax.broadcasted_iota(jnp.int32, (n, n), 1)
    causal = t_idx >= s_idx
    lower = causal.astype(F32)
    upper = (t_idx <= s_idx).astype(F32)
    lane_v = lax.broadcasted_iota(jnp.int32, (n, pair_w), 1)
    lane_s = lax.broadcasted_iota(jnp.int32, (SSD_DSTATE, pair_w), 1)
    lane_1 = lax.broadcasted_iota(jnp.int32, (1, pair_w), 1)
    grow = grow_ref[0]
    gcum_col = _dot_hi(lower, gcol_ref[...])
    gcum_row = _dot_hi(grow, upper)
    for g in range(SSD_GROUPS):
        q = c_ref[:, SSD_DSTATE * g:SSD_DSTATE * (g + 1)]
        k = b_ref[:, SSD_DSTATE * g:SSD_DSTATE * (g + 1)]
        qf = q.astype(F32)
        kt = k.astype(F32).T
        cb = _dot_nt(q, k)
        ys = []
        for p in range(pairs_per_group):
            pair = g * pairs_per_group + p
            col = group_w * g + pair_w * p
            xv = x_ref[:, col:col + pair_w]
            state = s_ref[pair]
            o = jnp.zeros((n, pair_w), F32)
            upd = jnp.zeros((SSD_DSTATE, pair_w), F32)
            last = []
            for side in range(2):
                hd = 2 * pair + side
                gc = gcum_col[:, SSD_HEADS + hd:SSD_HEADS + hd + 1]
                gr = gcum_row[SSD_HEADS + hd:SSD_HEADS + hd + 1, :]
                dt_row = grow[hd:hd + 1, :]
                g_last = gr[:, n - 1:n]
                dec = jnp.exp(jnp.where(causal, gc - gr, -jnp.inf))
                att = cb * dec * dt_row
                q_in = qf * jnp.exp(gc)
                mine_v = (lane_v >= SSD_HEADDIM) if side else (lane_v < SSD_HEADDIM)
                mine_s = (lane_s >= SSD_HEADDIM) if side else (lane_s < SSD_HEADDIM)
                xm = jnp.where(mine_v, xv, jnp.zeros_like(xv))
                sm = jnp.where(mine_s, state, 0.0).astype(BF16)
                o = o + _dot(att.astype(BF16), xm) + _dot(q_in.astype(BF16), sm)
                k_out = kt * (jnp.exp(g_last - gr) * dt_row)
                upd = upd + _dot(k_out.astype(BF16), xm)
                last.append(jnp.exp(g_last))
            s_ref[pair] = state * jnp.where(lane_1 < SSD_HEADDIM, last[0], last[1]) + upd
            ys.append(o + xv.astype(F32) * dskip_ref[:, col:col + pair_w])
        y = jnp.concatenate(ys, axis=-1)
        y = y * _silu(z_ref[:, group_w * g:group_w * (g + 1)].astype(F32))
        y = y * lax.rsqrt(jnp.mean(y * y, axis=-1, keepdims=True) + RMS_EPS)
        o_ref[:, group_w * g:group_w * (g + 1)] = (y * nw_ref[:, group_w * g:group_w * (g + 1)]).astype(BF16)


def _ssd(z, xbc, gates, gates_row, dskip, norm_w, batch, rows_per_batch):
    n = CHUNK
    nc = rows_per_batch // n
    row = lambda b, c: b * nc + c
    return pl.pallas_call(
        _ssd_kernel,
        grid=(batch, nc),
        in_specs=[
            pl.BlockSpec((n, SSD_DINNER), lambda b, c: (row(b, c), 0)),
            pl.BlockSpec((n, SSD_DINNER), lambda b, c: (row(b, c), 0)),
            pl.BlockSpec((n, SSD_GN), lambda b, c: (row(b, c), 4)),
            pl.BlockSpec((n, SSD_GN), lambda b, c: (row(b, c), 5)),
            pl.BlockSpec((n, LANES), lambda b, c: (row(b, c), 0)),
            pl.BlockSpec((1, 2 * SSD_HEADS, n), lambda b, c: (b, 0, c)),
            pl.BlockSpec((1, SSD_DINNER), lambda b, c: (0, 0)),
            pl.BlockSpec((1, SSD_DINNER), lambda b, c: (0, 0)),
        ],
        out_specs=pl.BlockSpec((n, SSD_DINNER), lambda b, c: (row(b, c), 0)),
        out_shape=jax.ShapeDtypeStruct((z.shape[0], SSD_DINNER), BF16),
        scratch_shapes=[pltpu.VMEM((SSD_HEADS // 2, SSD_DSTATE, 2 * SSD_HEADDIM), F32)],
        compiler_params=_params(("arbitrary", "arbitrary")),
        name="ssd",
    )(z, xbc, xbc, xbc, gates, gates_row, dskip, norm_w)


def _pad_lanes(v):
    v = v.reshape(1, -1).astype(F32)
    return jnp.pad(v, ((0, 0), (0, LANES - v.shape[1])))


def kernel(x, meta_tokens, ab_w_in, ab_ret_gn_w, ab_conv_q, ab_conv_k, ab_conv_v, ab_A_log, ab_dt_bias, ab_gdn_norm_w, ab_w_out, c_w_in, c_conv_w, c_conv_b, c_A_log, c_dt_bias, c_D, c_norm_w, c_w_out, mlp_w1, mlp_w2, ln1_w, ln1_b, ln2_w, ln2_b):
    batch, seq, d = x.shape
    assert d == D_MODEL and meta_tokens.shape == (N_META, D_MODEL)
    lp = PAD_FRONT + N_META + seq
    rows = batch * lp

    meta = jnp.broadcast_to(meta_tokens[None].astype(x.dtype), (batch, N_META, d))
    h = jnp.concatenate([jnp.zeros((batch, PAD_FRONT, d), x.dtype), meta, x], axis=1).reshape(rows, d)
    hb = h.astype(BF16)

    pos = jnp.arange(lp, dtype=F32) - PAD_FRONT
    inv_freq = 1.0 / (ROPE_BASE ** jnp.linspace(0.0, 1.0, RET_DK // 2, dtype=F32))
    ang = pos[:, None] * inv_freq[None]
    cosf = jnp.concatenate([jnp.cos(ang), jnp.cos(ang)], axis=-1)
    sinf = jnp.concatenate([-jnp.sin(ang), jnp.sin(ang)], axis=-1)

    w_in = ab_w_in[0]
    w_main = jnp.concatenate([w_in[:, :5120], w_in[:, 5128:]], axis=1).astype(BF16)
    w_gate = jnp.pad(w_in[:, 5120:5128], ((0, 0), (0, LANES - 8))).astype(BF16)
    ones = jnp.ones((1, COL_TILE), F32)
    zeros = jnp.zeros((1, 2 * COL_TILE), F32)
    p_qk = _proj(hb, w_main, (lambda j: j, 2), _epilogue_rope,
                 [cosf, sinf, jnp.concatenate([ones * RET_DK ** -0.5, ones], axis=1)],
                 [_per_row_in_batch(LANES), _per_row_in_batch(LANES), _per_col()], lp, "proj_ret_qk")
    p_plain = _proj(hb, w_main, (lambda j: jnp.where(j < 4, j + 2, j + 6), 6), _epilogue_plain,
                    [], [], lp, "proj_ab_plain")
    g_qk = _proj(hb, w_main, (lambda j: j + 6, 2), functools.partial(_epilogue_conv, l2norm=True),
                 [jnp.concatenate([ab_conv_q[0], ab_conv_k[0]], axis=1), zeros,
                  jnp.concatenate([ones * GDN_DK ** -0.5, ones], axis=1)],
                 [_per_col(rows=CONV_K), _per_col(), _per_col()], lp, "proj_gdn_qk")
    g_v = _proj(hb, w_main, (lambda j: j + 8, 2), functools.partial(_epilogue_conv, l2norm=False),
                [ab_conv_v[0], zeros, zeros], [_per_col(rows=CONV_K), _per_col(), _per_col()], lp, "proj_gdn_v")
    alog = _pad_lanes(jnp.concatenate([jnp.zeros((GDN_HEADS,), F32), ab_A_log[0]]))
    bias = _pad_lanes(jnp.concatenate([jnp.zeros((GDN_HEADS,), F32), ab_dt_bias[0]]))
    g0 = _gates(_gates_ab_kernel, hb, w_gate, alog, bias, lp, "gates_ab")
    y_ret = _retention(p_qk, p_plain, ab_ret_gn_w[0].reshape(1, -1), batch, lp)
    y_gdn = _gdn(g_qk, g_v, p_plain, g0, ab_gdn_norm_w[0].reshape(1, -1), batch, lp)
    h = _outproj_ln(y_ret, y_gdn, 0, 0, ab_w_out[0].astype(BF16), h,
                    ln1_w[0].reshape(1, -1), ln1_b[0].reshape(1, -1), lp)
    h, hb = _mlp_ln(h, mlp_w1[0].astype(BF16), mlp_w2[0].astype(BF16),
                    ln2_w[0].reshape(1, -1), ln2_b[0].reshape(1, -1), lp)

    w_in = c_w_in[0]
    w_main = w_in[:, :5120].astype(BF16)
    w_dt = w_in[:, 5120:]
    w_gate = jnp.pad(jnp.concatenate([w_dt, w_dt], axis=1), ((0, 0), (0, LANES - 2 * SSD_HEADS))).astype(BF16)
    p_z = _proj(hb, w_main, (lambda j: j, 4), _epilogue_plain, [], [], lp, "proj_ssd_z")
    p_xbc = _proj(hb, w_main, (lambda j: j + 4, 6), functools.partial(_epilogue_conv, l2norm=False),
                  [c_conv_w[0], c_conv_b[0].reshape(1, -1), jnp.zeros((1, 6 * COL_TILE), F32)],
                  [_per_col(rows=CONV_K), _per_col(), _per_col()], lp, "proj_ssd_xbc")
    alog = _pad_lanes(jnp.concatenate([jnp.zeros((SSD_HEADS,), F32), c_A_log[0]]))
    bias = _pad_lanes(jnp.concatenate([c_dt_bias[0], c_dt_bias[0]]))
    g1 = _gates(_gates_c_kernel, hb, w_gate, alog, bias, lp, "gates_c")
    g1_row = jnp.swapaxes(g1[:, :2 * SSD_HEADS].reshape(batch, lp, 2 * SSD_HEADS), 1, 2)
    dskip = jnp.repeat(c_D[0].astype(F32), SSD_HEADDIM).reshape(1, -1)
    y_ssd = _ssd(p_z, p_xbc, g1, g1_row, dskip, c_norm_w[0].reshape(1, -1), batch, lp)
    h = _outproj_ln(y_ssd, y_ssd, 0, 1, c_w_out[0].astype(BF16), h,
                    ln1_w[1].reshape(1, -1), ln1_b[1].reshape(1, -1), lp)
    h, _ = _mlp_ln(h, mlp_w1[1].astype(BF16), mlp_w2[1].astype(BF16),
                   ln2_w[1].reshape(1, -1), ln2_b[1].reshape(1, -1), lp)
    return h.reshape(batch, lp, d)[:, PAD_FRONT + N_META:]
```

```python
import functools

import jax
import jax.numpy as jnp
from jax import lax
from jax.experimental import pallas as pl
from jax.experimental.pallas import tpu as pltpu

F32 = jnp.float32
BF16 = jnp.bfloat16
HI = lax.Precision.HIGHEST

D_MODEL = 1024
DEPTH = 2
N_META = 16
CONV_K = 4
RET_HEADS = 4
RET_DK = 128
RET_DV = 256
ROPE_BASE = 10000.0
GDN_HEADS = 4
GDN_DK = 128
GDN_DV = 256
SSD_DINNER = 2048
SSD_HEADDIM = 64
SSD_HEADS = 32
SSD_GROUPS = 4
SSD_HPG = 8
SSD_DSTATE = 128
SSD_GN = 512
D_FF = 4096
DN_ALPHA = (2 * DEPTH) ** 0.25
LN_EPS = 1e-5
GN_EPS = 1e-5
RMS_EPS = 1e-6

LANES = 128
COL_TILE = 1024
CHUNK = 128
GDN_CHUNK = 64
PAD_FRONT = CHUNK - N_META
HALO = 8
CONV_ROWS = 64
VMEM_LIMIT = 56 * 1024 * 1024


def _row_tile(rows_per_batch):
    for tm in (640, 512, 256, 128):
        if rows_per_batch % tm == 0:
            return tm
    raise ValueError(f"unsupported padded sequence length {rows_per_batch}")


def _params(sem):
    return pltpu.CompilerParams(dimension_semantics=sem, vmem_limit_bytes=VMEM_LIMIT)


def _softplus(x):
    return jnp.maximum(x, 0.0) + jnp.log1p(jnp.exp(-jnp.abs(x)))


def _silu(x):
    return x * jax.nn.sigmoid(x)


def _dot(a, b):
    return jnp.dot(a, b, preferred_element_type=F32)


def _dot_nt(a, b):
    return lax.dot_general(a, b, (((1,), (1,)), ((), ())), preferred_element_type=F32)


def _dot_hi(a, b):
    return jnp.dot(a, b, preferred_element_type=F32, precision=HI)


def _split(a):
    hi = a.astype(BF16)
    return hi, (a - hi.astype(F32)).astype(BF16)


def _dot_split(a, b):
    return _dot(a[0], b[0]) + (_dot(a[1], b[0]) + _dot(a[0], b[1]))


def _epilogue_plain(acc_ref, o_ref, *, tm):
    o_ref[...] = acc_ref[HALO:HALO + tm, :].astype(BF16)


def _epilogue_rope(acc_ref, o_ref, cos_ref, sin_ref, scale_ref, *, tm):
    for hd in range(COL_TILE // RET_DK):
        cols = slice(RET_DK * hd, RET_DK * (hd + 1))
        x = acc_ref[HALO:HALO + tm, cols]
        y = x * cos_ref[...] + pltpu.roll(x, RET_DK // 2, 1) * sin_ref[...]
        o_ref[:, cols] = (y * scale_ref[:, cols]).astype(BF16)


def _epilogue_conv(acc_ref, o_ref, cw_ref, cb_ref, scale_ref, *, tm, l2norm):
    for r in range(0, tm, CONV_ROWS):
        rows = slice(r, r + CONV_ROWS)
        for c in range(0, COL_TILE, LANES):
            cols = slice(c, c + LANES)
            w = [cw_ref[tap:tap + 1, cols] for tap in range(CONV_K)]
            xw = acc_ref[r:r + CONV_ROWS + HALO, cols]
            s1 = pltpu.roll(xw, 1, 0)
            y = w[3] * xw + w[2] * s1 + pltpu.roll(w[1] * xw + w[0] * s1, 2, 0)
            y = _silu(y[HALO:] + cb_ref[:, cols])
            if l2norm:
                inv = lax.rsqrt(jnp.sum(y * y, axis=-1, keepdims=True) + 1e-6)
                y = y * (inv * scale_ref[:, cols])
            o_ref[rows, cols] = y.astype(BF16)


def _proj_kernel(*refs, tm, tiles_per_batch, n_extra, epilogue):
    h_ref, w_ref = refs[0], refs[1]
    extra = refs[2:2 + n_extra]
    o_ref, acc_a, acc_b = refs[2 + n_extra:]
    i = pl.program_id(1)
    starts_batch = (i % tiles_per_batch) == 0

    @pl.when(i == 0)
    def _():
        acc_b[...] = jnp.zeros(acc_b.shape, F32)

    def step(cur, prev):
        cur[HALO:HALO + tm, :] = _dot(h_ref[...], w_ref[...])
        cur[0:HALO, :] = jnp.where(starts_batch, 0.0, prev[tm:tm + HALO, :])
        epilogue(prev, o_ref, *extra, tm=tm)

    @pl.when(i % 2 == 0)
    def _():
        step(acc_a, acc_b)

    @pl.when(i % 2 == 1)
    def _():
        step(acc_b, acc_a)


def _proj(hb, w, col_tiles, epilogue, extra, extra_specs, rows_per_batch, name):
    rows = hb.shape[0]
    tm = _row_tile(rows_per_batch)
    tpb = rows_per_batch // tm
    n_tiles = rows // tm
    col_fn, ncol = col_tiles
    prev = lambda i: jnp.maximum(i - 1, 0)
    return pl.pallas_call(
        functools.partial(_proj_kernel, tm=tm, tiles_per_batch=tpb, n_extra=len(extra), epilogue=epilogue),
        grid=(ncol, n_tiles + 1),
        in_specs=[
            pl.BlockSpec((tm, D_MODEL), lambda j, i: (jnp.minimum(i, n_tiles - 1), 0)),
            pl.BlockSpec((D_MODEL, COL_TILE), lambda j, i: (0, col_fn(j))),
        ] + [spec(tm, tpb, prev) for spec in extra_specs],
        out_specs=pl.BlockSpec((tm, COL_TILE), lambda j, i: (prev(i), j)),
        out_shape=jax.ShapeDtypeStruct((rows, ncol * COL_TILE), BF16),
        scratch_shapes=[pltpu.VMEM((tm + HALO, COL_TILE), F32), pltpu.VMEM((tm + HALO, COL_TILE), F32)],
        compiler_params=_params(("arbitrary", "arbitrary")),
        name=name,
    )(hb, w, *extra)


def _per_col(width=COL_TILE, rows=1):
    return lambda tm, tpb, prev: pl.BlockSpec((rows, width), lambda j, i: (0, j))


def _per_row_in_batch(width):
    return lambda tm, tpb, prev: pl.BlockSpec((tm, width), lambda j, i: (prev(i) % tpb, 0))


def _gates_ab_kernel(h_ref, w_ref, alog_ref, bias_ref, o_ref, *, tm, tiles_per_batch):
    acc = _dot(h_ref[...], w_ref[...])
    lane = lax.broadcasted_iota(jnp.int32, acc.shape, 1)
    decay = -jnp.exp(alog_ref[...]) * _softplus(acc + bias_ref[...])
    o_ref[...] = jnp.where(lane < GDN_HEADS, jax.nn.sigmoid(acc), decay)


def _gates_c_kernel(h_ref, w_ref, alog_ref, bias_ref, o_ref, *, tm, tiles_per_batch):
    acc = _dot(h_ref[...], w_ref[...])
    lane = lax.broadcasted_iota(jnp.int32, acc.shape, 1)
    row = (pl.program_id(0) % tiles_per_batch) * tm + lax.broadcasted_iota(jnp.int32, (tm, 1), 0)
    dt = jnp.where(row >= PAD_FRONT, _softplus(acc + bias_ref[...]), 0.0)
    o_ref[...] = jnp.where(lane < SSD_HEADS, dt, dt * -jnp.exp(alog_ref[...]))


def _gates(kernel_fn, hb, w, alog, bias, rows_per_batch, name):
    rows = hb.shape[0]
    tm = _row_tile(rows_per_batch)
    vec = pl.BlockSpec((1, LANES), lambda i: (0, 0))
    return pl.pallas_call(
        functools.partial(kernel_fn, tm=tm, tiles_per_batch=rows_per_batch // tm),
        grid=(rows // tm,),
        in_specs=[
            pl.BlockSpec((tm, D_MODEL), lambda i: (i, 0)),
            pl.BlockSpec((D_MODEL, LANES), lambda i: (0, 0)),
            vec, vec,
        ],
        out_specs=pl.BlockSpec((tm, LANES), lambda i: (i, 0)),
        out_shape=jax.ShapeDtypeStruct((rows, LANES), F32),
        compiler_params=_params(("arbitrary",)),
        name=name,
    )(hb, w, alog, bias)


def _retention_kernel(q_ref, k_ref, v_ref, g_ref, gnw_ref, o_ref, s_ref, *, batch):
    n = CHUNK

    @pl.when(pl.program_id(0) == 0)
    def _():
        s_ref[...] = jnp.zeros(s_ref.shape, F32)

    t_idx = lax.broadcasted_iota(jnp.int32, (n, n), 0)
    s_idx = lax.broadcasted_iota(jnp.int32, (n, n), 1)
    t_col = lax.broadcasted_iota(jnp.int32, (n, 1), 0).astype(F32)
    gap = (t_idx - s_idx).astype(F32)
    chains = [(b, hd) for b in range(batch) for hd in range(RET_HEADS)]
    cs = range(len(chains))
    log_gamma = [jnp.log1p(-jnp.exp2(jnp.full((1, 1), -5.0 - hd, F32))) for hd in range(RET_HEADS)]
    decay = [jnp.exp(jnp.where(t_idx >= s_idx, gap * lg, -jnp.inf)) for lg in log_gamma]
    e_in = [jnp.exp((t_col + 1.0) * lg) for lg in log_gamma]
    e_out = [jnp.exp((n - 1.0 - t_col) * lg) for lg in log_gamma]
    e_all = [jnp.exp(n * lg) for lg in log_gamma]
    q = [q_ref[b, :, RET_DK * hd:RET_DK * (hd + 1)] for b, hd in chains]
    k = [k_ref[b, :, RET_DK * hd:RET_DK * (hd + 1)] for b, hd in chains]
    v = [v_ref[b, :, RET_DV * hd:RET_DV * (hd + 1)] for b, hd in chains]
    states = [s_ref[c] for c in cs]
    att = [(_dot_nt(q[c], k[c]) * decay[chains[c][1]]).astype(BF16) for c in cs]
    q_in = [(q[c].astype(F32) * e_in[chains[c][1]]).astype(BF16) for c in cs]
    o = [_dot(att[c], v[c]) + _dot(q_in[c], states[c].astype(BF16)) for c in cs]
    k_out = [(k[c].astype(F32) * e_out[chains[c][1]]).T.astype(BF16) for c in cs]
    for c in cs:
        s_ref[c] = states[c] * e_all[chains[c][1]] + _dot(k_out[c], v[c])
    for c, (b, hd) in enumerate(chains):
        cols = slice(RET_DV * hd, RET_DV * (hd + 1))
        oc = o[c] - jnp.mean(o[c], axis=-1, keepdims=True)
        on = oc * lax.rsqrt(jnp.mean(oc * oc, axis=-1, keepdims=True) + GN_EPS)
        gate = _silu(g_ref[b, :, cols].astype(F32))
        o_ref[b, :, cols] = (on * gnw_ref[:, cols] * gate).astype(BF16)


def _retention(qk, vg, gn_w, batch, rows_per_batch):
    n = CHUNK
    nc = rows_per_batch // n
    qk = qk.reshape(batch, rows_per_batch, qk.shape[1])
    vg = vg.reshape(batch, rows_per_batch, vg.shape[1])
    out = pl.pallas_call(
        functools.partial(_retention_kernel, batch=batch),
        grid=(nc,),
        in_specs=[
            pl.BlockSpec((batch, n, 512), lambda c: (0, c, 0)),
            pl.BlockSpec((batch, n, 512), lambda c: (0, c, 1)),
            pl.BlockSpec((batch, n, 1024), lambda c: (0, c, 0)),
            pl.BlockSpec((batch, n, 1024), lambda c: (0, c, 1)),
            pl.BlockSpec((1, 1024), lambda c: (0, 0)),
        ],
        out_specs=pl.BlockSpec((batch, n, 1024), lambda c: (0, c, 0)),
        out_shape=jax.ShapeDtypeStruct((batch, rows_per_batch, 1024), BF16),
        scratch_shapes=[pltpu.VMEM((batch * RET_HEADS, RET_DK, RET_DV), F32)],
        compiler_params=_params(("arbitrary",)),
        name="retention",
    )(qk, qk, vg, vg, gn_w)
    return out.reshape(batch * rows_per_batch, 1024)


def _gdn_kernel(q_ref, k_ref, v_ref, g_ref, gcol_ref, grow_ref, nw_ref, o_ref, s_ref, *, batch):
    n = GDN_CHUNK

    @pl.when(pl.program_id(0) == 0)
    def _():
        s_ref[...] = jnp.zeros(s_ref.shape, F32)

    t_idx = lax.broadcasted_iota(jnp.int32, (n, n), 0)
    s_idx = lax.broadcasted_iota(jnp.int32, (n, n), 1)
    lower = (t_idx >= s_idx).astype(F32)
    upper = (t_idx <= s_idx).astype(F32)
    eye = (t_idx == s_idx).astype(F32)
    chains = [(b, hd) for b in range(batch) for hd in range(GDN_HEADS)]
    cs = range(len(chains))
    gcol = [gcol_ref[b] for b in range(batch)]
    gcum_col = [_dot_hi(lower, gcol[b]) for b in range(batch)]
    gcum_row = [_dot_hi(grow_ref[b, 0], upper) for b in range(batch)]
    states = [s_ref[c] for c in cs]
    q = [q_ref[b, :, GDN_DK * hd:GDN_DK * (hd + 1)].astype(F32) for b, hd in chains]
    k = [k_ref[b, :, GDN_DK * hd:GDN_DK * (hd + 1)].astype(F32) for b, hd in chains]
    v = [v_ref[b, :, GDN_DV * hd:GDN_DV * (hd + 1)].astype(F32) for b, hd in chains]
    beta = [gcol[b][:, hd:hd + 1] for b, hd in chains]
    gc = [gcum_col[b][:, GDN_HEADS + hd:GDN_HEADS + hd + 1] for b, hd in chains]
    gr = [gcum_row[b][GDN_HEADS + hd:GDN_HEADS + hd + 1, :] for b, hd in chains]
    g_last = [gc[c][n - 1:n, :] for c in cs]
    seg = [gc[c] - gr[c] for c in cs]
    dec_strict = [jnp.exp(jnp.where(t_idx > s_idx, seg[c], -jnp.inf)) for c in cs]
    dec_incl = [jnp.exp(jnp.where(t_idx >= s_idx, seg[c], -jnp.inf)) for c in cs]
    kb = [k[c] * beta[c] for c in cs]
    k16 = [k[c].astype(BF16) for c in cs]
    a = [_dot_nt(kb[c].astype(BF16), k16[c]) * dec_strict[c] for c in cs]
    inv = [eye - a[c] for c in cs]
    power = [_split(a[c]) for c in cs]
    span = 2
    while span < n:
        power = [_split(_dot_split(power[c], power[c])) for c in cs]
        inv = [inv[c] + _dot_split(_split(inv[c]), power[c]) for c in cs]
        span *= 2
    e_gc = [jnp.exp(gc[c]) for c in cs]
    rhs = [jnp.concatenate([v[c] * beta[c], kb[c] * e_gc[c]], axis=-1) for c in cs]
    u = [_dot_split(_split(inv[c]), _split(rhs[c])) for c in cs]
    s16 = [states[c].astype(BF16) for c in cs]
    v_new = [u[c][:, :GDN_DV] - _dot(u[c][:, GDN_DV:].astype(BF16), s16[c]) for c in cs]
    v_new16 = [v_new[c].astype(BF16) for c in cs]
    att = [_dot_nt(q[c].astype(BF16), k16[c]) * dec_incl[c] for c in cs]
    o = [_dot((q[c] * e_gc[c]).astype(BF16), s16[c]) + _dot(att[c].astype(BF16), v_new16[c]) for c in cs]
    k_out = [k[c] * jnp.exp(g_last[c] - gc[c]) for c in cs]
    for c in cs:
        s_ref[c] = states[c] * jnp.exp(g_last[c]) + _dot(k_out[c].T.astype(BF16), v_new16[c])
    for c, (b, hd) in enumerate(chains):
        on = o[c] * lax.rsqrt(jnp.mean(o[c] * o[c], axis=-1, keepdims=True) + RMS_EPS)
        gate = _silu(g_ref[b, :, GDN_DV * hd:GDN_DV * (hd + 1)].astype(F32))
        o_ref[b, :, GDN_DV * hd:GDN_DV * (hd + 1)] = (on * nw_ref[...] * gate).astype(BF16)


def _gdn(qk, v, plain, gates, norm_w, batch, rows_per_batch):
    n = GDN_CHUNK
    nc = rows_per_batch // n
    qk = qk.reshape(batch, rows_per_batch, qk.shape[1])
    v = v.reshape(batch, rows_per_batch, v.shape[1])
    plain = plain.reshape(batch, rows_per_batch, plain.shape[1])
    gates = gates.reshape(batch, rows_per_batch, LANES)
    gates_row = jnp.swapaxes(gates[:, :, :8].reshape(batch, nc, n, 8), 2, 3)
    out = pl.pallas_call(
        functools.partial(_gdn_kernel, batch=batch),
        grid=(nc,),
        in_specs=[
            pl.BlockSpec((batch, n, 512), lambda c: (0, c, 0)),
            pl.BlockSpec((batch, n, 512), lambda c: (0, c, 1)),
            pl.BlockSpec((batch, n, 1024), lambda c: (0, c, 0)),
            pl.BlockSpec((batch, n, 1024), lambda c: (0, c, 2)),
            pl.BlockSpec((batch, n, LANES), lambda c: (0, c, 0)),
            pl.BlockSpec((batch, 1, 8, n), lambda c: (0, c, 0, 0)),
            pl.BlockSpec((1, GDN_DV), lambda c: (0, 0)),
        ],
        out_specs=pl.BlockSpec((batch, n, 1024), lambda c: (0, c, 0)),
        out_shape=jax.ShapeDtypeStruct((batch, rows_per_batch, 1024), BF16),
        scratch_shapes=[pltpu.VMEM((batch * GDN_HEADS, GDN_DK, GDN_DV), F32)],
        compiler_params=_params(("arbitrary",)),
        name="gated_delta",
    )(qk, qk, v, plain, gates, gates_row, norm_w)
    return out.reshape(batch * rows_per_batch, 1024)


def _layer_norm(x, w, b):
    xc = x - jnp.mean(x, axis=-1, keepdims=True)
    return xc * lax.rsqrt(jnp.mean(xc * xc, axis=-1, keepdims=True) + LN_EPS) * w + b


def _outproj_ln_kernel(ya_ref, yb_ref, w_ref, h_ref, lw_ref, lb_ref, o_ref):
    half = ya_ref.shape[1]
    acc = _dot(ya_ref[...], w_ref[0:half, :]) + _dot(yb_ref[...], w_ref[half:2 * half, :])
    o_ref[...] = _layer_norm(DN_ALPHA * h_ref[...] + acc, lw_ref[...], lb_ref[...])


def _outproj_ln(ya, yb, col_a, col_b, w, h, lw, lb, rows_per_batch):
    rows = h.shape[0]
    tm = _row_tile(rows_per_batch)
    half = w.shape[0] // 2
    vec = pl.BlockSpec((1, D_MODEL), lambda i: (0, 0))
    return pl.pallas_call(
        _outproj_ln_kernel,
        grid=(rows // tm,),
        in_specs=[
            pl.BlockSpec((tm, half), lambda i: (i, col_a)),
            pl.BlockSpec((tm, half), lambda i: (i, col_b)),
            pl.BlockSpec((2 * half, D_MODEL), lambda i: (0, 0), pipeline_mode=pl.Buffered(1)),
            pl.BlockSpec((tm, D_MODEL), lambda i: (i, 0)),
            vec, vec,
        ],
        out_specs=pl.BlockSpec((tm, D_MODEL), lambda i: (i, 0)),
        out_shape=jax.ShapeDtypeStruct((rows, D_MODEL), F32),
        compiler_params=_params(("arbitrary",)),
        name="outproj_ln",
    )(ya, yb, w, h, lw, lb)


def _mlp_ln_kernel(h_ref, w1_ref, w2_ref, lw_ref, lb_ref, o_ref, ob_ref, *, tm, tiles_per_batch):
    h = h_ref[...]
    hb = h.astype(BF16)
    acc = jnp.zeros(h.shape, F32)
    step = 1024
    for f in range(0, D_FF, step):
        a = _dot(hb, w1_ref[:, f:f + step])
        a = jnp.square(jnp.maximum(a, 0.0)).astype(BF16)
        acc = acc + _dot(a, w2_ref[f:f + step, :])
    out = _layer_norm(DN_ALPHA * h + acc, lw_ref[...], lb_ref[...])
    o_ref[...] = out
    row = (pl.program_id(0) % tiles_per_batch) * tm + lax.broadcasted_iota(jnp.int32, (tm, 1), 0)
    ob_ref[...] = jnp.where(row >= PAD_FRONT, out, 0.0).astype(BF16)


def _mlp_ln(h, w1, w2, lw, lb, rows_per_batch):
    rows = h.shape[0]
    tm = _row_tile(rows_per_batch)
    vec = pl.BlockSpec((1, D_MODEL), lambda i: (0, 0))
    tile = pl.BlockSpec((tm, D_MODEL), lambda i: (i, 0))
    return pl.pallas_call(
        functools.partial(_mlp_ln_kernel, tm=tm, tiles_per_batch=rows_per_batch // tm),
        grid=(rows // tm,),
        in_specs=[
            tile,
            pl.BlockSpec((D_MODEL, D_FF), lambda i: (0, 0), pipeline_mode=pl.Buffered(1)),
            pl.BlockSpec((D_FF, D_MODEL), lambda i: (0, 0), pipeline_mode=pl.Buffered(1)),
            vec, vec,
        ],
        out_specs=[tile, tile],
        out_shape=[jax.ShapeDtypeStruct((rows, D_MODEL), F32), jax.ShapeDtypeStruct((rows, D_MODEL), BF16)],
        compiler_params=_params(("arbitrary",)),
        name="mlp_ln",
    )(h, w1, w2, lw, lb)


def _ssd_kernel(z_ref, x_ref, b_ref, c_ref, gcol_ref, grow_ref, dskip_ref, nw_ref, o_ref, s_ref):
    cidx = pl.program_id(1)
    n = CHUNK
    pair_w = 2 * SSD_HEADDIM
    pairs_per_group = SSD_HPG // 2
    group_w = SSD_HPG * SSD_HEADDIM

    @pl.when(cidx == 0)
    def _():
        s_ref[...] = jnp.zeros(s_ref.shape, F32)

    t_idx = lax.broadcasted_iota(jnp.int32, (n, n), 0)
    s_idx = lax.broadcasted_iota(jnp.int32, (n, n), 1)
    causal = t_idx >= s_idx
    lower = causal.astype(F32)
    upper = (t_idx <= s_idx).astype(F32)
    lane_v = lax.broadcasted_iota(jnp.int32, (n, pair_w), 1)
    lane_s = lax.broadcasted_iota(jnp.int32, (SSD_DSTATE, pair_w), 1)
    lane_1 = lax.broadcasted_iota(jnp.int32, (1, pair_w), 1)
    grow = grow_ref[0]
    gcum_col = _dot_hi(lower, gcol_ref[...])
    gcum_row = _dot_hi(grow, upper)
    for g in range(SSD_GROUPS):
        q = c_ref[:, SSD_DSTATE * g:SSD_DSTATE * (g + 1)]
        k = b_ref[:, SSD_DSTATE * g:SSD_DSTATE * (g + 1)]
        qf = q.astype(F32)
        kt = k.astype(F32).T
        cb = _dot_nt(q, k)
        ys = []
        for p in range(pairs_per_group):
            pair = g * pairs_per_group + p
            col = group_w * g + pair_w * p
            xv = x_ref[:, col:col + pair_w]
            state = s_ref[pair]
            o = jnp.zeros((n, pair_w), F32)
            upd = jnp.zeros((SSD_DSTATE, pair_w), F32)
            last = []
            for side in range(2):
                hd = 2 * pair + side
                gc = gcum_col[:, SSD_HEADS + hd:SSD_HEADS + hd + 1]
                gr = gcum_row[SSD_HEADS + hd:SSD_HEADS + hd + 1, :]
                dt_row = grow[hd:hd + 1, :]
                g_last = gr[:, n - 1:n]
                dec = jnp.exp(jnp.where(causal, gc - gr, -jnp.inf))
                att = cb * dec * dt_row
                q_in = qf * jnp.exp(gc)
                mine_v = (lane_v >= SSD_HEADDIM) if side else (lane_v < SSD_HEADDIM)
                mine_s = (lane_s >= SSD_HEADDIM) if side else (lane_s < SSD_HEADDIM)
                xm = jnp.where(mine_v, xv, jnp.zeros_like(xv))
                sm = jnp.where(mine_s, state, 0.0).astype(BF16)
                o = o + _dot(att.astype(BF16), xm) + _dot(q_in.astype(BF16), sm)
                k_out = kt * (jnp.exp(g_last - gr) * dt_row)
                upd = upd + _dot(k_out.astype(BF16), xm)
                last.append(jnp.exp(g_last))
            s_ref[pair] = state * jnp.where(lane_1 < SSD_HEADDIM, last[0], last[1]) + upd
            ys.append(o + xv.astype(F32) * dskip_ref[:, col:col + pair_w])
        y = jnp.concatenate(ys, axis=-1)
        y = y * _silu(z_ref[:, group_w * g:group_w * (g + 1)].astype(F32))
        y = y * lax.rsqrt(jnp.mean(y * y, axis=-1, keepdims=True) + RMS_EPS)
        o_ref[:, group_w * g:group_w * (g + 1)] = (y * nw_ref[:, group_w * g:group_w * (g + 1)]).astype(BF16)


def _ssd(z, xbc, gates, gates_row, dskip, norm_w, batch, rows_per_batch):
    n = CHUNK
    nc = rows_per_batch // n
    row = lambda b, c: b * nc + c
    return pl.pallas_call(
        _ssd_kernel,
        grid=(batch, nc),
        in_specs=[
            pl.BlockSpec((n, SSD_DINNER), lambda b, c: (row(b, c), 0)),
            pl.BlockSpec((n, SSD_DINNER), lambda b, c: (row(b, c), 0)),
            pl.BlockSpec((n, SSD_GN), lambda b, c: (row(b, c), 4)),
            pl.BlockSpec((n, SSD_GN), lambda b, c: (row(b, c), 5)),
            pl.BlockSpec((n, LANES), lambda b, c: (row(b, c), 0)),
            pl.BlockSpec((1, 2 * SSD_HEADS, n), lambda b, c: (b, 0, c)),
            pl.BlockSpec((1, SSD_DINNER), lambda b, c: (0, 0)),
            pl.BlockSpec((1, SSD_DINNER), lambda b, c: (0, 0)),
        ],
        out_specs=pl.BlockSpec((n, SSD_DINNER), lambda b, c: (row(b, c), 0)),
        out_shape=jax.ShapeDtypeStruct((z.shape[0], SSD_DINNER), BF16),
        scratch_shapes=[pltpu.VMEM((SSD_HEADS // 2, SSD_DSTATE, 2 * SSD_HEADDIM), F32)],
        compiler_params=_params(("arbitrary", "arbitrary")),
        name="ssd",
    )(z, xbc, xbc, xbc, gates, gates_row, dskip, norm_w)


def _pad_lanes(v):
    v = v.reshape(1, -1).astype(F32)
    return jnp.pad(v, ((0, 0), (0, LANES - v.shape[1])))


def kernel(x, meta_tokens, ab_w_in, ab_ret_gn_w, ab_conv_q, ab_conv_k, ab_conv_v, ab_A_log, ab_dt_bias, ab_gdn_norm_w, ab_w_out, c_w_in, c_conv_w, c_conv_b, c_A_log, c_dt_bias, c_D, c_norm_w, c_w_out, mlp_w1, mlp_w2, ln1_w, ln1_b, ln2_w, ln2_b):
    batch, seq, d = x.shape
    assert d == D_MODEL and meta_tokens.shape == (N_META, D_MODEL)
    lp = PAD_FRONT + N_META + seq
    rows = batch * lp

    meta = jnp.broadcast_to(meta_tokens[None].astype(x.dtype), (batch, N_META, d))
    h = jnp.concatenate([jnp.zeros((batch, PAD_FRONT, d), x.dtype), meta, x], axis=1).reshape(rows, d)
    hb = h.astype(BF16)

    pos = jnp.arange(lp, dtype=F32) - PAD_FRONT
    inv_freq = 1.0 / (ROPE_BASE ** jnp.linspace(0.0, 1.0, RET_DK // 2, dtype=F32))
    ang = pos[:, None] * inv_freq[None]
    cosf = jnp.concatenate([jnp.cos(ang), jnp.cos(ang)], axis=-1)
    sinf = jnp.concatenate([-jnp.sin(ang), jnp.sin(ang)], axis=-1)

    w_in = ab_w_in[0]
    w_main = jnp.concatenate([w_in[:, :5120], w_in[:, 5128:]], axis=1).astype(BF16)
    w_gate = jnp.pad(w_in[:, 5120:5128], ((0, 0), (0, LANES - 8))).astype(BF16)
    ones = jnp.ones((1, COL_TILE // 2), F32)
    zeros = jnp.zeros((1, COL_TILE), F32)
    p_qk = _proj(hb, w_main, (lambda j: j, 1), _epilogue_rope,
                 [cosf, sinf, jnp.concatenate([ones * RET_DK ** -0.5, ones], axis=1)],
                 [_per_row_in_batch(LANES), _per_row_in_batch(LANES), _per_col()], lp, "proj_ret_qk")
    p_plain = _proj(hb, w_main, (lambda j: jnp.where(j < 2, j + 1, 5), 3), _epilogue_plain,
                    [], [], lp, "proj_ab_plain")
    g_qk = _proj(hb, w_main, (lambda j: j + 3, 1), functools.partial(_epilogue_conv, l2norm=True),
                 [jnp.concatenate([ab_conv_q[0], ab_conv_k[0]], axis=1), zeros,
                  jnp.concatenate([ones * GDN_DK ** -0.5, ones], axis=1)],
                 [_per_col(rows=CONV_K), _per_col(), _per_col()], lp, "proj_gdn_qk")
    g_v = _proj(hb, w_main, (lambda j: j + 4, 1), functools.partial(_epilogue_conv, l2norm=False),
                [ab_conv_v[0], zeros, zeros], [_per_col(rows=CONV_K), _per_col(), _per_col()], lp, "proj_gdn_v")
    alog = _pad_lanes(jnp.concatenate([jnp.zeros((GDN_HEADS,), F32), ab_A_log[0]]))
    bias = _pad_lanes(jnp.concatenate([jnp.zeros((GDN_HEADS,), F32), ab_dt_bias[0]]))
    g0 = _gates(_gates_ab_kernel, hb, w_gate, alog, bias, lp, "gates_ab")
    y_ret = _retention(p_qk, p_plain, ab_ret_gn_w[0].reshape(1, -1), batch, lp)
    y_gdn = _gdn(g_qk, g_v, p_plain, g0, ab_gdn_norm_w[0].reshape(1, -1), batch, lp)
    h = _outproj_ln(y_ret, y_gdn, 0, 0, ab_w_out[0].astype(BF16), h,
                    ln1_w[0].reshape(1, -1), ln1_b[0].reshape(1, -1), lp)
    h, hb = _mlp_ln(h, mlp_w1[0].astype(BF16), mlp_w2[0].astype(BF16),
                    ln2_w[0].reshape(1, -1), ln2_b[0].reshape(1, -1), lp)

    w_in = c_w_in[0]
    w_main = w_in[:, :5120].astype(BF16)
    w_dt = w_in[:, 5120:]
    w_gate = jnp.pad(jnp.concatenate([w_dt, w_dt], axis=1), ((0, 0), (0, LANES - 2 * SSD_HEADS))).astype(BF16)
    p_z = _proj(hb, w_main, (lambda j: j, 2), _epilogue_plain, [], [], lp, "proj_ssd_z")
    p_xbc = _proj(hb, w_main, (lambda j: j + 2, 3), functools.partial(_epilogue_conv, l2norm=False),
                  [c_conv_w[0], c_conv_b[0].reshape(1, -1), jnp.zeros((1, 3 * COL_TILE), F32)],
                  [_per_col(rows=CONV_K), _per_col(), _per_col()], lp, "proj_ssd_xbc")
    alog = _pad_lanes(jnp.concatenate([jnp.zeros((SSD_HEADS,), F32), c_A_log[0]]))
    bias = _pad_lanes(jnp.concatenate([c_dt_bias[0], c_dt_bias[0]]))
    g1 = _gates(_gates_c_kernel, hb, w_gate, alog, bias, lp, "gates_c")
    g1_row = jnp.swapaxes(g1[:, :2 * SSD_HEADS].reshape(batch, lp, 2 * SSD_HEADS), 1, 2)
    dskip = jnp.repeat(c_D[0].astype(F32), SSD_HEADDIM).reshape(1, -1)
    y_ssd = _ssd(p_z, p_xbc, g1, g1_row, dskip, c_norm_w[0].reshape(1, -1), batch, lp)
    h = _outproj_ln(y_ssd, y_ssd, 0, 1, c_w_out[0].astype(BF16), h,
                    ln1_w[1].reshape(1, -1), ln1_b[1].reshape(1, -1), lp)
    h, _ = _mlp_ln(h, mlp_w1[1].astype(BF16), mlp_w2[1].astype(BF16),
                   ln2_w[1].reshape(1, -1), ln2_b[1].reshape(1, -1), lp)
    return h.reshape(batch, lp, d)[:, PAD_FRONT + N_META:]
```

```python
import functools

import jax
import jax.numpy as jnp
from jax import lax
from jax.experimental import pallas as pl
from jax.experimental.pallas import tpu as pltpu

F32 = jnp.float32
BF16 = jnp.bfloat16
HI = lax.Precision.HIGHEST

D_MODEL = 1024
DEPTH = 2
N_META = 16
CONV_K = 4
RET_HEADS = 4
RET_DK = 128
RET_DV = 256
ROPE_BASE = 10000.0
GDN_HEADS = 4
GDN_DK = 128
GDN_DV = 256
SSD_DINNER = 2048
SSD_HEADDIM = 64
SSD_HEADS = 32
SSD_GROUPS = 4
SSD_HPG = 8
SSD_DSTATE = 128
SSD_GN = 512
D_FF = 4096
DN_ALPHA = (2 * DEPTH) ** 0.25
LN_EPS = 1e-5
GN_EPS = 1e-5
RMS_EPS = 1e-6

LANES = 128
COL_TILE = 1024
CHUNK = 128
GDN_CHUNK = 64
PAD_FRONT = CHUNK - N_META
HALO = 8
CONV_ROWS = 64
VMEM_LIMIT = 56 * 1024 * 1024


def _row_tile(rows_per_batch):
    for tm in (640, 512, 256, 128):
        if rows_per_batch % tm == 0:
            return tm
    raise ValueError(f"unsupported padded sequence length {rows_per_batch}")


def _params(sem):
    return pltpu.CompilerParams(dimension_semantics=sem, vmem_limit_bytes=VMEM_LIMIT)


def _softplus(x):
    return jnp.maximum(x, 0.0) + jnp.log1p(jnp.exp(-jnp.abs(x)))


def _silu(x):
    return x * jax.nn.sigmoid(x)


def _dot(a, b):
    return jnp.dot(a, b, preferred_element_type=F32)


def _dot_nt(a, b):
    return lax.dot_general(a, b, (((1,), (1,)), ((), ())), preferred_element_type=F32)


def _dot_hi(a, b):
    return jnp.dot(a, b, preferred_element_type=F32, precision=HI)


def _wide(a):
    hi = a.astype(BF16).astype(F32)
    return jnp.concatenate([a, a - hi, a], axis=1).astype(BF16)


def _tall(b):
    hi = b.astype(BF16)
    lo = (b - hi.astype(F32)).astype(BF16)
    return jnp.concatenate([hi, hi, lo], axis=0)


def _epilogue_plain(acc_ref, o_ref, *, tm):
    o_ref[...] = acc_ref[HALO:HALO + tm, :].astype(BF16)


def _epilogue_rope(acc_ref, o_ref, cos_ref, sin_ref, scale_ref, *, tm):
    for hd in range(COL_TILE // RET_DK):
        cols = slice(RET_DK * hd, RET_DK * (hd + 1))
        x = acc_ref[HALO:HALO + tm, cols]
        y = x * cos_ref[...] + pltpu.roll(x, RET_DK // 2, 1) * sin_ref[...]
        o_ref[:, cols] = (y * scale_ref[:, cols]).astype(BF16)


def _epilogue_conv(acc_ref, o_ref, cw_ref, cb_ref, scale_ref, *, tm, l2norm):
    for r in range(0, tm, CONV_ROWS):
        rows = slice(r, r + CONV_ROWS)
        for c in range(0, COL_TILE, LANES):
            cols = slice(c, c + LANES)
            w = [cw_ref[tap:tap + 1, cols] for tap in range(CONV_K)]
            xw = acc_ref[r:r + CONV_ROWS + HALO, cols]
            s1 = pltpu.roll(xw, 1, 0)
            y = w[3] * xw + w[2] * s1 + pltpu.roll(w[1] * xw + w[0] * s1, 2, 0)
            y = _silu(y[HALO:] + cb_ref[:, cols])
            if l2norm:
                inv = lax.rsqrt(jnp.sum(y * y, axis=-1, keepdims=True) + 1e-6)
                y = y * (inv * scale_ref[:, cols])
            o_ref[rows, cols] = y.astype(BF16)


def _lagged_steps(i, n_tiles, acc_a, acc_b, produce, consume):
    last_prev = acc_b if n_tiles % 2 == 0 else acc_a

    @pl.when(i == 0)
    def _():
        acc_b[...] = jnp.zeros(acc_b.shape, F32)

    @pl.when(jnp.logical_and(i % 2 == 0, i < n_tiles))
    def _():
        produce(acc_a, acc_b)
        consume(acc_b)

    @pl.when(jnp.logical_and(i % 2 == 1, i < n_tiles))
    def _():
        produce(acc_b, acc_a)
        consume(acc_a)

    @pl.when(i == n_tiles)
    def _():
        consume(last_prev)


def _proj_kernel(*refs, tm, tiles_per_batch, n_tiles, n_extra, epilogue):
    h_ref, w_ref = refs[0], refs[1]
    extra = refs[2:2 + n_extra]
    o_ref, acc_a, acc_b = refs[2 + n_extra:]
    i = pl.program_id(1)
    starts_batch = (i % tiles_per_batch) == 0

    def produce(cur, prev):
        cur[HALO:HALO + tm, :] = _dot(h_ref[...], w_ref[...])
        cur[0:HALO, :] = jnp.where(starts_batch, 0.0, prev[tm:tm + HALO, :])

    _lagged_steps(i, n_tiles, acc_a, acc_b, produce, lambda prev: epilogue(prev, o_ref, *extra, tm=tm))


def _proj(hb, w, col_tiles, epilogue, extra, extra_specs, rows_per_batch, name):
    rows = hb.shape[0]
    tm = _row_tile(rows_per_batch)
    tpb = rows_per_batch // tm
    n_tiles = rows // tm
    col_fn, ncol = col_tiles
    prev = lambda i: jnp.maximum(i - 1, 0)
    return pl.pallas_call(
        functools.partial(_proj_kernel, tm=tm, tiles_per_batch=tpb, n_tiles=n_tiles, n_extra=len(extra),
                          epilogue=epilogue),
        grid=(ncol, n_tiles + 1),
        in_specs=[
            pl.BlockSpec((tm, D_MODEL), lambda j, i: (jnp.minimum(i, n_tiles - 1), 0)),
            pl.BlockSpec((D_MODEL, COL_TILE), lambda j, i: (0, col_fn(j))),
        ] + [spec(tm, tpb, prev) for spec in extra_specs],
        out_specs=pl.BlockSpec((tm, COL_TILE), lambda j, i: (prev(i), j)),
        out_shape=jax.ShapeDtypeStruct((rows, ncol * COL_TILE), BF16),
        scratch_shapes=[pltpu.VMEM((tm + HALO, COL_TILE), F32), pltpu.VMEM((tm + HALO, COL_TILE), F32)],
        compiler_params=_params(("arbitrary", "arbitrary")),
        name=name,
    )(hb, w, *extra)


def _per_col(width=COL_TILE, rows=1):
    return lambda tm, tpb, prev: pl.BlockSpec((rows, width), lambda j, i: (0, j))


def _per_row_in_batch(width):
    return lambda tm, tpb, prev: pl.BlockSpec((tm, width), lambda j, i: (prev(i) % tpb, 0))


def _gates_ab_kernel(h_ref, w_ref, alog_ref, bias_ref, o_ref, *, tm, tiles_per_batch):
    acc = _dot(h_ref[...], w_ref[...])
    lane = lax.broadcasted_iota(jnp.int32, acc.shape, 1)
    decay = -jnp.exp(alog_ref[...]) * _softplus(acc + bias_ref[...])
    o_ref[...] = jnp.where(lane < GDN_HEADS, jax.nn.sigmoid(acc), decay)


def _gates_c_kernel(h_ref, w_ref, alog_ref, bias_ref, o_ref, *, tm, tiles_per_batch):
    acc = _dot(h_ref[...], w_ref[...])
    lane = lax.broadcasted_iota(jnp.int32, acc.shape, 1)
    row = (pl.program_id(0) % tiles_per_batch) * tm + lax.broadcasted_iota(jnp.int32, (tm, 1), 0)
    dt = jnp.where(row >= PAD_FRONT, _softplus(acc + bias_ref[...]), 0.0)
    o_ref[...] = jnp.where(lane < SSD_HEADS, dt, dt * -jnp.exp(alog_ref[...]))


def _gates(kernel_fn, hb, w, alog, bias, rows_per_batch, name):
    rows = hb.shape[0]
    tm = _row_tile(rows_per_batch)
    vec = pl.BlockSpec((1, LANES), lambda i: (0, 0))
    return pl.pallas_call(
        functools.partial(kernel_fn, tm=tm, tiles_per_batch=rows_per_batch // tm),
        grid=(rows // tm,),
        in_specs=[
            pl.BlockSpec((tm, D_MODEL), lambda i: (i, 0)),
            pl.BlockSpec((D_MODEL, LANES), lambda i: (0, 0)),
            vec, vec,
        ],
        out_specs=pl.BlockSpec((tm, LANES), lambda i: (i, 0)),
        out_shape=jax.ShapeDtypeStruct((rows, LANES), F32),
        compiler_params=_params(("arbitrary",)),
        name=name,
    )(hb, w, alog, bias)


def _retention_kernel(q_ref, k_ref, v_ref, g_ref, gnw_ref, o_ref, s_ref, *, batch):
    n = CHUNK

    @pl.when(pl.program_id(0) == 0)
    def _():
        s_ref[...] = jnp.zeros(s_ref.shape, F32)

    t_idx = lax.broadcasted_iota(jnp.int32, (n, n), 0)
    s_idx = lax.broadcasted_iota(jnp.int32, (n, n), 1)
    t_col = lax.broadcasted_iota(jnp.int32, (n, 1), 0).astype(F32)
    gap = (t_idx - s_idx).astype(F32)
    chains = [(b, hd) for b in range(batch) for hd in range(RET_HEADS)]
    cs = range(len(chains))
    log_gamma = [jnp.log1p(-jnp.exp2(jnp.full((1, 1), -5.0 - hd, F32))) for hd in range(RET_HEADS)]
    decay = [jnp.exp(jnp.where(t_idx >= s_idx, gap * lg, -jnp.inf)) for lg in log_gamma]
    e_in = [jnp.exp((t_col + 1.0) * lg) for lg in log_gamma]
    e_out = [jnp.exp((n - 1.0 - t_col) * lg) for lg in log_gamma]
    e_all = [jnp.exp(n * lg) for lg in log_gamma]
    q = [q_ref[b, :, RET_DK * hd:RET_DK * (hd + 1)] for b, hd in chains]
    k = [k_ref[b, :, RET_DK * hd:RET_DK * (hd + 1)] for b, hd in chains]
    v = [v_ref[b, :, RET_DV * hd:RET_DV * (hd + 1)] for b, hd in chains]
    states = [s_ref[c] for c in cs]
    att = [(_dot_nt(q[c], k[c]) * decay[chains[c][1]]).astype(BF16) for c in cs]
    q_in = [(q[c].astype(F32) * e_in[chains[c][1]]).astype(BF16) for c in cs]
    o = [_dot(att[c], v[c]) + _dot(q_in[c], states[c].astype(BF16)) for c in cs]
    k_out = [(k[c].astype(F32) * e_out[chains[c][1]]).T.astype(BF16) for c in cs]
    for c in cs:
        s_ref[c] = states[c] * e_all[chains[c][1]] + _dot(k_out[c], v[c])
    for c, (b, hd) in enumerate(chains):
        cols = slice(RET_DV * hd, RET_DV * (hd + 1))
        oc = o[c] - jnp.mean(o[c], axis=-1, keepdims=True)
        on = oc * lax.rsqrt(jnp.mean(oc * oc, axis=-1, keepdims=True) + GN_EPS)
        gate = _silu(g_ref[b, :, cols].astype(F32))
        o_ref[b, :, cols] = (on * gnw_ref[:, cols] * gate).astype(BF16)


def _retention(qk, vg, gn_w, batch, rows_per_batch):
    n = CHUNK
    nc = rows_per_batch // n
    qk = qk.reshape(batch, rows_per_batch, qk.shape[1])
    vg = vg.reshape(batch, rows_per_batch, vg.shape[1])
    out = pl.pallas_call(
        functools.partial(_retention_kernel, batch=batch),
        grid=(nc,),
        in_specs=[
            pl.BlockSpec((batch, n, 512), lambda c: (0, c, 0)),
            pl.BlockSpec((batch, n, 512), lambda c: (0, c, 1)),
            pl.BlockSpec((batch, n, 1024), lambda c: (0, c, 0)),
            pl.BlockSpec((batch, n, 1024), lambda c: (0, c, 1)),
            pl.BlockSpec((1, 1024), lambda c: (0, 0)),
        ],
        out_specs=pl.BlockSpec((batch, n, 1024), lambda c: (0, c, 0)),
        out_shape=jax.ShapeDtypeStruct((batch, rows_per_batch, 1024), BF16),
        scratch_shapes=[pltpu.VMEM((batch * RET_HEADS, RET_DK, RET_DV), F32)],
        compiler_params=_params(("arbitrary",)),
        name="retention",
    )(qk, qk, vg, vg, gn_w)
    return out.reshape(batch * rows_per_batch, 1024)


def _gdn_kernel(q_ref, k_ref, v_ref, g_ref, gcol_ref, grow_ref, nw_ref, o_ref, s_ref, *, batch):
    n = GDN_CHUNK

    @pl.when(pl.program_id(0) == 0)
    def _():
        s_ref[...] = jnp.zeros(s_ref.shape, F32)

    t_idx = lax.broadcasted_iota(jnp.int32, (n, n), 0)
    s_idx = lax.broadcasted_iota(jnp.int32, (n, n), 1)
    lower = (t_idx >= s_idx).astype(F32)
    upper = (t_idx <= s_idx).astype(F32)
    eye = (t_idx == s_idx).astype(F32)
    chains = [(b, hd) for b in range(batch) for hd in range(GDN_HEADS)]
    cs = range(len(chains))
    gcol = [gcol_ref[b] for b in range(batch)]
    gcum_col = [_dot_hi(lower, gcol[b]) for b in range(batch)]
    gcum_row = [_dot_hi(grow_ref[b, 0], upper) for b in range(batch)]
    states = [s_ref[c] for c in cs]
    q = [q_ref[b, :, GDN_DK * hd:GDN_DK * (hd + 1)].astype(F32) for b, hd in chains]
    k = [k_ref[b, :, GDN_DK * hd:GDN_DK * (hd + 1)].astype(F32) for b, hd in chains]
    v = [v_ref[b, :, GDN_DV * hd:GDN_DV * (hd + 1)].astype(F32) for b, hd in chains]
    beta = [gcol[b][:, hd:hd + 1] for b, hd in chains]
    gc = [gcum_col[b][:, GDN_HEADS + hd:GDN_HEADS + hd + 1] for b, hd in chains]
    gr = [gcum_row[b][GDN_HEADS + hd:GDN_HEADS + hd + 1, :] for b, hd in chains]
    g_last = [gc[c][n - 1:n, :] for c in cs]
    seg = [gc[c] - gr[c] for c in cs]
    dec_strict = [jnp.exp(jnp.where(t_idx > s_idx, seg[c], -jnp.inf)) for c in cs]
    dec_incl = [jnp.exp(jnp.where(t_idx >= s_idx, seg[c], -jnp.inf)) for c in cs]
    kb = [k[c] * beta[c] for c in cs]
    kq = [_dot_nt(jnp.concatenate([kb[c], q[c]], axis=0).astype(BF16), k[c].astype(BF16)) for c in cs]
    a = [kq[c][:n] * dec_strict[c] for c in cs]
    att = [(kq[c][n:] * dec_incl[c]).astype(BF16) for c in cs]
    inv = [eye - a[c] for c in cs]
    p = [_dot(_wide(a[c]), _tall(a[c])) for c in cs]
    span = 4
    while span < n:
        r = [_dot(_wide(jnp.concatenate([p[c], inv[c]], axis=0)), _tall(p[c])) for c in cs]
        inv = [inv[c] + r[c][n:] for c in cs]
        p = [r[c][:n] for c in cs]
        span *= 2
    inv = [inv[c] + _dot(_wide(inv[c]), _tall(p[c])) for c in cs]
    e_gc = [jnp.exp(gc[c]) for c in cs]
    rhs = [jnp.concatenate([v[c] * beta[c], kb[c] * e_gc[c]], axis=-1) for c in cs]
    u = [_dot(_wide(inv[c]), _tall(rhs[c])) for c in cs]
    s16 = [states[c].astype(BF16) for c in cs]
    ws = [_dot(jnp.concatenate([u[c][:, GDN_DV:], q[c] * e_gc[c]], axis=0).astype(BF16), s16[c]) for c in cs]
    v_new16 = [(u[c][:, :GDN_DV] - ws[c][:n]).astype(BF16) for c in cs]
    o = [ws[c][n:] + _dot(att[c], v_new16[c]) for c in cs]
    k_out = [k[c] * jnp.exp(g_last[c] - gc[c]) for c in cs]
    for c in cs:
        s_ref[c] = states[c] * jnp.exp(g_last[c]) + _dot(k_out[c].T.astype(BF16), v_new16[c])
    for c, (b, hd) in enumerate(chains):
        on = o[c] * lax.rsqrt(jnp.mean(o[c] * o[c], axis=-1, keepdims=True) + RMS_EPS)
        gate = _silu(g_ref[b, :, GDN_DV * hd:GDN_DV * (hd + 1)].astype(F32))
        o_ref[b, :, GDN_DV * hd:GDN_DV * (hd + 1)] = (on * nw_ref[...] * gate).astype(BF16)


def _gdn(qk, v, plain, gates, norm_w, batch, rows_per_batch):
    n = GDN_CHUNK
    nc = rows_per_batch // n
    qk = qk.reshape(batch, rows_per_batch, qk.shape[1])
    v = v.reshape(batch, rows_per_batch, v.shape[1])
    plain = plain.reshape(batch, rows_per_batch, plain.shape[1])
    gates = gates.reshape(batch, rows_per_batch, LANES)
    gates_row = jnp.swapaxes(gates[:, :, :8].reshape(batch, nc, n, 8), 2, 3)
    out = pl.pallas_call(
        functools.partial(_gdn_kernel, batch=batch),
        grid=(nc,),
        in_specs=[
            pl.BlockSpec((batch, n, 512), lambda c: (0, c, 0)),
            pl.BlockSpec((batch, n, 512), lambda c: (0, c, 1)),
            pl.BlockSpec((batch, n, 1024), lambda c: (0, c, 0)),
            pl.BlockSpec((batch, n, 1024), lambda c: (0, c, 2)),
            pl.BlockSpec((batch, n, LANES), lambda c: (0, c, 0)),
            pl.BlockSpec((batch, 1, 8, n), lambda c: (0, c, 0, 0)),
            pl.BlockSpec((1, GDN_DV), lambda c: (0, 0)),
        ],
        out_specs=pl.BlockSpec((batch, n, 1024), lambda c: (0, c, 0)),
        out_shape=jax.ShapeDtypeStruct((batch, rows_per_batch, 1024), BF16),
        scratch_shapes=[pltpu.VMEM((batch * GDN_HEADS, GDN_DK, GDN_DV), F32)],
        compiler_params=_params(("arbitrary",)),
        name="gated_delta",
    )(qk, qk, v, plain, gates, gates_row, norm_w)
    return out.reshape(batch * rows_per_batch, 1024)


def _layer_norm(x, w, b):
    xc = x - jnp.mean(x, axis=-1, keepdims=True)
    return xc * lax.rsqrt(jnp.mean(xc * xc, axis=-1, keepdims=True) + LN_EPS) * w + b


def _outproj_ln_kernel(ya_ref, yb_ref, w_ref, h_ref, lw_ref, lb_ref, o_ref, acc_a, acc_b, *, n_tiles):
    half = ya_ref.shape[1]

    def produce(cur, prev):
        acc = _dot(ya_ref[...], w_ref[0:half, :]) + _dot(yb_ref[...], w_ref[half:2 * half, :])
        cur[...] = DN_ALPHA * h_ref[...] + acc

    def consume(prev):
        o_ref[...] = _layer_norm(prev[...], lw_ref[...], lb_ref[...])

    _lagged_steps(pl.program_id(0), n_tiles, acc_a, acc_b, produce, consume)


def _outproj_ln(ya, yb, col_a, col_b, w, h, lw, lb, rows_per_batch):
    rows = h.shape[0]
    tm = _row_tile(rows_per_batch)
    n_tiles = rows // tm
    half = w.shape[0] // 2
    cur = lambda i: jnp.minimum(i, n_tiles - 1)
    prev = lambda i: jnp.maximum(i - 1, 0)
    vec = pl.BlockSpec((1, D_MODEL), lambda i: (0, 0))
    return pl.pallas_call(
        functools.partial(_outproj_ln_kernel, n_tiles=n_tiles),
        grid=(n_tiles + 1,),
        in_specs=[
            pl.BlockSpec((tm, half), lambda i: (cur(i), col_a)),
            pl.BlockSpec((tm, half), lambda i: (cur(i), col_b)),
            pl.BlockSpec((2 * half, D_MODEL), lambda i: (0, 0), pipeline_mode=pl.Buffered(1)),
            pl.BlockSpec((tm, D_MODEL), lambda i: (cur(i), 0)),
            vec, vec,
        ],
        out_specs=pl.BlockSpec((tm, D_MODEL), lambda i: (prev(i), 0)),
        out_shape=jax.ShapeDtypeStruct((rows, D_MODEL), F32),
        scratch_shapes=[pltpu.VMEM((tm, D_MODEL), F32), pltpu.VMEM((tm, D_MODEL), F32)],
        compiler_params=_params(("arbitrary",)),
        name="outproj_ln",
    )(ya, yb, w, h, lw, lb)


def _mlp(h_ref, w1_ref, w2_ref, lw_ref, lb_ref):
    h = h_ref[...]
    hb = h.astype(BF16)
    acc = jnp.zeros(h.shape, F32)
    step = 1024
    for f in range(0, D_FF, step):
        a = _dot(hb, w1_ref[:, f:f + step])
        a = jnp.square(jnp.maximum(a, 0.0)).astype(BF16)
        acc = acc + _dot(a, w2_ref[f:f + step, :])
    return _layer_norm(DN_ALPHA * h + acc, lw_ref[...], lb_ref[...])


def _mlp_ln_kernel(h_ref, w1_ref, w2_ref, lw_ref, lb_ref, o_ref, ob_ref, *, tm, tiles_per_batch):
    out = _mlp(h_ref, w1_ref, w2_ref, lw_ref, lb_ref)
    o_ref[...] = out
    row = (pl.program_id(0) % tiles_per_batch) * tm + lax.broadcasted_iota(jnp.int32, (tm, 1), 0)
    ob_ref[...] = jnp.where(row >= PAD_FRONT, out, 0.0).astype(BF16)


def _mlp_ln_final_kernel(h_ref, w1_ref, w2_ref, lw_ref, lb_ref, o_ref):
    o_ref[...] = _mlp(h_ref, w1_ref, w2_ref, lw_ref, lb_ref)


def _mlp_weight_specs(index):
    return [
        pl.BlockSpec((D_MODEL, D_FF), index, pipeline_mode=pl.Buffered(1)),
        pl.BlockSpec((D_FF, D_MODEL), index, pipeline_mode=pl.Buffered(1)),
        pl.BlockSpec((1, D_MODEL), index),
        pl.BlockSpec((1, D_MODEL), index),
    ]


def _mlp_ln(h, w1, w2, lw, lb, rows_per_batch):
    rows = h.shape[0]
    tm = _row_tile(rows_per_batch)
    tile = pl.BlockSpec((tm, D_MODEL), lambda i: (i, 0))
    return pl.pallas_call(
        functools.partial(_mlp_ln_kernel, tm=tm, tiles_per_batch=rows_per_batch // tm),
        grid=(rows // tm,),
        in_specs=[tile] + _mlp_weight_specs(lambda i: (0, 0)),
        out_specs=[tile, tile],
        out_shape=[jax.ShapeDtypeStruct((rows, D_MODEL), F32), jax.ShapeDtypeStruct((rows, D_MODEL), BF16)],
        compiler_params=_params(("arbitrary",)),
        name="mlp_ln",
    )(h, w1, w2, lw, lb)


def _mlp_ln_final(h, w1, w2, lw, lb, batch, seq, rows_per_batch):
    tm = next(t for t in (1024, 512, 256, 128) if seq % t == 0)
    tiles = seq // tm
    first_row = PAD_FRONT + N_META
    return pl.pallas_call(
        _mlp_ln_final_kernel,
        grid=(batch, tiles),
        in_specs=[pl.BlockSpec((pl.Element(tm), pl.Element(D_MODEL)),
                               lambda b, i: (pl.multiple_of(b * rows_per_batch + first_row + i * tm, 128), 0))]
        + _mlp_weight_specs(lambda b, i: (0, 0)),
        out_specs=pl.BlockSpec((tm, D_MODEL), lambda b, i: (b * tiles + i, 0)),
        out_shape=jax.ShapeDtypeStruct((batch * seq, D_MODEL), F32),
        compiler_params=_params(("arbitrary", "arbitrary")),
        name="mlp_ln_final",
    )(h, w1, w2, lw, lb)


def _ssd_kernel(z_ref, x_ref, b_ref, c_ref, gcol_ref, grow_ref, dskip_ref, nw_ref, o_ref, s_ref):
    cidx = pl.program_id(1)
    n = CHUNK
    pair_w = 2 * SSD_HEADDIM
    pairs_per_group = SSD_HPG // 2
    group_w = SSD_HPG * SSD_HEADDIM

    @pl.when(cidx == 0)
    def _():
        s_ref[...] = jnp.zeros(s_ref.shape, F32)

    t_idx = lax.broadcasted_iota(jnp.int32, (n, n), 0)
    s_idx = lax.broadcasted_iota(jnp.int32, (n, n), 1)
    causal = t_idx >= s_idx
    lower = causal.astype(F32)
    upper = (t_idx <= s_idx).astype(F32)
    lane_v = lax.broadcasted_iota(jnp.int32, (n, pair_w), 1)
    lane_s = lax.broadcasted_iota(jnp.int32, (SSD_DSTATE, pair_w), 1)
    lane_1 = lax.broadcasted_iota(jnp.int32, (1, pair_w), 1)
    grow = grow_ref[0]
    gcum_col = _dot_hi(lower, gcol_ref[...])
    gcum_row = _dot_hi(grow, upper)
    for g in range(SSD_GROUPS):
        q = c_ref[:, SSD_DSTATE * g:SSD_DSTATE * (g + 1)]
        k = b_ref[:, SSD_DSTATE * g:SSD_DSTATE * (g + 1)]
        qf = q.astype(F32)
        kt = k.astype(F32).T
        cb = _dot_nt(q, k)
        ys = []
        for p in range(pairs_per_group):
            pair = g * pairs_per_group + p
            col = group_w * g + pair_w * p
            xv = x_ref[:, col:col + pair_w]
            state = s_ref[pair]
            o = jnp.zeros((n, pair_w), F32)
            upd = jnp.zeros((SSD_DSTATE, pair_w), F32)
            last = []
            for side in range(2):
                hd = 2 * pair + side
                gc = gcum_col[:, SSD_HEADS + hd:SSD_HEADS + hd + 1]
                gr = gcum_row[SSD_HEADS + hd:SSD_HEADS + hd + 1, :]
                dt_row = grow[hd:hd + 1, :]
                g_last = gr[:, n - 1:n]
                dec = jnp.exp(jnp.where(causal, gc - gr, -jnp.inf))
                att = cb * dec * dt_row
                q_in = qf * jnp.exp(gc)
                mine_v = (lane_v >= SSD_HEADDIM) if side else (lane_v < SSD_HEADDIM)
                mine_s = (lane_s >= SSD_HEADDIM) if side else (lane_s < SSD_HEADDIM)
                xm = jnp.where(mine_v, xv, jnp.zeros_like(xv))
                sm = jnp.where(mine_s, state, 0.0).astype(BF16)
                o = o + _dot(att.astype(BF16), xm) + _dot(q_in.astype(BF16), sm)
                k_out = kt * (jnp.exp(g_last - gr) * dt_row)
                upd = upd + _dot(k_out.astype(BF16), xm)
                last.append(jnp.exp(g_last))
            s_ref[pair] = state * jnp.where(lane_1 < SSD_HEADDIM, last[0], last[1]) + upd
            ys.append(o + xv.astype(F32) * dskip_ref[:, col:col + pair_w])
        y = jnp.concatenate(ys, axis=-1)
        y = y * _silu(z_ref[:, group_w * g:group_w * (g + 1)].astype(F32))
        y = y * lax.rsqrt(jnp.mean(y * y, axis=-1, keepdims=True) + RMS_EPS)
        o_ref[:, group_w * g:group_w * (g + 1)] = (y * nw_ref[:, group_w * g:group_w * (g + 1)]).astype(BF16)


def _ssd(z, xbc, gates, gates_row, dskip, norm_w, batch, rows_per_batch):
    n = CHUNK
    nc = rows_per_batch // n
    row = lambda b, c: b * nc + c
    return pl.pallas_call(
        _ssd_kernel,
        grid=(batch, nc),
        in_specs=[
            pl.BlockSpec((n, SSD_DINNER), lambda b, c: (row(b, c), 0)),
            pl.BlockSpec((n, SSD_DINNER), lambda b, c: (row(b, c), 0)),
            pl.BlockSpec((n, SSD_GN), lambda b, c: (row(b, c), 4)),
            pl.BlockSpec((n, SSD_GN), lambda b, c: (row(b, c), 5)),
            pl.BlockSpec((n, LANES), lambda b, c: (row(b, c), 0)),
            pl.BlockSpec((1, 2 * SSD_HEADS, n), lambda b, c: (b, 0, c)),
            pl.BlockSpec((1, SSD_DINNER), lambda b, c: (0, 0)),
            pl.BlockSpec((1, SSD_DINNER), lambda b, c: (0, 0)),
        ],
        out_specs=pl.BlockSpec((n, SSD_DINNER), lambda b, c: (row(b, c), 0)),
        out_shape=jax.ShapeDtypeStruct((z.shape[0], SSD_DINNER), BF16),
        scratch_shapes=[pltpu.VMEM((SSD_HEADS // 2, SSD_DSTATE, 2 * SSD_HEADDIM), F32)],
        compiler_params=_params(("arbitrary", "arbitrary")),
        name="ssd",
    )(z, xbc, xbc, xbc, gates, gates_row, dskip, norm_w)


def _pad_lanes(v):
    v = v.reshape(1, -1).astype(F32)
    return jnp.pad(v, ((0, 0), (0, LANES - v.shape[1])))


def kernel(x, meta_tokens, ab_w_in, ab_ret_gn_w, ab_conv_q, ab_conv_k, ab_conv_v, ab_A_log, ab_dt_bias, ab_gdn_norm_w, ab_w_out, c_w_in, c_conv_w, c_conv_b, c_A_log, c_dt_bias, c_D, c_norm_w, c_w_out, mlp_w1, mlp_w2, ln1_w, ln1_b, ln2_w, ln2_b):
    batch, seq, d = x.shape
    assert d == D_MODEL and meta_tokens.shape == (N_META, D_MODEL)
    lp = PAD_FRONT + N_META + seq
    rows = batch * lp

    meta = jnp.broadcast_to(meta_tokens[None].astype(x.dtype), (batch, N_META, d))
    h = jnp.concatenate([jnp.zeros((batch, PAD_FRONT, d), x.dtype), meta, x], axis=1).reshape(rows, d)
    hb = h.astype(BF16)

    pos = jnp.arange(lp, dtype=F32) - PAD_FRONT
    inv_freq = 1.0 / (ROPE_BASE ** jnp.linspace(0.0, 1.0, RET_DK // 2, dtype=F32))
    ang = pos[:, None] * inv_freq[None]
    cosf = jnp.concatenate([jnp.cos(ang), jnp.cos(ang)], axis=-1)
    sinf = jnp.concatenate([-jnp.sin(ang), jnp.sin(ang)], axis=-1)

    w_in = ab_w_in[0]
    w_main = jnp.concatenate([w_in[:, :5120], w_in[:, 5128:]], axis=1).astype(BF16)
    w_gate = jnp.pad(w_in[:, 5120:5128], ((0, 0), (0, LANES - 8))).astype(BF16)
    ones = jnp.ones((1, COL_TILE // 2), F32)
    zeros = jnp.zeros((1, COL_TILE), F32)
    p_qk = _proj(hb, w_main, (lambda j: j, 1), _epilogue_rope,
                 [cosf, sinf, jnp.concatenate([ones * RET_DK ** -0.5, ones], axis=1)],
                 [_per_row_in_batch(LANES), _per_row_in_batch(LANES), _per_col()], lp, "proj_ret_qk")
    p_plain = _proj(hb, w_main, (lambda j: jnp.where(j < 2, j + 1, 5), 3), _epilogue_plain,
                    [], [], lp, "proj_ab_plain")
    g_qk = _proj(hb, w_main, (lambda j: j + 3, 1), functools.partial(_epilogue_conv, l2norm=True),
                 [jnp.concatenate([ab_conv_q[0], ab_conv_k[0]], axis=1), zeros,
                  jnp.concatenate([ones * GDN_DK ** -0.5, ones], axis=1)],
                 [_per_col(rows=CONV_K), _per_col(), _per_col()], lp, "proj_gdn_qk")
    g_v = _proj(hb, w_main, (lambda j: j + 4, 1), functools.partial(_epilogue_conv, l2norm=False),
                [ab_conv_v[0], zeros, zeros], [_per_col(rows=CONV_K), _per_col(), _per_col()], lp, "proj_gdn_v")
    alog = _pad_lanes(jnp.concatenate([jnp.zeros((GDN_HEADS,), F32), ab_A_log[0]]))
    bias = _pad_lanes(jnp.concatenate([jnp.zeros((GDN_HEADS,), F32), ab_dt_bias[0]]))
    g0 = _gates(_gates_ab_kernel, hb, w_gate, alog, bias, lp, "gates_ab")
    y_ret = _retention(p_qk, p_plain, ab_ret_gn_w[0].reshape(1, -1), batch, lp)
    y_gdn = _gdn(g_qk, g_v, p_plain, g0, ab_gdn_norm_w[0].reshape(1, -1), batch, lp)
    h = _outproj_ln(y_ret, y_gdn, 0, 0, ab_w_out[0].astype(BF16), h,
                    ln1_w[0].reshape(1, -1), ln1_b[0].reshape(1, -1), lp)
    h, hb = _mlp_ln(h, mlp_w1[0].astype(BF16), mlp_w2[0].astype(BF16),
                    ln2_w[0].reshape(1, -1), ln2_b[0].reshape(1, -1), lp)

    w_in = c_w_in[0]
    w_main = w_in[:, :5120].astype(BF16)
    w_dt = w_in[:, 5120:]
    w_gate = jnp.pad(jnp.concatenate([w_dt, w_dt], axis=1), ((0, 0), (0, LANES - 2 * SSD_HEADS))).astype(BF16)
    p_z = _proj(hb, w_main, (lambda j: j, 2), _epilogue_plain, [], [], lp, "proj_ssd_z")
    p_xbc = _proj(hb, w_main, (lambda j: j + 2, 3), functools.partial(_epilogue_conv, l2norm=False),
                  [c_conv_w[0], c_conv_b[0].reshape(1, -1), jnp.zeros((1, 3 * COL_TILE), F32)],
                  [_per_col(rows=CONV_K), _per_col(), _per_col()], lp, "proj_ssd_xbc")
    alog = _pad_lanes(jnp.concatenate([jnp.zeros((SSD_HEADS,), F32), c_A_log[0]]))
    bias = _pad_lanes(jnp.concatenate([c_dt_bias[0], c_dt_bias[0]]))
    g1 = _gates(_gates_c_kernel, hb, w_gate, alog, bias, lp, "gates_c")
    g1_row = jnp.swapaxes(g1[:, :2 * SSD_HEADS].reshape(batch, lp, 2 * SSD_HEADS), 1, 2)
    dskip = jnp.repeat(c_D[0].astype(F32), SSD_HEADDIM).reshape(1, -1)
    y_ssd = _ssd(p_z, p_xbc, g1, g1_row, dskip, c_norm_w[0].reshape(1, -1), batch, lp)
    h = _outproj_ln(y_ssd, y_ssd, 0, 1, c_w_out[0].astype(BF16), h,
                    ln1_w[1].reshape(1, -1), ln1_b[1].reshape(1, -1), lp)
    out = _mlp_ln_final(h, mlp_w1[1].astype(BF16), mlp_w2[1].astype(BF16),
                        ln2_w[1].reshape(1, -1), ln2_b[1].reshape(1, -1), batch, seq, lp)
    return out.reshape(batch, seq, d)
```

```python
import functools

import jax
import jax.numpy as jnp
from jax import lax
from jax.experimental import pallas as pl
from jax.experimental.pallas import tpu as pltpu

F32 = jnp.float32
BF16 = jnp.bfloat16
HI = lax.Precision.HIGHEST

D_MODEL = 1024
DEPTH = 2
N_META = 16
CONV_K = 4
RET_HEADS = 4
RET_DK = 128
RET_DV = 256
ROPE_BASE = 10000.0
GDN_HEADS = 4
GDN_DK = 128
GDN_DV = 256
SSD_DINNER = 2048
SSD_HEADDIM = 64
SSD_HEADS = 32
SSD_GROUPS = 4
SSD_HPG = 8
SSD_DSTATE = 128
SSD_GN = 512
D_FF = 4096
DN_ALPHA = (2 * DEPTH) ** 0.25
LN_EPS = 1e-5
GN_EPS = 1e-5
RMS_EPS = 1e-6

LANES = 128
COL_TILE = 1024
CHUNK = 128
GDN_CHUNK = 64
GDN_SUBCHUNKS = 2
PAD_FRONT = CHUNK - N_META
HALO = 8
CONV_ROWS = 64
VMEM_LIMIT = 56 * 1024 * 1024


def _row_tile(rows_per_batch):
    for tm in (640, 512, 256, 128):
        if rows_per_batch % tm == 0:
            return tm
    raise ValueError(f"unsupported padded sequence length {rows_per_batch}")


def _params(sem):
    return pltpu.CompilerParams(dimension_semantics=sem, vmem_limit_bytes=VMEM_LIMIT)


def _softplus(x):
    return jnp.maximum(x, 0.0) + jnp.log1p(jnp.exp(-jnp.abs(x)))


def _silu(x):
    return x * jax.nn.sigmoid(x)


def _dot(a, b):
    return jnp.dot(a, b, preferred_element_type=F32)


def _dot_nt(a, b):
    return lax.dot_general(a, b, (((1,), (1,)), ((), ())), preferred_element_type=F32)


def _dot_hi(a, b):
    return jnp.dot(a, b, preferred_element_type=F32, precision=HI)


def _wide(a):
    hi = a.astype(BF16).astype(F32)
    return jnp.concatenate([a, a - hi], axis=1).astype(BF16)


def _tall(b):
    hi = b.astype(BF16)
    return jnp.concatenate([hi, hi], axis=0)


def _epilogue_plain(acc_ref, o_ref, *, tm):
    o_ref[...] = acc_ref[HALO:HALO + tm, :].astype(BF16)


def _epilogue_rope(acc_ref, o_ref, cos_ref, sin_ref, scale_ref, *, tm):
    for hd in range(COL_TILE // RET_DK):
        cols = slice(RET_DK * hd, RET_DK * (hd + 1))
        x = acc_ref[HALO:HALO + tm, cols]
        y = x * cos_ref[...] + pltpu.roll(x, RET_DK // 2, 1) * sin_ref[...]
        o_ref[:, cols] = (y * scale_ref[:, cols]).astype(BF16)


def _epilogue_conv(acc_ref, o_ref, cw_ref, cb_ref, scale_ref, *, tm, l2norm):
    for r in range(0, tm, CONV_ROWS):
        rows = slice(r, r + CONV_ROWS)
        for c in range(0, COL_TILE, LANES):
            cols = slice(c, c + LANES)
            w = [cw_ref[tap:tap + 1, cols] for tap in range(CONV_K)]
            xw = acc_ref[r:r + CONV_ROWS + HALO, cols]
            s1 = pltpu.roll(xw, 1, 0)
            y = w[3] * xw + w[2] * s1 + pltpu.roll(w[1] * xw + w[0] * s1, 2, 0)
            y = _silu(y[HALO:] + cb_ref[:, cols])
            if l2norm:
                inv = lax.rsqrt(jnp.sum(y * y, axis=-1, keepdims=True) + 1e-6)
                y = y * (inv * scale_ref[:, cols])
            o_ref[rows, cols] = y.astype(BF16)


def _lagged_steps(i, n_tiles, acc_a, acc_b, produce, consume):
    last_prev = acc_b if n_tiles % 2 == 0 else acc_a

    @pl.when(i == 0)
    def _():
        acc_b[...] = jnp.zeros(acc_b.shape, F32)

    @pl.when(jnp.logical_and(i % 2 == 0, i < n_tiles))
    def _():
        produce(acc_a, acc_b)
        consume(acc_b)

    @pl.when(jnp.logical_and(i % 2 == 1, i < n_tiles))
    def _():
        produce(acc_b, acc_a)
        consume(acc_a)

    @pl.when(i == n_tiles)
    def _():
        consume(last_prev)


def _proj_kernel(*refs, tm, tiles_per_batch, n_tiles, n_extra, epilogue):
    h_ref, w_ref = refs[0], refs[1]
    extra = refs[2:2 + n_extra]
    o_ref, acc_a, acc_b = refs[2 + n_extra:]
    i = pl.program_id(1)
    starts_batch = (i % tiles_per_batch) == 0

    def produce(cur, prev):
        cur[HALO:HALO + tm, :] = _dot(h_ref[...], w_ref[...])
        cur[0:HALO, :] = jnp.where(starts_batch, 0.0, prev[tm:tm + HALO, :])

    _lagged_steps(i, n_tiles, acc_a, acc_b, produce, lambda prev: epilogue(prev, o_ref, *extra, tm=tm))


def _proj(hb, w, col_tiles, epilogue, extra, extra_specs, rows_per_batch, name):
    rows = hb.shape[0]
    tm = _row_tile(rows_per_batch)
    tpb = rows_per_batch // tm
    n_tiles = rows // tm
    col_fn, ncol = col_tiles
    prev = lambda i: jnp.maximum(i - 1, 0)
    return pl.pallas_call(
        functools.partial(_proj_kernel, tm=tm, tiles_per_batch=tpb, n_tiles=n_tiles, n_extra=len(extra),
                          epilogue=epilogue),
        grid=(ncol, n_tiles + 1),
        in_specs=[
            pl.BlockSpec((tm, D_MODEL), lambda j, i: (jnp.minimum(i, n_tiles - 1), 0)),
            pl.BlockSpec((D_MODEL, COL_TILE), lambda j, i: (0, col_fn(j))),
        ] + [spec(tm, tpb, prev) for spec in extra_specs],
        out_specs=pl.BlockSpec((tm, COL_TILE), lambda j, i: (prev(i), j)),
        out_shape=jax.ShapeDtypeStruct((rows, ncol * COL_TILE), BF16),
        scratch_shapes=[pltpu.VMEM((tm + HALO, COL_TILE), F32), pltpu.VMEM((tm + HALO, COL_TILE), F32)],
        compiler_params=_params(("arbitrary", "arbitrary")),
        name=name,
    )(hb, w, *extra)


def _per_col(width=COL_TILE, rows=1):
    return lambda tm, tpb, prev: pl.BlockSpec((rows, width), lambda j, i: (0, j))


def _per_row_in_batch(width):
    return lambda tm, tpb, prev: pl.BlockSpec((tm, width), lambda j, i: (prev(i) % tpb, 0))


def _gates_ab_kernel(h_ref, w_ref, alog_ref, bias_ref, o_ref, *, tm, tiles_per_batch):
    acc = _dot(h_ref[...], w_ref[...])
    lane = lax.broadcasted_iota(jnp.int32, acc.shape, 1)
    decay = -jnp.exp(alog_ref[...]) * _softplus(acc + bias_ref[...])
    o_ref[...] = jnp.where(lane < GDN_HEADS, jax.nn.sigmoid(acc), decay)


def _gates_c_kernel(h_ref, w_ref, alog_ref, bias_ref, o_ref, *, tm, tiles_per_batch):
    acc = _dot(h_ref[...], w_ref[...])
    lane = lax.broadcasted_iota(jnp.int32, acc.shape, 1)
    row = (pl.program_id(0) % tiles_per_batch) * tm + lax.broadcasted_iota(jnp.int32, (tm, 1), 0)
    dt = jnp.where(row >= PAD_FRONT, _softplus(acc + bias_ref[...]), 0.0)
    o_ref[...] = jnp.where(lane < SSD_HEADS, dt, dt * -jnp.exp(alog_ref[...]))


def _gates(kernel_fn, hb, w, alog, bias, rows_per_batch, name):
    rows = hb.shape[0]
    tm = _row_tile(rows_per_batch)
    vec = pl.BlockSpec((1, LANES), lambda i: (0, 0))
    return pl.pallas_call(
        functools.partial(kernel_fn, tm=tm, tiles_per_batch=rows_per_batch // tm),
        grid=(rows // tm,),
        in_specs=[
            pl.BlockSpec((tm, D_MODEL), lambda i: (i, 0)),
            pl.BlockSpec((D_MODEL, LANES), lambda i: (0, 0)),
            vec, vec,
        ],
        out_specs=pl.BlockSpec((tm, LANES), lambda i: (i, 0)),
        out_shape=jax.ShapeDtypeStruct((rows, LANES), F32),
        compiler_params=_params(("arbitrary",)),
        name=name,
    )(hb, w, alog, bias)


def _retention_kernel(q_ref, k_ref, v_ref, g_ref, gnw_ref, o_ref, s_ref, *, batch):
    n = CHUNK

    @pl.when(pl.program_id(0) == 0)
    def _():
        s_ref[...] = jnp.zeros(s_ref.shape, F32)

    t_idx = lax.broadcasted_iota(jnp.int32, (n, n), 0)
    s_idx = lax.broadcasted_iota(jnp.int32, (n, n), 1)
    t_col = lax.broadcasted_iota(jnp.int32, (n, 1), 0).astype(F32)
    gap = (t_idx - s_idx).astype(F32)
    chains = [(b, hd) for b in range(batch) for hd in range(RET_HEADS)]
    cs = range(len(chains))
    log_gamma = [jnp.log1p(-jnp.exp2(jnp.full((1, 1), -5.0 - hd, F32))) for hd in range(RET_HEADS)]
    decay = [jnp.exp(jnp.where(t_idx >= s_idx, gap * lg, -jnp.inf)) for lg in log_gamma]
    e_in = [jnp.exp((t_col + 1.0) * lg) for lg in log_gamma]
    e_out = [jnp.exp((n - 1.0 - t_col) * lg) for lg in log_gamma]
    e_all = [jnp.exp(n * lg) for lg in log_gamma]
    q = [q_ref[b, :, RET_DK * hd:RET_DK * (hd + 1)] for b, hd in chains]
    k = [k_ref[b, :, RET_DK * hd:RET_DK * (hd + 1)] for b, hd in chains]
    v = [v_ref[b, :, RET_DV * hd:RET_DV * (hd + 1)] for b, hd in chains]
    states = [s_ref[c] for c in cs]
    att = [(_dot_nt(q[c], k[c]) * decay[chains[c][1]]).astype(BF16) for c in cs]
    q_in = [(q[c].astype(F32) * e_in[chains[c][1]]).astype(BF16) for c in cs]
    o = [_dot(att[c], v[c]) + _dot(q_in[c], states[c].astype(BF16)) for c in cs]
    k_out = [(k[c].astype(F32) * e_out[chains[c][1]]).T.astype(BF16) for c in cs]
    for c in cs:
        s_ref[c] = states[c] * e_all[chains[c][1]] + _dot(k_out[c], v[c])
    for c, (b, hd) in enumerate(chains):
        cols = slice(RET_DV * hd, RET_DV * (hd + 1))
        oc = o[c] - jnp.mean(o[c], axis=-1, keepdims=True)
        on = oc * lax.rsqrt(jnp.mean(oc * oc, axis=-1, keepdims=True) + GN_EPS)
        gate = _silu(g_ref[b, :, cols].astype(F32))
        o_ref[b, :, cols] = (on * gnw_ref[:, cols] * gate).astype(BF16)


def _retention(qk, vg, gn_w, batch, rows_per_batch):
    n = CHUNK
    nc = rows_per_batch // n
    qk = qk.reshape(batch, rows_per_batch, qk.shape[1])
    vg = vg.reshape(batch, rows_per_batch, vg.shape[1])
    out = pl.pallas_call(
        functools.partial(_retention_kernel, batch=batch),
        grid=(nc,),
        in_specs=[
            pl.BlockSpec((batch, n, 512), lambda c: (0, c, 0)),
            pl.BlockSpec((batch, n, 512), lambda c: (0, c, 1)),
            pl.BlockSpec((batch, n, 1024), lambda c: (0, c, 0)),
            pl.BlockSpec((batch, n, 1024), lambda c: (0, c, 1)),
            pl.BlockSpec((1, 1024), lambda c: (0, 0)),
        ],
        out_specs=pl.BlockSpec((batch, n, 1024), lambda c: (0, c, 0)),
        out_shape=jax.ShapeDtypeStruct((batch, rows_per_batch, 1024), BF16),
        scratch_shapes=[pltpu.VMEM((batch * RET_HEADS, RET_DK, RET_DV), F32)],
        compiler_params=_params(("arbitrary",)),
        name="retention",
    )(qk, qk, vg, vg, gn_w)
    return out.reshape(batch * rows_per_batch, 1024)


def _gdn_kernel(q_ref, k_ref, v_ref, g_ref, gcol_ref, grow_ref, nw_ref, o_ref, s_ref, *, batch):
    n = GDN_CHUNK

    @pl.when(pl.program_id(0) == 0)
    def _():
        s_ref[...] = jnp.zeros(s_ref.shape, F32)

    t_idx = lax.broadcasted_iota(jnp.int32, (n, n), 0)
    s_idx = lax.broadcasted_iota(jnp.int32, (n, n), 1)
    lower = (t_idx >= s_idx).astype(F32)
    upper = (t_idx <= s_idx).astype(F32)
    eye = (t_idx == s_idx).astype(F32)
    blocks = [(b, sub) for b in range(batch) for sub in range(GDN_SUBCHUNKS)]
    chains = [(b, sub, hd) for b, sub in blocks for hd in range(GDN_HEADS)]
    cs = range(len(chains))
    rows = lambda sub: slice(n * sub, n * (sub + 1))
    gcol = {bs: gcol_ref[bs[0], rows(bs[1])] for bs in blocks}
    gcum_col = {bs: _dot_hi(lower, gcol[bs]) for bs in blocks}
    gcum_row = {bs: _dot_hi(grow_ref[bs[0], bs[1]], upper) for bs in blocks}
    q = [q_ref[b, rows(sub), GDN_DK * hd:GDN_DK * (hd + 1)].astype(F32) for b, sub, hd in chains]
    k = [k_ref[b, rows(sub), GDN_DK * hd:GDN_DK * (hd + 1)].astype(F32) for b, sub, hd in chains]
    v = [v_ref[b, rows(sub), GDN_DV * hd:GDN_DV * (hd + 1)].astype(F32) for b, sub, hd in chains]
    beta = [gcol[b, sub][:, hd:hd + 1] for b, sub, hd in chains]
    gc = [gcum_col[b, sub][:, GDN_HEADS + hd:GDN_HEADS + hd + 1] for b, sub, hd in chains]
    gr = [gcum_row[b, sub][GDN_HEADS + hd:GDN_HEADS + hd + 1, :] for b, sub, hd in chains]
    g_last = [gc[c][n - 1:n, :] for c in cs]
    seg = [gc[c] - gr[c] for c in cs]
    dec_strict = [jnp.exp(jnp.where(t_idx > s_idx, seg[c], -jnp.inf)) for c in cs]
    dec_incl = [jnp.exp(jnp.where(t_idx >= s_idx, seg[c], -jnp.inf)) for c in cs]
    kb = [k[c] * beta[c] for c in cs]
    kq = [_dot_nt(jnp.concatenate([kb[c], q[c]], axis=0).astype(BF16), k[c].astype(BF16)) for c in cs]
    a = [kq[c][:n] * dec_strict[c] for c in cs]
    att = [(kq[c][n:] * dec_incl[c]).astype(BF16) for c in cs]
    inv = [eye - a[c] for c in cs]
    p = [_dot(_wide(a[c]), _tall(a[c])) for c in cs]
    span = 4
    while span < n:
        r = [_dot(_wide(jnp.concatenate([p[c], inv[c]], axis=0)), _tall(p[c])) for c in cs]
        inv = [inv[c] + r[c][n:] for c in cs]
        p = [r[c][:n] for c in cs]
        span *= 2
    inv = [inv[c] + _dot(_wide(inv[c]), _tall(p[c])) for c in cs]
    e_gc = [jnp.exp(gc[c]) for c in cs]
    rhs = [jnp.concatenate([v[c] * beta[c], kb[c] * e_gc[c]], axis=-1) for c in cs]
    u = [_dot(_wide(inv[c]), _tall(rhs[c])) for c in cs]
    lhs = [jnp.concatenate([u[c][:, GDN_DV:], q[c] * e_gc[c]], axis=0).astype(BF16) for c in cs]
    k_out = [(k[c] * jnp.exp(g_last[c] - gc[c])).T.astype(BF16) for c in cs]
    e_all = [jnp.exp(g_last[c]) for c in cs]

    lanes = [(b, hd) for b in range(batch) for hd in range(GDN_HEADS)]
    states = [s_ref[i] for i in range(len(lanes))]
    for sub in range(GDN_SUBCHUNKS):
        idx = [chains.index((b, sub, hd)) for b, hd in lanes]
        ws = [_dot(lhs[c], states[i].astype(BF16)) for i, c in enumerate(idx)]
        v_new = [(u[c][:, :GDN_DV] - ws[i][:n]).astype(BF16) for i, c in enumerate(idx)]
        o = [ws[i][n:] + _dot(att[c], v_new[i]) for i, c in enumerate(idx)]
        states = [states[i] * e_all[c] + _dot(k_out[c], v_new[i]) for i, c in enumerate(idx)]
        for i, (b, hd) in enumerate(lanes):
            cols = slice(GDN_DV * hd, GDN_DV * (hd + 1))
            on = o[i] * lax.rsqrt(jnp.mean(o[i] * o[i], axis=-1, keepdims=True) + RMS_EPS)
            gate = _silu(g_ref[b, rows(sub), cols].astype(F32))
            o_ref[b, rows(sub), cols] = (on * nw_ref[...] * gate).astype(BF16)
    for i in range(len(lanes)):
        s_ref[i] = states[i]


def _gdn(qk, v, plain, gates, norm_w, batch, rows_per_batch):
    n = GDN_CHUNK
    step_rows = n * GDN_SUBCHUNKS
    assert rows_per_batch % step_rows == 0
    nc = rows_per_batch // n
    qk = qk.reshape(batch, rows_per_batch, qk.shape[1])
    v = v.reshape(batch, rows_per_batch, v.shape[1])
    plain = plain.reshape(batch, rows_per_batch, plain.shape[1])
    gates = gates.reshape(batch, rows_per_batch, LANES)
    gates_row = jnp.swapaxes(gates[:, :, :8].reshape(batch, nc, n, 8), 2, 3)
    out = pl.pallas_call(
        functools.partial(_gdn_kernel, batch=batch),
        grid=(rows_per_batch // step_rows,),
        in_specs=[
            pl.BlockSpec((batch, step_rows, 512), lambda c: (0, c, 0)),
            pl.BlockSpec((batch, step_rows, 512), lambda c: (0, c, 1)),
            pl.BlockSpec((batch, step_rows, 1024), lambda c: (0, c, 0)),
            pl.BlockSpec((batch, step_rows, 1024), lambda c: (0, c, 2)),
            pl.BlockSpec((batch, step_rows, LANES), lambda c: (0, c, 0)),
            pl.BlockSpec((batch, GDN_SUBCHUNKS, 8, n), lambda c: (0, c, 0, 0)),
            pl.BlockSpec((1, GDN_DV), lambda c: (0, 0)),
        ],
        out_specs=pl.BlockSpec((batch, step_rows, 1024), lambda c: (0, c, 0)),
        out_shape=jax.ShapeDtypeStruct((batch, rows_per_batch, 1024), BF16),
        scratch_shapes=[pltpu.VMEM((batch * GDN_HEADS, GDN_DK, GDN_DV), F32)],
        compiler_params=_params(("arbitrary",)),
        name="gated_delta",
    )(qk, qk, v, plain, gates, gates_row, norm_w)
    return out.reshape(batch * rows_per_batch, 1024)


def _layer_norm(x, w, b):
    xc = x - jnp.mean(x, axis=-1, keepdims=True)
    return xc * lax.rsqrt(jnp.mean(xc * xc, axis=-1, keepdims=True) + LN_EPS) * w + b


def _outproj_ln_kernel(ya_ref, yb_ref, w_ref, h_ref, lw_ref, lb_ref, o_ref, acc_a, acc_b, *, n_tiles):
    half = ya_ref.shape[1]

    def produce(cur, prev):
        acc = _dot(ya_ref[...], w_ref[0:half, :]) + _dot(yb_ref[...], w_ref[half:2 * half, :])
        cur[...] = DN_ALPHA * h_ref[...] + acc

    def consume(prev):
        o_ref[...] = _layer_norm(prev[...], lw_ref[...], lb_ref[...])

    _lagged_steps(pl.program_id(0), n_tiles, acc_a, acc_b, produce, consume)


def _outproj_ln(ya, yb, col_a, col_b, w, h, lw, lb, rows_per_batch):
    rows = h.shape[0]
    tm = _row_tile(rows_per_batch)
    n_tiles = rows // tm
    half = w.shape[0] // 2
    cur = lambda i: jnp.minimum(i, n_tiles - 1)
    prev = lambda i: jnp.maximum(i - 1, 0)
    vec = pl.BlockSpec((1, D_MODEL), lambda i: (0, 0))
    return pl.pallas_call(
        functools.partial(_outproj_ln_kernel, n_tiles=n_tiles),
        grid=(n_tiles + 1,),
        in_specs=[
            pl.BlockSpec((tm, half), lambda i: (cur(i), col_a)),
            pl.BlockSpec((tm, half), lambda i: (cur(i), col_b)),
            pl.BlockSpec((2 * half, D_MODEL), lambda i: (0, 0), pipeline_mode=pl.Buffered(1)),
            pl.BlockSpec((tm, D_MODEL), lambda i: (cur(i), 0)),
            vec, vec,
        ],
        out_specs=pl.BlockSpec((tm, D_MODEL), lambda i: (prev(i), 0)),
        out_shape=jax.ShapeDtypeStruct((rows, D_MODEL), F32),
        scratch_shapes=[pltpu.VMEM((tm, D_MODEL), F32), pltpu.VMEM((tm, D_MODEL), F32)],
        compiler_params=_params(("arbitrary",)),
        name="outproj_ln",
    )(ya, yb, w, h, lw, lb)


def _mlp(h_ref, w1_ref, w2_ref, lw_ref, lb_ref):
    h = h_ref[...]
    hb = h.astype(BF16)
    acc = jnp.zeros(h.shape, F32)
    step = 1024
    for f in range(0, D_FF, step):
        a = _dot(hb, w1_ref[:, f:f + step])
        a = jnp.square(jnp.maximum(a, 0.0)).astype(BF16)
        acc = acc + _dot(a, w2_ref[f:f + step, :])
    return _layer_norm(DN_ALPHA * h + acc, lw_ref[...], lb_ref[...])


def _mlp_ln_kernel(h_ref, w1_ref, w2_ref, lw_ref, lb_ref, o_ref, ob_ref, *, tm, tiles_per_batch):
    out = _mlp(h_ref, w1_ref, w2_ref, lw_ref, lb_ref)
    o_ref[...] = out
    row = (pl.program_id(0) % tiles_per_batch) * tm + lax.broadcasted_iota(jnp.int32, (tm, 1), 0)
    ob_ref[...] = jnp.where(row >= PAD_FRONT, out, 0.0).astype(BF16)


def _mlp_ln_final_kernel(h_ref, w1_ref, w2_ref, lw_ref, lb_ref, o_ref):
    o_ref[...] = _mlp(h_ref, w1_ref, w2_ref, lw_ref, lb_ref)


def _mlp_weight_specs(index):
    return [
        pl.BlockSpec((D_MODEL, D_FF), index, pipeline_mode=pl.Buffered(1)),
        pl.BlockSpec((D_FF, D_MODEL), index, pipeline_mode=pl.Buffered(1)),
        pl.BlockSpec((1, D_MODEL), index),
        pl.BlockSpec((1, D_MODEL), index),
    ]


def _mlp_ln(h, w1, w2, lw, lb, rows_per_batch):
    rows = h.shape[0]
    tm = _row_tile(rows_per_batch)
    tile = pl.BlockSpec((tm, D_MODEL), lambda i: (i, 0))
    return pl.pallas_call(
        functools.partial(_mlp_ln_kernel, tm=tm, tiles_per_batch=rows_per_batch // tm),
        grid=(rows // tm,),
        in_specs=[tile] + _mlp_weight_specs(lambda i: (0, 0)),
        out_specs=[tile, tile],
        out_shape=[jax.ShapeDtypeStruct((rows, D_MODEL), F32), jax.ShapeDtypeStruct((rows, D_MODEL), BF16)],
        compiler_params=_params(("arbitrary",)),
        name="mlp_ln",
    )(h, w1, w2, lw, lb)


def _mlp_ln_final(h, w1, w2, lw, lb, batch, seq, rows_per_batch):
    tm = next(t for t in (1024, 512, 256, 128) if seq % t == 0)
    tiles = seq // tm
    first_row = PAD_FRONT + N_META
    return pl.pallas_call(
        _mlp_ln_final_kernel,
        grid=(batch, tiles),
        in_specs=[pl.BlockSpec((pl.Element(tm), pl.Element(D_MODEL)),
                               lambda b, i: (pl.multiple_of(b * rows_per_batch + first_row + i * tm, 128), 0))]
        + _mlp_weight_specs(lambda b, i: (0, 0)),
        out_specs=pl.BlockSpec((tm, D_MODEL), lambda b, i: (b * tiles + i, 0)),
        out_shape=jax.ShapeDtypeStruct((batch * seq, D_MODEL), F32),
        compiler_params=_params(("arbitrary", "arbitrary")),
        name="mlp_ln_final",
    )(h, w1, w2, lw, lb)


def _ssd_kernel(z_ref, x_ref, b_ref, c_ref, gcol_ref, grow_ref, dskip_ref, nw_ref, o_ref, s_ref):
    cidx = pl.program_id(1)
    n = CHUNK
    pair_w = 2 * SSD_HEADDIM
    pairs_per_group = SSD_HPG // 2
    group_w = SSD_HPG * SSD_HEADDIM

    @pl.when(cidx == 0)
    def _():
        s_ref[...] = jnp.zeros(s_ref.shape, F32)

    t_idx = lax.broadcasted_iota(jnp.int32, (n, n), 0)
    s_idx = lax.broadcasted_iota(jnp.int32, (n, n), 1)
    causal = t_idx >= s_idx
    lower = causal.astype(F32)
    upper = (t_idx <= s_idx).astype(F32)
    lane_v = lax.broadcasted_iota(jnp.int32, (n, pair_w), 1)
    lane_s = lax.broadcasted_iota(jnp.int32, (SSD_DSTATE, pair_w), 1)
    lane_1 = lax.broadcasted_iota(jnp.int32, (1, pair_w), 1)
    grow = grow_ref[0]
    gcum_col = _dot_hi(lower, gcol_ref[...])
    gcum_row = _dot_hi(grow, upper)
    for g in range(SSD_GROUPS):
        q = c_ref[:, SSD_DSTATE * g:SSD_DSTATE * (g + 1)]
        k = b_ref[:, SSD_DSTATE * g:SSD_DSTATE * (g + 1)]
        qf = q.astype(F32)
        kt = k.astype(F32).T
        cb = _dot_nt(q, k)
        ys = []
        for p in range(pairs_per_group):
            pair = g * pairs_per_group + p
            col = group_w * g + pair_w * p
            xv = x_ref[:, col:col + pair_w]
            state = s_ref[pair]
            o = jnp.zeros((n, pair_w), F32)
            upd = jnp.zeros((SSD_DSTATE, pair_w), F32)
            last = []
            for side in range(2):
                hd = 2 * pair + side
                gc = gcum_col[:, SSD_HEADS + hd:SSD_HEADS + hd + 1]
                gr = gcum_row[SSD_HEADS + hd:SSD_HEADS + hd + 1, :]
                dt_row = grow[hd:hd + 1, :]
                g_last = gr[:, n - 1:n]
                dec = jnp.exp(jnp.where(causal, gc - gr, -jnp.inf))
                att = cb * dec * dt_row
                q_in = qf * jnp.exp(gc)
                mine_v = (lane_v >= SSD_HEADDIM) if side else (lane_v < SSD_HEADDIM)
                mine_s = (lane_s >= SSD_HEADDIM) if side else (lane_s < SSD_HEADDIM)
                xm = jnp.where(mine_v, xv, jnp.zeros_like(xv))
                sm = jnp.where(mine_s, state, 0.0).astype(BF16)
                o = o + _dot(att.astype(BF16), xm) + _dot(q_in.astype(BF16), sm)
                k_out = kt * (jnp.exp(g_last - gr) * dt_row)
                upd = upd + _dot(k_out.astype(BF16), xm)
                last.append(jnp.exp(g_last))
            s_ref[pair] = state * jnp.where(lane_1 < SSD_HEADDIM, last[0], last[1]) + upd
            ys.append(o + xv.astype(F32) * dskip_ref[:, col:col + pair_w])
        y = jnp.concatenate(ys, axis=-1)
        y = y * _silu(z_ref[:, group_w * g:group_w * (g + 1)].astype(F32))
        y = y * lax.rsqrt(jnp.mean(y * y, axis=-1, keepdims=True) + RMS_EPS)
        o_ref[:, group_w * g:group_w * (g + 1)] = (y * nw_ref[:, group_w * g:group_w * (g + 1)]).astype(BF16)


def _ssd(z, xbc, gates, gates_row, dskip, norm_w, batch, rows_per_batch):
    n = CHUNK
    nc = rows_per_batch // n
    row = lambda b, c: b * nc + c
    return pl.pallas_call(
        _ssd_kernel,
        grid=(batch, nc),
        in_specs=[
            pl.BlockSpec((n, SSD_DINNER), lambda b, c: (row(b, c), 0)),
            pl.BlockSpec((n, SSD_DINNER), lambda b, c: (row(b, c), 0)),
            pl.BlockSpec((n, SSD_GN), lambda b, c: (row(b, c), 4)),
            pl.BlockSpec((n, SSD_GN), lambda b, c: (row(b, c), 5)),
            pl.BlockSpec((n, LANES), lambda b, c: (row(b, c), 0)),
            pl.BlockSpec((1, 2 * SSD_HEADS, n), lambda b, c: (b, 0, c)),
            pl.BlockSpec((1, SSD_DINNER), lambda b, c: (0, 0)),
            pl.BlockSpec((1, SSD_DINNER), lambda b, c: (0, 0)),
        ],
        out_specs=pl.BlockSpec((n, SSD_DINNER), lambda b, c: (row(b, c), 0)),
        out_shape=jax.ShapeDtypeStruct((z.shape[0], SSD_DINNER), BF16),
        scratch_shapes=[pltpu.VMEM((SSD_HEADS // 2, SSD_DSTATE, 2 * SSD_HEADDIM), F32)],
        compiler_params=_params(("arbitrary", "arbitrary")),
        name="ssd",
    )(z, xbc, xbc, xbc, gates, gates_row, dskip, norm_w)


def _pad_lanes(v):
    v = v.reshape(1, -1).astype(F32)
    return jnp.pad(v, ((0, 0), (0, LANES - v.shape[1])))


def kernel(x, meta_tokens, ab_w_in, ab_ret_gn_w, ab_conv_q, ab_conv_k, ab_conv_v, ab_A_log, ab_dt_bias, ab_gdn_norm_w, ab_w_out, c_w_in, c_conv_w, c_conv_b, c_A_log, c_dt_bias, c_D, c_norm_w, c_w_out, mlp_w1, mlp_w2, ln1_w, ln1_b, ln2_w, ln2_b):
    batch, seq, d = x.shape
    assert d == D_MODEL and meta_tokens.shape == (N_META, D_MODEL)
    lp = PAD_FRONT + N_META + seq
    rows = batch * lp

    meta = jnp.broadcast_to(meta_tokens[None].astype(x.dtype), (batch, N_META, d))
    h = jnp.concatenate([jnp.zeros((batch, PAD_FRONT, d), x.dtype), meta, x], axis=1).reshape(rows, d)
    hb = h.astype(BF16)

    pos = jnp.arange(lp, dtype=F32) - PAD_FRONT
    inv_freq = 1.0 / (ROPE_BASE ** jnp.linspace(0.0, 1.0, RET_DK // 2, dtype=F32))
    ang = pos[:, None] * inv_freq[None]
    cosf = jnp.concatenate([jnp.cos(ang), jnp.cos(ang)], axis=-1)
    sinf = jnp.concatenate([-jnp.sin(ang), jnp.sin(ang)], axis=-1)

    w_in = ab_w_in[0]
    w_main = jnp.concatenate([w_in[:, :5120], w_in[:, 5128:]], axis=1).astype(BF16)
    w_gate = jnp.pad(w_in[:, 5120:5128], ((0, 0), (0, LANES - 8))).astype(BF16)
    ones = jnp.ones((1, COL_TILE // 2), F32)
    zeros = jnp.zeros((1, COL_TILE), F32)
    p_qk = _proj(hb, w_main, (lambda j: j, 1), _epilogue_rope,
                 [cosf, sinf, jnp.concatenate([ones * RET_DK ** -0.5, ones], axis=1)],
                 [_per_row_in_batch(LANES), _per_row_in_batch(LANES), _per_col()], lp, "proj_ret_qk")
    p_plain = _proj(hb, w_main, (lambda j: jnp.where(j < 2, j + 1, 5), 3), _epilogue_plain,
                    [], [], lp, "proj_ab_plain")
    g_qk = _proj(hb, w_main, (lambda j: j + 3, 1), functools.partial(_epilogue_conv, l2norm=True),
                 [jnp.concatenate([ab_conv_q[0], ab_conv_k[0]], axis=1), zeros,
                  jnp.concatenate([ones * GDN_DK ** -0.5, ones], axis=1)],
                 [_per_col(rows=CONV_K), _per_col(), _per_col()], lp, "proj_gdn_qk")
    g_v = _proj(hb, w_main, (lambda j: j + 4, 1), functools.partial(_epilogue_conv, l2norm=False),
                [ab_conv_v[0], zeros, zeros], [_per_col(rows=CONV_K), _per_col(), _per_col()], lp, "proj_gdn_v")
    alog = _pad_lanes(jnp.concatenate([jnp.zeros((GDN_HEADS,), F32), ab_A_log[0]]))
    bias = _pad_lanes(jnp.concatenate([jnp.zeros((GDN_HEADS,), F32), ab_dt_bias[0]]))
    g0 = _gates(_gates_ab_kernel, hb, w_gate, alog, bias, lp, "gates_ab")
    y_ret = _retention(p_qk, p_plain, ab_ret_gn_w[0].reshape(1, -1), batch, lp)
    y_gdn = _gdn(g_qk, g_v, p_plain, g0, ab_gdn_norm_w[0].reshape(1, -1), batch, lp)
    h = _outproj_ln(y_ret, y_gdn, 0, 0, ab_w_out[0].astype(BF16), h,
                    ln1_w[0].reshape(1, -1), ln1_b[0].reshape(1, -1), lp)
    h, hb = _mlp_ln(h, mlp_w1[0].astype(BF16), mlp_w2[0].astype(BF16),
                    ln2_w[0].reshape(1, -1), ln2_b[0].reshape(1, -1), lp)

    w_in = c_w_in[0]
    w_main = w_in[:, :5120].astype(BF16)
    w_dt = w_in[:, 5120:]
    w_gate = jnp.pad(jnp.concatenate([w_dt, w_dt], axis=1), ((0, 0), (0, LANES - 2 * SSD_HEADS))).astype(BF16)
    p_z = _proj(hb, w_main, (lambda j: j, 2), _epilogue_plain, [], [], lp, "proj_ssd_z")
    p_xbc = _proj(hb, w_main, (lambda j: j + 2, 3), functools.partial(_epilogue_conv, l2norm=False),
                  [c_conv_w[0], c_conv_b[0].reshape(1, -1), jnp.zeros((1, 3 * COL_TILE), F32)],
                  [_per_col(rows=CONV_K), _per_col(), _per_col()], lp, "proj_ssd_xbc")
    alog = _pad_lanes(jnp.concatenate([jnp.zeros((SSD_HEADS,), F32), c_A_log[0]]))
    bias = _pad_lanes(jnp.concatenate([c_dt_bias[0], c_dt_bias[0]]))
    g1 = _gates(_gates_c_kernel, hb, w_gate, alog, bias, lp, "gates_c")
    g1_row = jnp.swapaxes(g1[:, :2 * SSD_HEADS].reshape(batch, lp, 2 * SSD_HEADS), 1, 2)
    dskip = jnp.repeat(c_D[0].astype(F32), SSD_HEADDIM).reshape(1, -1)
    y_ssd = _ssd(p_z, p_xbc, g1, g1_row, dskip, c_norm_w[0].reshape(1, -1), batch, lp)
    h = _outproj_ln(y_ssd, y_ssd, 0, 1, c_w_out[0].astype(BF16), h,
                    ln1_w[1].reshape(1, -1), ln1_b[1].reshape(1, -1), lp)
    out = _mlp_ln_final(h, mlp_w1[1].astype(BF16), mlp_w2[1].astype(BF16),
                        ln2_w[1].reshape(1, -1), ln2_b[1].reshape(1, -1), batch, seq, lp)
    return out.reshape(batch, seq, d)
```

```python
import functools

import jax
import jax.numpy as jnp
from jax import lax
from jax.experimental import pallas as pl
from jax.experimental.pallas import tpu as pltpu

F32 = jnp.float32
BF16 = jnp.bfloat16
HI = lax.Precision.HIGHEST

D_MODEL = 1024
DEPTH = 2
N_META = 16
CONV_K = 4
RET_HEADS = 4
RET_DK = 128
RET_DV = 256
ROPE_BASE = 10000.0
GDN_HEADS = 4
GDN_DK = 128
GDN_DV = 256
SSD_DINNER = 2048
SSD_HEADDIM = 64
SSD_HEADS = 32
SSD_GROUPS = 4
SSD_HPG = 8
SSD_DSTATE = 128
SSD_GN = 512
D_FF = 4096
DN_ALPHA = (2 * DEPTH) ** 0.25
LN_EPS = 1e-5
GN_EPS = 1e-5
RMS_EPS = 1e-6

LANES = 128
COL_TILE = 1024
CHUNK = 128
GDN_CHUNK = 64
GDN_SUBCHUNKS = 2
PAD_FRONT = CHUNK - N_META
HALO = 8
CONV_ROWS = 64
VMEM_LIMIT = 56 * 1024 * 1024


def _row_tile(rows_per_batch):
    for tm in (640, 512, 256, 128):
        if rows_per_batch % tm == 0:
            return tm
    raise ValueError(f"unsupported padded sequence length {rows_per_batch}")


def _params(sem):
    return pltpu.CompilerParams(dimension_semantics=sem, vmem_limit_bytes=VMEM_LIMIT)


def _softplus(x):
    return jnp.maximum(x, 0.0) + jnp.log1p(jnp.exp(-jnp.abs(x)))


def _silu(x):
    return x * jax.nn.sigmoid(x)


def _dot(a, b):
    return jnp.dot(a, b, preferred_element_type=F32)


def _dot_nt(a, b):
    return lax.dot_general(a, b, (((1,), (1,)), ((), ())), preferred_element_type=F32)


def _dot_hi(a, b):
    return jnp.dot(a, b, preferred_element_type=F32, precision=HI)


def _wide(a):
    hi = a.astype(BF16).astype(F32)
    return jnp.concatenate([a, a - hi], axis=1).astype(BF16)


def _tall(b):
    hi = b.astype(BF16)
    return jnp.concatenate([hi, hi], axis=0)


def _epilogue_plain(acc_ref, o_ref, *, tm):
    o_ref[...] = acc_ref[HALO:HALO + tm, :].astype(BF16)


def _epilogue_rope(acc_ref, o_ref, cos_ref, sin_ref, scale_ref, *, tm):
    for hd in range(COL_TILE // RET_DK):
        cols = slice(RET_DK * hd, RET_DK * (hd + 1))
        x = acc_ref[HALO:HALO + tm, cols]
        y = x * cos_ref[...] + pltpu.roll(x, RET_DK // 2, 1) * sin_ref[...]
        o_ref[:, cols] = (y * scale_ref[:, cols]).astype(BF16)


def _epilogue_conv(acc_ref, o_ref, cw_ref, cb_ref, scale_ref, *, tm, l2norm):
    for r in range(0, tm, CONV_ROWS):
        rows = slice(r, r + CONV_ROWS)
        for c in range(0, COL_TILE, LANES):
            cols = slice(c, c + LANES)
            w = [cw_ref[tap:tap + 1, cols] for tap in range(CONV_K)]
            xw = acc_ref[r:r + CONV_ROWS + HALO, cols]
            s1 = pltpu.roll(xw, 1, 0)
            y = w[3] * xw + w[2] * s1 + pltpu.roll(w[1] * xw + w[0] * s1, 2, 0)
            y = _silu(y[HALO:] + cb_ref[:, cols])
            if l2norm:
                inv = lax.rsqrt(jnp.sum(y * y, axis=-1, keepdims=True) + 1e-6)
                y = y * (inv * scale_ref[:, cols])
            o_ref[rows, cols] = y.astype(BF16)


def _lagged_steps(i, n_tiles, acc_a, acc_b, produce, consume):
    last_prev = acc_b if n_tiles % 2 == 0 else acc_a

    @pl.when(i == 0)
    def _():
        acc_b[...] = jnp.zeros(acc_b.shape, F32)

    @pl.when(jnp.logical_and(i % 2 == 0, i < n_tiles))
    def _():
        produce(acc_a, acc_b)
        consume(acc_b)

    @pl.when(jnp.logical_and(i % 2 == 1, i < n_tiles))
    def _():
        produce(acc_b, acc_a)
        consume(acc_a)

    @pl.when(i == n_tiles)
    def _():
        consume(last_prev)


def _proj_kernel(*refs, tm, tiles_per_batch, n_tiles, n_extra, epilogue):
    h_ref, w_ref = refs[0], refs[1]
    extra = refs[2:2 + n_extra]
    o_ref, acc_a, acc_b = refs[2 + n_extra:]
    i = pl.program_id(1)
    starts_batch = (i % tiles_per_batch) == 0

    def produce(cur, prev):
        cur[HALO:HALO + tm, :] = _dot(h_ref[...], w_ref[...])
        cur[0:HALO, :] = jnp.where(starts_batch, 0.0, prev[tm:tm + HALO, :])

    _lagged_steps(i, n_tiles, acc_a, acc_b, produce, lambda prev: epilogue(prev, o_ref, *extra, tm=tm))


def _proj(hb, w, col_tiles, epilogue, extra, extra_specs, rows_per_batch, name):
    rows = hb.shape[0]
    tm = _row_tile(rows_per_batch)
    tpb = rows_per_batch // tm
    n_tiles = rows // tm
    col_fn, ncol = col_tiles
    prev = lambda i: jnp.maximum(i - 1, 0)
    return pl.pallas_call(
        functools.partial(_proj_kernel, tm=tm, tiles_per_batch=tpb, n_tiles=n_tiles, n_extra=len(extra),
                          epilogue=epilogue),
        grid=(ncol, n_tiles + 1),
        in_specs=[
            pl.BlockSpec((tm, D_MODEL), lambda j, i: (jnp.minimum(i, n_tiles - 1), 0)),
            pl.BlockSpec((D_MODEL, COL_TILE), lambda j, i: (0, col_fn(j))),
        ] + [spec(tm, tpb, prev) for spec in extra_specs],
        out_specs=pl.BlockSpec((tm, COL_TILE), lambda j, i: (prev(i), j)),
        out_shape=jax.ShapeDtypeStruct((rows, ncol * COL_TILE), BF16),
        scratch_shapes=[pltpu.VMEM((tm + HALO, COL_TILE), F32), pltpu.VMEM((tm + HALO, COL_TILE), F32)],
        compiler_params=_params(("arbitrary", "arbitrary")),
        name=name,
    )(hb, w, *extra)


def _per_col(width=COL_TILE, rows=1):
    return lambda tm, tpb, prev: pl.BlockSpec((rows, width), lambda j, i: (0, j))


def _per_row_in_batch(width):
    return lambda tm, tpb, prev: pl.BlockSpec((tm, width), lambda j, i: (prev(i) % tpb, 0))


def _gates_ab_kernel(h_ref, w_ref, alog_ref, bias_ref, o_ref, ot_ref):
    acc = _dot(h_ref[...], w_ref[...])
    lane = lax.broadcasted_iota(jnp.int32, acc.shape, 1)
    decay = -jnp.exp(alog_ref[...]) * _softplus(acc + bias_ref[...])
    gates = jnp.where(lane < GDN_HEADS, jax.nn.sigmoid(acc), decay)
    o_ref[...] = gates
    ot_ref[...] = gates.T


def _ssd_gates(hb, live, w_ref, alog_ref, bias_ref):
    acc = _dot(hb, w_ref[...])
    lane = lax.broadcasted_iota(jnp.int32, acc.shape, 1)
    dt = jnp.where(live, _softplus(acc + bias_ref[...]), 0.0)
    return jnp.where(lane < SSD_HEADS, dt, dt * -jnp.exp(alog_ref[...]))


def _gates_ab(hb, w, alog, bias, rows_per_batch):
    rows = hb.shape[0]
    tm = _row_tile(rows_per_batch)
    vec = pl.BlockSpec((1, LANES), lambda i: (0, 0))
    return pl.pallas_call(
        _gates_ab_kernel,
        grid=(rows // tm,),
        in_specs=[
            pl.BlockSpec((tm, D_MODEL), lambda i: (i, 0)),
            pl.BlockSpec((D_MODEL, LANES), lambda i: (0, 0)),
            vec, vec,
        ],
        out_specs=[pl.BlockSpec((tm, LANES), lambda i: (i, 0)), pl.BlockSpec((LANES, tm), lambda i: (0, i))],
        out_shape=[jax.ShapeDtypeStruct((rows, LANES), F32), jax.ShapeDtypeStruct((LANES, rows), F32)],
        compiler_params=_params(("arbitrary",)),
        name="gates_ab",
    )(hb, w, alog, bias)


def _retention_kernel(q_ref, k_ref, v_ref, g_ref, gnw_ref, o_ref, s_ref, *, batch):
    n = CHUNK

    @pl.when(pl.program_id(0) == 0)
    def _():
        s_ref[...] = jnp.zeros(s_ref.shape, F32)

    t_idx = lax.broadcasted_iota(jnp.int32, (n, n), 0)
    s_idx = lax.broadcasted_iota(jnp.int32, (n, n), 1)
    t_col = lax.broadcasted_iota(jnp.int32, (n, 1), 0).astype(F32)
    gap = (t_idx - s_idx).astype(F32)
    chains = [(b, hd) for b in range(batch) for hd in range(RET_HEADS)]
    cs = range(len(chains))
    log_gamma = [jnp.log1p(-jnp.exp2(jnp.full((1, 1), -5.0 - hd, F32))) for hd in range(RET_HEADS)]
    decay = [jnp.exp(jnp.where(t_idx >= s_idx, gap * lg, -jnp.inf)) for lg in log_gamma]
    e_in = [jnp.exp((t_col + 1.0) * lg) for lg in log_gamma]
    e_out = [jnp.exp((n - 1.0 - t_col) * lg) for lg in log_gamma]
    e_all = [jnp.exp(n * lg) for lg in log_gamma]
    q = [q_ref[b, :, RET_DK * hd:RET_DK * (hd + 1)] for b, hd in chains]
    k = [k_ref[b, :, RET_DK * hd:RET_DK * (hd + 1)] for b, hd in chains]
    v = [v_ref[b, :, RET_DV * hd:RET_DV * (hd + 1)] for b, hd in chains]
    states = [s_ref[c] for c in cs]
    att = [(_dot_nt(q[c], k[c]) * decay[chains[c][1]]).astype(BF16) for c in cs]
    q_in = [(q[c].astype(F32) * e_in[chains[c][1]]).astype(BF16) for c in cs]
    o = [_dot(att[c], v[c]) + _dot(q_in[c], states[c].astype(BF16)) for c in cs]
    k_out = [(k[c].astype(F32) * e_out[chains[c][1]]).T.astype(BF16) for c in cs]
    for c in cs:
        s_ref[c] = states[c] * e_all[chains[c][1]] + _dot(k_out[c], v[c])
    for c, (b, hd) in enumerate(chains):
        cols = slice(RET_DV * hd, RET_DV * (hd + 1))
        oc = o[c] - jnp.mean(o[c], axis=-1, keepdims=True)
        on = oc * lax.rsqrt(jnp.mean(oc * oc, axis=-1, keepdims=True) + GN_EPS)
        gate = _silu(g_ref[b, :, cols].astype(F32))
        o_ref[b, :, cols] = (on * gnw_ref[:, cols] * gate).astype(BF16)


def _retention(qk, vg, gn_w, batch, rows_per_batch):
    n = CHUNK
    nc = rows_per_batch // n
    qk = qk.reshape(batch, rows_per_batch, qk.shape[1])
    vg = vg.reshape(batch, rows_per_batch, vg.shape[1])
    out = pl.pallas_call(
        functools.partial(_retention_kernel, batch=batch),
        grid=(nc,),
        in_specs=[
            pl.BlockSpec((batch, n, 512), lambda c: (0, c, 0)),
            pl.BlockSpec((batch, n, 512), lambda c: (0, c, 1)),
            pl.BlockSpec((batch, n, 1024), lambda c: (0, c, 0)),
            pl.BlockSpec((batch, n, 1024), lambda c: (0, c, 1)),
            pl.BlockSpec((1, 1024), lambda c: (0, 0)),
        ],
        out_specs=pl.BlockSpec((batch, n, 1024), lambda c: (0, c, 0)),
        out_shape=jax.ShapeDtypeStruct((batch, rows_per_batch, 1024), BF16),
        scratch_shapes=[pltpu.VMEM((batch * RET_HEADS, RET_DK, RET_DV), F32)],
        compiler_params=_params(("arbitrary",)),
        name="retention",
    )(qk, qk, vg, vg, gn_w)
    return out.reshape(batch * rows_per_batch, 1024)


def _gdn_kernel(*refs, batch):
    n = GDN_CHUNK
    q_ref, k_ref, v_ref, g_ref, gcol_ref = refs[:5]
    grow_refs = refs[5:5 + batch]
    nw_ref, o_ref, s_ref = refs[5 + batch:]

    @pl.when(pl.program_id(0) == 0)
    def _():
        s_ref[...] = jnp.zeros(s_ref.shape, F32)

    t_idx = lax.broadcasted_iota(jnp.int32, (n, n), 0)
    s_idx = lax.broadcasted_iota(jnp.int32, (n, n), 1)
    lower = (t_idx >= s_idx).astype(F32)
    upper = (t_idx <= s_idx).astype(F32)
    eye = (t_idx == s_idx).astype(F32)
    blocks = [(b, sub) for b in range(batch) for sub in range(GDN_SUBCHUNKS)]
    chains = [(b, sub, hd) for b, sub in blocks for hd in range(GDN_HEADS)]
    cs = range(len(chains))
    rows = lambda sub: slice(n * sub, n * (sub + 1))
    gcol = {bs: gcol_ref[bs[0], rows(bs[1])] for bs in blocks}
    gcum_col = {bs: _dot_hi(lower, gcol[bs]) for bs in blocks}
    gcum_row = {bs: _dot_hi(grow_refs[bs[0]][:, rows(bs[1])], upper) for bs in blocks}
    q = [q_ref[b, rows(sub), GDN_DK * hd:GDN_DK * (hd + 1)].astype(F32) for b, sub, hd in chains]
    k = [k_ref[b, rows(sub), GDN_DK * hd:GDN_DK * (hd + 1)].astype(F32) for b, sub, hd in chains]
    v = [v_ref[b, rows(sub), GDN_DV * hd:GDN_DV * (hd + 1)].astype(F32) for b, sub, hd in chains]
    beta = [gcol[b, sub][:, hd:hd + 1] for b, sub, hd in chains]
    gc = [gcum_col[b, sub][:, GDN_HEADS + hd:GDN_HEADS + hd + 1] for b, sub, hd in chains]
    gr = [gcum_row[b, sub][GDN_HEADS + hd:GDN_HEADS + hd + 1, :] for b, sub, hd in chains]
    g_last = [gc[c][n - 1:n, :] for c in cs]
    seg = [gc[c] - gr[c] for c in cs]
    dec_strict = [jnp.exp(jnp.where(t_idx > s_idx, seg[c], -jnp.inf)) for c in cs]
    dec_incl = [jnp.exp(jnp.where(t_idx >= s_idx, seg[c], -jnp.inf)) for c in cs]
    kb = [k[c] * beta[c] for c in cs]
    kq = [_dot_nt(jnp.concatenate([kb[c], q[c]], axis=0).astype(BF16), k[c].astype(BF16)) for c in cs]
    a = [kq[c][:n] * dec_strict[c] for c in cs]
    att = [(kq[c][n:] * dec_incl[c]).astype(BF16) for c in cs]
    inv = [eye - a[c] for c in cs]
    p = [_dot(_wide(a[c]), _tall(a[c])) for c in cs]
    span = 4
    while span < n:
        r = [_dot(_wide(jnp.concatenate([p[c], inv[c]], axis=0)), _tall(p[c])) for c in cs]
        inv = [inv[c] + r[c][n:] for c in cs]
        p = [r[c][:n] for c in cs]
        span *= 2
    inv = [inv[c] + _dot(_wide(inv[c]), _tall(p[c])) for c in cs]
    e_gc = [jnp.exp(gc[c]) for c in cs]
    rhs = [jnp.concatenate([v[c] * beta[c], kb[c] * e_gc[c]], axis=-1) for c in cs]
    u = [_dot(_wide(inv[c]), _tall(rhs[c])) for c in cs]
    lhs = [jnp.concatenate([u[c][:, GDN_DV:], q[c] * e_gc[c]], axis=0).astype(BF16) for c in cs]
    k_out = [(k[c] * jnp.exp(g_last[c] - gc[c])).T.astype(BF16) for c in cs]
    e_all = [jnp.exp(g_last[c]) for c in cs]

    lanes = [(b, hd) for b in range(batch) for hd in range(GDN_HEADS)]
    states = [s_ref[i] for i in range(len(lanes))]
    for sub in range(GDN_SUBCHUNKS):
        idx = [chains.index((b, sub, hd)) for b, hd in lanes]
        ws = [_dot(lhs[c], states[i].astype(BF16)) for i, c in enumerate(idx)]
        v_new = [(u[c][:, :GDN_DV] - ws[i][:n]).astype(BF16) for i, c in enumerate(idx)]
        o = [ws[i][n:] + _dot(att[c], v_new[i]) for i, c in enumerate(idx)]
        states = [states[i] * e_all[c] + _dot(k_out[c], v_new[i]) for i, c in enumerate(idx)]
        for i, (b, hd) in enumerate(lanes):
            cols = slice(GDN_DV * hd, GDN_DV * (hd + 1))
            on = o[i] * lax.rsqrt(jnp.mean(o[i] * o[i], axis=-1, keepdims=True) + RMS_EPS)
            gate = _silu(g_ref[b, rows(sub), cols].astype(F32))
            o_ref[b, rows(sub), cols] = (on * nw_ref[...] * gate).astype(BF16)
    for i in range(len(lanes)):
        s_ref[i] = states[i]


def _gdn(qk, v, plain, gates, gates_t, norm_w, batch, rows_per_batch):
    n = GDN_CHUNK
    step_rows = n * GDN_SUBCHUNKS
    assert rows_per_batch % step_rows == 0 and step_rows % LANES == 0
    steps = rows_per_batch // step_rows
    qk = qk.reshape(batch, rows_per_batch, qk.shape[1])
    v = v.reshape(batch, rows_per_batch, v.shape[1])
    plain = plain.reshape(batch, rows_per_batch, plain.shape[1])
    gates = gates.reshape(batch, rows_per_batch, LANES)
    out = pl.pallas_call(
        functools.partial(_gdn_kernel, batch=batch),
        grid=(steps,),
        in_specs=[
            pl.BlockSpec((batch, step_rows, 512), lambda c: (0, c, 0)),
            pl.BlockSpec((batch, step_rows, 512), lambda c: (0, c, 1)),
            pl.BlockSpec((batch, step_rows, 1024), lambda c: (0, c, 0)),
            pl.BlockSpec((batch, step_rows, 1024), lambda c: (0, c, 2)),
            pl.BlockSpec((batch, step_rows, LANES), lambda c: (0, c, 0)),
        ] + [pl.BlockSpec((8, step_rows), lambda c, b=b: (0, b * steps + c)) for b in range(batch)] + [
            pl.BlockSpec((1, GDN_DV), lambda c: (0, 0)),
        ],
        out_specs=pl.BlockSpec((batch, step_rows, 1024), lambda c: (0, c, 0)),
        out_shape=jax.ShapeDtypeStruct((batch, rows_per_batch, 1024), BF16),
        scratch_shapes=[pltpu.VMEM((batch * GDN_HEADS, GDN_DK, GDN_DV), F32)],
        compiler_params=_params(("arbitrary",)),
        name="gated_delta",
    )(qk, qk, v, plain, gates, *([gates_t] * batch), norm_w)
    return out.reshape(batch * rows_per_batch, 1024)


def _layer_norm(x, w, b):
    xc = x - jnp.mean(x, axis=-1, keepdims=True)
    return xc * lax.rsqrt(jnp.mean(xc * xc, axis=-1, keepdims=True) + LN_EPS) * w + b


def _outproj_ln_kernel(ya_ref, yb_ref, w_ref, h_ref, lw_ref, lb_ref, o_ref, acc_a, acc_b, *, n_tiles):
    half = ya_ref.shape[1]

    def produce(cur, prev):
        acc = _dot(ya_ref[...], w_ref[0:half, :]) + _dot(yb_ref[...], w_ref[half:2 * half, :])
        cur[...] = DN_ALPHA * h_ref[...] + acc

    def consume(prev):
        o_ref[...] = _layer_norm(prev[...], lw_ref[...], lb_ref[...])

    _lagged_steps(pl.program_id(0), n_tiles, acc_a, acc_b, produce, consume)


def _outproj_ln(ya, yb, col_a, col_b, w, h, lw, lb, rows_per_batch):
    rows = h.shape[0]
    tm = _row_tile(rows_per_batch)
    n_tiles = rows // tm
    half = w.shape[0] // 2
    cur = lambda i: jnp.minimum(i, n_tiles - 1)
    prev = lambda i: jnp.maximum(i - 1, 0)
    vec = pl.BlockSpec((1, D_MODEL), lambda i: (0, 0))
    return pl.pallas_call(
        functools.partial(_outproj_ln_kernel, n_tiles=n_tiles),
        grid=(n_tiles + 1,),
        in_specs=[
            pl.BlockSpec((tm, half), lambda i: (cur(i), col_a)),
            pl.BlockSpec((tm, half), lambda i: (cur(i), col_b)),
            pl.BlockSpec((2 * half, D_MODEL), lambda i: (0, 0), pipeline_mode=pl.Buffered(1)),
            pl.BlockSpec((tm, D_MODEL), lambda i: (cur(i), 0)),
            vec, vec,
        ],
        out_specs=pl.BlockSpec((tm, D_MODEL), lambda i: (prev(i), 0)),
        out_shape=jax.ShapeDtypeStruct((rows, D_MODEL), F32),
        scratch_shapes=[pltpu.VMEM((tm, D_MODEL), F32), pltpu.VMEM((tm, D_MODEL), F32)],
        compiler_params=_params(("arbitrary",)),
        name="outproj_ln",
    )(ya, yb, w, h, lw, lb)


def _mlp(h_ref, w1_ref, w2_ref, lw_ref, lb_ref):
    h = h_ref[...]
    hb = h.astype(BF16)
    acc = jnp.zeros(h.shape, F32)
    step = 1024
    for f in range(0, D_FF, step):
        a = _dot(hb, w1_ref[:, f:f + step])
        a = jnp.square(jnp.maximum(a, 0.0)).astype(BF16)
        acc = acc + _dot(a, w2_ref[f:f + step, :])
    return _layer_norm(DN_ALPHA * h + acc, lw_ref[...], lb_ref[...])


def _mlp_ln_kernel(h_ref, w1_ref, w2_ref, lw_ref, lb_ref, wg_ref, alog_ref, bias_ref, o_ref, ob_ref, g_ref, gt_ref,
                   *, tm, tiles_per_batch):
    out = _mlp(h_ref, w1_ref, w2_ref, lw_ref, lb_ref)
    o_ref[...] = out
    row = (pl.program_id(0) % tiles_per_batch) * tm + lax.broadcasted_iota(jnp.int32, (tm, 1), 0)
    live = row >= PAD_FRONT
    hb = jnp.where(live, out, 0.0).astype(BF16)
    ob_ref[...] = hb
    gates = _ssd_gates(hb, live, wg_ref, alog_ref, bias_ref)
    g_ref[...] = gates
    gt_ref[...] = gates.T


def _mlp_ln_final_kernel(h_ref, w1_ref, w2_ref, lw_ref, lb_ref, o_ref):
    o_ref[...] = _mlp(h_ref, w1_ref, w2_ref, lw_ref, lb_ref)


def _mlp_weight_specs(index):
    return [
        pl.BlockSpec((D_MODEL, D_FF), index, pipeline_mode=pl.Buffered(1)),
        pl.BlockSpec((D_FF, D_MODEL), index, pipeline_mode=pl.Buffered(1)),
        pl.BlockSpec((1, D_MODEL), index),
        pl.BlockSpec((1, D_MODEL), index),
    ]


def _mlp_ln(h, w1, w2, lw, lb, w_gate, alog, bias, rows_per_batch):
    rows = h.shape[0]
    tm = _row_tile(rows_per_batch)
    tile = pl.BlockSpec((tm, D_MODEL), lambda i: (i, 0))
    vec = pl.BlockSpec((1, LANES), lambda i: (0, 0))
    return pl.pallas_call(
        functools.partial(_mlp_ln_kernel, tm=tm, tiles_per_batch=rows_per_batch // tm),
        grid=(rows // tm,),
        in_specs=[tile] + _mlp_weight_specs(lambda i: (0, 0)) + [pl.BlockSpec((D_MODEL, LANES), lambda i: (0, 0)),
                                                                   vec, vec],
        out_specs=[tile, tile, pl.BlockSpec((tm, LANES), lambda i: (i, 0)), pl.BlockSpec((LANES, tm), lambda i: (0, i))],
        out_shape=[jax.ShapeDtypeStruct((rows, D_MODEL), F32), jax.ShapeDtypeStruct((rows, D_MODEL), BF16),
                   jax.ShapeDtypeStruct((rows, LANES), F32), jax.ShapeDtypeStruct((LANES, rows), F32)],
        compiler_params=_params(("arbitrary",)),
        name="mlp_ln",
    )(h, w1, w2, lw, lb, w_gate, alog, bias)


def _mlp_ln_final(h, w1, w2, lw, lb, batch, seq, rows_per_batch):
    tm = next(t for t in (1024, 512, 256, 128) if seq % t == 0)
    tiles = seq // tm
    first_row = PAD_FRONT + N_META
    return pl.pallas_call(
        _mlp_ln_final_kernel,
        grid=(batch, tiles),
        in_specs=[pl.BlockSpec((pl.Element(tm), pl.Element(D_MODEL)),
                               lambda b, i: (pl.multiple_of(b * rows_per_batch + first_row + i * tm, 128), 0))]
        + _mlp_weight_specs(lambda b, i: (0, 0)),
        out_specs=pl.BlockSpec((tm, D_MODEL), lambda b, i: (b * tiles + i, 0)),
        out_shape=jax.ShapeDtypeStruct((batch * seq, D_MODEL), F32),
        compiler_params=_params(("arbitrary", "arbitrary")),
        name="mlp_ln_final",
    )(h, w1, w2, lw, lb)


def _ssd_kernel(*refs, batch):
    n = CHUNK
    z_ref, x_ref, b_ref, c_ref, gcol_ref = refs[:5]
    grow_refs = refs[5:5 + batch]
    dskip_ref, nw_ref, o_ref, s_ref = refs[5 + batch:]
    pair_w = 2 * SSD_HEADDIM
    pairs_per_group = SSD_HPG // 2
    group_w = SSD_HPG * SSD_HEADDIM
    n_pairs = SSD_HEADS // 2

    @pl.when(pl.program_id(0) == 0)
    def _():
        s_ref[...] = jnp.zeros(s_ref.shape, F32)

    t_idx = lax.broadcasted_iota(jnp.int32, (n, n), 0)
    s_idx = lax.broadcasted_iota(jnp.int32, (n, n), 1)
    causal = t_idx >= s_idx
    lower = causal.astype(F32)
    upper = (t_idx <= s_idx).astype(F32)
    lane_v = lax.broadcasted_iota(jnp.int32, (n, pair_w), 1)
    lane_s = lax.broadcasted_iota(jnp.int32, (SSD_DSTATE, pair_w), 1)
    lane_1 = lax.broadcasted_iota(jnp.int32, (1, pair_w), 1)
    bs = range(batch)
    grow = [grow_refs[b][...] for b in bs]
    gcum_col = [_dot_hi(lower, gcol_ref[b]) for b in bs]
    gcum_row = [_dot_hi(grow[b], upper) for b in bs]
    for g in range(SSD_GROUPS):
        q = [c_ref[b, :, SSD_DSTATE * g:SSD_DSTATE * (g + 1)] for b in bs]
        k = [b_ref[b, :, SSD_DSTATE * g:SSD_DSTATE * (g + 1)] for b in bs]
        qf = [q[b].astype(F32) for b in bs]
        kt = [k[b].astype(F32).T for b in bs]
        cb = [_dot_nt(q[b], k[b]) for b in bs]
        ys = [[] for _ in bs]
        for p in range(pairs_per_group):
            pair = g * pairs_per_group + p
            col = group_w * g + pair_w * p
            for b in bs:
                xv = x_ref[b, :, col:col + pair_w]
                state = s_ref[b * n_pairs + pair]
                o = jnp.zeros((n, pair_w), F32)
                upd = jnp.zeros((SSD_DSTATE, pair_w), F32)
                last = []
                for side in range(2):
                    hd = 2 * pair + side
                    gc = gcum_col[b][:, SSD_HEADS + hd:SSD_HEADS + hd + 1]
                    gr = gcum_row[b][SSD_HEADS + hd:SSD_HEADS + hd + 1, :]
                    dt_row = grow[b][hd:hd + 1, :]
                    g_last = gr[:, n - 1:n]
                    dec = jnp.exp(jnp.where(causal, gc - gr, -jnp.inf))
                    att = cb[b] * dec * dt_row
                    q_in = qf[b] * jnp.exp(gc)
                    mine_v = (lane_v >= SSD_HEADDIM) if side else (lane_v < SSD_HEADDIM)
                    mine_s = (lane_s >= SSD_HEADDIM) if side else (lane_s < SSD_HEADDIM)
                    xm = jnp.where(mine_v, xv, jnp.zeros_like(xv))
                    sm = jnp.where(mine_s, state, 0.0).astype(BF16)
                    o = o + _dot(att.astype(BF16), xm) + _dot(q_in.astype(BF16), sm)
                    k_out = kt[b] * (jnp.exp(g_last - gr) * dt_row)
                    upd = upd + _dot(k_out.astype(BF16), xm)
                    last.append(jnp.exp(g_last))
                s_ref[b * n_pairs + pair] = state * jnp.where(lane_1 < SSD_HEADDIM, last[0], last[1]) + upd
                ys[b].append(o + xv.astype(F32) * dskip_ref[:, col:col + pair_w])
        cols = slice(group_w * g, group_w * (g + 1))
        for b in bs:
            y = jnp.concatenate(ys[b], axis=-1)
            y = y * _silu(z_ref[b, :, cols].astype(F32))
            y = y * lax.rsqrt(jnp.mean(y * y, axis=-1, keepdims=True) + RMS_EPS)
            o_ref[b, :, cols] = (y * nw_ref[:, cols]).astype(BF16)


def _ssd(z, xbc, gates, gates_t, dskip, norm_w, batch, rows_per_batch):
    n = CHUNK
    steps = rows_per_batch // n
    z = z.reshape(batch, rows_per_batch, z.shape[1])
    xbc = xbc.reshape(batch, rows_per_batch, xbc.shape[1])
    gates = gates.reshape(batch, rows_per_batch, LANES)
    out = pl.pallas_call(
        functools.partial(_ssd_kernel, batch=batch),
        grid=(steps,),
        in_specs=[
            pl.BlockSpec((batch, n, SSD_DINNER), lambda c: (0, c, 0)),
            pl.BlockSpec((batch, n, SSD_DINNER), lambda c: (0, c, 0)),
            pl.BlockSpec((batch, n, SSD_GN), lambda c: (0, c, 4)),
            pl.BlockSpec((batch, n, SSD_GN), lambda c: (0, c, 5)),
            pl.BlockSpec((batch, n, LANES), lambda c: (0, c, 0)),
        ] + [pl.BlockSpec((2 * SSD_HEADS, n), lambda c, b=b: (0, b * steps + c)) for b in range(batch)] + [
            pl.BlockSpec((1, SSD_DINNER), lambda c: (0, 0)),
            pl.BlockSpec((1, SSD_DINNER), lambda c: (0, 0)),
        ],
        out_specs=pl.BlockSpec((batch, n, SSD_DINNER), lambda c: (0, c, 0)),
        out_shape=jax.ShapeDtypeStruct((batch, rows_per_batch, SSD_DINNER), BF16),
        scratch_shapes=[pltpu.VMEM((batch * SSD_HEADS // 2, SSD_DSTATE, 2 * SSD_HEADDIM), F32)],
        compiler_params=_params(("arbitrary",)),
        name="ssd",
    )(z, xbc, xbc, xbc, gates, *([gates_t] * batch), dskip, norm_w)
    return out.reshape(batch * rows_per_batch, SSD_DINNER)


def _pad_lanes(v):
    v = v.reshape(1, -1).astype(F32)
    return jnp.pad(v, ((0, 0), (0, LANES - v.shape[1])))


def kernel(x, meta_tokens, ab_w_in, ab_ret_gn_w, ab_conv_q, ab_conv_k, ab_conv_v, ab_A_log, ab_dt_bias, ab_gdn_norm_w, ab_w_out, c_w_in, c_conv_w, c_conv_b, c_A_log, c_dt_bias, c_D, c_norm_w, c_w_out, mlp_w1, mlp_w2, ln1_w, ln1_b, ln2_w, ln2_b):
    batch, seq, d = x.shape
    assert d == D_MODEL and meta_tokens.shape == (N_META, D_MODEL)
    lp = PAD_FRONT + N_META + seq
    rows = batch * lp

    meta = jnp.broadcast_to(meta_tokens[None].astype(x.dtype), (batch, N_META, d))
    h = jnp.concatenate([jnp.zeros((batch, PAD_FRONT, d), x.dtype), meta, x], axis=1).reshape(rows, d)
    hb = h.astype(BF16)

    pos = jnp.arange(lp, dtype=F32) - PAD_FRONT
    inv_freq = 1.0 / (ROPE_BASE ** jnp.linspace(0.0, 1.0, RET_DK // 2, dtype=F32))
    ang = pos[:, None] * inv_freq[None]
    cosf = jnp.concatenate([jnp.cos(ang), jnp.cos(ang)], axis=-1)
    sinf = jnp.concatenate([-jnp.sin(ang), jnp.sin(ang)], axis=-1)

    w_in = ab_w_in[0]
    w_main = jnp.concatenate([w_in[:, :5120], w_in[:, 5128:]], axis=1).astype(BF16)
    w_gate = jnp.pad(w_in[:, 5120:5128], ((0, 0), (0, LANES - 8))).astype(BF16)
    ones = jnp.ones((1, COL_TILE // 2), F32)
    zeros = jnp.zeros((1, COL_TILE), F32)
    p_qk = _proj(hb, w_main, (lambda j: j, 1), _epilogue_rope,
                 [cosf, sinf, jnp.concatenate([ones * RET_DK ** -0.5, ones], axis=1)],
                 [_per_row_in_batch(LANES), _per_row_in_batch(LANES), _per_col()], lp, "proj_ret_qk")
    p_plain = _proj(hb, w_main, (lambda j: jnp.where(j < 2, j + 1, 5), 3), _epilogue_plain,
                    [], [], lp, "proj_ab_plain")
    g_qk = _proj(hb, w_main, (lambda j: j + 3, 1), functools.partial(_epilogue_conv, l2norm=True),
                 [jnp.concatenate([ab_conv_q[0], ab_conv_k[0]], axis=1), zeros,
                  jnp.concatenate([ones * GDN_DK ** -0.5, ones], axis=1)],
                 [_per_col(rows=CONV_K), _per_col(), _per_col()], lp, "proj_gdn_qk")
    g_v = _proj(hb, w_main, (lambda j: j + 4, 1), functools.partial(_epilogue_conv, l2norm=False),
                [ab_conv_v[0], zeros, zeros], [_per_col(rows=CONV_K), _per_col(), _per_col()], lp, "proj_gdn_v")
    alog = _pad_lanes(jnp.concatenate([jnp.zeros((GDN_HEADS,), F32), ab_A_log[0]]))
    bias = _pad_lanes(jnp.concatenate([jnp.zeros((GDN_HEADS,), F32), ab_dt_bias[0]]))
    g0, g0_t = _gates_ab(hb, w_gate, alog, bias, lp)
    y_ret = _retention(p_qk, p_plain, ab_ret_gn_w[0].reshape(1, -1), batch, lp)
    y_gdn = _gdn(g_qk, g_v, p_plain, g0, g0_t, ab_gdn_norm_w[0].reshape(1, -1), batch, lp)
    h = _outproj_ln(y_ret, y_gdn, 0, 0, ab_w_out[0].astype(BF16), h,
                    ln1_w[0].reshape(1, -1), ln1_b[0].reshape(1, -1), lp)
    w_in = c_w_in[0]
    w_dt = w_in[:, 5120:]
    w_gate = jnp.pad(jnp.concatenate([w_dt, w_dt], axis=1), ((0, 0), (0, LANES - 2 * SSD_HEADS))).astype(BF16)
    alog = _pad_lanes(jnp.concatenate([jnp.zeros((SSD_HEADS,), F32), c_A_log[0]]))
    bias = _pad_lanes(jnp.concatenate([c_dt_bias[0], c_dt_bias[0]]))
    h, hb, g1, g1_t = _mlp_ln(h, mlp_w1[0].astype(BF16), mlp_w2[0].astype(BF16),
                              ln2_w[0].reshape(1, -1), ln2_b[0].reshape(1, -1), w_gate, alog, bias, lp)

    w_main = w_in[:, :5120].astype(BF16)
    p_z = _proj(hb, w_main, (lambda j: j, 2), _epilogue_plain, [], [], lp, "proj_ssd_z")
    p_xbc = _proj(hb, w_main, (lambda j: j + 2, 3), functools.partial(_epilogue_conv, l2norm=False),
                  [c_conv_w[0], c_conv_b[0].reshape(1, -1), jnp.zeros((1, 3 * COL_TILE), F32)],
                  [_per_col(rows=CONV_K), _per_col(), _per_col()], lp, "proj_ssd_xbc")
    dskip = jnp.repeat(c_D[0].astype(F32), SSD_HEADDIM).reshape(1, -1)
    y_ssd = _ssd(p_z, p_xbc, g1, g1_t, dskip, c_norm_w[0].reshape(1, -1), batch, lp)
    h = _outproj_ln(y_ssd, y_ssd, 0, 1, c_w_out[0].astype(BF16), h,
                    ln1_w[1].reshape(1, -1), ln1_b[1].reshape(1, -1), lp)
    out = _mlp_ln_final(h, mlp_w1[1].astype(BF16), mlp_w2[1].astype(BF16),
                        ln2_w[1].reshape(1, -1), ln2_b[1].reshape(1, -1), batch, seq, lp)
    return out.reshape(batch, seq, d)
```

```python
import functools

import jax
import jax.numpy as jnp
from jax import lax
from jax.experimental import pallas as pl
from jax.experimental.pallas import tpu as pltpu

F32 = jnp.float32
BF16 = jnp.bfloat16
HI = lax.Precision.HIGHEST

D_MODEL = 1024
DEPTH = 2
N_META = 16
CONV_K = 4
RET_HEADS = 4
RET_DK = 128
RET_DV = 256
ROPE_BASE = 10000.0
GDN_HEADS = 4
GDN_DK = 128
GDN_DV = 256
SSD_DINNER = 2048
SSD_HEADDIM = 64
SSD_HEADS = 32
SSD_GROUPS = 4
SSD_HPG = 8
SSD_DSTATE = 128
SSD_GN = 512
D_FF = 4096
DN_ALPHA = (2 * DEPTH) ** 0.25
LN_EPS = 1e-5
GN_EPS = 1e-5
RMS_EPS = 1e-6

LANES = 128
COL_TILE = 1024
CHUNK = 128
GDN_CHUNK = 64
GDN_SUBCHUNKS = 2
PAD_FRONT = CHUNK - N_META
HALO = 8
CONV_ROWS = 64
VMEM_LIMIT = 56 * 1024 * 1024


def _row_tile(rows_per_batch):
    for tm in (640, 512, 256, 128):
        if rows_per_batch % tm == 0:
            return tm
    raise ValueError(f"unsupported padded sequence length {rows_per_batch}")


def _params(sem):
    return pltpu.CompilerParams(dimension_semantics=sem, vmem_limit_bytes=VMEM_LIMIT)


def _softplus(x):
    return jnp.maximum(x, 0.0) + jnp.log1p(jnp.exp(-jnp.abs(x)))


def _silu(x):
    return x * jax.nn.sigmoid(x)


def _dot(a, b):
    return jnp.dot(a, b, preferred_element_type=F32)


def _dot_nt(a, b):
    return lax.dot_general(a, b, (((1,), (1,)), ((), ())), preferred_element_type=F32)


def _dot_hi(a, b):
    return jnp.dot(a, b, preferred_element_type=F32, precision=HI)


def _wide(a):
    hi = a.astype(BF16).astype(F32)
    return jnp.concatenate([a, a - hi], axis=1).astype(BF16)


def _tall(b):
    hi = b.astype(BF16)
    return jnp.concatenate([hi, hi], axis=0)


def _epilogue_plain(acc_ref, o_ref, *, tm):
    o_ref[...] = acc_ref[HALO:HALO + tm, :].astype(BF16)


def _epilogue_rope(acc_ref, o_ref, cos_ref, sin_ref, scale_ref, *, tm):
    for hd in range(COL_TILE // RET_DK):
        cols = slice(RET_DK * hd, RET_DK * (hd + 1))
        x = acc_ref[HALO:HALO + tm, cols]
        y = x * cos_ref[...] + pltpu.roll(x, RET_DK // 2, 1) * sin_ref[...]
        o_ref[:, cols] = (y * scale_ref[:, cols]).astype(BF16)


def _epilogue_conv(acc_ref, o_ref, cw_ref, cb_ref, scale_ref, *, tm, l2norm):
    for r in range(0, tm, CONV_ROWS):
        rows = slice(r, r + CONV_ROWS)
        for c in range(0, COL_TILE, LANES):
            cols = slice(c, c + LANES)
            w = [cw_ref[tap:tap + 1, cols] for tap in range(CONV_K)]
            xw = acc_ref[r:r + CONV_ROWS + HALO, cols]
            s1 = pltpu.roll(xw, 1, 0)
            y = w[3] * xw + w[2] * s1 + pltpu.roll(w[1] * xw + w[0] * s1, 2, 0)
            y = _silu(y[HALO:] + cb_ref[:, cols])
            if l2norm:
                inv = lax.rsqrt(jnp.sum(y * y, axis=-1, keepdims=True) + 1e-6)
                y = y * (inv * scale_ref[:, cols])
            o_ref[rows, cols] = y.astype(BF16)


def _lagged_steps(i, n_tiles, acc_a, acc_b, produce, consume):
    last_prev = acc_b if n_tiles % 2 == 0 else acc_a

    @pl.when(i == 0)
    def _():
        acc_b[...] = jnp.zeros(acc_b.shape, F32)

    @pl.when(jnp.logical_and(i % 2 == 0, i < n_tiles))
    def _():
        produce(acc_a, acc_b)
        consume(acc_b)

    @pl.when(jnp.logical_and(i % 2 == 1, i < n_tiles))
    def _():
        produce(acc_b, acc_a)
        consume(acc_a)

    @pl.when(i == n_tiles)
    def _():
        consume(last_prev)


def _proj_kernel(*refs, tm, tiles_per_batch, n_extra, epilogue):
    h_ref, w_ref = refs[0], refs[1]
    extra = refs[2:2 + n_extra]
    o_ref, acc_ref, wb_ref = refs[2 + n_extra:]
    i = pl.program_id(1)

    @pl.when(i == 0)
    def _():
        wb_ref[...] = w_ref[...].astype(BF16)

    @pl.when(i % tiles_per_batch == 0)
    def _():
        acc_ref[0:HALO, :] = jnp.zeros((HALO, COL_TILE), F32)

    acc_ref[HALO:HALO + tm, :] = _dot(h_ref[...], wb_ref[...])
    epilogue(acc_ref, o_ref, *extra, tm=tm)
    acc_ref[0:HALO, :] = acc_ref[tm:tm + HALO, :]


def _proj(hb, w, col_tiles, epilogue, extra, extra_specs, rows_per_batch, name):
    rows = hb.shape[0]
    tm = _row_tile(rows_per_batch)
    tpb = rows_per_batch // tm
    col_fn, ncol = col_tiles
    return pl.pallas_call(
        functools.partial(_proj_kernel, tm=tm, tiles_per_batch=tpb, n_extra=len(extra), epilogue=epilogue),
        grid=(ncol, rows // tm),
        in_specs=[
            pl.BlockSpec((tm, D_MODEL), lambda j, i: (i, 0)),
            pl.BlockSpec((D_MODEL, COL_TILE), lambda j, i: (0, col_fn(j))),
        ] + [spec(tm, tpb) for spec in extra_specs],
        out_specs=pl.BlockSpec((tm, COL_TILE), lambda j, i: (i, j)),
        out_shape=jax.ShapeDtypeStruct((rows, ncol * COL_TILE), BF16),
        scratch_shapes=[pltpu.VMEM((tm + HALO, COL_TILE), F32), pltpu.VMEM((D_MODEL, COL_TILE), BF16)],
        compiler_params=_params(("arbitrary", "arbitrary")),
        name=name,
    )(hb, w, *extra)


def _per_col(width=COL_TILE, rows=1):
    return lambda tm, tpb: pl.BlockSpec((rows, width), lambda j, i: (0, j))


def _per_row_in_batch(width):
    return lambda tm, tpb: pl.BlockSpec((tm, width), lambda j, i: (i % tpb, 0))


def _gates_ab_kernel(h_ref, w_ref, alog_ref, bias_ref, o_ref, ot_ref):
    acc = _dot(h_ref[...], w_ref[...])
    lane = lax.broadcasted_iota(jnp.int32, acc.shape, 1)
    decay = -jnp.exp(alog_ref[...]) * _softplus(acc + bias_ref[...])
    gates = jnp.where(lane < GDN_HEADS, jax.nn.sigmoid(acc), decay)
    o_ref[...] = gates
    ot_ref[...] = gates.T


def _ssd_gates(hb, live, w_ref, alog_ref, bias_ref):
    acc = _dot(hb, w_ref[...])
    lane = lax.broadcasted_iota(jnp.int32, acc.shape, 1)
    dt = jnp.where(live, _softplus(acc + bias_ref[...]), 0.0)
    return jnp.where(lane < SSD_HEADS, dt, dt * -jnp.exp(alog_ref[...]))


def _gates_ab(hb, w, alog, bias, rows_per_batch):
    rows = hb.shape[0]
    tm = _row_tile(rows_per_batch)
    vec = pl.BlockSpec((1, LANES), lambda i: (0, 0))
    return pl.pallas_call(
        _gates_ab_kernel,
        grid=(rows // tm,),
        in_specs=[
            pl.BlockSpec((tm, D_MODEL), lambda i: (i, 0)),
            pl.BlockSpec((D_MODEL, LANES), lambda i: (0, 0)),
            vec, vec,
        ],
        out_specs=[pl.BlockSpec((tm, LANES), lambda i: (i, 0)), pl.BlockSpec((LANES, tm), lambda i: (0, i))],
        out_shape=[jax.ShapeDtypeStruct((rows, LANES), F32), jax.ShapeDtypeStruct((LANES, rows), F32)],
        compiler_params=_params(("arbitrary",)),
        name="gates_ab",
    )(hb, w, alog, bias)


def _retention_kernel(q_ref, k_ref, v_ref, g_ref, gnw_ref, o_ref, s_ref, *, batch):
    n = CHUNK

    @pl.when(pl.program_id(0) == 0)
    def _():
        s_ref[...] = jnp.zeros(s_ref.shape, F32)

    t_idx = lax.broadcasted_iota(jnp.int32, (n, n), 0)
    s_idx = lax.broadcasted_iota(jnp.int32, (n, n), 1)
    t_col = lax.broadcasted_iota(jnp.int32, (n, 1), 0).astype(F32)
    gap = (t_idx - s_idx).astype(F32)
    chains = [(b, hd) for b in range(batch) for hd in range(RET_HEADS)]
    cs = range(len(chains))
    log_gamma = [jnp.log1p(-jnp.exp2(jnp.full((1, 1), -5.0 - hd, F32))) for hd in range(RET_HEADS)]
    decay = [jnp.exp(jnp.where(t_idx >= s_idx, gap * lg, -jnp.inf)) for lg in log_gamma]
    e_in = [jnp.exp((t_col + 1.0) * lg) for lg in log_gamma]
    e_out = [jnp.exp((n - 1.0 - t_col) * lg) for lg in log_gamma]
    e_all = [jnp.exp(n * lg) for lg in log_gamma]
    q = [q_ref[b, :, RET_DK * hd:RET_DK * (hd + 1)] for b, hd in chains]
    k = [k_ref[b, :, RET_DK * hd:RET_DK * (hd + 1)] for b, hd in chains]
    v = [v_ref[b, :, RET_DV * hd:RET_DV * (hd + 1)] for b, hd in chains]
    states = [s_ref[c] for c in cs]
    att = [(_dot_nt(q[c], k[c]) * decay[chains[c][1]]).astype(BF16) for c in cs]
    q_in = [(q[c].astype(F32) * e_in[chains[c][1]]).astype(BF16) for c in cs]
    o = [_dot(att[c], v[c]) + _dot(q_in[c], states[c].astype(BF16)) for c in cs]
    k_out = [(k[c].astype(F32) * e_out[chains[c][1]]).T.astype(BF16) for c in cs]
    for c in cs:
        s_ref[c] = states[c] * e_all[chains[c][1]] + _dot(k_out[c], v[c])
    for c, (b, hd) in enumerate(chains):
        cols = slice(RET_DV * hd, RET_DV * (hd + 1))
        oc = o[c] - jnp.mean(o[c], axis=-1, keepdims=True)
        on = oc * lax.rsqrt(jnp.mean(oc * oc, axis=-1, keepdims=True) + GN_EPS)
        gate = _silu(g_ref[b, :, cols].astype(F32))
        o_ref[b, :, cols] = (on * gnw_ref[:, cols] * gate).astype(BF16)


def _retention(qk, vg, gn_w, batch, rows_per_batch):
    n = CHUNK
    nc = rows_per_batch // n
    qk = qk.reshape(batch, rows_per_batch, qk.shape[1])
    vg = vg.reshape(batch, rows_per_batch, vg.shape[1])
    out = pl.pallas_call(
        functools.partial(_retention_kernel, batch=batch),
        grid=(nc,),
        in_specs=[
            pl.BlockSpec((batch, n, 512), lambda c: (0, c, 0)),
            pl.BlockSpec((batch, n, 512), lambda c: (0, c, 1)),
            pl.BlockSpec((batch, n, 1024), lambda c: (0, c, 0)),
            pl.BlockSpec((batch, n, 1024), lambda c: (0, c, 1)),
            pl.BlockSpec((1, 1024), lambda c: (0, 0)),
        ],
        out_specs=pl.BlockSpec((batch, n, 1024), lambda c: (0, c, 0)),
        out_shape=jax.ShapeDtypeStruct((batch, rows_per_batch, 1024), BF16),
        scratch_shapes=[pltpu.VMEM((batch * RET_HEADS, RET_DK, RET_DV), F32)],
        compiler_params=_params(("arbitrary",)),
        name="retention",
    )(qk, qk, vg, vg, gn_w)
    return out.reshape(batch * rows_per_batch, 1024)


def _gdn_kernel(*refs, batch):
    n = GDN_CHUNK
    q_ref, k_ref, v_ref, g_ref, gcol_ref = refs[:5]
    grow_refs = refs[5:5 + batch]
    nw_ref, o_ref, s_ref = refs[5 + batch:]

    @pl.when(pl.program_id(0) == 0)
    def _():
        s_ref[...] = jnp.zeros(s_ref.shape, F32)

    t_idx = lax.broadcasted_iota(jnp.int32, (n, n), 0)
    s_idx = lax.broadcasted_iota(jnp.int32, (n, n), 1)
    lower = (t_idx >= s_idx).astype(F32)
    upper = (t_idx <= s_idx).astype(F32)
    eye = (t_idx == s_idx).astype(F32)
    blocks = [(b, sub) for b in range(batch) for sub in range(GDN_SUBCHUNKS)]
    chains = [(b, sub, hd) for b, sub in blocks for hd in range(GDN_HEADS)]
    cs = range(len(chains))
    rows = lambda sub: slice(n * sub, n * (sub + 1))
    gcol = {bs: gcol_ref[bs[0], rows(bs[1])] for bs in blocks}
    gcum_col = {bs: _dot_hi(lower, gcol[bs]) for bs in blocks}
    gcum_row = {bs: _dot_hi(grow_refs[bs[0]][:, rows(bs[1])], upper) for bs in blocks}
    q = [q_ref[b, rows(sub), GDN_DK * hd:GDN_DK * (hd + 1)].astype(F32) for b, sub, hd in chains]
    k = [k_ref[b, rows(sub), GDN_DK * hd:GDN_DK * (hd + 1)].astype(F32) for b, sub, hd in chains]
    v = [v_ref[b, rows(sub), GDN_DV * hd:GDN_DV * (hd + 1)].astype(F32) for b, sub, hd in chains]
    beta = [gcol[b, sub][:, hd:hd + 1] for b, sub, hd in chains]
    gc = [gcum_col[b, sub][:, GDN_HEADS + hd:GDN_HEADS + hd + 1] for b, sub, hd in chains]
    gr = [gcum_row[b, sub][GDN_HEADS + hd:GDN_HEADS + hd + 1, :] for b, sub, hd in chains]
    g_last = [gc[c][n - 1:n, :] for c in cs]
    seg = [gc[c] - gr[c] for c in cs]
    dec_strict = [jnp.exp(jnp.where(t_idx > s_idx, seg[c], -jnp.inf)) for c in cs]
    dec_incl = [jnp.exp(jnp.where(t_idx >= s_idx, seg[c], -jnp.inf)) for c in cs]
    kb = [k[c] * beta[c] for c in cs]
    kq = [_dot_nt(jnp.concatenate([kb[c], q[c]], axis=0).astype(BF16), k[c].astype(BF16)) for c in cs]
    a = [kq[c][:n] * dec_strict[c] for c in cs]
    att = [(kq[c][n:] * dec_incl[c]).astype(BF16) for c in cs]
    inv = [eye - a[c] for c in cs]
    p = [_dot(_wide(a[c]), _tall(a[c])) for c in cs]
    span = 4
    while span < n:
        r = [_dot(_wide(jnp.concatenate([p[c], inv[c]], axis=0)), _tall(p[c])) for c in cs]
        inv = [inv[c] + r[c][n:] for c in cs]
        p = [r[c][:n] for c in cs]
        span *= 2
    inv = [inv[c] + _dot(_wide(inv[c]), _tall(p[c])) for c in cs]
    e_gc = [jnp.exp(gc[c]) for c in cs]
    rhs = [jnp.concatenate([v[c] * beta[c], kb[c] * e_gc[c]], axis=-1) for c in cs]
    u = [_dot(_wide(inv[c]), _tall(rhs[c])) for c in cs]
    lhs = [jnp.concatenate([u[c][:, GDN_DV:], q[c] * e_gc[c]], axis=0).astype(BF16) for c in cs]
    k_out = [(k[c] * jnp.exp(g_last[c] - gc[c])).T.astype(BF16) for c in cs]
    e_all = [jnp.exp(g_last[c]) for c in cs]

    lanes = [(b, hd) for b in range(batch) for hd in range(GDN_HEADS)]
    states = [s_ref[i] for i in range(len(lanes))]
    for sub in range(GDN_SUBCHUNKS):
        idx = [chains.index((b, sub, hd)) for b, hd in lanes]
        ws = [_dot(lhs[c], states[i].astype(BF16)) for i, c in enumerate(idx)]
        v_new = [(u[c][:, :GDN_DV] - ws[i][:n]).astype(BF16) for i, c in enumerate(idx)]
        o = [ws[i][n:] + _dot(att[c], v_new[i]) for i, c in enumerate(idx)]
        states = [states[i] * e_all[c] + _dot(k_out[c], v_new[i]) for i, c in enumerate(idx)]
        for i, (b, hd) in enumerate(lanes):
            cols = slice(GDN_DV * hd, GDN_DV * (hd + 1))
            on = o[i] * lax.rsqrt(jnp.mean(o[i] * o[i], axis=-1, keepdims=True) + RMS_EPS)
            gate = _silu(g_ref[b, rows(sub), cols].astype(F32))
            o_ref[b, rows(sub), cols] = (on * nw_ref[...] * gate).astype(BF16)
    for i in range(len(lanes)):
        s_ref[i] = states[i]


def _gdn(qk, v, gate, gates, gates_t, norm_w, batch, rows_per_batch):
    n = GDN_CHUNK
    step_rows = n * GDN_SUBCHUNKS
    assert rows_per_batch % step_rows == 0 and step_rows % LANES == 0
    steps = rows_per_batch // step_rows
    qk = qk.reshape(batch, rows_per_batch, qk.shape[1])
    v = v.reshape(batch, rows_per_batch, v.shape[1])
    gate = gate.reshape(batch, rows_per_batch, gate.shape[1])
    gates = gates.reshape(batch, rows_per_batch, LANES)
    out = pl.pallas_call(
        functools.partial(_gdn_kernel, batch=batch),
        grid=(steps,),
        in_specs=[
            pl.BlockSpec((batch, step_rows, 512), lambda c: (0, c, 0)),
            pl.BlockSpec((batch, step_rows, 512), lambda c: (0, c, 1)),
            pl.BlockSpec((batch, step_rows, 1024), lambda c: (0, c, 0)),
            pl.BlockSpec((batch, step_rows, 1024), lambda c: (0, c, 0)),
            pl.BlockSpec((batch, step_rows, LANES), lambda c: (0, c, 0)),
        ] + [pl.BlockSpec((8, step_rows), lambda c, b=b: (0, b * steps + c)) for b in range(batch)] + [
            pl.BlockSpec((1, GDN_DV), lambda c: (0, 0)),
        ],
        out_specs=pl.BlockSpec((batch, step_rows, 1024), lambda c: (0, c, 0)),
        out_shape=jax.ShapeDtypeStruct((batch, rows_per_batch, 1024), BF16),
        scratch_shapes=[pltpu.VMEM((batch * GDN_HEADS, GDN_DK, GDN_DV), F32)],
        compiler_params=_params(("arbitrary",)),
        name="gated_delta",
    )(qk, qk, v, gate, gates, *([gates_t] * batch), norm_w)
    return out.reshape(batch * rows_per_batch, 1024)


def _layer_norm(x, w, b):
    xc = x - jnp.mean(x, axis=-1, keepdims=True)
    return xc * lax.rsqrt(jnp.mean(xc * xc, axis=-1, keepdims=True) + LN_EPS) * w + b


def _outproj_ln_kernel(ya_ref, yb_ref, w_ref, h_ref, lw_ref, lb_ref, o_ref, acc_a, acc_b, *, n_tiles):
    half = ya_ref.shape[1]

    def produce(cur, prev):
        acc = _dot(ya_ref[...], w_ref[0:half, :]) + _dot(yb_ref[...], w_ref[half:2 * half, :])
        cur[...] = DN_ALPHA * h_ref[...] + acc

    def consume(prev):
        o_ref[...] = _layer_norm(prev[...], lw_ref[...], lb_ref[...])

    _lagged_steps(pl.program_id(0), n_tiles, acc_a, acc_b, produce, consume)


def _outproj_ln(ya, yb, col_a, col_b, w, h, lw, lb, rows_per_batch):
    rows = h.shape[0]
    tm = _row_tile(rows_per_batch)
    n_tiles = rows // tm
    half = w.shape[0] // 2
    cur = lambda i: jnp.minimum(i, n_tiles - 1)
    prev = lambda i: jnp.maximum(i - 1, 0)
    vec = pl.BlockSpec((1, D_MODEL), lambda i: (0, 0))
    return pl.pallas_call(
        functools.partial(_outproj_ln_kernel, n_tiles=n_tiles),
        grid=(n_tiles + 1,),
        in_specs=[
            pl.BlockSpec((tm, half), lambda i: (cur(i), col_a)),
            pl.BlockSpec((tm, half), lambda i: (cur(i), col_b)),
            pl.BlockSpec((2 * half, D_MODEL), lambda i: (0, 0), pipeline_mode=pl.Buffered(1)),
            pl.BlockSpec((tm, D_MODEL), lambda i: (cur(i), 0)),
            vec, vec,
        ],
        out_specs=pl.BlockSpec((tm, D_MODEL), lambda i: (prev(i), 0)),
        out_shape=jax.ShapeDtypeStruct((rows, D_MODEL), F32),
        scratch_shapes=[pltpu.VMEM((tm, D_MODEL), F32), pltpu.VMEM((tm, D_MODEL), F32)],
        compiler_params=_params(("arbitrary",)),
        name="outproj_ln",
    )(ya, yb, w, h, lw, lb)


def _mlp(h_ref, w1_ref, w2_ref, lw_ref, lb_ref):
    h = h_ref[...]
    hb = h.astype(BF16)
    acc = jnp.zeros(h.shape, F32)
    step = 1024
    for f in range(0, D_FF, step):
        a = _dot(hb, w1_ref[:, f:f + step])
        a = jnp.square(jnp.maximum(a, 0.0)).astype(BF16)
        acc = acc + _dot(a, w2_ref[f:f + step, :])
    return _layer_norm(DN_ALPHA * h + acc, lw_ref[...], lb_ref[...])


def _mlp_ln_kernel(h_ref, w1_ref, w2_ref, lw_ref, lb_ref, wg_ref, alog_ref, bias_ref, o_ref, ob_ref, g_ref, gt_ref,
                   *, tm, tiles_per_batch):
    out = _mlp(h_ref, w1_ref, w2_ref, lw_ref, lb_ref)
    o_ref[...] = out
    row = (pl.program_id(0) % tiles_per_batch) * tm + lax.broadcasted_iota(jnp.int32, (tm, 1), 0)
    live = row >= PAD_FRONT
    hb = jnp.where(live, out, 0.0).astype(BF16)
    ob_ref[...] = hb
    gates = _ssd_gates(hb, live, wg_ref, alog_ref, bias_ref)
    g_ref[...] = gates
    gt_ref[...] = gates.T


def _mlp_ln_final_kernel(h_ref, w1_ref, w2_ref, lw_ref, lb_ref, o_ref):
    o_ref[...] = _mlp(h_ref, w1_ref, w2_ref, lw_ref, lb_ref)


def _mlp_weight_specs(index):
    return [
        pl.BlockSpec((D_MODEL, D_FF), index, pipeline_mode=pl.Buffered(1)),
        pl.BlockSpec((D_FF, D_MODEL), index, pipeline_mode=pl.Buffered(1)),
        pl.BlockSpec((1, D_MODEL), index),
        pl.BlockSpec((1, D_MODEL), index),
    ]


def _mlp_ln(h, w1, w2, lw, lb, w_gate, alog, bias, rows_per_batch):
    rows = h.shape[0]
    tm = _row_tile(rows_per_batch)
    tile = pl.BlockSpec((tm, D_MODEL), lambda i: (i, 0))
    vec = pl.BlockSpec((1, LANES), lambda i: (0, 0))
    return pl.pallas_call(
        functools.partial(_mlp_ln_kernel, tm=tm, tiles_per_batch=rows_per_batch // tm),
        grid=(rows // tm,),
        in_specs=[tile] + _mlp_weight_specs(lambda i: (0, 0)) + [pl.BlockSpec((D_MODEL, LANES), lambda i: (0, 0)),
                                                                   vec, vec],
        out_specs=[tile, tile, pl.BlockSpec((tm, LANES), lambda i: (i, 0)), pl.BlockSpec((LANES, tm), lambda i: (0, i))],
        out_shape=[jax.ShapeDtypeStruct((rows, D_MODEL), F32), jax.ShapeDtypeStruct((rows, D_MODEL), BF16),
                   jax.ShapeDtypeStruct((rows, LANES), F32), jax.ShapeDtypeStruct((LANES, rows), F32)],
        compiler_params=_params(("arbitrary",)),
        name="mlp_ln",
    )(h, w1, w2, lw, lb, w_gate, alog, bias)


def _mlp_ln_final(h, w1, w2, lw, lb, batch, seq, rows_per_batch):
    tm = next(t for t in (1024, 512, 256, 128) if seq % t == 0)
    tiles = seq // tm
    first_row = PAD_FRONT + N_META
    return pl.pallas_call(
        _mlp_ln_final_kernel,
        grid=(batch, tiles),
        in_specs=[pl.BlockSpec((pl.Element(tm), pl.Element(D_MODEL)),
                               lambda b, i: (pl.multiple_of(b * rows_per_batch + first_row + i * tm, 128), 0))]
        + _mlp_weight_specs(lambda b, i: (0, 0)),
        out_specs=pl.BlockSpec((tm, D_MODEL), lambda b, i: (b * tiles + i, 0)),
        out_shape=jax.ShapeDtypeStruct((batch * seq, D_MODEL), F32),
        compiler_params=_params(("arbitrary", "arbitrary")),
        name="mlp_ln_final",
    )(h, w1, w2, lw, lb)


def _ssd_kernel(*refs, batch):
    n = CHUNK
    z_ref, x_ref, b_ref, c_ref, gcol_ref = refs[:5]
    grow_refs = refs[5:5 + batch]
    dskip_ref, nw_ref, o_ref, s_ref = refs[5 + batch:]
    pair_w = 2 * SSD_HEADDIM
    pairs_per_group = SSD_HPG // 2
    group_w = SSD_HPG * SSD_HEADDIM
    n_pairs = SSD_HEADS // 2

    @pl.when(pl.program_id(0) == 0)
    def _():
        s_ref[...] = jnp.zeros(s_ref.shape, F32)

    t_idx = lax.broadcasted_iota(jnp.int32, (n, n), 0)
    s_idx = lax.broadcasted_iota(jnp.int32, (n, n), 1)
    causal = t_idx >= s_idx
    lower = causal.astype(F32)
    upper = (t_idx <= s_idx).astype(F32)
    lane_v = lax.broadcasted_iota(jnp.int32, (n, pair_w), 1)
    lane_s = lax.broadcasted_iota(jnp.int32, (SSD_DSTATE, pair_w), 1)
    lane_1 = lax.broadcasted_iota(jnp.int32, (1, pair_w), 1)
    bs = range(batch)
    grow = [grow_refs[b][...] for b in bs]
    gcum_col = [_dot_hi(lower, gcol_ref[b]) for b in bs]
    gcum_row = [_dot_hi(grow[b], upper) for b in bs]
    for g in range(SSD_GROUPS):
        q = [c_ref[b, :, SSD_DSTATE * g:SSD_DSTATE * (g + 1)] for b in bs]
        k = [b_ref[b, :, SSD_DSTATE * g:SSD_DSTATE * (g + 1)] for b in bs]
        qf = [q[b].astype(F32) for b in bs]
        kt = [k[b].astype(F32).T for b in bs]
        cb = [_dot_nt(q[b], k[b]) for b in bs]
        ys = [[] for _ in bs]
        for p in range(pairs_per_group):
            pair = g * pairs_per_group + p
            col = group_w * g + pair_w * p
            for b in bs:
                xv = x_ref[b, :, col:col + pair_w]
                state = s_ref[b * n_pairs + pair]
                o = jnp.zeros((n, pair_w), F32)
                upd = jnp.zeros((SSD_DSTATE, pair_w), F32)
                last = []
                for side in range(2):
                    hd = 2 * pair + side
                    gc = gcum_col[b][:, SSD_HEADS + hd:SSD_HEADS + hd + 1]
                    gr = gcum_row[b][SSD_HEADS + hd:SSD_HEADS + hd + 1, :]
                    dt_row = grow[b][hd:hd + 1, :]
                    g_last = gr[:, n - 1:n]
                    dec = jnp.exp(jnp.where(causal, gc - gr, -jnp.inf))
                    att = cb[b] * dec * dt_row
                    q_in = qf[b] * jnp.exp(gc)
                    mine_v = (lane_v >= SSD_HEADDIM) if side else (lane_v < SSD_HEADDIM)
                    mine_s = (lane_s >= SSD_HEADDIM) if side else (lane_s < SSD_HEADDIM)
                    xm = jnp.where(mine_v, xv, jnp.zeros_like(xv))
                    sm = jnp.where(mine_s, state, 0.0).astype(BF16)
                    o = o + _dot(att.astype(BF16), xm) + _dot(q_in.astype(BF16), sm)
                    k_out = kt[b] * (jnp.exp(g_last - gr) * dt_row)
                    upd = upd + _dot(k_out.astype(BF16), xm)
                    last.append(jnp.exp(g_last))
                s_ref[b * n_pairs + pair] = state * jnp.where(lane_1 < SSD_HEADDIM, last[0], last[1]) + upd
                ys[b].append(o + xv.astype(F32) * dskip_ref[:, col:col + pair_w])
        cols = slice(group_w * g, group_w * (g + 1))
        for b in bs:
            y = jnp.concatenate(ys[b], axis=-1)
            y = y * _silu(z_ref[b, :, cols].astype(F32))
            y = y * lax.rsqrt(jnp.mean(y * y, axis=-1, keepdims=True) + RMS_EPS)
            o_ref[b, :, cols] = (y * nw_ref[:, cols]).astype(BF16)


def _ssd(z, xbc, gates, gates_t, dskip, norm_w, batch, rows_per_batch):
    n = CHUNK
    steps = rows_per_batch // n
    z = z.reshape(batch, rows_per_batch, z.shape[1])
    xbc = xbc.reshape(batch, rows_per_batch, xbc.shape[1])
    gates = gates.reshape(batch, rows_per_batch, LANES)
    out = pl.pallas_call(
        functools.partial(_ssd_kernel, batch=batch),
        grid=(steps,),
        in_specs=[
            pl.BlockSpec((batch, n, SSD_DINNER), lambda c: (0, c, 0)),
            pl.BlockSpec((batch, n, SSD_DINNER), lambda c: (0, c, 0)),
            pl.BlockSpec((batch, n, SSD_GN), lambda c: (0, c, 4)),
            pl.BlockSpec((batch, n, SSD_GN), lambda c: (0, c, 5)),
            pl.BlockSpec((batch, n, LANES), lambda c: (0, c, 0)),
        ] + [pl.BlockSpec((2 * SSD_HEADS, n), lambda c, b=b: (0, b * steps + c)) for b in range(batch)] + [
            pl.BlockSpec((1, SSD_DINNER), lambda c: (0, 0)),
            pl.BlockSpec((1, SSD_DINNER), lambda c: (0, 0)),
        ],
        out_specs=pl.BlockSpec((batch, n, SSD_DINNER), lambda c: (0, c, 0)),
        out_shape=jax.ShapeDtypeStruct((batch, rows_per_batch, SSD_DINNER), BF16),
        scratch_shapes=[pltpu.VMEM((batch * SSD_HEADS // 2, SSD_DSTATE, 2 * SSD_HEADDIM), F32)],
        compiler_params=_params(("arbitrary",)),
        name="ssd",
    )(z, xbc, xbc, xbc, gates, *([gates_t] * batch), dskip, norm_w)
    return out.reshape(batch * rows_per_batch, SSD_DINNER)


def _pad_lanes(v):
    v = v.reshape(1, -1).astype(F32)
    return jnp.pad(v, ((0, 0), (0, LANES - v.shape[1])))


def kernel(x, meta_tokens, ab_w_in, ab_ret_gn_w, ab_conv_q, ab_conv_k, ab_conv_v, ab_A_log, ab_dt_bias, ab_gdn_norm_w, ab_w_out, c_w_in, c_conv_w, c_conv_b, c_A_log, c_dt_bias, c_D, c_norm_w, c_w_out, mlp_w1, mlp_w2, ln1_w, ln1_b, ln2_w, ln2_b):
    batch, seq, d = x.shape
    assert d == D_MODEL and meta_tokens.shape == (N_META, D_MODEL)
    lp = PAD_FRONT + N_META + seq
    rows = batch * lp

    meta = jnp.broadcast_to(meta_tokens[None].astype(x.dtype), (batch, N_META, d))
    h = jnp.concatenate([jnp.zeros((batch, PAD_FRONT, d), x.dtype), meta, x], axis=1).reshape(rows, d)
    hb = h.astype(BF16)

    pos = jnp.arange(lp, dtype=F32) - PAD_FRONT
    inv_freq = 1.0 / (ROPE_BASE ** jnp.linspace(0.0, 1.0, RET_DK // 2, dtype=F32))
    ang = pos[:, None] * inv_freq[None]
    cosf = jnp.concatenate([jnp.cos(ang), jnp.cos(ang)], axis=-1)
    sinf = jnp.concatenate([-jnp.sin(ang), jnp.sin(ang)], axis=-1)

    w_in = ab_w_in[0]
    w_gg = w_in[:, 5128:]
    w_gate = jnp.pad(w_in[:, 5120:5128], ((0, 0), (0, LANES - 8))).astype(BF16)
    ones = jnp.ones((1, COL_TILE // 2), F32)
    zeros = jnp.zeros((1, COL_TILE), F32)
    p_qk = _proj(hb, w_in, (lambda j: j, 1), _epilogue_rope,
                 [cosf, sinf, jnp.concatenate([ones * RET_DK ** -0.5, ones], axis=1)],
                 [_per_row_in_batch(LANES), _per_row_in_batch(LANES), _per_col()], lp, "proj_ret_qk")
    p_vg = _proj(hb, w_in, (lambda j: j + 1, 2), _epilogue_plain, [], [], lp, "proj_ret_vg")
    p_gg = _proj(hb, w_gg, (lambda j: j, 1), _epilogue_plain, [], [], lp, "proj_gdn_g")
    g_qk = _proj(hb, w_in, (lambda j: j + 3, 1), functools.partial(_epilogue_conv, l2norm=True),
                 [jnp.concatenate([ab_conv_q[0], ab_conv_k[0]], axis=1), zeros,
                  jnp.concatenate([ones * GDN_DK ** -0.5, ones], axis=1)],
                 [_per_col(rows=CONV_K), _per_col(), _per_col()], lp, "proj_gdn_qk")
    g_v = _proj(hb, w_in, (lambda j: j + 4, 1), functools.partial(_epilogue_conv, l2norm=False),
                [ab_conv_v[0], zeros, zeros], [_per_col(rows=CONV_K), _per_col(), _per_col()], lp, "proj_gdn_v")
    alog = _pad_lanes(jnp.concatenate([jnp.zeros((GDN_HEADS,), F32), ab_A_log[0]]))
    bias = _pad_lanes(jnp.concatenate([jnp.zeros((GDN_HEADS,), F32), ab_dt_bias[0]]))
    g0, g0_t = _gates_ab(hb, w_gate, alog, bias, lp)
    y_ret = _retention(p_qk, p_vg, ab_ret_gn_w[0].reshape(1, -1), batch, lp)
    y_gdn = _gdn(g_qk, g_v, p_gg, g0, g0_t, ab_gdn_norm_w[0].reshape(1, -1), batch, lp)
    h = _outproj_ln(y_ret, y_gdn, 0, 0, ab_w_out[0].astype(BF16), h,
                    ln1_w[0].reshape(1, -1), ln1_b[0].reshape(1, -1), lp)
    w_in = c_w_in[0]
    w_dt = w_in[:, 5120:]
    w_gate = jnp.pad(jnp.concatenate([w_dt, w_dt], axis=1), ((0, 0), (0, LANES - 2 * SSD_HEADS))).astype(BF16)
    alog = _pad_lanes(jnp.concatenate([jnp.zeros((SSD_HEADS,), F32), c_A_log[0]]))
    bias = _pad_lanes(jnp.concatenate([c_dt_bias[0], c_dt_bias[0]]))
    h, hb, g1, g1_t = _mlp_ln(h, mlp_w1[0].astype(BF16), mlp_w2[0].astype(BF16),
                              ln2_w[0].reshape(1, -1), ln2_b[0].reshape(1, -1), w_gate, alog, bias, lp)

    p_z = _proj(hb, w_in, (lambda j: j, 2), _epilogue_plain, [], [], lp, "proj_ssd_z")
    p_xbc = _proj(hb, w_in, (lambda j: j + 2, 3), functools.partial(_epilogue_conv, l2norm=False),
                  [c_conv_w[0], c_conv_b[0].reshape(1, -1), jnp.zeros((1, 3 * COL_TILE), F32)],
                  [_per_col(rows=CONV_K), _per_col(), _per_col()], lp, "proj_ssd_xbc")
    dskip = jnp.repeat(c_D[0].astype(F32), SSD_HEADDIM).reshape(1, -1)
    y_ssd = _ssd(p_z, p_xbc, g1, g1_t, dskip, c_norm_w[0].reshape(1, -1), batch, lp)
    h = _outproj_ln(y_ssd, y_ssd, 0, 1, c_w_out[0].astype(BF16), h,
                    ln1_w[1].reshape(1, -1), ln1_b[1].reshape(1, -1), lp)
    out = _mlp_ln_final(h, mlp_w1[1].astype(BF16), mlp_w2[1].astype(BF16),
                        ln2_w[1].reshape(1, -1), ln2_b[1].reshape(1, -1), batch, seq, lp)
    return out.reshape(batch, seq, d)
```

```python
import functools

import jax
import jax.numpy as jnp
from jax import lax
from jax.experimental import pallas as pl
from jax.experimental.pallas import tpu as pltpu

F32 = jnp.float32
BF16 = jnp.bfloat16
HI = lax.Precision.HIGHEST

D_MODEL = 1024
DEPTH = 2
N_META = 16
CONV_K = 4
RET_HEADS = 4
RET_DK = 128
RET_DV = 256
ROPE_BASE = 10000.0
GDN_HEADS = 4
GDN_DK = 128
GDN_DV = 256
SSD_DINNER = 2048
SSD_HEADDIM = 64
SSD_HEADS = 32
SSD_GROUPS = 4
SSD_HPG = 8
SSD_DSTATE = 128
SSD_GN = 512
D_FF = 4096
DN_ALPHA = (2 * DEPTH) ** 0.25
LN_EPS = 1e-5
GN_EPS = 1e-5
RMS_EPS = 1e-6

LANES = 128
COL_TILE = 1024
CHUNK = 128
GDN_CHUNK = 64
GDN_SUBCHUNKS = 2
PAD_FRONT = CHUNK - N_META
HALO = 8
CONV_ROWS = 64
PROJ_ROW_TILE = 1664
VMEM_LIMIT = 56 * 1024 * 1024


def _row_tile(rows_per_batch, largest=640):
    for tm in (1664, 640, 512, 256, 128):
        if tm <= largest and rows_per_batch % tm == 0:
            return tm
    raise ValueError(f"unsupported padded sequence length {rows_per_batch}")


def _params(sem):
    return pltpu.CompilerParams(dimension_semantics=sem, vmem_limit_bytes=VMEM_LIMIT)


def _softplus(x):
    return jnp.maximum(x, 0.0) + jnp.log1p(jnp.exp(-jnp.abs(x)))


def _silu(x):
    return x * jax.nn.sigmoid(x)


def _dot(a, b):
    return jnp.dot(a, b, preferred_element_type=F32)


def _dot_nt(a, b):
    return lax.dot_general(a, b, (((1,), (1,)), ((), ())), preferred_element_type=F32)


def _dot_hi(a, b):
    return jnp.dot(a, b, preferred_element_type=F32, precision=HI)


def _wide(a):
    hi = a.astype(BF16).astype(F32)
    return jnp.concatenate([a, a - hi], axis=1).astype(BF16)


def _tall(b):
    hi = b.astype(BF16)
    return jnp.concatenate([hi, hi], axis=0)


def _epilogue_plain(acc_ref, o_ref, *, tm):
    o_ref[...] = acc_ref[HALO:HALO + tm, :].astype(BF16)


def _epilogue_rope(acc_ref, o_ref, cos_ref, sin_ref, scale_ref, *, tm):
    for hd in range(COL_TILE // RET_DK):
        cols = slice(RET_DK * hd, RET_DK * (hd + 1))
        x = acc_ref[HALO:HALO + tm, cols]
        y = x * cos_ref[...] + pltpu.roll(x, RET_DK // 2, 1) * sin_ref[...]
        o_ref[:, cols] = (y * scale_ref[:, cols]).astype(BF16)


def _epilogue_conv(acc_ref, o_ref, cw_ref, cb_ref, scale_ref, *, tm, l2norm):
    for r in range(0, tm, CONV_ROWS):
        rows = slice(r, r + CONV_ROWS)
        for c in range(0, COL_TILE, LANES):
            cols = slice(c, c + LANES)
            w = [cw_ref[tap:tap + 1, cols] for tap in range(CONV_K)]
            xw = acc_ref[r:r + CONV_ROWS + HALO, cols]
            s1 = pltpu.roll(xw, 1, 0)
            y = w[3] * xw + w[2] * s1 + pltpu.roll(w[1] * xw + w[0] * s1, 2, 0)
            y = _silu(y[HALO:] + cb_ref[:, cols])
            if l2norm:
                inv = lax.rsqrt(jnp.sum(y * y, axis=-1, keepdims=True) + 1e-6)
                y = y * (inv * scale_ref[:, cols])
            o_ref[rows, cols] = y.astype(BF16)


def _lagged_steps(i, n_tiles, acc_a, acc_b, produce, consume):
    last_prev = acc_b if n_tiles % 2 == 0 else acc_a

    @pl.when(i == 0)
    def _():
        acc_b[...] = jnp.zeros(acc_b.shape, F32)

    @pl.when(jnp.logical_and(i % 2 == 0, i < n_tiles))
    def _():
        produce(acc_a, acc_b)
        consume(acc_b)

    @pl.when(jnp.logical_and(i % 2 == 1, i < n_tiles))
    def _():
        produce(acc_b, acc_a)
        consume(acc_a)

    @pl.when(i == n_tiles)
    def _():
        consume(last_prev)


def _proj_kernel(*refs, tm, tiles_per_batch, n_extra, epilogue):
    h_ref, w_ref = refs[0], refs[1]
    extra = refs[2:2 + n_extra]
    o_ref, acc_ref, wb_ref = refs[2 + n_extra:]
    i = pl.program_id(1)

    @pl.when(i == 0)
    def _():
        wb_ref[...] = w_ref[...].astype(BF16)

    @pl.when(i % tiles_per_batch == 0)
    def _():
        acc_ref[0:HALO, :] = jnp.zeros((HALO, COL_TILE), F32)

    acc_ref[HALO:HALO + tm, :] = _dot(h_ref[...], wb_ref[...])
    epilogue(acc_ref, o_ref, *extra, tm=tm)
    acc_ref[0:HALO, :] = acc_ref[tm:tm + HALO, :]


def _proj(hb, w, col_tiles, epilogue, extra, extra_specs, rows_per_batch, name):
    rows = hb.shape[0]
    tm = _row_tile(rows_per_batch, largest=PROJ_ROW_TILE)
    tpb = rows_per_batch // tm
    col_fn, ncol = col_tiles
    return pl.pallas_call(
        functools.partial(_proj_kernel, tm=tm, tiles_per_batch=tpb, n_extra=len(extra), epilogue=epilogue),
        grid=(ncol, rows // tm),
        in_specs=[
            pl.BlockSpec((tm, D_MODEL), lambda j, i: (i, 0)),
            pl.BlockSpec((None, D_MODEL, COL_TILE), lambda j, i: (0, 0, col_fn(j))),
        ] + [spec(tm, tpb) for spec in extra_specs],
        out_specs=pl.BlockSpec((tm, COL_TILE), lambda j, i: (i, j)),
        out_shape=jax.ShapeDtypeStruct((rows, ncol * COL_TILE), BF16),
        scratch_shapes=[pltpu.VMEM((tm + HALO, COL_TILE), F32), pltpu.VMEM((D_MODEL, COL_TILE), BF16)],
        compiler_params=_params(("arbitrary", "arbitrary")),
        name=name,
    )(hb, w, *extra)


def _per_col(width=COL_TILE, rows=1):
    return lambda tm, tpb: pl.BlockSpec((rows, width), lambda j, i: (0, j))


def _per_row_in_batch(width):
    return lambda tm, tpb: pl.BlockSpec((tm, width), lambda j, i: (i % tpb, 0))


def _gates_ab_kernel(h_ref, w_ref, alog_ref, bias_ref, o_ref, ot_ref):
    acc = _dot(h_ref[...], w_ref[...])
    lane = lax.broadcasted_iota(jnp.int32, acc.shape, 1)
    decay = -jnp.exp(alog_ref[...]) * _softplus(acc + bias_ref[...])
    gates = jnp.where(lane < GDN_HEADS, jax.nn.sigmoid(acc), decay)
    o_ref[...] = gates
    ot_ref[...] = gates.T


def _ssd_gates(hb, live, w_ref, alog_ref, bias_ref):
    acc = _dot(hb, w_ref[...])
    lane = lax.broadcasted_iota(jnp.int32, acc.shape, 1)
    dt = jnp.where(live, _softplus(acc + bias_ref[...]), 0.0)
    return jnp.where(lane < SSD_HEADS, dt, dt * -jnp.exp(alog_ref[...]))


def _gates_ab(hb, w, alog, bias, rows_per_batch):
    rows = hb.shape[0]
    tm = _row_tile(rows_per_batch)
    vec = pl.BlockSpec((1, LANES), lambda i: (0, 0))
    return pl.pallas_call(
        _gates_ab_kernel,
        grid=(rows // tm,),
        in_specs=[
            pl.BlockSpec((tm, D_MODEL), lambda i: (i, 0)),
            pl.BlockSpec((D_MODEL, LANES), lambda i: (0, 0)),
            vec, vec,
        ],
        out_specs=[pl.BlockSpec((tm, LANES), lambda i: (i, 0)), pl.BlockSpec((LANES, tm), lambda i: (0, i))],
        out_shape=[jax.ShapeDtypeStruct((rows, LANES), F32), jax.ShapeDtypeStruct((LANES, rows), F32)],
        compiler_params=_params(("arbitrary",)),
        name="gates_ab",
    )(hb, w, alog, bias)


def _retention_kernel(q_ref, k_ref, v_ref, g_ref, gnw_ref, o_ref, s_ref, *, batch):
    n = CHUNK

    @pl.when(pl.program_id(0) == 0)
    def _():
        s_ref[...] = jnp.zeros(s_ref.shape, F32)

    t_idx = lax.broadcasted_iota(jnp.int32, (n, n), 0)
    s_idx = lax.broadcasted_iota(jnp.int32, (n, n), 1)
    t_col = lax.broadcasted_iota(jnp.int32, (n, 1), 0).astype(F32)
    gap = (t_idx - s_idx).astype(F32)
    chains = [(b, hd) for b in range(batch) for hd in range(RET_HEADS)]
    cs = range(len(chains))
    log_gamma = [jnp.log1p(-jnp.exp2(jnp.full((1, 1), -5.0 - hd, F32))) for hd in range(RET_HEADS)]
    decay = [jnp.exp(jnp.where(t_idx >= s_idx, gap * lg, -jnp.inf)) for lg in log_gamma]
    e_in = [jnp.exp((t_col + 1.0) * lg) for lg in log_gamma]
    e_out = [jnp.exp((n - 1.0 - t_col) * lg) for lg in log_gamma]
    e_all = [jnp.exp(n * lg) for lg in log_gamma]
    q = [q_ref[b, :, RET_DK * hd:RET_DK * (hd + 1)] for b, hd in chains]
    k = [k_ref[b, :, RET_DK * hd:RET_DK * (hd + 1)] for b, hd in chains]
    v = [v_ref[b, :, RET_DV * hd:RET_DV * (hd + 1)] for b, hd in chains]
    states = [s_ref[c] for c in cs]
    att = [(_dot_nt(q[c], k[c]) * decay[chains[c][1]]).astype(BF16) for c in cs]
    q_in = [(q[c].astype(F32) * e_in[chains[c][1]]).astype(BF16) for c in cs]
    o = [_dot(att[c], v[c]) + _dot(q_in[c], states[c].astype(BF16)) for c in cs]
    k_out = [(k[c].astype(F32) * e_out[chains[c][1]]).T.astype(BF16) for c in cs]
    for c in cs:
        s_ref[c] = states[c] * e_all[chains[c][1]] + _dot(k_out[c], v[c])
    for c, (b, hd) in enumerate(chains):
        cols = slice(RET_DV * hd, RET_DV * (hd + 1))
        oc = o[c] - jnp.mean(o[c], axis=-1, keepdims=True)
        on = oc * lax.rsqrt(jnp.mean(oc * oc, axis=-1, keepdims=True) + GN_EPS)
        gate = _silu(g_ref[b, :, cols].astype(F32))
        o_ref[b, :, cols] = (on * gnw_ref[:, cols] * gate).astype(BF16)


def _retention(qk, vg, gn_w, batch, rows_per_batch):
    n = CHUNK
    nc = rows_per_batch // n
    qk = qk.reshape(batch, rows_per_batch, qk.shape[1])
    vg = vg.reshape(batch, rows_per_batch, vg.shape[1])
    out = pl.pallas_call(
        functools.partial(_retention_kernel, batch=batch),
        grid=(nc,),
        in_specs=[
            pl.BlockSpec((batch, n, 512), lambda c: (0, c, 0)),
            pl.BlockSpec((batch, n, 512), lambda c: (0, c, 1)),
            pl.BlockSpec((batch, n, 1024), lambda c: (0, c, 0)),
            pl.BlockSpec((batch, n, 1024), lambda c: (0, c, 1)),
            pl.BlockSpec((1, 1024), lambda c: (0, 0)),
        ],
        out_specs=pl.BlockSpec((batch, n, 1024), lambda c: (0, c, 0)),
        out_shape=jax.ShapeDtypeStruct((batch, rows_per_batch, 1024), BF16),
        scratch_shapes=[pltpu.VMEM((batch * RET_HEADS, RET_DK, RET_DV), F32)],
        compiler_params=_params(("arbitrary",)),
        name="retention",
    )(qk, qk, vg, vg, gn_w)
    return out.reshape(batch * rows_per_batch, 1024)


def _gdn_kernel(*refs, batch):
    n = GDN_CHUNK
    q_ref, k_ref, v_ref, g_ref, gcol_ref = refs[:5]
    grow_refs = refs[5:5 + batch]
    nw_ref, o_ref, s_ref = refs[5 + batch:]

    @pl.when(pl.program_id(0) == 0)
    def _():
        s_ref[...] = jnp.zeros(s_ref.shape, F32)

    t_idx = lax.broadcasted_iota(jnp.int32, (n, n), 0)
    s_idx = lax.broadcasted_iota(jnp.int32, (n, n), 1)
    lower = (t_idx >= s_idx).astype(F32)
    upper = (t_idx <= s_idx).astype(F32)
    eye = (t_idx == s_idx).astype(F32)
    blocks = [(b, sub) for b in range(batch) for sub in range(GDN_SUBCHUNKS)]
    chains = [(b, sub, hd) for b, sub in blocks for hd in range(GDN_HEADS)]
    cs = range(len(chains))
    rows = lambda sub: slice(n * sub, n * (sub + 1))
    gcol = {bs: gcol_ref[bs[0], rows(bs[1])] for bs in blocks}
    gcum_col = {bs: _dot_hi(lower, gcol[bs]) for bs in blocks}
    gcum_row = {bs: _dot_hi(grow_refs[bs[0]][:, rows(bs[1])], upper) for bs in blocks}
    q = [q_ref[b, rows(sub), GDN_DK * hd:GDN_DK * (hd + 1)].astype(F32) for b, sub, hd in chains]
    k = [k_ref[b, rows(sub), GDN_DK * hd:GDN_DK * (hd + 1)].astype(F32) for b, sub, hd in chains]
    v = [v_ref[b, rows(sub), GDN_DV * hd:GDN_DV * (hd + 1)].astype(F32) for b, sub, hd in chains]
    beta = [gcol[b, sub][:, hd:hd + 1] for b, sub, hd in chains]
    gc = [gcum_col[b, sub][:, GDN_HEADS + hd:GDN_HEADS + hd + 1] for b, sub, hd in chains]
    gr = [gcum_row[b, sub][GDN_HEADS + hd:GDN_HEADS + hd + 1, :] for b, sub, hd in chains]
    g_last = [gc[c][n - 1:n, :] for c in cs]
    seg = [gc[c] - gr[c] for c in cs]
    dec_strict = [jnp.exp(jnp.where(t_idx > s_idx, seg[c], -jnp.inf)) for c in cs]
    dec_incl = [jnp.exp(jnp.where(t_idx >= s_idx, seg[c], -jnp.inf)) for c in cs]
    kb = [k[c] * beta[c] for c in cs]
    kq = [_dot_nt(jnp.concatenate([kb[c], q[c]], axis=0).astype(BF16), k[c].astype(BF16)) for c in cs]
    a = [kq[c][:n] * dec_strict[c] for c in cs]
    att = [(kq[c][n:] * dec_incl[c]).astype(BF16) for c in cs]
    inv = [eye - a[c] for c in cs]
    p = [_dot(_wide(a[c]), _tall(a[c])) for c in cs]
    span = 4
    while span < n:
        r = [_dot(_wide(jnp.concatenate([p[c], inv[c]], axis=0)), _tall(p[c])) for c in cs]
        inv = [inv[c] + r[c][n:] for c in cs]
        p = [r[c][:n] for c in cs]
        span *= 2
    inv = [inv[c] + _dot(_wide(inv[c]), _tall(p[c])) for c in cs]
    e_gc = [jnp.exp(gc[c]) for c in cs]
    rhs = [jnp.concatenate([v[c] * beta[c], kb[c] * e_gc[c]], axis=-1) for c in cs]
    u = [_dot(_wide(inv[c]), _tall(rhs[c])) for c in cs]
    lhs = [jnp.concatenate([u[c][:, GDN_DV:], q[c] * e_gc[c]], axis=0).astype(BF16) for c in cs]
    k_out = [(k[c] * jnp.exp(g_last[c] - gc[c])).T.astype(BF16) for c in cs]
    e_all = [jnp.exp(g_last[c]) for c in cs]

    lanes = [(b, hd) for b in range(batch) for hd in range(GDN_HEADS)]
    states = [s_ref[i] for i in range(len(lanes))]
    for sub in range(GDN_SUBCHUNKS):
        idx = [chains.index((b, sub, hd)) for b, hd in lanes]
        ws = [_dot(lhs[c], states[i].astype(BF16)) for i, c in enumerate(idx)]
        v_new = [(u[c][:, :GDN_DV] - ws[i][:n]).astype(BF16) for i, c in enumerate(idx)]
        o = [ws[i][n:] + _dot(att[c], v_new[i]) for i, c in enumerate(idx)]
        states = [states[i] * e_all[c] + _dot(k_out[c], v_new[i]) for i, c in enumerate(idx)]
        for i, (b, hd) in enumerate(lanes):
            cols = slice(GDN_DV * hd, GDN_DV * (hd + 1))
            on = o[i] * lax.rsqrt(jnp.mean(o[i] * o[i], axis=-1, keepdims=True) + RMS_EPS)
            gate = _silu(g_ref[b, rows(sub), cols].astype(F32))
            o_ref[b, rows(sub), cols] = (on * nw_ref[...] * gate).astype(BF16)
    for i in range(len(lanes)):
        s_ref[i] = states[i]


def _gdn(qk, v, gate, gates, gates_t, norm_w, batch, rows_per_batch):
    n = GDN_CHUNK
    step_rows = n * GDN_SUBCHUNKS
    assert rows_per_batch % step_rows == 0 and step_rows % LANES == 0
    steps = rows_per_batch // step_rows
    qk = qk.reshape(batch, rows_per_batch, qk.shape[1])
    v = v.reshape(batch, rows_per_batch, v.shape[1])
    gate = gate.reshape(batch, rows_per_batch, gate.shape[1])
    gates = gates.reshape(batch, rows_per_batch, LANES)
    out = pl.pallas_call(
        functools.partial(_gdn_kernel, batch=batch),
        grid=(steps,),
        in_specs=[
            pl.BlockSpec((batch, step_rows, 512), lambda c: (0, c, 0)),
            pl.BlockSpec((batch, step_rows, 512), lambda c: (0, c, 1)),
            pl.BlockSpec((batch, step_rows, 1024), lambda c: (0, c, 0)),
            pl.BlockSpec((batch, step_rows, 1024), lambda c: (0, c, 0)),
            pl.BlockSpec((batch, step_rows, LANES), lambda c: (0, c, 0)),
        ] + [pl.BlockSpec((8, step_rows), lambda c, b=b: (0, b * steps + c)) for b in range(batch)] + [
            pl.BlockSpec((1, GDN_DV), lambda c: (0, 0)),
        ],
        out_specs=pl.BlockSpec((batch, step_rows, 1024), lambda c: (0, c, 0)),
        out_shape=jax.ShapeDtypeStruct((batch, rows_per_batch, 1024), BF16),
        scratch_shapes=[pltpu.VMEM((batch * GDN_HEADS, GDN_DK, GDN_DV), F32)],
        compiler_params=_params(("arbitrary",)),
        name="gated_delta",
    )(qk, qk, v, gate, gates, *([gates_t] * batch), norm_w)
    return out.reshape(batch * rows_per_batch, 1024)


def _layer_norm(x, w, b):
    xc = x - jnp.mean(x, axis=-1, keepdims=True)
    return xc * lax.rsqrt(jnp.mean(xc * xc, axis=-1, keepdims=True) + LN_EPS) * w + b


def _outproj_ln_kernel(ya_ref, yb_ref, w_ref, h_ref, lw_ref, lb_ref, o_ref, acc_a, acc_b, *, n_tiles):
    half = ya_ref.shape[1]

    def produce(cur, prev):
        acc = _dot(ya_ref[...], w_ref[0:half, :]) + _dot(yb_ref[...], w_ref[half:2 * half, :])
        cur[...] = DN_ALPHA * h_ref[...] + acc

    def consume(prev):
        o_ref[...] = _layer_norm(prev[...], lw_ref[...], lb_ref[...])

    _lagged_steps(pl.program_id(0), n_tiles, acc_a, acc_b, produce, consume)


def _outproj_ln(ya, yb, col_a, col_b, w, h, lw, lb, rows_per_batch):
    rows = h.shape[0]
    tm = _row_tile(rows_per_batch)
    n_tiles = rows // tm
    half = w.shape[0] // 2
    cur = lambda i: jnp.minimum(i, n_tiles - 1)
    prev = lambda i: jnp.maximum(i - 1, 0)
    vec = pl.BlockSpec((1, D_MODEL), lambda i: (0, 0))
    return pl.pallas_call(
        functools.partial(_outproj_ln_kernel, n_tiles=n_tiles),
        grid=(n_tiles + 1,),
        in_specs=[
            pl.BlockSpec((tm, half), lambda i: (cur(i), col_a)),
            pl.BlockSpec((tm, half), lambda i: (cur(i), col_b)),
            pl.BlockSpec((2 * half, D_MODEL), lambda i: (0, 0), pipeline_mode=pl.Buffered(1)),
            pl.BlockSpec((tm, D_MODEL), lambda i: (cur(i), 0)),
            vec, vec,
        ],
        out_specs=pl.BlockSpec((tm, D_MODEL), lambda i: (prev(i), 0)),
        out_shape=jax.ShapeDtypeStruct((rows, D_MODEL), F32),
        scratch_shapes=[pltpu.VMEM((tm, D_MODEL), F32), pltpu.VMEM((tm, D_MODEL), F32)],
        compiler_params=_params(("arbitrary",)),
        name="outproj_ln",
    )(ya, yb, w, h, lw, lb)


def _mlp(h_ref, w1_ref, w2_ref, lw_ref, lb_ref):
    h = h_ref[...]
    hb = h.astype(BF16)
    acc = jnp.zeros(h.shape, F32)
    step = 1024
    for f in range(0, D_FF, step):
        a = _dot(hb, w1_ref[:, f:f + step])
        a = jnp.square(jnp.maximum(a, 0.0)).astype(BF16)
        acc = acc + _dot(a, w2_ref[f:f + step, :])
    return _layer_norm(DN_ALPHA * h + acc, lw_ref[...], lb_ref[...])


def _mlp_ln_kernel(h_ref, w1_ref, w2_ref, lw_ref, lb_ref, wg_ref, alog_ref, bias_ref, o_ref, ob_ref, g_ref, gt_ref,
                   *, tm, tiles_per_batch):
    out = _mlp(h_ref, w1_ref, w2_ref, lw_ref, lb_ref)
    o_ref[...] = out
    row = (pl.program_id(0) % tiles_per_batch) * tm + lax.broadcasted_iota(jnp.int32, (tm, 1), 0)
    live = row >= PAD_FRONT
    hb = jnp.where(live, out, 0.0).astype(BF16)
    ob_ref[...] = hb
    gates = _ssd_gates(hb, live, wg_ref, alog_ref, bias_ref)
    g_ref[...] = gates
    gt_ref[...] = gates.T


def _mlp_ln_final_kernel(h_ref, w1_ref, w2_ref, lw_ref, lb_ref, o_ref):
    o_ref[...] = _mlp(h_ref, w1_ref, w2_ref, lw_ref, lb_ref)


def _mlp_weight_specs(index):
    return [
        pl.BlockSpec((D_MODEL, D_FF), index, pipeline_mode=pl.Buffered(1)),
        pl.BlockSpec((D_FF, D_MODEL), index, pipeline_mode=pl.Buffered(1)),
        pl.BlockSpec((1, D_MODEL), index),
        pl.BlockSpec((1, D_MODEL), index),
    ]


def _mlp_ln(h, w1, w2, lw, lb, w_gate, alog, bias, rows_per_batch):
    rows = h.shape[0]
    tm = _row_tile(rows_per_batch)
    tile = pl.BlockSpec((tm, D_MODEL), lambda i: (i, 0))
    vec = pl.BlockSpec((1, LANES), lambda i: (0, 0))
    return pl.pallas_call(
        functools.partial(_mlp_ln_kernel, tm=tm, tiles_per_batch=rows_per_batch // tm),
        grid=(rows // tm,),
        in_specs=[tile] + _mlp_weight_specs(lambda i: (0, 0)) + [pl.BlockSpec((D_MODEL, LANES), lambda i: (0, 0)),
                                                                   vec, vec],
        out_specs=[tile, tile, pl.BlockSpec((tm, LANES), lambda i: (i, 0)), pl.BlockSpec((LANES, tm), lambda i: (0, i))],
        out_shape=[jax.ShapeDtypeStruct((rows, D_MODEL), F32), jax.ShapeDtypeStruct((rows, D_MODEL), BF16),
                   jax.ShapeDtypeStruct((rows, LANES), F32), jax.ShapeDtypeStruct((LANES, rows), F32)],
        compiler_params=_params(("arbitrary",)),
        name="mlp_ln",
    )(h, w1, w2, lw, lb, w_gate, alog, bias)


def _mlp_ln_final(h, w1, w2, lw, lb, batch, seq, rows_per_batch):
    tm = next(t for t in (1024, 512, 256, 128) if seq % t == 0)
    tiles = seq // tm
    first_row = PAD_FRONT + N_META
    return pl.pallas_call(
        _mlp_ln_final_kernel,
        grid=(batch, tiles),
        in_specs=[pl.BlockSpec((pl.Element(tm), pl.Element(D_MODEL)),
                               lambda b, i: (pl.multiple_of(b * rows_per_batch + first_row + i * tm, 128), 0))]
        + _mlp_weight_specs(lambda b, i: (0, 0)),
        out_specs=pl.BlockSpec((tm, D_MODEL), lambda b, i: (b * tiles + i, 0)),
        out_shape=jax.ShapeDtypeStruct((batch * seq, D_MODEL), F32),
        compiler_params=_params(("arbitrary", "arbitrary")),
        name="mlp_ln_final",
    )(h, w1, w2, lw, lb)


def _ssd_kernel(*refs, batch):
    n = CHUNK
    z_ref, x_ref, b_ref, c_ref, gcol_ref = refs[:5]
    grow_refs = refs[5:5 + batch]
    dskip_ref, nw_ref, o_ref, s_ref = refs[5 + batch:]
    pair_w = 2 * SSD_HEADDIM
    pairs_per_group = SSD_HPG // 2
    group_w = SSD_HPG * SSD_HEADDIM
    n_pairs = SSD_HEADS // 2

    @pl.when(pl.program_id(0) == 0)
    def _():
        s_ref[...] = jnp.zeros(s_ref.shape, F32)

    t_idx = lax.broadcasted_iota(jnp.int32, (n, n), 0)
    s_idx = lax.broadcasted_iota(jnp.int32, (n, n), 1)
    causal = t_idx >= s_idx
    lower = causal.astype(F32)
    upper = (t_idx <= s_idx).astype(F32)
    lane_v = lax.broadcasted_iota(jnp.int32, (n, pair_w), 1)
    lane_s = lax.broadcasted_iota(jnp.int32, (SSD_DSTATE, pair_w), 1)
    lane_1 = lax.broadcasted_iota(jnp.int32, (1, pair_w), 1)
    bs = range(batch)
    grow = [grow_refs[b][...] for b in bs]
    gcum_col = [_dot_hi(lower, gcol_ref[b]) for b in bs]
    gcum_row = [_dot_hi(grow[b], upper) for b in bs]
    for g in range(SSD_GROUPS):
        q = [c_ref[b, :, SSD_DSTATE * g:SSD_DSTATE * (g + 1)] for b in bs]
        k = [b_ref[b, :, SSD_DSTATE * g:SSD_DSTATE * (g + 1)] for b in bs]
        qf = [q[b].astype(F32) for b in bs]
        kt = [k[b].astype(F32).T for b in bs]
        cb = [_dot_nt(q[b], k[b]) for b in bs]
        ys = [[] for _ in bs]
        for p in range(pairs_per_group):
            pair = g * pairs_per_group + p
            col = group_w * g + pair_w * p
            for b in bs:
                xv = x_ref[b, :, col:col + pair_w]
                state = s_ref[b * n_pairs + pair]
                o = jnp.zeros((n, pair_w), F32)
                upd = jnp.zeros((SSD_DSTATE, pair_w), F32)
                last = []
                for side in range(2):
                    hd = 2 * pair + side
                    gc = gcum_col[b][:, SSD_HEADS + hd:SSD_HEADS + hd + 1]
                    gr = gcum_row[b][SSD_HEADS + hd:SSD_HEADS + hd + 1, :]
                    dt_row = grow[b][hd:hd + 1, :]
                    g_last = gr[:, n - 1:n]
                    dec = jnp.exp(jnp.where(causal, gc - gr, -jnp.inf))
                    att = cb[b] * dec * dt_row
                    q_in = qf[b] * jnp.exp(gc)
                    mine_v = (lane_v >= SSD_HEADDIM) if side else (lane_v < SSD_HEADDIM)
                    mine_s = (lane_s >= SSD_HEADDIM) if side else (lane_s < SSD_HEADDIM)
                    xm = jnp.where(mine_v, xv, jnp.zeros_like(xv))
                    sm = jnp.where(mine_s, state, 0.0).astype(BF16)
                    o = o + _dot(att.astype(BF16), xm) + _dot(q_in.astype(BF16), sm)
                    k_out = kt[b] * (jnp.exp(g_last - gr) * dt_row)
                    upd = upd + _dot(k_out.astype(BF16), xm)
                    last.append(jnp.exp(g_last))
                s_ref[b * n_pairs + pair] = state * jnp.where(lane_1 < SSD_HEADDIM, last[0], last[1]) + upd
                ys[b].append(o + xv.astype(F32) * dskip_ref[:, col:col + pair_w])
        cols = slice(group_w * g, group_w * (g + 1))
        for b in bs:
            y = jnp.concatenate(ys[b], axis=-1)
            y = y * _silu(z_ref[b, :, cols].astype(F32))
            y = y * lax.rsqrt(jnp.mean(y * y, axis=-1, keepdims=True) + RMS_EPS)
            o_ref[b, :, cols] = (y * nw_ref[:, cols]).astype(BF16)


def _ssd(z, xbc, gates, gates_t, dskip, norm_w, batch, rows_per_batch):
    n = CHUNK
    steps = rows_per_batch // n
    z = z.reshape(batch, rows_per_batch, z.shape[1])
    xbc = xbc.reshape(batch, rows_per_batch, xbc.shape[1])
    gates = gates.reshape(batch, rows_per_batch, LANES)
    out = pl.pallas_call(
        functools.partial(_ssd_kernel, batch=batch),
        grid=(steps,),
        in_specs=[
            pl.BlockSpec((batch, n, SSD_DINNER), lambda c: (0, c, 0)),
            pl.BlockSpec((batch, n, SSD_DINNER), lambda c: (0, c, 0)),
            pl.BlockSpec((batch, n, SSD_GN), lambda c: (0, c, 4)),
            pl.BlockSpec((batch, n, SSD_GN), lambda c: (0, c, 5)),
            pl.BlockSpec((batch, n, LANES), lambda c: (0, c, 0)),
        ] + [pl.BlockSpec((2 * SSD_HEADS, n), lambda c, b=b: (0, b * steps + c)) for b in range(batch)] + [
            pl.BlockSpec((1, SSD_DINNER), lambda c: (0, 0)),
            pl.BlockSpec((1, SSD_DINNER), lambda c: (0, 0)),
        ],
        out_specs=pl.BlockSpec((batch, n, SSD_DINNER), lambda c: (0, c, 0)),
        out_shape=jax.ShapeDtypeStruct((batch, rows_per_batch, SSD_DINNER), BF16),
        scratch_shapes=[pltpu.VMEM((batch * SSD_HEADS // 2, SSD_DSTATE, 2 * SSD_HEADDIM), F32)],
        compiler_params=_params(("arbitrary",)),
        name="ssd",
    )(z, xbc, xbc, xbc, gates, *([gates_t] * batch), dskip, norm_w)
    return out.reshape(batch * rows_per_batch, SSD_DINNER)


def _pad_lanes(v):
    v = v.reshape(1, -1).astype(F32)
    return jnp.pad(v, ((0, 0), (0, LANES - v.shape[1])))


def kernel(x, meta_tokens, ab_w_in, ab_ret_gn_w, ab_conv_q, ab_conv_k, ab_conv_v, ab_A_log, ab_dt_bias, ab_gdn_norm_w, ab_w_out, c_w_in, c_conv_w, c_conv_b, c_A_log, c_dt_bias, c_D, c_norm_w, c_w_out, mlp_w1, mlp_w2, ln1_w, ln1_b, ln2_w, ln2_b):
    batch, seq, d = x.shape
    assert d == D_MODEL and meta_tokens.shape == (N_META, D_MODEL)
    lp = PAD_FRONT + N_META + seq
    rows = batch * lp

    meta = jnp.broadcast_to(meta_tokens[None].astype(x.dtype), (batch, N_META, d))
    h = jnp.concatenate([jnp.zeros((batch, PAD_FRONT, d), x.dtype), meta, x], axis=1).reshape(rows, d)
    hb = h.astype(BF16)

    pos = jnp.arange(lp, dtype=F32) - PAD_FRONT
    inv_freq = 1.0 / (ROPE_BASE ** jnp.linspace(0.0, 1.0, RET_DK // 2, dtype=F32))
    ang = pos[:, None] * inv_freq[None]
    cosf = jnp.concatenate([jnp.cos(ang), jnp.cos(ang)], axis=-1)
    sinf = jnp.concatenate([-jnp.sin(ang), jnp.sin(ang)], axis=-1)

    w_in = ab_w_in[:1]
    w_gg = w_in[:, :, 5128:]
    w_gate = jnp.pad(w_in[0, :, 5120:5128], ((0, 0), (0, LANES - 8))).astype(BF16)
    ones = jnp.ones((1, COL_TILE // 2), F32)
    zeros = jnp.zeros((1, COL_TILE), F32)
    p_qk = _proj(hb, w_in, (lambda j: j, 1), _epilogue_rope,
                 [cosf, sinf, jnp.concatenate([ones * RET_DK ** -0.5, ones], axis=1)],
                 [_per_row_in_batch(LANES), _per_row_in_batch(LANES), _per_col()], lp, "proj_ret_qk")
    p_vg = _proj(hb, w_in, (lambda j: j + 1, 2), _epilogue_plain, [], [], lp, "proj_ret_vg")
    p_gg = _proj(hb, w_gg, (lambda j: j, 1), _epilogue_plain, [], [], lp, "proj_gdn_g")
    g_qk = _proj(hb, w_in, (lambda j: j + 3, 1), functools.partial(_epilogue_conv, l2norm=True),
                 [jnp.concatenate([ab_conv_q[0], ab_conv_k[0]], axis=1), zeros,
                  jnp.concatenate([ones * GDN_DK ** -0.5, ones], axis=1)],
                 [_per_col(rows=CONV_K), _per_col(), _per_col()], lp, "proj_gdn_qk")
    g_v = _proj(hb, w_in, (lambda j: j + 4, 1), functools.partial(_epilogue_conv, l2norm=False),
                [ab_conv_v[0], zeros, zeros], [_per_col(rows=CONV_K), _per_col(), _per_col()], lp, "proj_gdn_v")
    alog = _pad_lanes(jnp.concatenate([jnp.zeros((GDN_HEADS,), F32), ab_A_log[0]]))
    bias = _pad_lanes(jnp.concatenate([jnp.zeros((GDN_HEADS,), F32), ab_dt_bias[0]]))
    g0, g0_t = _gates_ab(hb, w_gate, alog, bias, lp)
    y_ret = _retention(p_qk, p_vg, ab_ret_gn_w[0].reshape(1, -1), batch, lp)
    y_gdn = _gdn(g_qk, g_v, p_gg, g0, g0_t, ab_gdn_norm_w[0].reshape(1, -1), batch, lp)
    h = _outproj_ln(y_ret, y_gdn, 0, 0, ab_w_out[0].astype(BF16), h,
                    ln1_w[0].reshape(1, -1), ln1_b[0].reshape(1, -1), lp)
    w_in = c_w_in[:1]
    w_dt = w_in[0, :, 5120:]
    w_gate = jnp.pad(jnp.concatenate([w_dt, w_dt], axis=1), ((0, 0), (0, LANES - 2 * SSD_HEADS))).astype(BF16)
    alog = _pad_lanes(jnp.concatenate([jnp.zeros((SSD_HEADS,), F32), c_A_log[0]]))
    bias = _pad_lanes(jnp.concatenate([c_dt_bias[0], c_dt_bias[0]]))
    h, hb, g1, g1_t = _mlp_ln(h, mlp_w1[0].astype(BF16), mlp_w2[0].astype(BF16),
                              ln2_w[0].reshape(1, -1), ln2_b[0].reshape(1, -1), w_gate, alog, bias, lp)

    p_z = _proj(hb, w_in, (lambda j: j, 2), _epilogue_plain, [], [], lp, "proj_ssd_z")
    p_xbc = _proj(hb, w_in, (lambda j: j + 2, 3), functools.partial(_epilogue_conv, l2norm=False),
                  [c_conv_w[0], c_conv_b[0].reshape(1, -1), jnp.zeros((1, 3 * COL_TILE), F32)],
                  [_per_col(rows=CONV_K), _per_col(), _per_col()], lp, "proj_ssd_xbc")
    dskip = jnp.repeat(c_D[0].astype(F32), SSD_HEADDIM).reshape(1, -1)
    y_ssd = _ssd(p_z, p_xbc, g1, g1_t, dskip, c_norm_w[0].reshape(1, -1), batch, lp)
    h = _outproj_ln(y_ssd, y_ssd, 0, 1, c_w_out[0].astype(BF16), h,
                    ln1_w[1].reshape(1, -1), ln1_b[1].reshape(1, -1), lp)
    out = _mlp_ln_final(h, mlp_w1[1].astype(BF16), mlp_w2[1].astype(BF16),
                        ln2_w[1].reshape(1, -1), ln2_b[1].reshape(1, -1), batch, seq, lp)
    return out.reshape(batch, seq, d)
```

```python
import functools

import jax
import jax.numpy as jnp
from jax import lax
from jax.experimental import pallas as pl
from jax.experimental.pallas import tpu as pltpu

F32 = jnp.float32
BF16 = jnp.bfloat16
HI = lax.Precision.HIGHEST

D_MODEL = 1024
DEPTH = 2
N_META = 16
CONV_K = 4
RET_HEADS = 4
RET_DK = 128
RET_DV = 256
ROPE_BASE = 10000.0
GDN_HEADS = 4
GDN_DK = 128
GDN_DV = 256
SSD_DINNER = 2048
SSD_HEADDIM = 64
SSD_HEADS = 32
SSD_GROUPS = 4
SSD_HPG = 8
SSD_DSTATE = 128
SSD_GN = 512
D_FF = 4096
DN_ALPHA = (2 * DEPTH) ** 0.25
LN_EPS = 1e-5
GN_EPS = 1e-5
RMS_EPS = 1e-6

LANES = 128
COL_TILE = 1024
CHUNK = 128
GDN_CHUNK = 64
GDN_SUBCHUNKS = 2
GDN_BASE = 8
PAD_FRONT = CHUNK - N_META
HALO = 8
CONV_ROWS = 64
PROJ_ROW_TILE = 1664
VMEM_LIMIT = 56 * 1024 * 1024


def _row_tile(rows_per_batch, largest=640):
    for tm in (1664, 640, 512, 256, 128):
        if tm <= largest and rows_per_batch % tm == 0:
            return tm
    raise ValueError(f"unsupported padded sequence length {rows_per_batch}")


def _params(sem):
    return pltpu.CompilerParams(dimension_semantics=sem, vmem_limit_bytes=VMEM_LIMIT)


def _softplus(x):
    return jnp.maximum(x, 0.0) + jnp.log1p(jnp.exp(-jnp.abs(x)))


def _silu(x):
    return x * jax.nn.sigmoid(x)


def _dot(a, b):
    return jnp.dot(a, b, preferred_element_type=F32)


def _dot_nt(a, b):
    return lax.dot_general(a, b, (((1,), (1,)), ((), ())), preferred_element_type=F32)


def _dot_hi(a, b):
    return jnp.dot(a, b, preferred_element_type=F32, precision=HI)


def _wide(a):
    hi = a.astype(BF16).astype(F32)
    return jnp.concatenate([a, a - hi], axis=1).astype(BF16)


def _tall(b):
    hi = b.astype(BF16)
    return jnp.concatenate([hi, hi], axis=0)


def _epilogue_plain(acc_ref, o_ref, *, tm):
    o_ref[...] = acc_ref[HALO:HALO + tm, :].astype(BF16)


def _epilogue_rope(acc_ref, o_ref, cos_ref, sin_ref, scale_ref, *, tm):
    for hd in range(COL_TILE // RET_DK):
        cols = slice(RET_DK * hd, RET_DK * (hd + 1))
        x = acc_ref[HALO:HALO + tm, cols]
        y = x * cos_ref[...] + pltpu.roll(x, RET_DK // 2, 1) * sin_ref[...]
        o_ref[:, cols] = (y * scale_ref[:, cols]).astype(BF16)


def _epilogue_conv(acc_ref, o_ref, cw_ref, cb_ref, scale_ref, *, tm, l2norm):
    for r in range(0, tm, CONV_ROWS):
        rows = slice(r, r + CONV_ROWS)
        for c in range(0, COL_TILE, LANES):
            cols = slice(c, c + LANES)
            w = [cw_ref[tap:tap + 1, cols] for tap in range(CONV_K)]
            xw = acc_ref[r:r + CONV_ROWS + HALO, cols]
            s1 = pltpu.roll(xw, 1, 0)
            y = w[3] * xw + w[2] * s1 + pltpu.roll(w[1] * xw + w[0] * s1, 2, 0)
            y = _silu(y[HALO:] + cb_ref[:, cols])
            if l2norm:
                inv = lax.rsqrt(jnp.sum(y * y, axis=-1, keepdims=True) + 1e-6)
                y = y * (inv * scale_ref[:, cols])
            o_ref[rows, cols] = y.astype(BF16)


def _lagged_steps(i, n_tiles, acc_a, acc_b, produce, consume):
    last_prev = acc_b if n_tiles % 2 == 0 else acc_a

    @pl.when(i == 0)
    def _():
        acc_b[...] = jnp.zeros(acc_b.shape, F32)

    @pl.when(jnp.logical_and(i % 2 == 0, i < n_tiles))
    def _():
        produce(acc_a, acc_b)
        consume(acc_b)

    @pl.when(jnp.logical_and(i % 2 == 1, i < n_tiles))
    def _():
        produce(acc_b, acc_a)
        consume(acc_a)

    @pl.when(i == n_tiles)
    def _():
        consume(last_prev)


def _proj_kernel(*refs, tm, tiles_per_batch, n_extra, epilogue):
    h_ref, w_ref = refs[0], refs[1]
    extra = refs[2:2 + n_extra]
    o_ref, acc_ref, wb_ref = refs[2 + n_extra:]
    i = pl.program_id(1)

    @pl.when(i == 0)
    def _():
        wb_ref[...] = w_ref[...].astype(BF16)

    @pl.when(i % tiles_per_batch == 0)
    def _():
        acc_ref[0:HALO, :] = jnp.zeros((HALO, COL_TILE), F32)

    acc_ref[HALO:HALO + tm, :] = _dot(h_ref[...], wb_ref[...])
    epilogue(acc_ref, o_ref, *extra, tm=tm)
    acc_ref[0:HALO, :] = acc_ref[tm:tm + HALO, :]


def _proj(hb, w, col_tiles, epilogue, extra, extra_specs, rows_per_batch, name):
    rows = hb.shape[0]
    tm = _row_tile(rows_per_batch, largest=PROJ_ROW_TILE)
    tpb = rows_per_batch // tm
    col_fn, ncol = col_tiles
    return pl.pallas_call(
        functools.partial(_proj_kernel, tm=tm, tiles_per_batch=tpb, n_extra=len(extra), epilogue=epilogue),
        grid=(ncol, rows // tm),
        in_specs=[
            pl.BlockSpec((tm, D_MODEL), lambda j, i: (i, 0)),
            pl.BlockSpec((None, D_MODEL, COL_TILE), lambda j, i: (0, 0, col_fn(j))),
        ] + [spec(tm, tpb) for spec in extra_specs],
        out_specs=pl.BlockSpec((tm, COL_TILE), lambda j, i: (i, j)),
        out_shape=jax.ShapeDtypeStruct((rows, ncol * COL_TILE), BF16),
        scratch_shapes=[pltpu.VMEM((tm + HALO, COL_TILE), F32), pltpu.VMEM((D_MODEL, COL_TILE), BF16)],
        compiler_params=_params(("arbitrary", "arbitrary")),
        name=name,
    )(hb, w, *extra)


def _per_col(width=COL_TILE, rows=1):
    return lambda tm, tpb: pl.BlockSpec((rows, width), lambda j, i: (0, j))


def _per_row_in_batch(width):
    return lambda tm, tpb: pl.BlockSpec((tm, width), lambda j, i: (i % tpb, 0))


def _gates_ab_kernel(h_ref, w_ref, alog_ref, bias_ref, o_ref, ot_ref):
    acc = _dot(h_ref[...], w_ref[...])
    lane = lax.broadcasted_iota(jnp.int32, acc.shape, 1)
    decay = -jnp.exp(alog_ref[...]) * _softplus(acc + bias_ref[...])
    gates = jnp.where(lane < GDN_HEADS, jax.nn.sigmoid(acc), decay)
    o_ref[...] = gates
    ot_ref[...] = gates.T


def _ssd_gates(hb, live, w_ref, alog_ref, bias_ref):
    acc = _dot(hb, w_ref[...])
    lane = lax.broadcasted_iota(jnp.int32, acc.shape, 1)
    dt = jnp.where(live, _softplus(acc + bias_ref[...]), 0.0)
    return jnp.where(lane < SSD_HEADS, dt, dt * -jnp.exp(alog_ref[...]))


def _gates_ab(hb, w, alog, bias, rows_per_batch):
    rows = hb.shape[0]
    tm = _row_tile(rows_per_batch)
    vec = pl.BlockSpec((1, LANES), lambda i: (0, 0))
    return pl.pallas_call(
        _gates_ab_kernel,
        grid=(rows // tm,),
        in_specs=[
            pl.BlockSpec((tm, D_MODEL), lambda i: (i, 0)),
            pl.BlockSpec((D_MODEL, LANES), lambda i: (0, 0)),
            vec, vec,
        ],
        out_specs=[pl.BlockSpec((tm, LANES), lambda i: (i, 0)), pl.BlockSpec((LANES, tm), lambda i: (0, i))],
        out_shape=[jax.ShapeDtypeStruct((rows, LANES), F32), jax.ShapeDtypeStruct((LANES, rows), F32)],
        compiler_params=_params(("arbitrary",)),
        name="gates_ab",
    )(hb, w, alog, bias)


def _retention_kernel(q_ref, k_ref, v_ref, g_ref, gnw_ref, o_ref, s_ref, *, batch):
    n = CHUNK

    @pl.when(pl.program_id(0) == 0)
    def _():
        s_ref[...] = jnp.zeros(s_ref.shape, F32)

    t_idx = lax.broadcasted_iota(jnp.int32, (n, n), 0)
    s_idx = lax.broadcasted_iota(jnp.int32, (n, n), 1)
    t_col = lax.broadcasted_iota(jnp.int32, (n, 1), 0).astype(F32)
    gap = (t_idx - s_idx).astype(F32)
    chains = [(b, hd) for b in range(batch) for hd in range(RET_HEADS)]
    cs = range(len(chains))
    log_gamma = [jnp.log1p(-jnp.exp2(jnp.full((1, 1), -5.0 - hd, F32))) for hd in range(RET_HEADS)]
    decay = [jnp.exp(jnp.where(t_idx >= s_idx, gap * lg, -jnp.inf)) for lg in log_gamma]
    e_in = [jnp.exp((t_col + 1.0) * lg) for lg in log_gamma]
    e_out = [jnp.exp((n - 1.0 - t_col) * lg) for lg in log_gamma]
    e_all = [jnp.exp(n * lg) for lg in log_gamma]
    q = [q_ref[b, :, RET_DK * hd:RET_DK * (hd + 1)] for b, hd in chains]
    k = [k_ref[b, :, RET_DK * hd:RET_DK * (hd + 1)] for b, hd in chains]
    v = [v_ref[b, :, RET_DV * hd:RET_DV * (hd + 1)] for b, hd in chains]
    states = [s_ref[c] for c in cs]
    att = [(_dot_nt(q[c], k[c]) * decay[chains[c][1]]).astype(BF16) for c in cs]
    q_in = [(q[c].astype(F32) * e_in[chains[c][1]]).astype(BF16) for c in cs]
    o = [_dot(att[c], v[c]) + _dot(q_in[c], states[c].astype(BF16)) for c in cs]
    k_out = [(k[c].astype(F32) * e_out[chains[c][1]]).T.astype(BF16) for c in cs]
    for c in cs:
        s_ref[c] = states[c] * e_all[chains[c][1]] + _dot(k_out[c], v[c])
    for c, (b, hd) in enumerate(chains):
        cols = slice(RET_DV * hd, RET_DV * (hd + 1))
        oc = o[c] - jnp.mean(o[c], axis=-1, keepdims=True)
        on = oc * lax.rsqrt(jnp.mean(oc * oc, axis=-1, keepdims=True) + GN_EPS)
        gate = _silu(g_ref[b, :, cols].astype(F32))
        o_ref[b, :, cols] = (on * gnw_ref[:, cols] * gate).astype(BF16)


def _retention(qk, vg, gn_w, batch, rows_per_batch):
    n = CHUNK
    nc = rows_per_batch // n
    qk = qk.reshape(batch, rows_per_batch, qk.shape[1])
    vg = vg.reshape(batch, rows_per_batch, vg.shape[1])
    out = pl.pallas_call(
        functools.partial(_retention_kernel, batch=batch),
        grid=(nc,),
        in_specs=[
            pl.BlockSpec((batch, n, 512), lambda c: (0, c, 0)),
            pl.BlockSpec((batch, n, 512), lambda c: (0, c, 1)),
            pl.BlockSpec((batch, n, 1024), lambda c: (0, c, 0)),
            pl.BlockSpec((batch, n, 1024), lambda c: (0, c, 1)),
            pl.BlockSpec((1, 1024), lambda c: (0, 0)),
        ],
        out_specs=pl.BlockSpec((batch, n, 1024), lambda c: (0, c, 0)),
        out_shape=jax.ShapeDtypeStruct((batch, rows_per_batch, 1024), BF16),
        scratch_shapes=[pltpu.VMEM((batch * RET_HEADS, RET_DK, RET_DV), F32)],
        compiler_params=_params(("arbitrary",)),
        name="retention",
    )(qk, qk, vg, vg, gn_w)
    return out.reshape(batch * rows_per_batch, 1024)


def _gdn_kernel(*refs, batch):
    n = GDN_CHUNK
    q_ref, k_ref, v_ref, g_ref, gcol_ref = refs[:5]
    grow_refs = refs[5:5 + batch]
    nw_ref, o_ref, s_ref = refs[5 + batch:]

    @pl.when(pl.program_id(0) == 0)
    def _():
        s_ref[...] = jnp.zeros(s_ref.shape, F32)

    t_idx = lax.broadcasted_iota(jnp.int32, (n, n), 0)
    s_idx = lax.broadcasted_iota(jnp.int32, (n, n), 1)
    lower = (t_idx >= s_idx).astype(F32)
    upper = (t_idx <= s_idx).astype(F32)
    eye = (t_idx == s_idx).astype(F32)
    blocks = [(b, sub) for b in range(batch) for sub in range(GDN_SUBCHUNKS)]
    chains = [(b, sub, hd) for b, sub in blocks for hd in range(GDN_HEADS)]
    cs = range(len(chains))
    rows = lambda sub: slice(n * sub, n * (sub + 1))
    gcol = {bs: gcol_ref[bs[0], rows(bs[1])] for bs in blocks}
    gcum_col = {bs: _dot_hi(lower, gcol[bs]) for bs in blocks}
    gcum_row = {bs: _dot_hi(grow_refs[bs[0]][:, rows(bs[1])], upper) for bs in blocks}
    q = [q_ref[b, rows(sub), GDN_DK * hd:GDN_DK * (hd + 1)].astype(F32) for b, sub, hd in chains]
    k = [k_ref[b, rows(sub), GDN_DK * hd:GDN_DK * (hd + 1)].astype(F32) for b, sub, hd in chains]
    v = [v_ref[b, rows(sub), GDN_DV * hd:GDN_DV * (hd + 1)].astype(F32) for b, sub, hd in chains]
    beta = [gcol[b, sub][:, hd:hd + 1] for b, sub, hd in chains]
    gc = [gcum_col[b, sub][:, GDN_HEADS + hd:GDN_HEADS + hd + 1] for b, sub, hd in chains]
    gr = [gcum_row[b, sub][GDN_HEADS + hd:GDN_HEADS + hd + 1, :] for b, sub, hd in chains]
    g_last = [gc[c][n - 1:n, :] for c in cs]
    seg = [gc[c] - gr[c] for c in cs]
    dec_strict = [jnp.exp(jnp.where(t_idx > s_idx, seg[c], -jnp.inf)) for c in cs]
    dec_incl = [jnp.exp(jnp.where(t_idx >= s_idx, seg[c], -jnp.inf)) for c in cs]
    kb = [k[c] * beta[c] for c in cs]
    kq = [_dot_nt(jnp.concatenate([kb[c], q[c]], axis=0).astype(BF16), k[c].astype(BF16)) for c in cs]
    a = [kq[c][:n] * dec_strict[c] for c in cs]
    att = [(kq[c][n:] * dec_incl[c]).astype(BF16) for c in cs]
    same_block = lambda size: (t_idx // size) == (s_idx // size)
    diag = [jnp.where(same_block(GDN_BASE), a[c], 0.0) for c in cs]
    inv = [eye - diag[c] for c in cs]
    p = [_dot(_wide(diag[c]), _tall(diag[c])) for c in cs]
    span = 4
    while span < GDN_BASE:
        r = [_dot(_wide(jnp.concatenate([p[c], inv[c]], axis=0)), _tall(p[c])) for c in cs]
        inv = [inv[c] + r[c][n:] for c in cs]
        p = [r[c][:n] for c in cs]
        span *= 2
    inv = [inv[c] + _dot(_wide(inv[c]), _tall(p[c])) for c in cs]
    size = GDN_BASE
    while size < n:
        joins = jnp.logical_and(same_block(2 * size), jnp.logical_not(same_block(size)))
        off = [jnp.where(joins, a[c], 0.0) for c in cs]
        half = [_dot(_wide(inv[c]), _tall(off[c])) for c in cs]
        inv = [inv[c] - _dot(_wide(half[c]), _tall(inv[c])) for c in cs]
        size *= 2
    e_gc = [jnp.exp(gc[c]) for c in cs]
    rhs = [jnp.concatenate([v[c] * beta[c], kb[c] * e_gc[c]], axis=-1) for c in cs]
    u = [_dot(_wide(inv[c]), _tall(rhs[c])) for c in cs]
    lhs = [jnp.concatenate([u[c][:, GDN_DV:], q[c] * e_gc[c]], axis=0).astype(BF16) for c in cs]
    k_out = [(k[c] * jnp.exp(g_last[c] - gc[c])).T.astype(BF16) for c in cs]
    e_all = [jnp.exp(g_last[c]) for c in cs]

    lanes = [(b, hd) for b in range(batch) for hd in range(GDN_HEADS)]
    states = [s_ref[i] for i in range(len(lanes))]
    for sub in range(GDN_SUBCHUNKS):
        idx = [chains.index((b, sub, hd)) for b, hd in lanes]
        ws = [_dot(lhs[c], states[i].astype(BF16)) for i, c in enumerate(idx)]
        v_new = [(u[c][:, :GDN_DV] - ws[i][:n]).astype(BF16) for i, c in enumerate(idx)]
        o = [ws[i][n:] + _dot(att[c], v_new[i]) for i, c in enumerate(idx)]
        states = [states[i] * e_all[c] + _dot(k_out[c], v_new[i]) for i, c in enumerate(idx)]
        for i, (b, hd) in enumerate(lanes):
            cols = slice(GDN_DV * hd, GDN_DV * (hd + 1))
            on = o[i] * lax.rsqrt(jnp.mean(o[i] * o[i], axis=-1, keepdims=True) + RMS_EPS)
            gate = _silu(g_ref[b, rows(sub), cols].astype(F32))
            o_ref[b, rows(sub), cols] = (on * nw_ref[...] * gate).astype(BF16)
    for i in range(len(lanes)):
        s_ref[i] = states[i]


def _gdn(qk, v, gate, gates, gates_t, norm_w, batch, rows_per_batch):
    n = GDN_CHUNK
    step_rows = n * GDN_SUBCHUNKS
    assert rows_per_batch % step_rows == 0 and step_rows % LANES == 0
    steps = rows_per_batch // step_rows
    qk = qk.reshape(batch, rows_per_batch, qk.shape[1])
    v = v.reshape(batch, rows_per_batch, v.shape[1])
    gate = gate.reshape(batch, rows_per_batch, gate.shape[1])
    gates = gates.reshape(batch, rows_per_batch, LANES)
    out = pl.pallas_call(
        functools.partial(_gdn_kernel, batch=batch),
        grid=(steps,),
        in_specs=[
            pl.BlockSpec((batch, step_rows, 512), lambda c: (0, c, 0)),
            pl.BlockSpec((batch, step_rows, 512), lambda c: (0, c, 1)),
            pl.BlockSpec((batch, step_rows, 1024), lambda c: (0, c, 0)),
            pl.BlockSpec((batch, step_rows, 1024), lambda c: (0, c, 0)),
            pl.BlockSpec((batch, step_rows, LANES), lambda c: (0, c, 0)),
        ] + [pl.BlockSpec((8, step_rows), lambda c, b=b: (0, b * steps + c)) for b in range(batch)] + [
            pl.BlockSpec((1, GDN_DV), lambda c: (0, 0)),
        ],
        out_specs=pl.BlockSpec((batch, step_rows, 1024), lambda c: (0, c, 0)),
        out_shape=jax.ShapeDtypeStruct((batch, rows_per_batch, 1024), BF16),
        scratch_shapes=[pltpu.VMEM((batch * GDN_HEADS, GDN_DK, GDN_DV), F32)],
        compiler_params=_params(("arbitrary",)),
        name="gated_delta",
    )(qk, qk, v, gate, gates, *([gates_t] * batch), norm_w)
    return out.reshape(batch * rows_per_batch, 1024)


def _layer_norm(x, w, b):
    xc = x - jnp.mean(x, axis=-1, keepdims=True)
    return xc * lax.rsqrt(jnp.mean(xc * xc, axis=-1, keepdims=True) + LN_EPS) * w + b


def _outproj_ln_kernel(ya_ref, yb_ref, w_ref, h_ref, lw_ref, lb_ref, o_ref, acc_a, acc_b, *, n_tiles):
    half = ya_ref.shape[1]

    def produce(cur, prev):
        acc = _dot(ya_ref[...], w_ref[0:half, :]) + _dot(yb_ref[...], w_ref[half:2 * half, :])
        cur[...] = DN_ALPHA * h_ref[...] + acc

    def consume(prev):
        o_ref[...] = _layer_norm(prev[...], lw_ref[...], lb_ref[...])

    _lagged_steps(pl.program_id(0), n_tiles, acc_a, acc_b, produce, consume)


def _outproj_ln(ya, yb, col_a, col_b, w, h, lw, lb, rows_per_batch):
    rows = h.shape[0]
    tm = _row_tile(rows_per_batch)
    n_tiles = rows // tm
    half = w.shape[0] // 2
    cur = lambda i: jnp.minimum(i, n_tiles - 1)
    prev = lambda i: jnp.maximum(i - 1, 0)
    vec = pl.BlockSpec((1, D_MODEL), lambda i: (0, 0))
    return pl.pallas_call(
        functools.partial(_outproj_ln_kernel, n_tiles=n_tiles),
        grid=(n_tiles + 1,),
        in_specs=[
            pl.BlockSpec((tm, half), lambda i: (cur(i), col_a)),
            pl.BlockSpec((tm, half), lambda i: (cur(i), col_b)),
            pl.BlockSpec((2 * half, D_MODEL), lambda i: (0, 0), pipeline_mode=pl.Buffered(1)),
            pl.BlockSpec((tm, D_MODEL), lambda i: (cur(i), 0)),
            vec, vec,
        ],
        out_specs=pl.BlockSpec((tm, D_MODEL), lambda i: (prev(i), 0)),
        out_shape=jax.ShapeDtypeStruct((rows, D_MODEL), F32),
        scratch_shapes=[pltpu.VMEM((tm, D_MODEL), F32), pltpu.VMEM((tm, D_MODEL), F32)],
        compiler_params=_params(("arbitrary",)),
        name="outproj_ln",
    )(ya, yb, w, h, lw, lb)


def _mlp(h_ref, w1_ref, w2_ref, lw_ref, lb_ref):
    h = h_ref[...]
    hb = h.astype(BF16)
    acc = jnp.zeros(h.shape, F32)
    step = 1024
    for f in range(0, D_FF, step):
        a = _dot(hb, w1_ref[:, f:f + step])
        a = jnp.square(jnp.maximum(a, 0.0)).astype(BF16)
        acc = acc + _dot(a, w2_ref[f:f + step, :])
    return _layer_norm(DN_ALPHA * h + acc, lw_ref[...], lb_ref[...])


def _mlp_ln_kernel(h_ref, w1_ref, w2_ref, lw_ref, lb_ref, wg_ref, alog_ref, bias_ref, o_ref, ob_ref, g_ref, gt_ref,
                   *, tm, tiles_per_batch):
    out = _mlp(h_ref, w1_ref, w2_ref, lw_ref, lb_ref)
    o_ref[...] = out
    row = (pl.program_id(0) % tiles_per_batch) * tm + lax.broadcasted_iota(jnp.int32, (tm, 1), 0)
    live = row >= PAD_FRONT
    hb = jnp.where(live, out, 0.0).astype(BF16)
    ob_ref[...] = hb
    gates = _ssd_gates(hb, live, wg_ref, alog_ref, bias_ref)
    g_ref[...] = gates
    gt_ref[...] = gates.T


def _mlp_ln_final_kernel(h_ref, w1_ref, w2_ref, lw_ref, lb_ref, o_ref):
    o_ref[...] = _mlp(h_ref, w1_ref, w2_ref, lw_ref, lb_ref)


def _mlp_weight_specs(index):
    return [
        pl.BlockSpec((D_MODEL, D_FF), index, pipeline_mode=pl.Buffered(1)),
        pl.BlockSpec((D_FF, D_MODEL), index, pipeline_mode=pl.Buffered(1)),
        pl.BlockSpec((1, D_MODEL), index),
        pl.BlockSpec((1, D_MODEL), index),
    ]


def _mlp_ln(h, w1, w2, lw, lb, w_gate, alog, bias, rows_per_batch):
    rows = h.shape[0]
    tm = _row_tile(rows_per_batch)
    tile = pl.BlockSpec((tm, D_MODEL), lambda i: (i, 0))
    vec = pl.BlockSpec((1, LANES), lambda i: (0, 0))
    return pl.pallas_call(
        functools.partial(_mlp_ln_kernel, tm=tm, tiles_per_batch=rows_per_batch // tm),
        grid=(rows // tm,),
        in_specs=[tile] + _mlp_weight_specs(lambda i: (0, 0)) + [pl.BlockSpec((D_MODEL, LANES), lambda i: (0, 0)),
                                                                   vec, vec],
        out_specs=[tile, tile, pl.BlockSpec((tm, LANES), lambda i: (i, 0)), pl.BlockSpec((LANES, tm), lambda i: (0, i))],
        out_shape=[jax.ShapeDtypeStruct((rows, D_MODEL), F32), jax.ShapeDtypeStruct((rows, D_MODEL), BF16),
                   jax.ShapeDtypeStruct((rows, LANES), F32), jax.ShapeDtypeStruct((LANES, rows), F32)],
        compiler_params=_params(("arbitrary",)),
        name="mlp_ln",
    )(h, w1, w2, lw, lb, w_gate, alog, bias)


def _mlp_ln_final(h, w1, w2, lw, lb, batch, seq, rows_per_batch):
    tm = next(t for t in (1024, 512, 256, 128) if seq % t == 0)
    tiles = seq // tm
    first_row = PAD_FRONT + N_META
    return pl.pallas_call(
        _mlp_ln_final_kernel,
        grid=(batch, tiles),
        in_specs=[pl.BlockSpec((pl.Element(tm), pl.Element(D_MODEL)),
                               lambda b, i: (pl.multiple_of(b * rows_per_batch + first_row + i * tm, 128), 0))]
        + _mlp_weight_specs(lambda b, i: (0, 0)),
        out_specs=pl.BlockSpec((tm, D_MODEL), lambda b, i: (b * tiles + i, 0)),
        out_shape=jax.ShapeDtypeStruct((batch * seq, D_MODEL), F32),
        compiler_params=_params(("arbitrary", "arbitrary")),
        name="mlp_ln_final",
    )(h, w1, w2, lw, lb)


def _ssd_kernel(*refs, batch):
    n = CHUNK
    z_ref, x_ref, b_ref, c_ref, gcol_ref = refs[:5]
    grow_refs = refs[5:5 + batch]
    dskip_ref, nw_ref, o_ref, s_ref = refs[5 + batch:]
    pair_w = 2 * SSD_HEADDIM
    pairs_per_group = SSD_HPG // 2
    group_w = SSD_HPG * SSD_HEADDIM
    n_pairs = SSD_HEADS // 2

    @pl.when(pl.program_id(0) == 0)
    def _():
        s_ref[...] = jnp.zeros(s_ref.shape, F32)

    t_idx = lax.broadcasted_iota(jnp.int32, (n, n), 0)
    s_idx = lax.broadcasted_iota(jnp.int32, (n, n), 1)
    causal = t_idx >= s_idx
    lower = causal.astype(F32)
    upper = (t_idx <= s_idx).astype(F32)
    lane_v = lax.broadcasted_iota(jnp.int32, (n, pair_w), 1)
    lane_s = lax.broadcasted_iota(jnp.int32, (SSD_DSTATE, pair_w), 1)
    lane_1 = lax.broadcasted_iota(jnp.int32, (1, pair_w), 1)
    bs = range(batch)
    grow = [grow_refs[b][...] for b in bs]
    gcum_col = [_dot_hi(lower, gcol_ref[b]) for b in bs]
    gcum_row = [_dot_hi(grow[b], upper) for b in bs]
    for g in range(SSD_GROUPS):
        q = [c_ref[b, :, SSD_DSTATE * g:SSD_DSTATE * (g + 1)] for b in bs]
        k = [b_ref[b, :, SSD_DSTATE * g:SSD_DSTATE * (g + 1)] for b in bs]
        qf = [q[b].astype(F32) for b in bs]
        kt = [k[b].astype(F32).T for b in bs]
        cb = [_dot_nt(q[b], k[b]) for b in bs]
        ys = [[] for _ in bs]
        for p in range(pairs_per_group):
            pair = g * pairs_per_group + p
            col = group_w * g + pair_w * p
            for b in bs:
                xv = x_ref[b, :, col:col + pair_w]
                state = s_ref[b * n_pairs + pair]
                o = jnp.zeros((n, pair_w), F32)
                upd = jnp.zeros((SSD_DSTATE, pair_w), F32)
                last = []
                for side in range(2):
                    hd = 2 * pair + side
                    gc = gcum_col[b][:, SSD_HEADS + hd:SSD_HEADS + hd + 1]
                    gr = gcum_row[b][SSD_HEADS + hd:SSD_HEADS + hd + 1, :]
                    dt_row = grow[b][hd:hd + 1, :]
                    g_last = gr[:, n - 1:n]
                    dec = jnp.exp(jnp.where(causal, gc - gr, -jnp.inf))
                    att = cb[b] * dec * dt_row
                    q_in = qf[b] * jnp.exp(gc)
                    mine_v = (lane_v >= SSD_HEADDIM) if side else (lane_v < SSD_HEADDIM)
                    mine_s = (lane_s >= SSD_HEADDIM) if side else (lane_s < SSD_HEADDIM)
                    xm = jnp.where(mine_v, xv, jnp.zeros_like(xv))
                    sm = jnp.where(mine_s, state, 0.0).astype(BF16)
                    o = o + _dot(att.astype(BF16), xm) + _dot(q_in.astype(BF16), sm)
                    k_out = kt[b] * (jnp.exp(g_last - gr) * dt_row)
                    upd = upd + _dot(k_out.astype(BF16), xm)
                    last.append(jnp.exp(g_last))
                s_ref[b * n_pairs + pair] = state * jnp.where(lane_1 < SSD_HEADDIM, last[0], last[1]) + upd
                ys[b].append(o + xv.astype(F32) * dskip_ref[:, col:col + pair_w])
        cols = slice(group_w * g, group_w * (g + 1))
        for b in bs:
            y = jnp.concatenate(ys[b], axis=-1)
            y = y * _silu(z_ref[b, :, cols].astype(F32))
            y = y * lax.rsqrt(jnp.mean(y * y, axis=-1, keepdims=True) + RMS_EPS)
            o_ref[b, :, cols] = (y * nw_ref[:, cols]).astype(BF16)


def _ssd(z, xbc, gates, gates_t, dskip, norm_w, batch, rows_per_batch):
    n = CHUNK
    steps = rows_per_batch // n
    z = z.reshape(batch, rows_per_batch, z.shape[1])
    xbc = xbc.reshape(batch, rows_per_batch, xbc.shape[1])
    gates = gates.reshape(batch, rows_per_batch, LANES)
    out = pl.pallas_call(
        functools.partial(_ssd_kernel, batch=batch),
        grid=(steps,),
        in_specs=[
            pl.BlockSpec((batch, n, SSD_DINNER), lambda c: (0, c, 0)),
            pl.BlockSpec((batch, n, SSD_DINNER), lambda c: (0, c, 0)),
            pl.BlockSpec((batch, n, SSD_GN), lambda c: (0, c, 4)),
            pl.BlockSpec((batch, n, SSD_GN), lambda c: (0, c, 5)),
            pl.BlockSpec((batch, n, LANES), lambda c: (0, c, 0)),
        ] + [pl.BlockSpec((2 * SSD_HEADS, n), lambda c, b=b: (0, b * steps + c)) for b in range(batch)] + [
            pl.BlockSpec((1, SSD_DINNER), lambda c: (0, 0)),
            pl.BlockSpec((1, SSD_DINNER), lambda c: (0, 0)),
        ],
        out_specs=pl.BlockSpec((batch, n, SSD_DINNER), lambda c: (0, c, 0)),
        out_shape=jax.ShapeDtypeStruct((batch, rows_per_batch, SSD_DINNER), BF16),
        scratch_shapes=[pltpu.VMEM((batch * SSD_HEADS // 2, SSD_DSTATE, 2 * SSD_HEADDIM), F32)],
        compiler_params=_params(("arbitrary",)),
        name="ssd",
    )(z, xbc, xbc, xbc, gates, *([gates_t] * batch), dskip, norm_w)
    return out.reshape(batch * rows_per_batch, SSD_DINNER)


def _pad_lanes(v):
    v = v.reshape(1, -1).astype(F32)
    return jnp.pad(v, ((0, 0), (0, LANES - v.shape[1])))


def kernel(x, meta_tokens, ab_w_in, ab_ret_gn_w, ab_conv_q, ab_conv_k, ab_conv_v, ab_A_log, ab_dt_bias, ab_gdn_norm_w, ab_w_out, c_w_in, c_conv_w, c_conv_b, c_A_log, c_dt_bias, c_D, c_norm_w, c_w_out, mlp_w1, mlp_w2, ln1_w, ln1_b, ln2_w, ln2_b):
    batch, seq, d = x.shape
    assert d == D_MODEL and meta_tokens.shape == (N_META, D_MODEL)
    lp = PAD_FRONT + N_META + seq
    rows = batch * lp

    meta = jnp.broadcast_to(meta_tokens[None].astype(x.dtype), (batch, N_META, d))
    h = jnp.concatenate([jnp.zeros((batch, PAD_FRONT, d), x.dtype), meta, x], axis=1).reshape(rows, d)
    hb = h.astype(BF16)

    pos = jnp.arange(lp, dtype=F32) - PAD_FRONT
    inv_freq = 1.0 / (ROPE_BASE ** jnp.linspace(0.0, 1.0, RET_DK // 2, dtype=F32))
    ang = pos[:, None] * inv_freq[None]
    cosf = jnp.concatenate([jnp.cos(ang), jnp.cos(ang)], axis=-1)
    sinf = jnp.concatenate([-jnp.sin(ang), jnp.sin(ang)], axis=-1)

    w_in = ab_w_in[:1]
    w_gg = w_in[:, :, 5128:]
    w_gate = jnp.pad(w_in[0, :, 5120:5128], ((0, 0), (0, LANES - 8))).astype(BF16)
    ones = jnp.ones((1, COL_TILE // 2), F32)
    zeros = jnp.zeros((1, COL_TILE), F32)
    p_qk = _proj(hb, w_in, (lambda j: j, 1), _epilogue_rope,
                 [cosf, sinf, jnp.concatenate([ones * RET_DK ** -0.5, ones], axis=1)],
                 [_per_row_in_batch(LANES), _per_row_in_batch(LANES), _per_col()], lp, "proj_ret_qk")
    p_vg = _proj(hb, w_in, (lambda j: j + 1, 2), _epilogue_plain, [], [], lp, "proj_ret_vg")
    p_gg = _proj(hb, w_gg, (lambda j: j, 1), _epilogue_plain, [], [], lp, "proj_gdn_g")
    g_qk = _proj(hb, w_in, (lambda j: j + 3, 1), functools.partial(_epilogue_conv, l2norm=True),
                 [jnp.concatenate([ab_conv_q[0], ab_conv_k[0]], axis=1), zeros,
                  jnp.concatenate([ones * GDN_DK ** -0.5, ones], axis=1)],
                 [_per_col(rows=CONV_K), _per_col(), _per_col()], lp, "proj_gdn_qk")
    g_v = _proj(hb, w_in, (lambda j: j + 4, 1), functools.partial(_epilogue_conv, l2norm=False),
                [ab_conv_v[0], zeros, zeros], [_per_col(rows=CONV_K), _per_col(), _per_col()], lp, "proj_gdn_v")
    alog = _pad_lanes(jnp.concatenate([jnp.zeros((GDN_HEADS,), F32), ab_A_log[0]]))
    bias = _pad_lanes(jnp.concatenate([jnp.zeros((GDN_HEADS,), F32), ab_dt_bias[0]]))
    g0, g0_t = _gates_ab(hb, w_gate, alog, bias, lp)
    y_ret = _retention(p_qk, p_vg, ab_ret_gn_w[0].reshape(1, -1), batch, lp)
    y_gdn = _gdn(g_qk, g_v, p_gg, g0, g0_t, ab_gdn_norm_w[0].reshape(1, -1), batch, lp)
    h = _outproj_ln(y_ret, y_gdn, 0, 0, ab_w_out[0].astype(BF16), h,
                    ln1_w[0].reshape(1, -1), ln1_b[0].reshape(1, -1), lp)
    w_in = c_w_in[:1]
    w_dt = w_in[0, :, 5120:]
    w_gate = jnp.pad(jnp.concatenate([w_dt, w_dt], axis=1), ((0, 0), (0, LANES - 2 * SSD_HEADS))).astype(BF16)
    alog = _pad_lanes(jnp.concatenate([jnp.zeros((SSD_HEADS,), F32), c_A_log[0]]))
    bias = _pad_lanes(jnp.concatenate([c_dt_bias[0], c_dt_bias[0]]))
    h, hb, g1, g1_t = _mlp_ln(h, mlp_w1[0].astype(BF16), mlp_w2[0].astype(BF16),
                              ln2_w[0].reshape(1, -1), ln2_b[0].reshape(1, -1), w_gate, alog, bias, lp)

    p_z = _proj(hb, w_in, (lambda j: j, 2), _epilogue_plain, [], [], lp, "proj_ssd_z")
    p_xbc = _proj(hb, w_in, (lambda j: j + 2, 3), functools.partial(_epilogue_conv, l2norm=False),
                  [c_conv_w[0], c_conv_b[0].reshape(1, -1), jnp.zeros((1, 3 * COL_TILE), F32)],
                  [_per_col(rows=CONV_K), _per_col(), _per_col()], lp, "proj_ssd_xbc")
    dskip = jnp.repeat(c_D[0].astype(F32), SSD_HEADDIM).reshape(1, -1)
    y_ssd = _ssd(p_z, p_xbc, g1, g1_t, dskip, c_norm_w[0].reshape(1, -1), batch, lp)
    h = _outproj_ln(y_ssd, y_ssd, 0, 1, c_w_out[0].astype(BF16), h,
                    ln1_w[1].reshape(1, -1), ln1_b[1].reshape(1, -1), lp)
    out = _mlp_ln_final(h, mlp_w1[1].astype(BF16), mlp_w2[1].astype(BF16),
                        ln2_w[1].reshape(1, -1), ln2_b[1].reshape(1, -1), batch, seq, lp)
    return out.reshape(batch, seq, d)
```

```python
import functools

import jax
import jax.numpy as jnp
from jax import lax
from jax.experimental import pallas as pl
from jax.experimental.pallas import tpu as pltpu

F32 = jnp.float32
BF16 = jnp.bfloat16
HI = lax.Precision.HIGHEST

D_MODEL = 1024
DEPTH = 2
N_META = 16
CONV_K = 4
RET_HEADS = 4
RET_DK = 128
RET_DV = 256
ROPE_BASE = 10000.0
GDN_HEADS = 4
GDN_DK = 128
GDN_DV = 256
SSD_DINNER = 2048
SSD_HEADDIM = 64
SSD_HEADS = 32
SSD_GROUPS = 4
SSD_HPG = 8
SSD_DSTATE = 128
SSD_GN = 512
D_FF = 4096
DN_ALPHA = (2 * DEPTH) ** 0.25
LN_EPS = 1e-5
GN_EPS = 1e-5
RMS_EPS = 1e-6

LANES = 128
COL_TILE = 1024
CHUNK = 128
GDN_CHUNK = 64
GDN_SUBCHUNKS = 2
GDN_BASE = 8
PAD_FRONT = CHUNK - N_META
HALO = 8
CONV_ROWS = 64
PROJ_ROW_TILE = 1664
VMEM_LIMIT = 56 * 1024 * 1024


def _row_tile(rows_per_batch, largest=640):
    for tm in (1664, 640, 512, 256, 128):
        if tm <= largest and rows_per_batch % tm == 0:
            return tm
    raise ValueError(f"unsupported padded sequence length {rows_per_batch}")


def _params(sem):
    return pltpu.CompilerParams(dimension_semantics=sem, vmem_limit_bytes=VMEM_LIMIT)


def _softplus(x):
    return jnp.maximum(x, 0.0) + jnp.log1p(jnp.exp(-jnp.abs(x)))


def _silu(x):
    return x * jax.nn.sigmoid(x)


def _dot(a, b):
    return jnp.dot(a, b, preferred_element_type=F32)


def _dot_nt(a, b):
    return lax.dot_general(a, b, (((1,), (1,)), ((), ())), preferred_element_type=F32)


def _dot_hi(a, b):
    return jnp.dot(a, b, preferred_element_type=F32, precision=HI)


def _dot16(a, b):
    return _dot(a.astype(BF16), b.astype(BF16))


def _epilogue_plain(acc_ref, o_ref, *, tm):
    o_ref[...] = acc_ref[HALO:HALO + tm, :].astype(BF16)


def _epilogue_rope(acc_ref, o_ref, cos_ref, sin_ref, scale_ref, *, tm):
    for hd in range(COL_TILE // RET_DK):
        cols = slice(RET_DK * hd, RET_DK * (hd + 1))
        x = acc_ref[HALO:HALO + tm, cols]
        y = x * cos_ref[...] + pltpu.roll(x, RET_DK // 2, 1) * sin_ref[...]
        o_ref[:, cols] = (y * scale_ref[:, cols]).astype(BF16)


def _epilogue_conv(acc_ref, o_ref, cw_ref, cb_ref, scale_ref, *, tm, l2norm):
    for r in range(0, tm, CONV_ROWS):
        rows = slice(r, r + CONV_ROWS)
        for c in range(0, COL_TILE, LANES):
            cols = slice(c, c + LANES)
            w = [cw_ref[tap:tap + 1, cols] for tap in range(CONV_K)]
            xw = acc_ref[r:r + CONV_ROWS + HALO, cols]
            s1 = pltpu.roll(xw, 1, 0)
            y = w[3] * xw + w[2] * s1 + pltpu.roll(w[1] * xw + w[0] * s1, 2, 0)
            y = _silu(y[HALO:] + cb_ref[:, cols])
            if l2norm:
                inv = lax.rsqrt(jnp.sum(y * y, axis=-1, keepdims=True) + 1e-6)
                y = y * (inv * scale_ref[:, cols])
            o_ref[rows, cols] = y.astype(BF16)


def _lagged_steps(i, n_tiles, acc_a, acc_b, produce, consume):
    last_prev = acc_b if n_tiles % 2 == 0 else acc_a

    @pl.when(i == 0)
    def _():
        acc_b[...] = jnp.zeros(acc_b.shape, F32)

    @pl.when(jnp.logical_and(i % 2 == 0, i < n_tiles))
    def _():
        produce(acc_a, acc_b)
        consume(acc_b)

    @pl.when(jnp.logical_and(i % 2 == 1, i < n_tiles))
    def _():
        produce(acc_b, acc_a)
        consume(acc_a)

    @pl.when(i == n_tiles)
    def _():
        consume(last_prev)


def _proj_kernel(*refs, tm, tiles_per_batch, n_extra, epilogue):
    h_ref, w_ref = refs[0], refs[1]
    extra = refs[2:2 + n_extra]
    o_ref, acc_ref = refs[2 + n_extra:]

    @pl.when(pl.program_id(1) % tiles_per_batch == 0)
    def _():
        acc_ref[0:HALO, :] = jnp.zeros((HALO, COL_TILE), F32)

    acc_ref[HALO:HALO + tm, :] = _dot(h_ref[...], w_ref[...])
    epilogue(acc_ref, o_ref, *extra, tm=tm)
    acc_ref[0:HALO, :] = acc_ref[tm:tm + HALO, :]


def _proj(hb, w, col_tiles, epilogue, extra, extra_specs, rows_per_batch, name):
    rows = hb.shape[0]
    tm = _row_tile(rows_per_batch, largest=PROJ_ROW_TILE)
    tpb = rows_per_batch // tm
    col_fn, ncol = col_tiles
    return pl.pallas_call(
        functools.partial(_proj_kernel, tm=tm, tiles_per_batch=tpb, n_extra=len(extra), epilogue=epilogue),
        grid=(ncol, rows // tm),
        in_specs=[
            pl.BlockSpec((tm, D_MODEL), lambda j, i: (i, 0)),
            pl.BlockSpec((None, D_MODEL, COL_TILE), lambda j, i: (0, 0, col_fn(j))),
        ] + [spec(tm, tpb) for spec in extra_specs],
        out_specs=pl.BlockSpec((tm, COL_TILE), lambda j, i: (i, j)),
        out_shape=jax.ShapeDtypeStruct((rows, ncol * COL_TILE), BF16),
        scratch_shapes=[pltpu.VMEM((tm + HALO, COL_TILE), F32)],
        compiler_params=_params(("arbitrary", "arbitrary")),
        name=name,
    )(hb, w, *extra)


def _per_col(width=COL_TILE, rows=1):
    return lambda tm, tpb: pl.BlockSpec((rows, width), lambda j, i: (0, j))


def _per_row_in_batch(width):
    return lambda tm, tpb: pl.BlockSpec((tm, width), lambda j, i: (i % tpb, 0))


def _gates_ab_kernel(h_ref, w_ref, alog_ref, bias_ref, o_ref, ot_ref):
    acc = _dot(h_ref[...], w_ref[...])
    lane = lax.broadcasted_iota(jnp.int32, acc.shape, 1)
    decay = -jnp.exp(alog_ref[...]) * _softplus(acc + bias_ref[...])
    gates = jnp.where(lane < GDN_HEADS, jax.nn.sigmoid(acc), decay)
    o_ref[...] = gates
    ot_ref[...] = gates.T


def _ssd_gates(hb, live, w_ref, alog_ref, bias_ref):
    acc = _dot(hb, w_ref[...])
    lane = lax.broadcasted_iota(jnp.int32, acc.shape, 1)
    dt = jnp.where(live, _softplus(acc + bias_ref[...]), 0.0)
    return jnp.where(lane < SSD_HEADS, dt, dt * -jnp.exp(alog_ref[...]))


def _gates_ab(hb, w, alog, bias, rows_per_batch):
    rows = hb.shape[0]
    tm = _row_tile(rows_per_batch)
    vec = pl.BlockSpec((1, LANES), lambda i: (0, 0))
    return pl.pallas_call(
        _gates_ab_kernel,
        grid=(rows // tm,),
        in_specs=[
            pl.BlockSpec((tm, D_MODEL), lambda i: (i, 0)),
            pl.BlockSpec((D_MODEL, LANES), lambda i: (0, 0)),
            vec, vec,
        ],
        out_specs=[pl.BlockSpec((tm, LANES), lambda i: (i, 0)), pl.BlockSpec((LANES, tm), lambda i: (0, i))],
        out_shape=[jax.ShapeDtypeStruct((rows, LANES), F32), jax.ShapeDtypeStruct((LANES, rows), F32)],
        compiler_params=_params(("arbitrary",)),
        name="gates_ab",
    )(hb, w, alog, bias)


def _retention_kernel(q_ref, k_ref, v_ref, g_ref, gnw_ref, o_ref, s_ref, *, batch):
    n = CHUNK

    @pl.when(pl.program_id(0) == 0)
    def _():
        s_ref[...] = jnp.zeros(s_ref.shape, F32)

    t_idx = lax.broadcasted_iota(jnp.int32, (n, n), 0)
    s_idx = lax.broadcasted_iota(jnp.int32, (n, n), 1)
    t_col = lax.broadcasted_iota(jnp.int32, (n, 1), 0).astype(F32)
    gap = (t_idx - s_idx).astype(F32)
    chains = [(b, hd) for b in range(batch) for hd in range(RET_HEADS)]
    cs = range(len(chains))
    log_gamma = [jnp.log1p(-jnp.exp2(jnp.full((1, 1), -5.0 - hd, F32))) for hd in range(RET_HEADS)]
    decay = [jnp.exp(jnp.where(t_idx >= s_idx, gap * lg, -jnp.inf)) for lg in log_gamma]
    e_in = [jnp.exp((t_col + 1.0) * lg) for lg in log_gamma]
    e_out = [jnp.exp((n - 1.0 - t_col) * lg) for lg in log_gamma]
    e_all = [jnp.exp(n * lg) for lg in log_gamma]
    q = [q_ref[b, :, RET_DK * hd:RET_DK * (hd + 1)] for b, hd in chains]
    k = [k_ref[b, :, RET_DK * hd:RET_DK * (hd + 1)] for b, hd in chains]
    v = [v_ref[b, :, RET_DV * hd:RET_DV * (hd + 1)] for b, hd in chains]
    states = [s_ref[c] for c in cs]
    att = [(_dot_nt(q[c], k[c]) * decay[chains[c][1]]).astype(BF16) for c in cs]
    q_in = [(q[c].astype(F32) * e_in[chains[c][1]]).astype(BF16) for c in cs]
    o = [_dot(att[c], v[c]) + _dot(q_in[c], states[c].astype(BF16)) for c in cs]
    k_out = [(k[c].astype(F32) * e_out[chains[c][1]]).T.astype(BF16) for c in cs]
    for c in cs:
        s_ref[c] = states[c] * e_all[chains[c][1]] + _dot(k_out[c], v[c])
    for c, (b, hd) in enumerate(chains):
        cols = slice(RET_DV * hd, RET_DV * (hd + 1))
        oc = o[c] - jnp.mean(o[c], axis=-1, keepdims=True)
        on = oc * lax.rsqrt(jnp.mean(oc * oc, axis=-1, keepdims=True) + GN_EPS)
        gate = _silu(g_ref[b, :, cols].astype(F32))
        o_ref[b, :, cols] = (on * gnw_ref[:, cols] * gate).astype(BF16)


def _retention(qk, vg, gn_w, batch, rows_per_batch):
    n = CHUNK
    nc = rows_per_batch // n
    qk = qk.reshape(batch, rows_per_batch, qk.shape[1])
    vg = vg.reshape(batch, rows_per_batch, vg.shape[1])
    out = pl.pallas_call(
        functools.partial(_retention_kernel, batch=batch),
        grid=(nc,),
        in_specs=[
            pl.BlockSpec((batch, n, 512), lambda c: (0, c, 0)),
            pl.BlockSpec((batch, n, 512), lambda c: (0, c, 1)),
            pl.BlockSpec((batch, n, 1024), lambda c: (0, c, 0)),
            pl.BlockSpec((batch, n, 1024), lambda c: (0, c, 1)),
            pl.BlockSpec((1, 1024), lambda c: (0, 0)),
        ],
        out_specs=pl.BlockSpec((batch, n, 1024), lambda c: (0, c, 0)),
        out_shape=jax.ShapeDtypeStruct((batch, rows_per_batch, 1024), BF16),
        scratch_shapes=[pltpu.VMEM((batch * RET_HEADS, RET_DK, RET_DV), F32)],
        compiler_params=_params(("arbitrary",)),
        name="retention",
    )(qk, qk, vg, vg, gn_w)
    return out.reshape(batch * rows_per_batch, 1024)


def _gdn_kernel(*refs, batch):
    n = GDN_CHUNK
    q_ref, k_ref, v_ref, g_ref, gcol_ref = refs[:5]
    grow_refs = refs[5:5 + batch]
    nw_ref, o_ref, s_ref = refs[5 + batch:]

    @pl.when(pl.program_id(0) == 0)
    def _():
        s_ref[...] = jnp.zeros(s_ref.shape, F32)

    t_idx = lax.broadcasted_iota(jnp.int32, (n, n), 0)
    s_idx = lax.broadcasted_iota(jnp.int32, (n, n), 1)
    lower = (t_idx >= s_idx).astype(F32)
    upper = (t_idx <= s_idx).astype(F32)
    eye = (t_idx == s_idx).astype(F32)
    blocks = [(b, sub) for b in range(batch) for sub in range(GDN_SUBCHUNKS)]
    chains = [(b, sub, hd) for b, sub in blocks for hd in range(GDN_HEADS)]
    cs = range(len(chains))
    rows = lambda sub: slice(n * sub, n * (sub + 1))
    gcol = {bs: gcol_ref[bs[0], rows(bs[1])] for bs in blocks}
    gcum_col = {bs: _dot_hi(lower, gcol[bs]) for bs in blocks}
    gcum_row = {bs: _dot_hi(grow_refs[bs[0]][:, rows(bs[1])], upper) for bs in blocks}
    q = [q_ref[b, rows(sub), GDN_DK * hd:GDN_DK * (hd + 1)].astype(F32) for b, sub, hd in chains]
    k = [k_ref[b, rows(sub), GDN_DK * hd:GDN_DK * (hd + 1)].astype(F32) for b, sub, hd in chains]
    v = [v_ref[b, rows(sub), GDN_DV * hd:GDN_DV * (hd + 1)].astype(F32) for b, sub, hd in chains]
    beta = [gcol[b, sub][:, hd:hd + 1] for b, sub, hd in chains]
    gc = [gcum_col[b, sub][:, GDN_HEADS + hd:GDN_HEADS + hd + 1] for b, sub, hd in chains]
    gr = [gcum_row[b, sub][GDN_HEADS + hd:GDN_HEADS + hd + 1, :] for b, sub, hd in chains]
    g_last = [gc[c][n - 1:n, :] for c in cs]
    seg = [gc[c] - gr[c] for c in cs]
    dec_strict = [jnp.exp(jnp.where(t_idx > s_idx, seg[c], -jnp.inf)) for c in cs]
    dec_incl = [jnp.exp(jnp.where(t_idx >= s_idx, seg[c], -jnp.inf)) for c in cs]
    kb = [k[c] * beta[c] for c in cs]
    kq = [_dot_nt(jnp.concatenate([kb[c], q[c]], axis=0).astype(BF16), k[c].astype(BF16)) for c in cs]
    a = [kq[c][:n] * dec_strict[c] for c in cs]
    att = [(kq[c][n:] * dec_incl[c]).astype(BF16) for c in cs]
    same_block = lambda size: (t_idx // size) == (s_idx // size)
    diag = [jnp.where(same_block(GDN_BASE), a[c], 0.0) for c in cs]
    inv = [eye - diag[c] for c in cs]
    p = [_dot16(diag[c], diag[c]) for c in cs]
    span = 4
    while span < GDN_BASE:
        r = [_dot16(jnp.concatenate([p[c], inv[c]], axis=0), p[c]) for c in cs]
        inv = [inv[c] + r[c][n:] for c in cs]
        p = [r[c][:n] for c in cs]
        span *= 2
    inv = [inv[c] + _dot16(inv[c], p[c]) for c in cs]
    size = GDN_BASE
    while size < n:
        joins = jnp.logical_and(same_block(2 * size), jnp.logical_not(same_block(size)))
        off = [jnp.where(joins, a[c], 0.0) for c in cs]
        half = [_dot16(inv[c], off[c]) for c in cs]
        inv = [inv[c] - _dot16(half[c], inv[c]) for c in cs]
        size *= 2
    e_gc = [jnp.exp(gc[c]) for c in cs]
    rhs = [jnp.concatenate([v[c] * beta[c], kb[c] * e_gc[c]], axis=-1) for c in cs]
    u = [_dot16(inv[c], rhs[c]) for c in cs]
    lhs = [jnp.concatenate([u[c][:, GDN_DV:], q[c] * e_gc[c]], axis=0).astype(BF16) for c in cs]
    k_out = [(k[c] * jnp.exp(g_last[c] - gc[c])).T.astype(BF16) for c in cs]
    e_all = [jnp.exp(g_last[c]) for c in cs]

    lanes = [(b, hd) for b in range(batch) for hd in range(GDN_HEADS)]
    states = [s_ref[i] for i in range(len(lanes))]
    for sub in range(GDN_SUBCHUNKS):
        idx = [chains.index((b, sub, hd)) for b, hd in lanes]
        ws = [_dot(lhs[c], states[i].astype(BF16)) for i, c in enumerate(idx)]
        v_new = [(u[c][:, :GDN_DV] - ws[i][:n]).astype(BF16) for i, c in enumerate(idx)]
        o = [ws[i][n:] + _dot(att[c], v_new[i]) for i, c in enumerate(idx)]
        states = [states[i] * e_all[c] + _dot(k_out[c], v_new[i]) for i, c in enumerate(idx)]
        for i, (b, hd) in enumerate(lanes):
            cols = slice(GDN_DV * hd, GDN_DV * (hd + 1))
            on = o[i] * lax.rsqrt(jnp.mean(o[i] * o[i], axis=-1, keepdims=True) + RMS_EPS)
            gate = _silu(g_ref[b, rows(sub), cols].astype(F32))
            o_ref[b, rows(sub), cols] = (on * nw_ref[...] * gate).astype(BF16)
    for i in range(len(lanes)):
        s_ref[i] = states[i]


def _gdn(qk, v, gate, gates, gates_t, norm_w, batch, rows_per_batch):
    n = GDN_CHUNK
    step_rows = n * GDN_SUBCHUNKS
    assert rows_per_batch % step_rows == 0 and step_rows % LANES == 0
    steps = rows_per_batch // step_rows
    qk = qk.reshape(batch, rows_per_batch, qk.shape[1])
    v = v.reshape(batch, rows_per_batch, v.shape[1])
    gate = gate.reshape(batch, rows_per_batch, gate.shape[1])
    gates = gates.reshape(batch, rows_per_batch, LANES)
    out = pl.pallas_call(
        functools.partial(_gdn_kernel, batch=batch),
        grid=(steps,),
        in_specs=[
            pl.BlockSpec((batch, step_rows, 512), lambda c: (0, c, 0)),
            pl.BlockSpec((batch, step_rows, 512), lambda c: (0, c, 1)),
            pl.BlockSpec((batch, step_rows, 1024), lambda c: (0, c, 0)),
            pl.BlockSpec((batch, step_rows, 1024), lambda c: (0, c, 0)),
            pl.BlockSpec((batch, step_rows, LANES), lambda c: (0, c, 0)),
        ] + [pl.BlockSpec((8, step_rows), lambda c, b=b: (0, b * steps + c)) for b in range(batch)] + [
            pl.BlockSpec((1, GDN_DV), lambda c: (0, 0)),
        ],
        out_specs=pl.BlockSpec((batch, step_rows, 1024), lambda c: (0, c, 0)),
        out_shape=jax.ShapeDtypeStruct((batch, rows_per_batch, 1024), BF16),
        scratch_shapes=[pltpu.VMEM((batch * GDN_HEADS, GDN_DK, GDN_DV), F32)],
        compiler_params=_params(("arbitrary",)),
        name="gated_delta",
    )(qk, qk, v, gate, gates, *([gates_t] * batch), norm_w)
    return out.reshape(batch * rows_per_batch, 1024)


def _layer_norm(x, w, b):
    xc = x - jnp.mean(x, axis=-1, keepdims=True)
    return xc * lax.rsqrt(jnp.mean(xc * xc, axis=-1, keepdims=True) + LN_EPS) * w + b


def _outproj_ln_kernel(ya_ref, yb_ref, w_ref, h_ref, lw_ref, lb_ref, o_ref, acc_a, acc_b, *, n_tiles):
    half = ya_ref.shape[1]

    def produce(cur, prev):
        acc = _dot(ya_ref[...], w_ref[0:half, :]) + _dot(yb_ref[...], w_ref[half:2 * half, :])
        cur[...] = DN_ALPHA * h_ref[...] + acc

    def consume(prev):
        o_ref[...] = _layer_norm(prev[...], lw_ref[...], lb_ref[...])

    _lagged_steps(pl.program_id(0), n_tiles, acc_a, acc_b, produce, consume)


def _outproj_ln(ya, yb, col_a, col_b, w, h, lw, lb, rows_per_batch):
    rows = h.shape[0]
    tm = _row_tile(rows_per_batch)
    n_tiles = rows // tm
    half = w.shape[0] // 2
    cur = lambda i: jnp.minimum(i, n_tiles - 1)
    prev = lambda i: jnp.maximum(i - 1, 0)
    vec = pl.BlockSpec((1, D_MODEL), lambda i: (0, 0))
    return pl.pallas_call(
        functools.partial(_outproj_ln_kernel, n_tiles=n_tiles),
        grid=(n_tiles + 1,),
        in_specs=[
            pl.BlockSpec((tm, half), lambda i: (cur(i), col_a)),
            pl.BlockSpec((tm, half), lambda i: (cur(i), col_b)),
            pl.BlockSpec((2 * half, D_MODEL), lambda i: (0, 0), pipeline_mode=pl.Buffered(1)),
            pl.BlockSpec((tm, D_MODEL), lambda i: (cur(i), 0)),
            vec, vec,
        ],
        out_specs=pl.BlockSpec((tm, D_MODEL), lambda i: (prev(i), 0)),
        out_shape=jax.ShapeDtypeStruct((rows, D_MODEL), F32),
        scratch_shapes=[pltpu.VMEM((tm, D_MODEL), F32), pltpu.VMEM((tm, D_MODEL), F32)],
        compiler_params=_params(("arbitrary",)),
        name="outproj_ln",
    )(ya, yb, w, h, lw, lb)


def _mlp(h_ref, w1_ref, w2_ref, lw_ref, lb_ref):
    h = h_ref[...]
    hb = h.astype(BF16)
    acc = jnp.zeros(h.shape, F32)
    step = 1024
    for f in range(0, D_FF, step):
        a = _dot(hb, w1_ref[:, f:f + step])
        a = jnp.square(jnp.maximum(a, 0.0)).astype(BF16)
        acc = acc + _dot(a, w2_ref[f:f + step, :])
    return _layer_norm(DN_ALPHA * h + acc, lw_ref[...], lb_ref[...])


def _mlp_ln_kernel(h_ref, w1_ref, w2_ref, lw_ref, lb_ref, wg_ref, alog_ref, bias_ref, o_ref, ob_ref, g_ref, gt_ref,
                   *, tm, tiles_per_batch):
    out = _mlp(h_ref, w1_ref, w2_ref, lw_ref, lb_ref)
    o_ref[...] = out
    row = (pl.program_id(0) % tiles_per_batch) * tm + lax.broadcasted_iota(jnp.int32, (tm, 1), 0)
    live = row >= PAD_FRONT
    hb = jnp.where(live, out, 0.0).astype(BF16)
    ob_ref[...] = hb
    gates = _ssd_gates(hb, live, wg_ref, alog_ref, bias_ref)
    g_ref[...] = gates
    gt_ref[...] = gates.T


def _mlp_ln_final_kernel(h_ref, w1_ref, w2_ref, lw_ref, lb_ref, o_ref):
    o_ref[...] = _mlp(h_ref, w1_ref, w2_ref, lw_ref, lb_ref)


def _mlp_weight_specs(index):
    return [
        pl.BlockSpec((D_MODEL, D_FF), index, pipeline_mode=pl.Buffered(1)),
        pl.BlockSpec((D_FF, D_MODEL), index, pipeline_mode=pl.Buffered(1)),
        pl.BlockSpec((1, D_MODEL), index),
        pl.BlockSpec((1, D_MODEL), index),
    ]


def _mlp_ln(h, w1, w2, lw, lb, w_gate, alog, bias, rows_per_batch):
    rows = h.shape[0]
    tm = _row_tile(rows_per_batch)
    tile = pl.BlockSpec((tm, D_MODEL), lambda i: (i, 0))
    vec = pl.BlockSpec((1, LANES), lambda i: (0, 0))
    return pl.pallas_call(
        functools.partial(_mlp_ln_kernel, tm=tm, tiles_per_batch=rows_per_batch // tm),
        grid=(rows // tm,),
        in_specs=[tile] + _mlp_weight_specs(lambda i: (0, 0)) + [pl.BlockSpec((D_MODEL, LANES), lambda i: (0, 0)),
                                                                   vec, vec],
        out_specs=[tile, tile, pl.BlockSpec((tm, LANES), lambda i: (i, 0)), pl.BlockSpec((LANES, tm), lambda i: (0, i))],
        out_shape=[jax.ShapeDtypeStruct((rows, D_MODEL), F32), jax.ShapeDtypeStruct((rows, D_MODEL), BF16),
                   jax.ShapeDtypeStruct((rows, LANES), F32), jax.ShapeDtypeStruct((LANES, rows), F32)],
        compiler_params=_params(("arbitrary",)),
        name="mlp_ln",
    )(h, w1, w2, lw, lb, w_gate, alog, bias)


def _mlp_ln_final(h, w1, w2, lw, lb, batch, seq, rows_per_batch):
    tm = next(t for t in (1024, 512, 256, 128) if seq % t == 0)
    tiles = seq // tm
    first_row = PAD_FRONT + N_META
    return pl.pallas_call(
        _mlp_ln_final_kernel,
        grid=(batch, tiles),
        in_specs=[pl.BlockSpec((pl.Element(tm), pl.Element(D_MODEL)),
                               lambda b, i: (pl.multiple_of(b * rows_per_batch + first_row + i * tm, 128), 0))]
        + _mlp_weight_specs(lambda b, i: (0, 0)),
        out_specs=pl.BlockSpec((tm, D_MODEL), lambda b, i: (b * tiles + i, 0)),
        out_shape=jax.ShapeDtypeStruct((batch * seq, D_MODEL), F32),
        compiler_params=_params(("arbitrary", "arbitrary")),
        name="mlp_ln_final",
    )(h, w1, w2, lw, lb)


def _ssd_kernel(*refs, batch):
    n = CHUNK
    z_ref, x_ref, b_ref, c_ref, gcol_ref = refs[:5]
    grow_refs = refs[5:5 + batch]
    dskip_ref, nw_ref, o_ref, s_ref = refs[5 + batch:]
    pair_w = 2 * SSD_HEADDIM
    pairs_per_group = SSD_HPG // 2
    group_w = SSD_HPG * SSD_HEADDIM
    n_pairs = SSD_HEADS // 2

    @pl.when(pl.program_id(0) == 0)
    def _():
        s_ref[...] = jnp.zeros(s_ref.shape, F32)

    t_idx = lax.broadcasted_iota(jnp.int32, (n, n), 0)
    s_idx = lax.broadcasted_iota(jnp.int32, (n, n), 1)
    causal = t_idx >= s_idx
    lower = causal.astype(F32)
    upper = (t_idx <= s_idx).astype(F32)
    lane_v = lax.broadcasted_iota(jnp.int32, (n, pair_w), 1)
    lane_s = lax.broadcasted_iota(jnp.int32, (SSD_DSTATE, pair_w), 1)
    lane_1 = lax.broadcasted_iota(jnp.int32, (1, pair_w), 1)
    bs = range(batch)
    grow = [grow_refs[b][...] for b in bs]
    gcum_col = [_dot_hi(lower, gcol_ref[b]) for b in bs]
    gcum_row = [_dot_hi(grow[b], upper) for b in bs]
    for g in range(SSD_GROUPS):
        q = [c_ref[b, :, SSD_DSTATE * g:SSD_DSTATE * (g + 1)] for b in bs]
        k = [b_ref[b, :, SSD_DSTATE * g:SSD_DSTATE * (g + 1)] for b in bs]
        qf = [q[b].astype(F32) for b in bs]
        kt = [k[b].astype(F32).T for b in bs]
        cb = [_dot_nt(q[b], k[b]) for b in bs]
        ys = [[] for _ in bs]
        for p in range(pairs_per_group):
            pair = g * pairs_per_group + p
            col = group_w * g + pair_w * p
            for b in bs:
                xv = x_ref[b, :, col:col + pair_w]
                state = s_ref[b * n_pairs + pair]
                o = jnp.zeros((n, pair_w), F32)
                upd = jnp.zeros((SSD_DSTATE, pair_w), F32)
                last = []
                for side in range(2):
                    hd = 2 * pair + side
                    gc = gcum_col[b][:, SSD_HEADS + hd:SSD_HEADS + hd + 1]
                    gr = gcum_row[b][SSD_HEADS + hd:SSD_HEADS + hd + 1, :]
                    dt_row = grow[b][hd:hd + 1, :]
                    g_last = gr[:, n - 1:n]
                    dec = jnp.exp(jnp.where(causal, gc - gr, -jnp.inf))
                    att = cb[b] * dec * dt_row
                    q_in = qf[b] * jnp.exp(gc)
                    mine_v = (lane_v >= SSD_HEADDIM) if side else (lane_v < SSD_HEADDIM)
                    mine_s = (lane_s >= SSD_HEADDIM) if side else (lane_s < SSD_HEADDIM)
                    xm = jnp.where(mine_v, xv, jnp.zeros_like(xv))
                    sm = jnp.where(mine_s, state, 0.0).astype(BF16)
                    o = o + _dot(att.astype(BF16), xm) + _dot(q_in.astype(BF16), sm)
                    k_out = kt[b] * (jnp.exp(g_last - gr) * dt_row)
                    upd = upd + _dot(k_out.astype(BF16), xm)
                    last.append(jnp.exp(g_last))
                s_ref[b * n_pairs + pair] = state * jnp.where(lane_1 < SSD_HEADDIM, last[0], last[1]) + upd
                ys[b].append(o + xv.astype(F32) * dskip_ref[:, col:col + pair_w])
        cols = slice(group_w * g, group_w * (g + 1))
        for b in bs:
            y = jnp.concatenate(ys[b], axis=-1)
            y = y * _silu(z_ref[b, :, cols].astype(F32))
            y = y * lax.rsqrt(jnp.mean(y * y, axis=-1, keepdims=True) + RMS_EPS)
            o_ref[b, :, cols] = (y * nw_ref[:, cols]).astype(BF16)


def _ssd(z, xbc, gates, gates_t, dskip, norm_w, batch, rows_per_batch):
    n = CHUNK
    steps = rows_per_batch // n
    z = z.reshape(batch, rows_per_batch, z.shape[1])
    xbc = xbc.reshape(batch, rows_per_batch, xbc.shape[1])
    gates = gates.reshape(batch, rows_per_batch, LANES)
    out = pl.pallas_call(
        functools.partial(_ssd_kernel, batch=batch),
        grid=(steps,),
        in_specs=[
            pl.BlockSpec((batch, n, SSD_DINNER), lambda c: (0, c, 0)),
            pl.BlockSpec((batch, n, SSD_DINNER), lambda c: (0, c, 0)),
            pl.BlockSpec((batch, n, SSD_GN), lambda c: (0, c, 4)),
            pl.BlockSpec((batch, n, SSD_GN), lambda c: (0, c, 5)),
            pl.BlockSpec((batch, n, LANES), lambda c: (0, c, 0)),
        ] + [pl.BlockSpec((2 * SSD_HEADS, n), lambda c, b=b: (0, b * steps + c)) for b in range(batch)] + [
            pl.BlockSpec((1, SSD_DINNER), lambda c: (0, 0)),
            pl.BlockSpec((1, SSD_DINNER), lambda c: (0, 0)),
        ],
        out_specs=pl.BlockSpec((batch, n, SSD_DINNER), lambda c: (0, c, 0)),
        out_shape=jax.ShapeDtypeStruct((batch, rows_per_batch, SSD_DINNER), BF16),
        scratch_shapes=[pltpu.VMEM((batch * SSD_HEADS // 2, SSD_DSTATE, 2 * SSD_HEADDIM), F32)],
        compiler_params=_params(("arbitrary",)),
        name="ssd",
    )(z, xbc, xbc, xbc, gates, *([gates_t] * batch), dskip, norm_w)
    return out.reshape(batch * rows_per_batch, SSD_DINNER)


def _pad_lanes(v):
    v = v.reshape(1, -1).astype(F32)
    return jnp.pad(v, ((0, 0), (0, LANES - v.shape[1])))


def kernel(x, meta_tokens, ab_w_in, ab_ret_gn_w, ab_conv_q, ab_conv_k, ab_conv_v, ab_A_log, ab_dt_bias, ab_gdn_norm_w, ab_w_out, c_w_in, c_conv_w, c_conv_b, c_A_log, c_dt_bias, c_D, c_norm_w, c_w_out, mlp_w1, mlp_w2, ln1_w, ln1_b, ln2_w, ln2_b):
    batch, seq, d = x.shape
    assert d == D_MODEL and meta_tokens.shape == (N_META, D_MODEL)
    lp = PAD_FRONT + N_META + seq
    rows = batch * lp

    meta = jnp.broadcast_to(meta_tokens[None].astype(x.dtype), (batch, N_META, d))
    h = jnp.concatenate([jnp.zeros((batch, PAD_FRONT, d), x.dtype), meta, x], axis=1).reshape(rows, d)
    hb = h.astype(BF16)

    pos = jnp.arange(lp, dtype=F32) - PAD_FRONT
    inv_freq = 1.0 / (ROPE_BASE ** jnp.linspace(0.0, 1.0, RET_DK // 2, dtype=F32))
    ang = pos[:, None] * inv_freq[None]
    cosf = jnp.concatenate([jnp.cos(ang), jnp.cos(ang)], axis=-1)
    sinf = jnp.concatenate([-jnp.sin(ang), jnp.sin(ang)], axis=-1)

    w_in = ab_w_in[:1].astype(BF16)
    w_gg = w_in[:, :, 5128:]
    w_gate = jnp.pad(w_in[0, :, 5120:5128], ((0, 0), (0, LANES - 8)))
    ones = jnp.ones((1, COL_TILE // 2), F32)
    zeros = jnp.zeros((1, COL_TILE), F32)
    p_qk = _proj(hb, w_in, (lambda j: j, 1), _epilogue_rope,
                 [cosf, sinf, jnp.concatenate([ones * RET_DK ** -0.5, ones], axis=1)],
                 [_per_row_in_batch(LANES), _per_row_in_batch(LANES), _per_col()], lp, "proj_ret_qk")
    p_vg = _proj(hb, w_in, (lambda j: j + 1, 2), _epilogue_plain, [], [], lp, "proj_ret_vg")
    p_gg = _proj(hb, w_gg, (lambda j: j, 1), _epilogue_plain, [], [], lp, "proj_gdn_g")
    g_qk = _proj(hb, w_in, (lambda j: j + 3, 1), functools.partial(_epilogue_conv, l2norm=True),
                 [jnp.concatenate([ab_conv_q[0], ab_conv_k[0]], axis=1), zeros,
                  jnp.concatenate([ones * GDN_DK ** -0.5, ones], axis=1)],
                 [_per_col(rows=CONV_K), _per_col(), _per_col()], lp, "proj_gdn_qk")
    g_v = _proj(hb, w_in, (lambda j: j + 4, 1), functools.partial(_epilogue_conv, l2norm=False),
                [ab_conv_v[0], zeros, zeros], [_per_col(rows=CONV_K), _per_col(), _per_col()], lp, "proj_gdn_v")
    alog = _pad_lanes(jnp.concatenate([jnp.zeros((GDN_HEADS,), F32), ab_A_log[0]]))
    bias = _pad_lanes(jnp.concatenate([jnp.zeros((GDN_HEADS,), F32), ab_dt_bias[0]]))
    g0, g0_t = _gates_ab(hb, w_gate, alog, bias, lp)
    y_ret = _retention(p_qk, p_vg, ab_ret_gn_w[0].reshape(1, -1), batch, lp)
    y_gdn = _gdn(g_qk, g_v, p_gg, g0, g0_t, ab_gdn_norm_w[0].reshape(1, -1), batch, lp)
    h = _outproj_ln(y_ret, y_gdn, 0, 0, ab_w_out[0].astype(BF16), h,
                    ln1_w[0].reshape(1, -1), ln1_b[0].reshape(1, -1), lp)
    w_in = c_w_in[:1].astype(BF16)
    w_dt = w_in[0, :, 5120:]
    w_gate = jnp.pad(jnp.concatenate([w_dt, w_dt], axis=1), ((0, 0), (0, LANES - 2 * SSD_HEADS)))
    alog = _pad_lanes(jnp.concatenate([jnp.zeros((SSD_HEADS,), F32), c_A_log[0]]))
    bias = _pad_lanes(jnp.concatenate([c_dt_bias[0], c_dt_bias[0]]))
    h, hb, g1, g1_t = _mlp_ln(h, mlp_w1[0].astype(BF16), mlp_w2[0].astype(BF16),
                              ln2_w[0].reshape(1, -1), ln2_b[0].reshape(1, -1), w_gate, alog, bias, lp)

    p_z = _proj(hb, w_in, (lambda j: j, 2), _epilogue_plain, [], [], lp, "proj_ssd_z")
    p_xbc = _proj(hb, w_in, (lambda j: j + 2, 3), functools.partial(_epilogue_conv, l2norm=False),
                  [c_conv_w[0], c_conv_b[0].reshape(1, -1), jnp.zeros((1, 3 * COL_TILE), F32)],
                  [_per_col(rows=CONV_K), _per_col(), _per_col()], lp, "proj_ssd_xbc")
    dskip = jnp.repeat(c_D[0].astype(F32), SSD_HEADDIM).reshape(1, -1)
    y_ssd = _ssd(p_z, p_xbc, g1, g1_t, dskip, c_norm_w[0].reshape(1, -1), batch, lp)
    h = _outproj_ln(y_ssd, y_ssd, 0, 1, c_w_out[0].astype(BF16), h,
                    ln1_w[1].reshape(1, -1), ln1_b[1].reshape(1, -1), lp)
    out = _mlp_ln_final(h, mlp_w1[1].astype(BF16), mlp_w2[1].astype(BF16),
                        ln2_w[1].reshape(1, -1), ln2_b[1].reshape(1, -1), batch, seq, lp)
    return out.reshape(batch, seq, d)
```

```python
import functools

import jax
import jax.numpy as jnp
from jax import lax
from jax.experimental import pallas as pl
from jax.experimental.pallas import tpu as pltpu

F32 = jnp.float32
BF16 = jnp.bfloat16
HI = lax.Precision.HIGHEST

D_MODEL = 1024
DEPTH = 2
N_META = 16
CONV_K = 4
RET_HEADS = 4
RET_DK = 128
RET_DV = 256
ROPE_BASE = 10000.0
GDN_HEADS = 4
GDN_DK = 128
GDN_DV = 256
SSD_DINNER = 2048
SSD_HEADDIM = 64
SSD_HEADS = 32
SSD_GROUPS = 4
SSD_HPG = 8
SSD_DSTATE = 128
SSD_GN = 512
D_FF = 4096
DN_ALPHA = (2 * DEPTH) ** 0.25
LN_EPS = 1e-5
GN_EPS = 1e-5
RMS_EPS = 1e-6

LANES = 128
COL_TILE = 1024
CHUNK = 128
GDN_CHUNK = 64
GDN_SUBCHUNKS = 2
GDN_BASE = 8
PAD_FRONT = CHUNK - N_META
HALO = 8
CONV_ROWS = 64
PROJ_ROW_TILE = 1664
VMEM_LIMIT = 56 * 1024 * 1024


def _row_tile(rows_per_batch, largest=640):
    for tm in (1664, 640, 512, 256, 128):
        if tm <= largest and rows_per_batch % tm == 0:
            return tm
    raise ValueError(f"unsupported padded sequence length {rows_per_batch}")


def _params(sem):
    return pltpu.CompilerParams(dimension_semantics=sem, vmem_limit_bytes=VMEM_LIMIT)


def _softplus(x):
    return jnp.maximum(x, 0.0) + jnp.log1p(jnp.exp(-jnp.abs(x)))


def _silu(x):
    return x * jax.nn.sigmoid(x)


def _dot(a, b):
    return jnp.dot(a, b, preferred_element_type=F32)


def _dot_nt(a, b):
    return lax.dot_general(a, b, (((1,), (1,)), ((), ())), preferred_element_type=F32)


def _dot_hi(a, b):
    return jnp.dot(a, b, preferred_element_type=F32, precision=HI)


def _dot16(a, b):
    return _dot(a.astype(BF16), b.astype(BF16))


def _epilogue_plain(acc_ref, o_ref, *, tm):
    o_ref[...] = acc_ref[HALO:HALO + tm, :].astype(BF16)


def _epilogue_rope(acc_ref, o_ref, cos_ref, sin_ref, scale_ref, *, tm):
    for hd in range(COL_TILE // RET_DK):
        cols = slice(RET_DK * hd, RET_DK * (hd + 1))
        x = acc_ref[HALO:HALO + tm, cols]
        y = x * cos_ref[...] + pltpu.roll(x, RET_DK // 2, 1) * sin_ref[...]
        o_ref[:, cols] = (y * scale_ref[:, cols]).astype(BF16)


def _epilogue_conv(acc_ref, o_ref, cw_ref, cb_ref, scale_ref, *, tm, l2norm):
    for r in range(0, tm, CONV_ROWS):
        rows = slice(r, r + CONV_ROWS)
        for c in range(0, COL_TILE, LANES):
            cols = slice(c, c + LANES)
            w = [cw_ref[tap:tap + 1, cols] for tap in range(CONV_K)]
            xw = acc_ref[r:r + CONV_ROWS + HALO, cols]
            s1 = pltpu.roll(xw, 1, 0)
            y = w[3] * xw + w[2] * s1 + pltpu.roll(w[1] * xw + w[0] * s1, 2, 0)
            y = _silu(y[HALO:] + cb_ref[:, cols])
            if l2norm:
                inv = lax.rsqrt(jnp.sum(y * y, axis=-1, keepdims=True) + 1e-6)
                y = y * (inv * scale_ref[:, cols])
            o_ref[rows, cols] = y.astype(BF16)


def _lagged_steps(i, n_tiles, acc_a, acc_b, produce, consume):
    last_prev = acc_b if n_tiles % 2 == 0 else acc_a

    @pl.when(i == 0)
    def _():
        acc_b[...] = jnp.zeros(acc_b.shape, F32)

    @pl.when(jnp.logical_and(i % 2 == 0, i < n_tiles))
    def _():
        produce(acc_a, acc_b)
        consume(acc_b)

    @pl.when(jnp.logical_and(i % 2 == 1, i < n_tiles))
    def _():
        produce(acc_b, acc_a)
        consume(acc_a)

    @pl.when(i == n_tiles)
    def _():
        consume(last_prev)


def _proj_kernel(*refs, tm, tiles_per_batch, n_extra, epilogue):
    h_ref, w_ref = refs[0], refs[1]
    extra = refs[2:2 + n_extra]
    o_ref, acc_ref = refs[2 + n_extra:]

    @pl.when(pl.program_id(1) % tiles_per_batch == 0)
    def _():
        acc_ref[0:HALO, :] = jnp.zeros((HALO, COL_TILE), F32)

    acc_ref[HALO:HALO + tm, :] = _dot(h_ref[...], w_ref[...])
    epilogue(acc_ref, o_ref, *extra, tm=tm)
    acc_ref[0:HALO, :] = acc_ref[tm:tm + HALO, :]


def _proj(hb, w, col_tiles, epilogue, extra, extra_specs, rows_per_batch, name):
    rows = hb.shape[0]
    tm = _row_tile(rows_per_batch, largest=PROJ_ROW_TILE)
    tpb = rows_per_batch // tm
    col_fn, ncol = col_tiles
    return pl.pallas_call(
        functools.partial(_proj_kernel, tm=tm, tiles_per_batch=tpb, n_extra=len(extra), epilogue=epilogue),
        grid=(ncol, rows // tm),
        in_specs=[
            pl.BlockSpec((tm, D_MODEL), lambda j, i: (i, 0)),
            pl.BlockSpec((None, D_MODEL, COL_TILE), lambda j, i: (0, 0, col_fn(j))),
        ] + [spec(tm, tpb) for spec in extra_specs],
        out_specs=pl.BlockSpec((tm, COL_TILE), lambda j, i: (i, j)),
        out_shape=jax.ShapeDtypeStruct((rows, ncol * COL_TILE), BF16),
        scratch_shapes=[pltpu.VMEM((tm + HALO, COL_TILE), F32)],
        compiler_params=_params(("arbitrary", "arbitrary")),
        name=name,
    )(hb, w, *extra)


def _per_col(width=COL_TILE, rows=1):
    return lambda tm, tpb: pl.BlockSpec((rows, width), lambda j, i: (0, j))


def _per_row_in_batch(width):
    return lambda tm, tpb: pl.BlockSpec((tm, width), lambda j, i: (i % tpb, 0))


def _gdn_gates(hb, w_ref, alog_ref, bias_ref):
    acc = _dot(hb, w_ref[...])
    lane = lax.broadcasted_iota(jnp.int32, acc.shape, 1)
    decay = -jnp.exp(alog_ref[...]) * _softplus(acc + bias_ref[...])
    return jnp.where(lane < GDN_HEADS, jax.nn.sigmoid(acc), decay)


def _assemble_kernel(x_ref, meta_ref, w_ref, alog_ref, bias_ref, h_ref, hb_ref, g_ref, gt_ref, *, tm):
    def emit(tile):
        hb = tile.astype(BF16)
        h_ref[...] = tile
        hb_ref[...] = hb
        gates = _gdn_gates(hb, w_ref, alog_ref, bias_ref)
        g_ref[...] = gates
        gt_ref[...] = gates.T

    first = pl.program_id(1) == 0

    @pl.when(first)
    def _():
        head = PAD_FRONT + N_META
        emit(jnp.concatenate([jnp.zeros((PAD_FRONT, D_MODEL), F32), meta_ref[...], x_ref[0:tm - head, :]], axis=0))

    @pl.when(jnp.logical_not(first))
    def _():
        emit(x_ref[...])


def _assemble(x, meta_tokens, w_gate, alog, bias, rows_per_batch):
    batch, seq, _ = x.shape
    rows = batch * rows_per_batch
    tm = _row_tile(rows_per_batch, largest=PROJ_ROW_TILE)
    tpb = rows_per_batch // tm
    head = PAD_FRONT + N_META
    vec = pl.BlockSpec((1, LANES), lambda b, i: (0, 0))
    tile = lambda width: pl.BlockSpec((tm, width), lambda b, i: (b * tpb + i, 0))
    return pl.pallas_call(
        functools.partial(_assemble_kernel, tm=tm),
        grid=(batch, tpb),
        in_specs=[
            pl.BlockSpec((pl.Element(tm), pl.Element(D_MODEL)),
                         lambda b, i: (pl.multiple_of(b * seq + jnp.maximum(i * tm - head, 0), LANES), 0)),
            pl.BlockSpec((N_META, D_MODEL), lambda b, i: (0, 0)),
            pl.BlockSpec((D_MODEL, LANES), lambda b, i: (0, 0)),
            vec, vec,
        ],
        out_specs=[tile(D_MODEL), tile(D_MODEL), tile(LANES), pl.BlockSpec((LANES, tm), lambda b, i: (0, b * tpb + i))],
        out_shape=[jax.ShapeDtypeStruct((rows, D_MODEL), F32), jax.ShapeDtypeStruct((rows, D_MODEL), BF16),
                   jax.ShapeDtypeStruct((rows, LANES), F32), jax.ShapeDtypeStruct((LANES, rows), F32)],
        compiler_params=_params(("arbitrary", "arbitrary")),
        name="assemble",
    )(x.reshape(batch * seq, D_MODEL), meta_tokens, w_gate, alog, bias)


def _ssd_gates(hb, live, w_ref, alog_ref, bias_ref):
    acc = _dot(hb, w_ref[...])
    lane = lax.broadcasted_iota(jnp.int32, acc.shape, 1)
    dt = jnp.where(live, _softplus(acc + bias_ref[...]), 0.0)
    return jnp.where(lane < SSD_HEADS, dt, dt * -jnp.exp(alog_ref[...]))


def _retention_kernel(q_ref, k_ref, v_ref, g_ref, gnw_ref, o_ref, s_ref, *, batch):
    n = CHUNK

    @pl.when(pl.program_id(0) == 0)
    def _():
        s_ref[...] = jnp.zeros(s_ref.shape, F32)

    t_idx = lax.broadcasted_iota(jnp.int32, (n, n), 0)
    s_idx = lax.broadcasted_iota(jnp.int32, (n, n), 1)
    t_col = lax.broadcasted_iota(jnp.int32, (n, 1), 0).astype(F32)
    gap = (t_idx - s_idx).astype(F32)
    chains = [(b, hd) for b in range(batch) for hd in range(RET_HEADS)]
    cs = range(len(chains))
    log_gamma = [jnp.log1p(-jnp.exp2(jnp.full((1, 1), -5.0 - hd, F32))) for hd in range(RET_HEADS)]
    decay = [jnp.exp(jnp.where(t_idx >= s_idx, gap * lg, -jnp.inf)) for lg in log_gamma]
    e_in = [jnp.exp((t_col + 1.0) * lg) for lg in log_gamma]
    e_out = [jnp.exp((n - 1.0 - t_col) * lg) for lg in log_gamma]
    e_all = [jnp.exp(n * lg) for lg in log_gamma]
    q = [q_ref[b, :, RET_DK * hd:RET_DK * (hd + 1)] for b, hd in chains]
    k = [k_ref[b, :, RET_DK * hd:RET_DK * (hd + 1)] for b, hd in chains]
    v = [v_ref[b, :, RET_DV * hd:RET_DV * (hd + 1)] for b, hd in chains]
    states = [s_ref[c] for c in cs]
    att = [(_dot_nt(q[c], k[c]) * decay[chains[c][1]]).astype(BF16) for c in cs]
    q_in = [(q[c].astype(F32) * e_in[chains[c][1]]).astype(BF16) for c in cs]
    o = [_dot(att[c], v[c]) + _dot(q_in[c], states[c].astype(BF16)) for c in cs]
    k_out = [(k[c].astype(F32) * e_out[chains[c][1]]).T.astype(BF16) for c in cs]
    for c in cs:
        s_ref[c] = states[c] * e_all[chains[c][1]] + _dot(k_out[c], v[c])
    for c, (b, hd) in enumerate(chains):
        cols = slice(RET_DV * hd, RET_DV * (hd + 1))
        oc = o[c] - jnp.mean(o[c], axis=-1, keepdims=True)
        on = oc * lax.rsqrt(jnp.mean(oc * oc, axis=-1, keepdims=True) + GN_EPS)
        gate = _silu(g_ref[b, :, cols].astype(F32))
        o_ref[b, :, cols] = (on * gnw_ref[:, cols] * gate).astype(BF16)


def _retention(qk, vg, gn_w, batch, rows_per_batch):
    n = CHUNK
    nc = rows_per_batch // n
    qk = qk.reshape(batch, rows_per_batch, qk.shape[1])
    vg = vg.reshape(batch, rows_per_batch, vg.shape[1])
    out = pl.pallas_call(
        functools.partial(_retention_kernel, batch=batch),
        grid=(nc,),
        in_specs=[
            pl.BlockSpec((batch, n, 512), lambda c: (0, c, 0)),
            pl.BlockSpec((batch, n, 512), lambda c: (0, c, 1)),
            pl.BlockSpec((batch, n, 1024), lambda c: (0, c, 0)),
            pl.BlockSpec((batch, n, 1024), lambda c: (0, c, 1)),
            pl.BlockSpec((1, 1024), lambda c: (0, 0)),
        ],
        out_specs=pl.BlockSpec((batch, n, 1024), lambda c: (0, c, 0)),
        out_shape=jax.ShapeDtypeStruct((batch, rows_per_batch, 1024), BF16),
        scratch_shapes=[pltpu.VMEM((batch * RET_HEADS, RET_DK, RET_DV), F32)],
        compiler_params=_params(("arbitrary",)),
        name="retention",
    )(qk, qk, vg, vg, gn_w)
    return out.reshape(batch * rows_per_batch, 1024)


def _gdn_kernel(*refs, batch):
    n = GDN_CHUNK
    q_ref, k_ref, v_ref, g_ref, gcol_ref = refs[:5]
    grow_refs = refs[5:5 + batch]
    nw_ref, o_ref, s_ref = refs[5 + batch:]

    @pl.when(pl.program_id(0) == 0)
    def _():
        s_ref[...] = jnp.zeros(s_ref.shape, F32)

    t_idx = lax.broadcasted_iota(jnp.int32, (n, n), 0)
    s_idx = lax.broadcasted_iota(jnp.int32, (n, n), 1)
    lower = (t_idx >= s_idx).astype(F32)
    upper = (t_idx <= s_idx).astype(F32)
    eye = (t_idx == s_idx).astype(F32)
    blocks = [(b, sub) for b in range(batch) for sub in range(GDN_SUBCHUNKS)]
    chains = [(b, sub, hd) for b, sub in blocks for hd in range(GDN_HEADS)]
    cs = range(len(chains))
    rows = lambda sub: slice(n * sub, n * (sub + 1))
    gcol = {bs: gcol_ref[bs[0], rows(bs[1])] for bs in blocks}
    gcum_col = {bs: _dot_hi(lower, gcol[bs]) for bs in blocks}
    gcum_row = {bs: _dot_hi(grow_refs[bs[0]][:, rows(bs[1])], upper) for bs in blocks}
    q = [q_ref[b, rows(sub), GDN_DK * hd:GDN_DK * (hd + 1)].astype(F32) for b, sub, hd in chains]
    k = [k_ref[b, rows(sub), GDN_DK * hd:GDN_DK * (hd + 1)].astype(F32) for b, sub, hd in chains]
    v = [v_ref[b, rows(sub), GDN_DV * hd:GDN_DV * (hd + 1)].astype(F32) for b, sub, hd in chains]
    beta = [gcol[b, sub][:, hd:hd + 1] for b, sub, hd in chains]
    gc = [gcum_col[b, sub][:, GDN_HEADS + hd:GDN_HEADS + hd + 1] for b, sub, hd in chains]
    gr = [gcum_row[b, sub][GDN_HEADS + hd:GDN_HEADS + hd + 1, :] for b, sub, hd in chains]
    g_last = [gc[c][n - 1:n, :] for c in cs]
    seg = [gc[c] - gr[c] for c in cs]
    dec_strict = [jnp.exp(jnp.where(t_idx > s_idx, seg[c], -jnp.inf)) for c in cs]
    dec_incl = [jnp.exp(jnp.where(t_idx >= s_idx, seg[c], -jnp.inf)) for c in cs]
    kb = [k[c] * beta[c] for c in cs]
    kq = [_dot_nt(jnp.concatenate([kb[c], q[c]], axis=0).astype(BF16), k[c].astype(BF16)) for c in cs]
    a = [kq[c][:n] * dec_strict[c] for c in cs]
    att = [(kq[c][n:] * dec_incl[c]).astype(BF16) for c in cs]
    same_block = lambda size: (t_idx // size) == (s_idx // size)
    diag = [jnp.where(same_block(GDN_BASE), a[c], 0.0) for c in cs]
    inv = [eye - diag[c] for c in cs]
    p = [_dot16(diag[c], diag[c]) for c in cs]
    span = 4
    while span < GDN_BASE:
        r = [_dot16(jnp.concatenate([p[c], inv[c]], axis=0), p[c]) for c in cs]
        inv = [inv[c] + r[c][n:] for c in cs]
        p = [r[c][:n] for c in cs]
        span *= 2
    inv = [inv[c] + _dot16(inv[c], p[c]) for c in cs]
    size = GDN_BASE
    while size < n:
        joins = jnp.logical_and(same_block(2 * size), jnp.logical_not(same_block(size)))
        off = [jnp.where(joins, a[c], 0.0) for c in cs]
        half = [_dot16(inv[c], off[c]) for c in cs]
        inv = [inv[c] - _dot16(half[c], inv[c]) for c in cs]
        size *= 2
    e_gc = [jnp.exp(gc[c]) for c in cs]
    rhs = [jnp.concatenate([v[c] * beta[c], kb[c] * e_gc[c]], axis=-1) for c in cs]
    u = [_dot16(inv[c], rhs[c]) for c in cs]
    lhs = [jnp.concatenate([u[c][:, GDN_DV:], q[c] * e_gc[c]], axis=0).astype(BF16) for c in cs]
    k_out = [(k[c] * jnp.exp(g_last[c] - gc[c])).T.astype(BF16) for c in cs]
    e_all = [jnp.exp(g_last[c]) for c in cs]

    lanes = [(b, hd) for b in range(batch) for hd in range(GDN_HEADS)]
    states = [s_ref[i] for i in range(len(lanes))]
    for sub in range(GDN_SUBCHUNKS):
        idx = [chains.index((b, sub, hd)) for b, hd in lanes]
        ws = [_dot(lhs[c], states[i].astype(BF16)) for i, c in enumerate(idx)]
        v_new = [(u[c][:, :GDN_DV] - ws[i][:n]).astype(BF16) for i, c in enumerate(idx)]
        o = [ws[i][n:] + _dot(att[c], v_new[i]) for i, c in enumerate(idx)]
        states = [states[i] * e_all[c] + _dot(k_out[c], v_new[i]) for i, c in enumerate(idx)]
        for i, (b, hd) in enumerate(lanes):
            cols = slice(GDN_DV * hd, GDN_DV * (hd + 1))
            on = o[i] * lax.rsqrt(jnp.mean(o[i] * o[i], axis=-1, keepdims=True) + RMS_EPS)
            gate = _silu(g_ref[b, rows(sub), cols].astype(F32))
            o_ref[b, rows(sub), cols] = (on * nw_ref[...] * gate).astype(BF16)
    for i in range(len(lanes)):
        s_ref[i] = states[i]


def _gdn(qk, v, gate, gates, gates_t, norm_w, batch, rows_per_batch):
    n = GDN_CHUNK
    step_rows = n * GDN_SUBCHUNKS
    assert rows_per_batch % step_rows == 0 and step_rows % LANES == 0
    steps = rows_per_batch // step_rows
    qk = qk.reshape(batch, rows_per_batch, qk.shape[1])
    v = v.reshape(batch, rows_per_batch, v.shape[1])
    gate = gate.reshape(batch, rows_per_batch, gate.shape[1])
    gates = gates.reshape(batch, rows_per_batch, LANES)
    out = pl.pallas_call(
        functools.partial(_gdn_kernel, batch=batch),
        grid=(steps,),
        in_specs=[
            pl.BlockSpec((batch, step_rows, 512), lambda c: (0, c, 0)),
            pl.BlockSpec((batch, step_rows, 512), lambda c: (0, c, 1)),
            pl.BlockSpec((batch, step_rows, 1024), lambda c: (0, c, 0)),
            pl.BlockSpec((batch, step_rows, 1024), lambda c: (0, c, 0)),
            pl.BlockSpec((batch, step_rows, LANES), lambda c: (0, c, 0)),
        ] + [pl.BlockSpec((8, step_rows), lambda c, b=b: (0, b * steps + c)) for b in range(batch)] + [
            pl.BlockSpec((1, GDN_DV), lambda c: (0, 0)),
        ],
        out_specs=pl.BlockSpec((batch, step_rows, 1024), lambda c: (0, c, 0)),
        out_shape=jax.ShapeDtypeStruct((batch, rows_per_batch, 1024), BF16),
        scratch_shapes=[pltpu.VMEM((batch * GDN_HEADS, GDN_DK, GDN_DV), F32)],
        compiler_params=_params(("arbitrary",)),
        name="gated_delta",
    )(qk, qk, v, gate, gates, *([gates_t] * batch), norm_w)
    return out.reshape(batch * rows_per_batch, 1024)


def _layer_norm(x, w, b):
    xc = x - jnp.mean(x, axis=-1, keepdims=True)
    return xc * lax.rsqrt(jnp.mean(xc * xc, axis=-1, keepdims=True) + LN_EPS) * w + b


def _outproj_ln_kernel(ya_ref, yb_ref, w_ref, h_ref, lw_ref, lb_ref, o_ref, acc_a, acc_b, *, n_tiles):
    half = ya_ref.shape[1]

    def produce(cur, prev):
        acc = _dot(ya_ref[...], w_ref[0:half, :]) + _dot(yb_ref[...], w_ref[half:2 * half, :])
        cur[...] = DN_ALPHA * h_ref[...] + acc

    def consume(prev):
        o_ref[...] = _layer_norm(prev[...], lw_ref[...], lb_ref[...])

    _lagged_steps(pl.program_id(0), n_tiles, acc_a, acc_b, produce, consume)


def _outproj_ln(ya, yb, col_a, col_b, w, h, lw, lb, rows_per_batch):
    rows = h.shape[0]
    tm = _row_tile(rows_per_batch)
    n_tiles = rows // tm
    half = w.shape[0] // 2
    cur = lambda i: jnp.minimum(i, n_tiles - 1)
    prev = lambda i: jnp.maximum(i - 1, 0)
    vec = pl.BlockSpec((1, D_MODEL), lambda i: (0, 0))
    return pl.pallas_call(
        functools.partial(_outproj_ln_kernel, n_tiles=n_tiles),
        grid=(n_tiles + 1,),
        in_specs=[
            pl.BlockSpec((tm, half), lambda i: (cur(i), col_a)),
            pl.BlockSpec((tm, half), lambda i: (cur(i), col_b)),
            pl.BlockSpec((2 * half, D_MODEL), lambda i: (0, 0), pipeline_mode=pl.Buffered(1)),
            pl.BlockSpec((tm, D_MODEL), lambda i: (cur(i), 0)),
            vec, vec,
        ],
        out_specs=pl.BlockSpec((tm, D_MODEL), lambda i: (prev(i), 0)),
        out_shape=jax.ShapeDtypeStruct((rows, D_MODEL), F32),
        scratch_shapes=[pltpu.VMEM((tm, D_MODEL), F32), pltpu.VMEM((tm, D_MODEL), F32)],
        compiler_params=_params(("arbitrary",)),
        name="outproj_ln",
    )(ya, yb, w, h, lw, lb)


def _mlp(h_ref, w1_ref, w2_ref, lw_ref, lb_ref):
    h = h_ref[...]
    hb = h.astype(BF16)
    acc = jnp.zeros(h.shape, F32)
    step = 1024
    for f in range(0, D_FF, step):
        a = _dot(hb, w1_ref[:, f:f + step])
        a = jnp.square(jnp.maximum(a, 0.0)).astype(BF16)
        acc = acc + _dot(a, w2_ref[f:f + step, :])
    return _layer_norm(DN_ALPHA * h + acc, lw_ref[...], lb_ref[...])


def _mlp_ln_kernel(h_ref, w1_ref, w2_ref, lw_ref, lb_ref, wg_ref, alog_ref, bias_ref, o_ref, ob_ref, g_ref, gt_ref,
                   *, tm, tiles_per_batch):
    out = _mlp(h_ref, w1_ref, w2_ref, lw_ref, lb_ref)
    o_ref[...] = out
    row = (pl.program_id(0) % tiles_per_batch) * tm + lax.broadcasted_iota(jnp.int32, (tm, 1), 0)
    live = row >= PAD_FRONT
    hb = jnp.where(live, out, 0.0).astype(BF16)
    ob_ref[...] = hb
    gates = _ssd_gates(hb, live, wg_ref, alog_ref, bias_ref)
    g_ref[...] = gates
    gt_ref[...] = gates.T


def _mlp_ln_final_kernel(h_ref, w1_ref, w2_ref, lw_ref, lb_ref, o_ref):
    o_ref[...] = _mlp(h_ref, w1_ref, w2_ref, lw_ref, lb_ref)


def _mlp_weight_specs(index):
    return [
        pl.BlockSpec((D_MODEL, D_FF), index, pipeline_mode=pl.Buffered(1)),
        pl.BlockSpec((D_FF, D_MODEL), index, pipeline_mode=pl.Buffered(1)),
        pl.BlockSpec((1, D_MODEL), index),
        pl.BlockSpec((1, D_MODEL), index),
    ]


def _mlp_ln(h, w1, w2, lw, lb, w_gate, alog, bias, rows_per_batch):
    rows = h.shape[0]
    tm = _row_tile(rows_per_batch)
    tile = pl.BlockSpec((tm, D_MODEL), lambda i: (i, 0))
    vec = pl.BlockSpec((1, LANES), lambda i: (0, 0))
    return pl.pallas_call(
        functools.partial(_mlp_ln_kernel, tm=tm, tiles_per_batch=rows_per_batch // tm),
        grid=(rows // tm,),
        in_specs=[tile] + _mlp_weight_specs(lambda i: (0, 0)) + [pl.BlockSpec((D_MODEL, LANES), lambda i: (0, 0)),
                                                                   vec, vec],
        out_specs=[tile, tile, pl.BlockSpec((tm, LANES), lambda i: (i, 0)), pl.BlockSpec((LANES, tm), lambda i: (0, i))],
        out_shape=[jax.ShapeDtypeStruct((rows, D_MODEL), F32), jax.ShapeDtypeStruct((rows, D_MODEL), BF16),
                   jax.ShapeDtypeStruct((rows, LANES), F32), jax.ShapeDtypeStruct((LANES, rows), F32)],
        compiler_params=_params(("arbitrary",)),
        name="mlp_ln",
    )(h, w1, w2, lw, lb, w_gate, alog, bias)


def _mlp_ln_final(h, w1, w2, lw, lb, batch, seq, rows_per_batch):
    tm = next(t for t in (1024, 512, 256, 128) if seq % t == 0)
    tiles = seq // tm
    first_row = PAD_FRONT + N_META
    return pl.pallas_call(
        _mlp_ln_final_kernel,
        grid=(batch, tiles),
        in_specs=[pl.BlockSpec((pl.Element(tm), pl.Element(D_MODEL)),
                               lambda b, i: (pl.multiple_of(b * rows_per_batch + first_row + i * tm, 128), 0))]
        + _mlp_weight_specs(lambda b, i: (0, 0)),
        out_specs=pl.BlockSpec((tm, D_MODEL), lambda b, i: (b * tiles + i, 0)),
        out_shape=jax.ShapeDtypeStruct((batch * seq, D_MODEL), F32),
        compiler_params=_params(("arbitrary", "arbitrary")),
        name="mlp_ln_final",
    )(h, w1, w2, lw, lb)


def _ssd_kernel(*refs, batch):
    n = CHUNK
    z_ref, x_ref, b_ref, c_ref, gcol_ref = refs[:5]
    grow_refs = refs[5:5 + batch]
    dskip_ref, nw_ref, o_ref, s_ref = refs[5 + batch:]
    pair_w = 2 * SSD_HEADDIM
    pairs_per_group = SSD_HPG // 2
    group_w = SSD_HPG * SSD_HEADDIM
    n_pairs = SSD_HEADS // 2

    @pl.when(pl.program_id(0) == 0)
    def _():
        s_ref[...] = jnp.zeros(s_ref.shape, F32)

    t_idx = lax.broadcasted_iota(jnp.int32, (n, n), 0)
    s_idx = lax.broadcasted_iota(jnp.int32, (n, n), 1)
    causal = t_idx >= s_idx
    lower = causal.astype(F32)
    upper = (t_idx <= s_idx).astype(F32)
    lane_v = lax.broadcasted_iota(jnp.int32, (n, pair_w), 1)
    lane_s = lax.broadcasted_iota(jnp.int32, (SSD_DSTATE, pair_w), 1)
    lane_1 = lax.broadcasted_iota(jnp.int32, (1, pair_w), 1)
    bs = range(batch)
    grow = [grow_refs[b][...] for b in bs]
    gcum_col = [_dot_hi(lower, gcol_ref[b]) for b in bs]
    gcum_row = [_dot_hi(grow[b], upper) for b in bs]
    for g in range(SSD_GROUPS):
        q = [c_ref[b, :, SSD_DSTATE * g:SSD_DSTATE * (g + 1)] for b in bs]
        k = [b_ref[b, :, SSD_DSTATE * g:SSD_DSTATE * (g + 1)] for b in bs]
        qf = [q[b].astype(F32) for b in bs]
        kt = [k[b].astype(F32).T for b in bs]
        cb = [_dot_nt(q[b], k[b]) for b in bs]
        ys = [[] for _ in bs]
        for p in range(pairs_per_group):
            pair = g * pairs_per_group + p
            col = group_w * g + pair_w * p
            for b in bs:
                xv = x_ref[b, :, col:col + pair_w]
                state = s_ref[b * n_pairs + pair]
                o = jnp.zeros((n, pair_w), F32)
                upd = jnp.zeros((SSD_DSTATE, pair_w), F32)
                last = []
                for side in range(2):
                    hd = 2 * pair + side
                    gc = gcum_col[b][:, SSD_HEADS + hd:SSD_HEADS + hd + 1]
                    gr = gcum_row[b][SSD_HEADS + hd:SSD_HEADS + hd + 1, :]
                    dt_row = grow[b][hd:hd + 1, :]
                    g_last = gr[:, n - 1:n]
                    dec = jnp.exp(jnp.where(causal, gc - gr, -jnp.inf))
                    att = cb[b] * dec * dt_row
                    q_in = qf[b] * jnp.exp(gc)
                    mine_v = (lane_v >= SSD_HEADDIM) if side else (lane_v < SSD_HEADDIM)
                    mine_s = (lane_s >= SSD_HEADDIM) if side else (lane_s < SSD_HEADDIM)
                    xm = jnp.where(mine_v, xv, jnp.zeros_like(xv))
                    sm = jnp.where(mine_s, state, 0.0).astype(BF16)
                    o = o + _dot(att.astype(BF16), xm) + _dot(q_in.astype(BF16), sm)
                    k_out = kt[b] * (jnp.exp(g_last - gr) * dt_row)
                    upd = upd + _dot(k_out.astype(BF16), xm)
                    last.append(jnp.exp(g_last))
                s_ref[b * n_pairs + pair] = state * jnp.where(lane_1 < SSD_HEADDIM, last[0], last[1]) + upd
                ys[b].append(o + xv.astype(F32) * dskip_ref[:, col:col + pair_w])
        cols = slice(group_w * g, group_w * (g + 1))
        for b in bs:
            y = jnp.concatenate(ys[b], axis=-1)
            y = y * _silu(z_ref[b, :, cols].astype(F32))
            y = y * lax.rsqrt(jnp.mean(y * y, axis=-1, keepdims=True) + RMS_EPS)
            o_ref[b, :, cols] = (y * nw_ref[:, cols]).astype(BF16)


def _ssd(z, xbc, gates, gates_t, dskip, norm_w, batch, rows_per_batch):
    n = CHUNK
    steps = rows_per_batch // n
    z = z.reshape(batch, rows_per_batch, z.shape[1])
    xbc = xbc.reshape(batch, rows_per_batch, xbc.shape[1])
    gates = gates.reshape(batch, rows_per_batch, LANES)
    out = pl.pallas_call(
        functools.partial(_ssd_kernel, batch=batch),
        grid=(steps,),
        in_specs=[
            pl.BlockSpec((batch, n, SSD_DINNER), lambda c: (0, c, 0)),
            pl.BlockSpec((batch, n, SSD_DINNER), lambda c: (0, c, 0)),
            pl.BlockSpec((batch, n, SSD_GN), lambda c: (0, c, 4)),
            pl.BlockSpec((batch, n, SSD_GN), lambda c: (0, c, 5)),
            pl.BlockSpec((batch, n, LANES), lambda c: (0, c, 0)),
        ] + [pl.BlockSpec((2 * SSD_HEADS, n), lambda c, b=b: (0, b * steps + c)) for b in range(batch)] + [
            pl.BlockSpec((1, SSD_DINNER), lambda c: (0, 0)),
            pl.BlockSpec((1, SSD_DINNER), lambda c: (0, 0)),
        ],
        out_specs=pl.BlockSpec((batch, n, SSD_DINNER), lambda c: (0, c, 0)),
        out_shape=jax.ShapeDtypeStruct((batch, rows_per_batch, SSD_DINNER), BF16),
        scratch_shapes=[pltpu.VMEM((batch * SSD_HEADS // 2, SSD_DSTATE, 2 * SSD_HEADDIM), F32)],
        compiler_params=_params(("arbitrary",)),
        name="ssd",
    )(z, xbc, xbc, xbc, gates, *([gates_t] * batch), dskip, norm_w)
    return out.reshape(batch * rows_per_batch, SSD_DINNER)


def _pad_lanes(v):
    v = v.reshape(1, -1).astype(F32)
    return jnp.pad(v, ((0, 0), (0, LANES - v.shape[1])))


def kernel(x, meta_tokens, ab_w_in, ab_ret_gn_w, ab_conv_q, ab_conv_k, ab_conv_v, ab_A_log, ab_dt_bias, ab_gdn_norm_w, ab_w_out, c_w_in, c_conv_w, c_conv_b, c_A_log, c_dt_bias, c_D, c_norm_w, c_w_out, mlp_w1, mlp_w2, ln1_w, ln1_b, ln2_w, ln2_b):
    batch, seq, d = x.shape
    assert d == D_MODEL and meta_tokens.shape == (N_META, D_MODEL)
    lp = PAD_FRONT + N_META + seq

    pos = jnp.arange(lp, dtype=F32) - PAD_FRONT
    inv_freq = 1.0 / (ROPE_BASE ** jnp.linspace(0.0, 1.0, RET_DK // 2, dtype=F32))
    ang = pos[:, None] * inv_freq[None]
    cosf = jnp.concatenate([jnp.cos(ang), jnp.cos(ang)], axis=-1)
    sinf = jnp.concatenate([-jnp.sin(ang), jnp.sin(ang)], axis=-1)

    w_in = ab_w_in[:1].astype(BF16)
    w_gg = w_in[:, :, 5128:]
    w_gate = jnp.pad(w_in[0, :, 5120:5128], ((0, 0), (0, LANES - 8)))
    alog = _pad_lanes(jnp.concatenate([jnp.zeros((GDN_HEADS,), F32), ab_A_log[0]]))
    bias = _pad_lanes(jnp.concatenate([jnp.zeros((GDN_HEADS,), F32), ab_dt_bias[0]]))
    h, hb, g0, g0_t = _assemble(x, meta_tokens, w_gate, alog, bias, lp)
    ones = jnp.ones((1, COL_TILE // 2), F32)
    zeros = jnp.zeros((1, COL_TILE), F32)
    p_qk = _proj(hb, w_in, (lambda j: j, 1), _epilogue_rope,
                 [cosf, sinf, jnp.concatenate([ones * RET_DK ** -0.5, ones], axis=1)],
                 [_per_row_in_batch(LANES), _per_row_in_batch(LANES), _per_col()], lp, "proj_ret_qk")
    p_vg = _proj(hb, w_in, (lambda j: j + 1, 2), _epilogue_plain, [], [], lp, "proj_ret_vg")
    p_gg = _proj(hb, w_gg, (lambda j: j, 1), _epilogue_plain, [], [], lp, "proj_gdn_g")
    g_qk = _proj(hb, w_in, (lambda j: j + 3, 1), functools.partial(_epilogue_conv, l2norm=True),
                 [jnp.concatenate([ab_conv_q[0], ab_conv_k[0]], axis=1), zeros,
                  jnp.concatenate([ones * GDN_DK ** -0.5, ones], axis=1)],
                 [_per_col(rows=CONV_K), _per_col(), _per_col()], lp, "proj_gdn_qk")
    g_v = _proj(hb, w_in, (lambda j: j + 4, 1), functools.partial(_epilogue_conv, l2norm=False),
                [ab_conv_v[0], zeros, zeros], [_per_col(rows=CONV_K), _per_col(), _per_col()], lp, "proj_gdn_v")
    y_ret = _retention(p_qk, p_vg, ab_ret_gn_w[0].reshape(1, -1), batch, lp)
    y_gdn = _gdn(g_qk, g_v, p_gg, g0, g0_t, ab_gdn_norm_w[0].reshape(1, -1), batch, lp)
    h = _outproj_ln(y_ret, y_gdn, 0, 0, ab_w_out[0].astype(BF16), h,
                    ln1_w[0].reshape(1, -1), ln1_b[0].reshape(1, -1), lp)
    w_in = c_w_in[:1].astype(BF16)
    w_dt = w_in[0, :, 5120:]
    w_gate = jnp.pad(jnp.concatenate([w_dt, w_dt], axis=1), ((0, 0), (0, LANES - 2 * SSD_HEADS)))
    alog = _pad_lanes(jnp.concatenate([jnp.zeros((SSD_HEADS,), F32), c_A_log[0]]))
    bias = _pad_lanes(jnp.concatenate([c_dt_bias[0], c_dt_bias[0]]))
    h, hb, g1, g1_t = _mlp_ln(h, mlp_w1[0].astype(BF16), mlp_w2[0].astype(BF16),
                              ln2_w[0].reshape(1, -1), ln2_b[0].reshape(1, -1), w_gate, alog, bias, lp)

    p_z = _proj(hb, w_in, (lambda j: j, 2), _epilogue_plain, [], [], lp, "proj_ssd_z")
    p_xbc = _proj(hb, w_in, (lambda j: j + 2, 3), functools.partial(_epilogue_conv, l2norm=False),
                  [c_conv_w[0], c_conv_b[0].reshape(1, -1), jnp.zeros((1, 3 * COL_TILE), F32)],
                  [_per_col(rows=CONV_K), _per_col(), _per_col()], lp, "proj_ssd_xbc")
    dskip = jnp.repeat(c_D[0].astype(F32), SSD_HEADDIM).reshape(1, -1)
    y_ssd = _ssd(p_z, p_xbc, g1, g1_t, dskip, c_norm_w[0].reshape(1, -1), batch, lp)
    h = _outproj_ln(y_ssd, y_ssd, 0, 1, c_w_out[0].astype(BF16), h,
                    ln1_w[1].reshape(1, -1), ln1_b[1].reshape(1, -1), lp)
    out = _mlp_ln_final(h, mlp_w1[1].astype(BF16), mlp_w2[1].astype(BF16),
                        ln2_w[1].reshape(1, -1), ln2_b[1].reshape(1, -1), batch, seq, lp)
    return out.reshape(batch, seq, d)
```

```python
import functools

import jax
import jax.numpy as jnp
from jax import lax
from jax.experimental import pallas as pl
from jax.experimental.pallas import tpu as pltpu

F32 = jnp.float32
BF16 = jnp.bfloat16
HI = lax.Precision.HIGHEST

D_MODEL = 1024
DEPTH = 2
N_META = 16
CONV_K = 4
RET_HEADS = 4
RET_DK = 128
RET_DV = 256
ROPE_BASE = 10000.0
GDN_HEADS = 4
GDN_DK = 128
GDN_DV = 256
SSD_DINNER = 2048
SSD_HEADDIM = 64
SSD_HEADS = 32
SSD_GROUPS = 4
SSD_HPG = 8
SSD_DSTATE = 128
SSD_GN = 512
D_FF = 4096
DN_ALPHA = (2 * DEPTH) ** 0.25
LN_EPS = 1e-5
GN_EPS = 1e-5
RMS_EPS = 1e-6

LANES = 128
COL_TILE = 1024
CHUNK = 128
GDN_CHUNK = 64
GDN_SUBCHUNKS = 2
GDN_BASE = 8
PAD_FRONT = CHUNK - N_META
HALO = 8
CONV_ROWS = 64
PROJ_ROW_TILE = 1664
VMEM_LIMIT = 56 * 1024 * 1024


def _row_tile(rows_per_batch, largest=640):
    for tm in (1664, 640, 512, 256, 128):
        if tm <= largest and rows_per_batch % tm == 0:
            return tm
    raise ValueError(f"unsupported padded sequence length {rows_per_batch}")


def _params(sem):
    return pltpu.CompilerParams(dimension_semantics=sem, vmem_limit_bytes=VMEM_LIMIT)


def _softplus(x):
    return jnp.maximum(x, 0.0) + jnp.log1p(jnp.exp(-jnp.abs(x)))


def _silu(x):
    return x * jax.nn.sigmoid(x)


def _dot(a, b):
    return jnp.dot(a, b, preferred_element_type=F32)


def _dot_nt(a, b):
    return lax.dot_general(a, b, (((1,), (1,)), ((), ())), preferred_element_type=F32)


def _dot_hi(a, b):
    return jnp.dot(a, b, preferred_element_type=F32, precision=HI)


def _dot16(a, b):
    return _dot(a.astype(BF16), b.astype(BF16))


def _epilogue_plain(acc_ref, o_ref, *, tm):
    o_ref[...] = acc_ref[HALO:HALO + tm, :].astype(BF16)


def _epilogue_rope(acc_ref, o_ref, cos_ref, sin_ref, scale_ref, *, tm):
    for hd in range(COL_TILE // RET_DK):
        cols = slice(RET_DK * hd, RET_DK * (hd + 1))
        x = acc_ref[HALO:HALO + tm, cols]
        y = x * cos_ref[...] + pltpu.roll(x, RET_DK // 2, 1) * sin_ref[...]
        o_ref[:, cols] = (y * scale_ref[:, cols]).astype(BF16)


def _epilogue_conv(acc_ref, o_ref, cw_ref, cb_ref, scale_ref, *, tm, l2norm):
    for r in range(0, tm, CONV_ROWS):
        rows = slice(r, r + CONV_ROWS)
        for c in range(0, COL_TILE, LANES):
            cols = slice(c, c + LANES)
            w = [cw_ref[tap:tap + 1, cols] for tap in range(CONV_K)]
            xw = acc_ref[r:r + CONV_ROWS + HALO, cols]
            s1 = pltpu.roll(xw, 1, 0)
            y = w[3] * xw + w[2] * s1 + pltpu.roll(w[1] * xw + w[0] * s1, 2, 0)
            y = _silu(y[HALO:] + cb_ref[:, cols])
            if l2norm:
                inv = lax.rsqrt(jnp.sum(y * y, axis=-1, keepdims=True) + 1e-6)
                y = y * (inv * scale_ref[:, cols])
            o_ref[rows, cols] = y.astype(BF16)


def _lagged_steps(i, n_tiles, acc_a, acc_b, produce, consume):
    last_prev = acc_b if n_tiles % 2 == 0 else acc_a

    @pl.when(i == 0)
    def _():
        acc_b[...] = jnp.zeros(acc_b.shape, F32)

    @pl.when(jnp.logical_and(i % 2 == 0, i < n_tiles))
    def _():
        produce(acc_a, acc_b)
        consume(acc_b)

    @pl.when(jnp.logical_and(i % 2 == 1, i < n_tiles))
    def _():
        produce(acc_b, acc_a)
        consume(acc_a)

    @pl.when(i == n_tiles)
    def _():
        consume(last_prev)


def _proj_kernel(*refs, tm, tiles_per_batch, n_extra, epilogue):
    h_ref, w_ref = refs[0], refs[1]
    extra = refs[2:2 + n_extra]
    o_ref, acc_ref = refs[2 + n_extra:]

    @pl.when(pl.program_id(1) % tiles_per_batch == 0)
    def _():
        acc_ref[0:HALO, :] = jnp.zeros((HALO, COL_TILE), F32)

    acc_ref[HALO:HALO + tm, :] = _dot(h_ref[...], w_ref[...])
    epilogue(acc_ref, o_ref, *extra, tm=tm)
    acc_ref[0:HALO, :] = acc_ref[tm:tm + HALO, :]


def _proj(hb, w, col_tiles, epilogue, extra, extra_specs, rows_per_batch, name):
    rows = hb.shape[0]
    tm = _row_tile(rows_per_batch, largest=PROJ_ROW_TILE)
    tpb = rows_per_batch // tm
    col_fn, ncol = col_tiles
    return pl.pallas_call(
        functools.partial(_proj_kernel, tm=tm, tiles_per_batch=tpb, n_extra=len(extra), epilogue=epilogue),
        grid=(ncol, rows // tm),
        in_specs=[
            pl.BlockSpec((tm, D_MODEL), lambda j, i: (i, 0)),
            pl.BlockSpec((None, D_MODEL, COL_TILE), lambda j, i: (0, 0, col_fn(j))),
        ] + [spec(tm, tpb) for spec in extra_specs],
        out_specs=pl.BlockSpec((tm, COL_TILE), lambda j, i: (i, j)),
        out_shape=jax.ShapeDtypeStruct((rows, ncol * COL_TILE), BF16),
        scratch_shapes=[pltpu.VMEM((tm + HALO, COL_TILE), F32)],
        compiler_params=_params(("arbitrary", "arbitrary")),
        name=name,
    )(hb, w, *extra)


def _per_col(width=COL_TILE, rows=1):
    return lambda tm, tpb: pl.BlockSpec((rows, width), lambda j, i: (0, j))


def _per_row_in_batch(width):
    return lambda tm, tpb: pl.BlockSpec((tm, width), lambda j, i: (i % tpb, 0))


def _gdn_gates(hb, w_ref, alog_ref, bias_ref):
    acc = _dot(hb, w_ref[...])
    lane = lax.broadcasted_iota(jnp.int32, acc.shape, 1)
    decay = -jnp.exp(alog_ref[...]) * _softplus(acc + bias_ref[...])
    return jnp.where(lane < GDN_HEADS, jax.nn.sigmoid(acc), decay)


def _assemble_kernel(x_ref, meta_ref, w_ref, alog_ref, bias_ref, h_ref, hb_ref, g_ref, gt_ref, *, tm):
    def emit(tile):
        hb = tile.astype(BF16)
        h_ref[...] = tile
        hb_ref[...] = hb
        gates = _gdn_gates(hb, w_ref, alog_ref, bias_ref)
        g_ref[...] = gates
        gt_ref[...] = gates.T

    first = pl.program_id(1) == 0

    @pl.when(first)
    def _():
        head = PAD_FRONT + N_META
        emit(jnp.concatenate([jnp.zeros((PAD_FRONT, D_MODEL), F32), meta_ref[...], x_ref[0:tm - head, :]], axis=0))

    @pl.when(jnp.logical_not(first))
    def _():
        emit(x_ref[...])


def _assemble(x, meta_tokens, w_gate, alog, bias, rows_per_batch):
    batch, seq, _ = x.shape
    rows = batch * rows_per_batch
    tm = _row_tile(rows_per_batch, largest=PROJ_ROW_TILE)
    tpb = rows_per_batch // tm
    head = PAD_FRONT + N_META
    vec = pl.BlockSpec((1, LANES), lambda b, i: (0, 0))
    tile = lambda width: pl.BlockSpec((tm, width), lambda b, i: (b * tpb + i, 0))
    return pl.pallas_call(
        functools.partial(_assemble_kernel, tm=tm),
        grid=(batch, tpb),
        in_specs=[
            pl.BlockSpec((pl.Element(tm), pl.Element(D_MODEL)),
                         lambda b, i: (pl.multiple_of(b * seq + jnp.maximum(i * tm - head, 0), LANES), 0)),
            pl.BlockSpec((N_META, D_MODEL), lambda b, i: (0, 0)),
            pl.BlockSpec((D_MODEL, LANES), lambda b, i: (0, 0)),
            vec, vec,
        ],
        out_specs=[tile(D_MODEL), tile(D_MODEL), tile(LANES), pl.BlockSpec((LANES, tm), lambda b, i: (0, b * tpb + i))],
        out_shape=[jax.ShapeDtypeStruct((rows, D_MODEL), F32), jax.ShapeDtypeStruct((rows, D_MODEL), BF16),
                   jax.ShapeDtypeStruct((rows, LANES), F32), jax.ShapeDtypeStruct((LANES, rows), F32)],
        compiler_params=_params(("arbitrary", "arbitrary")),
        name="assemble",
    )(x.reshape(batch * seq, D_MODEL), meta_tokens, w_gate, alog, bias)


def _ssd_gates(hb, live, w_ref, alog_ref, bias_ref):
    acc = _dot(hb, w_ref[...])
    lane = lax.broadcasted_iota(jnp.int32, acc.shape, 1)
    dt = jnp.where(live, _softplus(acc + bias_ref[...]), 0.0)
    return jnp.where(lane < SSD_HEADS, dt, dt * -jnp.exp(alog_ref[...]))


def _retention_kernel(q_ref, k_ref, v_ref, g_ref, gnw_ref, o_ref, s_ref, *, batch, subchunks):
    n = CHUNK

    @pl.when(pl.program_id(0) == 0)
    def _():
        s_ref[...] = jnp.zeros(s_ref.shape, F32)

    t_idx = lax.broadcasted_iota(jnp.int32, (n, n), 0)
    s_idx = lax.broadcasted_iota(jnp.int32, (n, n), 1)
    t_col = lax.broadcasted_iota(jnp.int32, (n, 1), 0).astype(F32)
    gap = (t_idx - s_idx).astype(F32)
    log_gamma = [jnp.log1p(-jnp.exp2(jnp.full((1, 1), -5.0 - hd, F32))) for hd in range(RET_HEADS)]
    decay = [jnp.exp(jnp.where(t_idx >= s_idx, gap * lg, -jnp.inf)) for lg in log_gamma]
    e_in = [jnp.exp((t_col + 1.0) * lg) for lg in log_gamma]
    e_out = [jnp.exp((n - 1.0 - t_col) * lg) for lg in log_gamma]
    e_all = [jnp.exp(n * lg) for lg in log_gamma]
    rows = lambda sub: slice(n * sub, n * (sub + 1))
    chains = [(b, sub, hd) for b in range(batch) for sub in range(subchunks) for hd in range(RET_HEADS)]
    cs = range(len(chains))
    q = [q_ref[b, rows(sub), RET_DK * hd:RET_DK * (hd + 1)] for b, sub, hd in chains]
    k = [k_ref[b, rows(sub), RET_DK * hd:RET_DK * (hd + 1)] for b, sub, hd in chains]
    v = [v_ref[b, rows(sub), RET_DV * hd:RET_DV * (hd + 1)] for b, sub, hd in chains]
    att = [(_dot_nt(q[c], k[c]) * decay[chains[c][2]]).astype(BF16) for c in cs]
    intra = [_dot(att[c], v[c]) for c in cs]
    q_in = [(q[c].astype(F32) * e_in[chains[c][2]]).astype(BF16) for c in cs]
    k_out = [(k[c].astype(F32) * e_out[chains[c][2]]).T.astype(BF16) for c in cs]

    lanes = [(b, hd) for b in range(batch) for hd in range(RET_HEADS)]
    states = [s_ref[i] for i in range(len(lanes))]
    for sub in range(subchunks):
        idx = [chains.index((b, sub, hd)) for b, hd in lanes]
        o = [intra[c] + _dot(q_in[c], states[i].astype(BF16)) for i, c in enumerate(idx)]
        states = [states[i] * e_all[hd] + _dot(k_out[c], v[c]) for (i, c), (b, hd) in zip(enumerate(idx), lanes)]
        for i, (b, hd) in enumerate(lanes):
            cols = slice(RET_DV * hd, RET_DV * (hd + 1))
            oc = o[i] - jnp.mean(o[i], axis=-1, keepdims=True)
            on = oc * lax.rsqrt(jnp.mean(oc * oc, axis=-1, keepdims=True) + GN_EPS)
            gate = _silu(g_ref[b, rows(sub), cols].astype(F32))
            o_ref[b, rows(sub), cols] = (on * gnw_ref[:, cols] * gate).astype(BF16)
    for i in range(len(lanes)):
        s_ref[i] = states[i]


def _retention(qk, vg, gn_w, batch, rows_per_batch):
    n = CHUNK
    nc = rows_per_batch // n
    subchunks = next(s for s in (5, 4, 2, 1) if nc % s == 0)
    step_rows = n * subchunks
    qk = qk.reshape(batch, rows_per_batch, qk.shape[1])
    vg = vg.reshape(batch, rows_per_batch, vg.shape[1])
    out = pl.pallas_call(
        functools.partial(_retention_kernel, batch=batch, subchunks=subchunks),
        grid=(nc // subchunks,),
        in_specs=[
            pl.BlockSpec((batch, step_rows, 512), lambda c: (0, c, 0)),
            pl.BlockSpec((batch, step_rows, 512), lambda c: (0, c, 1)),
            pl.BlockSpec((batch, step_rows, 1024), lambda c: (0, c, 0)),
            pl.BlockSpec((batch, step_rows, 1024), lambda c: (0, c, 1)),
            pl.BlockSpec((1, 1024), lambda c: (0, 0)),
        ],
        out_specs=pl.BlockSpec((batch, step_rows, 1024), lambda c: (0, c, 0)),
        out_shape=jax.ShapeDtypeStruct((batch, rows_per_batch, 1024), BF16),
        scratch_shapes=[pltpu.VMEM((batch * RET_HEADS, RET_DK, RET_DV), F32)],
        compiler_params=_params(("arbitrary",)),
        name="retention",
    )(qk, qk, vg, vg, gn_w)
    return out.reshape(batch * rows_per_batch, 1024)


def _gdn_kernel(*refs, batch):
    n = GDN_CHUNK
    q_ref, k_ref, v_ref, g_ref, gcol_ref = refs[:5]
    grow_refs = refs[5:5 + batch]
    nw_ref, o_ref, s_ref = refs[5 + batch:]

    @pl.when(pl.program_id(0) == 0)
    def _():
        s_ref[...] = jnp.zeros(s_ref.shape, F32)

    t_idx = lax.broadcasted_iota(jnp.int32, (n, n), 0)
    s_idx = lax.broadcasted_iota(jnp.int32, (n, n), 1)
    lower = (t_idx >= s_idx).astype(F32)
    upper = (t_idx <= s_idx).astype(F32)
    eye = (t_idx == s_idx).astype(F32)
    blocks = [(b, sub) for b in range(batch) for sub in range(GDN_SUBCHUNKS)]
    chains = [(b, sub, hd) for b, sub in blocks for hd in range(GDN_HEADS)]
    cs = range(len(chains))
    rows = lambda sub: slice(n * sub, n * (sub + 1))
    gcol = {bs: gcol_ref[bs[0], rows(bs[1])] for bs in blocks}
    gcum_col = {bs: _dot_hi(lower, gcol[bs]) for bs in blocks}
    gcum_row = {bs: _dot_hi(grow_refs[bs[0]][:, rows(bs[1])], upper) for bs in blocks}
    q = [q_ref[b, rows(sub), GDN_DK * hd:GDN_DK * (hd + 1)].astype(F32) for b, sub, hd in chains]
    k = [k_ref[b, rows(sub), GDN_DK * hd:GDN_DK * (hd + 1)].astype(F32) for b, sub, hd in chains]
    v = [v_ref[b, rows(sub), GDN_DV * hd:GDN_DV * (hd + 1)].astype(F32) for b, sub, hd in chains]
    beta = [gcol[b, sub][:, hd:hd + 1] for b, sub, hd in chains]
    gc = [gcum_col[b, sub][:, GDN_HEADS + hd:GDN_HEADS + hd + 1] for b, sub, hd in chains]
    gr = [gcum_row[b, sub][GDN_HEADS + hd:GDN_HEADS + hd + 1, :] for b, sub, hd in chains]
    g_last = [gc[c][n - 1:n, :] for c in cs]
    seg = [gc[c] - gr[c] for c in cs]
    dec_strict = [jnp.exp(jnp.where(t_idx > s_idx, seg[c], -jnp.inf)) for c in cs]
    dec_incl = [jnp.exp(jnp.where(t_idx >= s_idx, seg[c], -jnp.inf)) for c in cs]
    kb = [k[c] * beta[c] for c in cs]
    kq = [_dot_nt(jnp.concatenate([kb[c], q[c]], axis=0).astype(BF16), k[c].astype(BF16)) for c in cs]
    a = [kq[c][:n] * dec_strict[c] for c in cs]
    att = [(kq[c][n:] * dec_incl[c]).astype(BF16) for c in cs]
    same_block = lambda size: (t_idx // size) == (s_idx // size)
    diag = [jnp.where(same_block(GDN_BASE), a[c], 0.0) for c in cs]
    inv = [eye - diag[c] for c in cs]
    p = [_dot16(diag[c], diag[c]) for c in cs]
    span = 4
    while span < GDN_BASE:
        r = [_dot16(jnp.concatenate([p[c], inv[c]], axis=0), p[c]) for c in cs]
        inv = [inv[c] + r[c][n:] for c in cs]
        p = [r[c][:n] for c in cs]
        span *= 2
    inv = [inv[c] + _dot16(inv[c], p[c]) for c in cs]
    size = GDN_BASE
    while size < n:
        joins = jnp.logical_and(same_block(2 * size), jnp.logical_not(same_block(size)))
        off = [jnp.where(joins, a[c], 0.0) for c in cs]
        half = [_dot16(inv[c], off[c]) for c in cs]
        inv = [inv[c] - _dot16(half[c], inv[c]) for c in cs]
        size *= 2
    e_gc = [jnp.exp(gc[c]) for c in cs]
    rhs = [jnp.concatenate([v[c] * beta[c], kb[c] * e_gc[c]], axis=-1) for c in cs]
    u = [_dot16(inv[c], rhs[c]) for c in cs]
    lhs = [jnp.concatenate([u[c][:, GDN_DV:], q[c] * e_gc[c]], axis=0).astype(BF16) for c in cs]
    k_out = [(k[c] * jnp.exp(g_last[c] - gc[c])).T.astype(BF16) for c in cs]
    e_all = [jnp.exp(g_last[c]) for c in cs]

    lanes = [(b, hd) for b in range(batch) for hd in range(GDN_HEADS)]
    states = [s_ref[i] for i in range(len(lanes))]
    for sub in range(GDN_SUBCHUNKS):
        idx = [chains.index((b, sub, hd)) for b, hd in lanes]
        ws = [_dot(lhs[c], states[i].astype(BF16)) for i, c in enumerate(idx)]
        v_new = [(u[c][:, :GDN_DV] - ws[i][:n]).astype(BF16) for i, c in enumerate(idx)]
        o = [ws[i][n:] + _dot(att[c], v_new[i]) for i, c in enumerate(idx)]
        states = [states[i] * e_all[c] + _dot(k_out[c], v_new[i]) for i, c in enumerate(idx)]
        for i, (b, hd) in enumerate(lanes):
            cols = slice(GDN_DV * hd, GDN_DV * (hd + 1))
            on = o[i] * lax.rsqrt(jnp.mean(o[i] * o[i], axis=-1, keepdims=True) + RMS_EPS)
            gate = _silu(g_ref[b, rows(sub), cols].astype(F32))
            o_ref[b, rows(sub), cols] = (on * nw_ref[...] * gate).astype(BF16)
    for i in range(len(lanes)):
        s_ref[i] = states[i]


def _gdn(qk, v, gate, gates, gates_t, norm_w, batch, rows_per_batch):
    n = GDN_CHUNK
    step_rows = n * GDN_SUBCHUNKS
    assert rows_per_batch % step_rows == 0 and step_rows % LANES == 0
    steps = rows_per_batch // step_rows
    qk = qk.reshape(batch, rows_per_batch, qk.shape[1])
    v = v.reshape(batch, rows_per_batch, v.shape[1])
    gate = gate.reshape(batch, rows_per_batch, gate.shape[1])
    gates = gates.reshape(batch, rows_per_batch, LANES)
    out = pl.pallas_call(
        functools.partial(_gdn_kernel, batch=batch),
        grid=(steps,),
        in_specs=[
            pl.BlockSpec((batch, step_rows, 512), lambda c: (0, c, 0)),
            pl.BlockSpec((batch, step_rows, 512), lambda c: (0, c, 1)),
            pl.BlockSpec((batch, step_rows, 1024), lambda c: (0, c, 0)),
            pl.BlockSpec((batch, step_rows, 1024), lambda c: (0, c, 0)),
            pl.BlockSpec((batch, step_rows, LANES), lambda c: (0, c, 0)),
        ] + [pl.BlockSpec((8, step_rows), lambda c, b=b: (0, b * steps + c)) for b in range(batch)] + [
            pl.BlockSpec((1, GDN_DV), lambda c: (0, 0)),
        ],
        out_specs=pl.BlockSpec((batch, step_rows, 1024), lambda c: (0, c, 0)),
        out_shape=jax.ShapeDtypeStruct((batch, rows_per_batch, 1024), BF16),
        scratch_shapes=[pltpu.VMEM((batch * GDN_HEADS, GDN_DK, GDN_DV), F32)],
        compiler_params=_params(("arbitrary",)),
        name="gated_delta",
    )(qk, qk, v, gate, gates, *([gates_t] * batch), norm_w)
    return out.reshape(batch * rows_per_batch, 1024)


def _layer_norm(x, w, b):
    xc = x - jnp.mean(x, axis=-1, keepdims=True)
    return xc * lax.rsqrt(jnp.mean(xc * xc, axis=-1, keepdims=True) + LN_EPS) * w + b


def _outproj_ln_kernel(ya_ref, yb_ref, w_ref, h_ref, lw_ref, lb_ref, o_ref, acc_a, acc_b, *, n_tiles):
    half = ya_ref.shape[1]

    def produce(cur, prev):
        acc = _dot(ya_ref[...], w_ref[0:half, :]) + _dot(yb_ref[...], w_ref[half:2 * half, :])
        cur[...] = DN_ALPHA * h_ref[...] + acc

    def consume(prev):
        o_ref[...] = _layer_norm(prev[...], lw_ref[...], lb_ref[...])

    _lagged_steps(pl.program_id(0), n_tiles, acc_a, acc_b, produce, consume)


def _outproj_ln(ya, yb, col_a, col_b, w, h, lw, lb, rows_per_batch):
    rows = h.shape[0]
    tm = _row_tile(rows_per_batch)
    n_tiles = rows // tm
    half = w.shape[0] // 2
    cur = lambda i: jnp.minimum(i, n_tiles - 1)
    prev = lambda i: jnp.maximum(i - 1, 0)
    vec = pl.BlockSpec((1, D_MODEL), lambda i: (0, 0))
    return pl.pallas_call(
        functools.partial(_outproj_ln_kernel, n_tiles=n_tiles),
        grid=(n_tiles + 1,),
        in_specs=[
            pl.BlockSpec((tm, half), lambda i: (cur(i), col_a)),
            pl.BlockSpec((tm, half), lambda i: (cur(i), col_b)),
            pl.BlockSpec((2 * half, D_MODEL), lambda i: (0, 0), pipeline_mode=pl.Buffered(1)),
            pl.BlockSpec((tm, D_MODEL), lambda i: (cur(i), 0)),
            vec, vec,
        ],
        out_specs=pl.BlockSpec((tm, D_MODEL), lambda i: (prev(i), 0)),
        out_shape=jax.ShapeDtypeStruct((rows, D_MODEL), F32),
        scratch_shapes=[pltpu.VMEM((tm, D_MODEL), F32), pltpu.VMEM((tm, D_MODEL), F32)],
        compiler_params=_params(("arbitrary",)),
        name="outproj_ln",
    )(ya, yb, w, h, lw, lb)


def _mlp(h_ref, w1_ref, w2_ref, lw_ref, lb_ref):
    h = h_ref[...]
    hb = h.astype(BF16)
    acc = jnp.zeros(h.shape, F32)
    step = 1024
    for f in range(0, D_FF, step):
        a = _dot(hb, w1_ref[:, f:f + step])
        a = jnp.square(jnp.maximum(a, 0.0)).astype(BF16)
        acc = acc + _dot(a, w2_ref[f:f + step, :])
    return _layer_norm(DN_ALPHA * h + acc, lw_ref[...], lb_ref[...])


def _mlp_ln_kernel(h_ref, w1_ref, w2_ref, lw_ref, lb_ref, wg_ref, alog_ref, bias_ref, o_ref, ob_ref, g_ref, gt_ref,
                   *, tm, tiles_per_batch):
    out = _mlp(h_ref, w1_ref, w2_ref, lw_ref, lb_ref)
    o_ref[...] = out
    row = (pl.program_id(0) % tiles_per_batch) * tm + lax.broadcasted_iota(jnp.int32, (tm, 1), 0)
    live = row >= PAD_FRONT
    hb = jnp.where(live, out, 0.0).astype(BF16)
    ob_ref[...] = hb
    gates = _ssd_gates(hb, live, wg_ref, alog_ref, bias_ref)
    g_ref[...] = gates
    gt_ref[...] = gates.T


def _mlp_ln_final_kernel(h_ref, w1_ref, w2_ref, lw_ref, lb_ref, o_ref):
    o_ref[...] = _mlp(h_ref, w1_ref, w2_ref, lw_ref, lb_ref)


def _mlp_weight_specs(index):
    return [
        pl.BlockSpec((D_MODEL, D_FF), index, pipeline_mode=pl.Buffered(1)),
        pl.BlockSpec((D_FF, D_MODEL), index, pipeline_mode=pl.Buffered(1)),
        pl.BlockSpec((1, D_MODEL), index),
        pl.BlockSpec((1, D_MODEL), index),
    ]


def _mlp_ln(h, w1, w2, lw, lb, w_gate, alog, bias, rows_per_batch):
    rows = h.shape[0]
    tm = _row_tile(rows_per_batch)
    tile = pl.BlockSpec((tm, D_MODEL), lambda i: (i, 0))
    vec = pl.BlockSpec((1, LANES), lambda i: (0, 0))
    return pl.pallas_call(
        functools.partial(_mlp_ln_kernel, tm=tm, tiles_per_batch=rows_per_batch // tm),
        grid=(rows // tm,),
        in_specs=[tile] + _mlp_weight_specs(lambda i: (0, 0)) + [pl.BlockSpec((D_MODEL, LANES), lambda i: (0, 0)),
                                                                   vec, vec],
        out_specs=[tile, tile, pl.BlockSpec((tm, LANES), lambda i: (i, 0)), pl.BlockSpec((LANES, tm), lambda i: (0, i))],
        out_shape=[jax.ShapeDtypeStruct((rows, D_MODEL), F32), jax.ShapeDtypeStruct((rows, D_MODEL), BF16),
                   jax.ShapeDtypeStruct((rows, LANES), F32), jax.ShapeDtypeStruct((LANES, rows), F32)],
        compiler_params=_params(("arbitrary",)),
        name="mlp_ln",
    )(h, w1, w2, lw, lb, w_gate, alog, bias)


def _mlp_ln_final(h, w1, w2, lw, lb, batch, seq, rows_per_batch):
    tm = next(t for t in (1024, 512, 256, 128) if seq % t == 0)
    tiles = seq // tm
    first_row = PAD_FRONT + N_META
    return pl.pallas_call(
        _mlp_ln_final_kernel,
        grid=(batch, tiles),
        in_specs=[pl.BlockSpec((pl.Element(tm), pl.Element(D_MODEL)),
                               lambda b, i: (pl.multiple_of(b * rows_per_batch + first_row + i * tm, 128), 0))]
        + _mlp_weight_specs(lambda b, i: (0, 0)),
        out_specs=pl.BlockSpec((tm, D_MODEL), lambda b, i: (b * tiles + i, 0)),
        out_shape=jax.ShapeDtypeStruct((batch * seq, D_MODEL), F32),
        compiler_params=_params(("arbitrary", "arbitrary")),
        name="mlp_ln_final",
    )(h, w1, w2, lw, lb)


def _ssd_kernel(*refs, batch):
    n = CHUNK
    z_ref, x_ref, b_ref, c_ref, gcol_ref = refs[:5]
    grow_refs = refs[5:5 + batch]
    dskip_ref, nw_ref, o_ref, s_ref = refs[5 + batch:]
    pair_w = 2 * SSD_HEADDIM
    pairs_per_group = SSD_HPG // 2
    group_w = SSD_HPG * SSD_HEADDIM
    n_pairs = SSD_HEADS // 2

    @pl.when(pl.program_id(0) == 0)
    def _():
        s_ref[...] = jnp.zeros(s_ref.shape, F32)

    t_idx = lax.broadcasted_iota(jnp.int32, (n, n), 0)
    s_idx = lax.broadcasted_iota(jnp.int32, (n, n), 1)
    causal = t_idx >= s_idx
    lower = causal.astype(F32)
    upper = (t_idx <= s_idx).astype(F32)
    lane_v = lax.broadcasted_iota(jnp.int32, (n, pair_w), 1)
    lane_s = lax.broadcasted_iota(jnp.int32, (SSD_DSTATE, pair_w), 1)
    lane_1 = lax.broadcasted_iota(jnp.int32, (1, pair_w), 1)
    bs = range(batch)
    grow = [grow_refs[b][...] for b in bs]
    gcum_col = [_dot_hi(lower, gcol_ref[b]) for b in bs]
    gcum_row = [_dot_hi(grow[b], upper) for b in bs]
    for g in range(SSD_GROUPS):
        q = [c_ref[b, :, SSD_DSTATE * g:SSD_DSTATE * (g + 1)] for b in bs]
        k = [b_ref[b, :, SSD_DSTATE * g:SSD_DSTATE * (g + 1)] for b in bs]
        qf = [q[b].astype(F32) for b in bs]
        kt = [k[b].astype(F32).T for b in bs]
        cb = [_dot_nt(q[b], k[b]) for b in bs]
        ys = [[] for _ in bs]
        for p in range(pairs_per_group):
            pair = g * pairs_per_group + p
            col = group_w * g + pair_w * p
            for b in bs:
                xv = x_ref[b, :, col:col + pair_w]
                state = s_ref[b * n_pairs + pair]
                o = jnp.zeros((n, pair_w), F32)
                upd = jnp.zeros((SSD_DSTATE, pair_w), F32)
                last = []
                for side in range(2):
                    hd = 2 * pair + side
                    gc = gcum_col[b][:, SSD_HEADS + hd:SSD_HEADS + hd + 1]
                    gr = gcum_row[b][SSD_HEADS + hd:SSD_HEADS + hd + 1, :]
                    dt_row = grow[b][hd:hd + 1, :]
                    g_last = gr[:, n - 1:n]
                    dec = jnp.exp(jnp.where(causal, gc - gr, -jnp.inf))
                    att = cb[b] * dec * dt_row
                    q_in = qf[b] * jnp.exp(gc)
                    mine_v = (lane_v >= SSD_HEADDIM) if side else (lane_v < SSD_HEADDIM)
                    mine_s = (lane_s >= SSD_HEADDIM) if side else (lane_s < SSD_HEADDIM)
                    xm = jnp.where(mine_v, xv, jnp.zeros_like(xv))
                    sm = jnp.where(mine_s, state, 0.0).astype(BF16)
                    o = o + _dot(att.astype(BF16), xm) + _dot(q_in.astype(BF16), sm)
                    k_out = kt[b] * (jnp.exp(g_last - gr) * dt_row)
                    upd = upd + _dot(k_out.astype(BF16), xm)
                    last.append(jnp.exp(g_last))
                s_ref[b * n_pairs + pair] = state * jnp.where(lane_1 < SSD_HEADDIM, last[0], last[1]) + upd
                ys[b].append(o + xv.astype(F32) * dskip_ref[:, col:col + pair_w])
        cols = slice(group_w * g, group_w * (g + 1))
        for b in bs:
            y = jnp.concatenate(ys[b], axis=-1)
            y = y * _silu(z_ref[b, :, cols].astype(F32))
            y = y * lax.rsqrt(jnp.mean(y * y, axis=-1, keepdims=True) + RMS_EPS)
            o_ref[b, :, cols] = (y * nw_ref[:, cols]).astype(BF16)


def _ssd(z, xbc, gates, gates_t, dskip, norm_w, batch, rows_per_batch):
    n = CHUNK
    steps = rows_per_batch // n
    z = z.reshape(batch, rows_per_batch, z.shape[1])
    xbc = xbc.reshape(batch, rows_per_batch, xbc.shape[1])
    gates = gates.reshape(batch, rows_per_batch, LANES)
    out = pl.pallas_call(
        functools.partial(_ssd_kernel, batch=batch),
        grid=(steps,),
        in_specs=[
            pl.BlockSpec((batch, n, SSD_DINNER), lambda c: (0, c, 0)),
            pl.BlockSpec((batch, n, SSD_DINNER), lambda c: (0, c, 0)),
            pl.BlockSpec((batch, n, SSD_GN), lambda c: (0, c, 4)),
            pl.BlockSpec((batch, n, SSD_GN), lambda c: (0, c, 5)),
            pl.BlockSpec((batch, n, LANES), lambda c: (0, c, 0)),
        ] + [pl.BlockSpec((2 * SSD_HEADS, n), lambda c, b=b: (0, b * steps + c)) for b in range(batch)] + [
            pl.BlockSpec((1, SSD_DINNER), lambda c: (0, 0)),
            pl.BlockSpec((1, SSD_DINNER), lambda c: (0, 0)),
        ],
        out_specs=pl.BlockSpec((batch, n, SSD_DINNER), lambda c: (0, c, 0)),
        out_shape=jax.ShapeDtypeStruct((batch, rows_per_batch, SSD_DINNER), BF16),
        scratch_shapes=[pltpu.VMEM((batch * SSD_HEADS // 2, SSD_DSTATE, 2 * SSD_HEADDIM), F32)],
        compiler_params=_params(("arbitrary",)),
        name="ssd",
    )(z, xbc, xbc, xbc, gates, *([gates_t] * batch), dskip, norm_w)
    return out.reshape(batch * rows_per_batch, SSD_DINNER)


def _pad_lanes(v):
    v = v.reshape(1, -1).astype(F32)
    return jnp.pad(v, ((0, 0), (0, LANES - v.shape[1])))


def kernel(x, meta_tokens, ab_w_in, ab_ret_gn_w, ab_conv_q, ab_conv_k, ab_conv_v, ab_A_log, ab_dt_bias, ab_gdn_norm_w, ab_w_out, c_w_in, c_conv_w, c_conv_b, c_A_log, c_dt_bias, c_D, c_norm_w, c_w_out, mlp_w1, mlp_w2, ln1_w, ln1_b, ln2_w, ln2_b):
    batch, seq, d = x.shape
    assert d == D_MODEL and meta_tokens.shape == (N_META, D_MODEL)
    lp = PAD_FRONT + N_META + seq

    pos = jnp.arange(lp, dtype=F32) - PAD_FRONT
    inv_freq = 1.0 / (ROPE_BASE ** jnp.linspace(0.0, 1.0, RET_DK // 2, dtype=F32))
    ang = pos[:, None] * inv_freq[None]
    cosf = jnp.concatenate([jnp.cos(ang), jnp.cos(ang)], axis=-1)
    sinf = jnp.concatenate([-jnp.sin(ang), jnp.sin(ang)], axis=-1)

    w_in = ab_w_in[:1].astype(BF16)
    w_gg = w_in[:, :, 5128:]
    w_gate = jnp.pad(w_in[0, :, 5120:5128], ((0, 0), (0, LANES - 8)))
    alog = _pad_lanes(jnp.concatenate([jnp.zeros((GDN_HEADS,), F32), ab_A_log[0]]))
    bias = _pad_lanes(jnp.concatenate([jnp.zeros((GDN_HEADS,), F32), ab_dt_bias[0]]))
    h, hb, g0, g0_t = _assemble(x, meta_tokens, w_gate, alog, bias, lp)
    ones = jnp.ones((1, COL_TILE // 2), F32)
    zeros = jnp.zeros((1, COL_TILE), F32)
    p_qk = _proj(hb, w_in, (lambda j: j, 1), _epilogue_rope,
                 [cosf, sinf, jnp.concatenate([ones * RET_DK ** -0.5, ones], axis=1)],
                 [_per_row_in_batch(LANES), _per_row_in_batch(LANES), _per_col()], lp, "proj_ret_qk")
    p_vg = _proj(hb, w_in, (lambda j: j + 1, 2), _epilogue_plain, [], [], lp, "proj_ret_vg")
    p_gg = _proj(hb, w_gg, (lambda j: j, 1), _epilogue_plain, [], [], lp, "proj_gdn_g")
    g_qk = _proj(hb, w_in, (lambda j: j + 3, 1), functools.partial(_epilogue_conv, l2norm=True),
                 [jnp.concatenate([ab_conv_q[0], ab_conv_k[0]], axis=1), zeros,
                  jnp.concatenate([ones * GDN_DK ** -0.5, ones], axis=1)],
                 [_per_col(rows=CONV_K), _per_col(), _per_col()], lp, "proj_gdn_qk")
    g_v = _proj(hb, w_in, (lambda j: j + 4, 1), functools.partial(_epilogue_conv, l2norm=False),
                [ab_conv_v[0], zeros, zeros], [_per_col(rows=CONV_K), _per_col(), _per_col()], lp, "proj_gdn_v")
    y_ret = _retention(p_qk, p_vg, ab_ret_gn_w[0].reshape(1, -1), batch, lp)
    y_gdn = _gdn(g_qk, g_v, p_gg, g0, g0_t, ab_gdn_norm_w[0].reshape(1, -1), batch, lp)
    h = _outproj_ln(y_ret, y_gdn, 0, 0, ab_w_out[0].astype(BF16), h,
                    ln1_w[0].reshape(1, -1), ln1_b[0].reshape(1, -1), lp)
    w_in = c_w_in[:1].astype(BF16)
    w_dt = w_in[0, :, 5120:]
    w_gate = jnp.pad(jnp.concatenate([w_dt, w_dt], axis=1), ((0, 0), (0, LANES - 2 * SSD_HEADS)))
    alog = _pad_lanes(jnp.concatenate([jnp.zeros((SSD_HEADS,), F32), c_A_log[0]]))
    bias = _pad_lanes(jnp.concatenate([c_dt_bias[0], c_dt_bias[0]]))
    h, hb, g1, g1_t = _mlp_ln(h, mlp_w1[0].astype(BF16), mlp_w2[0].astype(BF16),
                              ln2_w[0].reshape(1, -1), ln2_b[0].reshape(1, -1), w_gate, alog, bias, lp)

    p_z = _proj(hb, w_in, (lambda j: j, 2), _epilogue_plain, [], [], lp, "proj_ssd_z")
    p_xbc = _proj(hb, w_in, (lambda j: j + 2, 3), functools.partial(_epilogue_conv, l2norm=False),
                  [c_conv_w[0], c_conv_b[0].reshape(1, -1), jnp.zeros((1, 3 * COL_TILE), F32)],
                  [_per_col(rows=CONV_K), _per_col(), _per_col()], lp, "proj_ssd_xbc")
    dskip = jnp.repeat(c_D[0].astype(F32), SSD_HEADDIM).reshape(1, -1)
    y_ssd = _ssd(p_z, p_xbc, g1, g1_t, dskip, c_norm_w[0].reshape(1, -1), batch, lp)
    h = _outproj_ln(y_ssd, y_ssd, 0, 1, c_w_out[0].astype(BF16), h,
                    ln1_w[1].reshape(1, -1), ln1_b[1].reshape(1, -1), lp)
    out = _mlp_ln_final(h, mlp_w1[1].astype(BF16), mlp_w2[1].astype(BF16),
                        ln2_w[1].reshape(1, -1), ln2_b[1].reshape(1, -1), batch, seq, lp)
    return out.reshape(batch, seq, d)
```

```python
import functools

import jax
import jax.numpy as jnp
from jax import lax
from jax.experimental import pallas as pl
from jax.experimental.pallas import tpu as pltpu

F32 = jnp.float32
BF16 = jnp.bfloat16
HI = lax.Precision.HIGHEST

D_MODEL = 1024
DEPTH = 2
N_META = 16
CONV_K = 4
RET_HEADS = 4
RET_DK = 128
RET_DV = 256
ROPE_BASE = 10000.0
GDN_HEADS = 4
GDN_DK = 128
GDN_DV = 256
SSD_DINNER = 2048
SSD_HEADDIM = 64
SSD_HEADS = 32
SSD_GROUPS = 4
SSD_HPG = 8
SSD_DSTATE = 128
SSD_GN = 512
D_FF = 4096
DN_ALPHA = (2 * DEPTH) ** 0.25
LN_EPS = 1e-5
GN_EPS = 1e-5
RMS_EPS = 1e-6

LANES = 128
COL_TILE = 1024
CHUNK = 128
GDN_CHUNK = 64
GDN_SUBCHUNKS = 2
GDN_BASE = 8
PAD_FRONT = CHUNK - N_META
HALO = 8
CONV_ROWS = 64
PROJ_ROW_TILE = 1664
VMEM_LIMIT = 56 * 1024 * 1024


def _row_tile(rows_per_batch, largest=640):
    for tm in (1664, 640, 512, 256, 128):
        if tm <= largest and rows_per_batch % tm == 0:
            return tm
    raise ValueError(f"unsupported padded sequence length {rows_per_batch}")


def _params(sem):
    return pltpu.CompilerParams(dimension_semantics=sem, vmem_limit_bytes=VMEM_LIMIT)


def _softplus(x):
    return jnp.maximum(x, 0.0) + jnp.log1p(jnp.exp(-jnp.abs(x)))


def _silu(x):
    return x * jax.nn.sigmoid(x)


def _dot(a, b):
    return jnp.dot(a, b, preferred_element_type=F32)


def _dot_nt(a, b):
    return lax.dot_general(a, b, (((1,), (1,)), ((), ())), preferred_element_type=F32)


def _dot_hi(a, b):
    return jnp.dot(a, b, preferred_element_type=F32, precision=HI)


def _dot16(a, b):
    return _dot(a.astype(BF16), b.astype(BF16))


def _epilogue_plain(acc_ref, o_ref, *, tm):
    o_ref[...] = acc_ref[HALO:HALO + tm, :].astype(BF16)


def _epilogue_rope(acc_ref, o_ref, cos_ref, sin_ref, scale_ref, *, tm):
    for hd in range(COL_TILE // RET_DK):
        cols = slice(RET_DK * hd, RET_DK * (hd + 1))
        x = acc_ref[HALO:HALO + tm, cols]
        y = x * cos_ref[...] + pltpu.roll(x, RET_DK // 2, 1) * sin_ref[...]
        o_ref[:, cols] = (y * scale_ref[:, cols]).astype(BF16)


def _epilogue_conv(acc_ref, o_ref, cw_ref, cb_ref, scale_ref, *, tm, l2norm):
    for r in range(0, tm, CONV_ROWS):
        rows = slice(r, r + CONV_ROWS)
        for c in range(0, COL_TILE, LANES):
            cols = slice(c, c + LANES)
            w = [cw_ref[tap:tap + 1, cols] for tap in range(CONV_K)]
            xw = acc_ref[r:r + CONV_ROWS + HALO, cols]
            s1 = pltpu.roll(xw, 1, 0)
            y = w[3] * xw + w[2] * s1 + pltpu.roll(w[1] * xw + w[0] * s1, 2, 0)
            y = _silu(y[HALO:] + cb_ref[:, cols])
            if l2norm:
                inv = lax.rsqrt(jnp.sum(y * y, axis=-1, keepdims=True) + 1e-6)
                y = y * (inv * scale_ref[:, cols])
            o_ref[rows, cols] = y.astype(BF16)


def _lagged_steps(i, n_tiles, acc_a, acc_b, produce, consume):
    last_prev = acc_b if n_tiles % 2 == 0 else acc_a

    @pl.when(i == 0)
    def _():
        acc_b[...] = jnp.zeros(acc_b.shape, F32)

    @pl.when(jnp.logical_and(i % 2 == 0, i < n_tiles))
    def _():
        produce(acc_a, acc_b)
        consume(acc_b)

    @pl.when(jnp.logical_and(i % 2 == 1, i < n_tiles))
    def _():
        produce(acc_b, acc_a)
        consume(acc_a)

    @pl.when(i == n_tiles)
    def _():
        consume(last_prev)


def _proj_kernel(*refs, tm, tiles_per_batch, n_extra, epilogue):
    h_ref, w_ref = refs[0], refs[1]
    extra = refs[2:2 + n_extra]
    o_ref, acc_ref = refs[2 + n_extra:]

    @pl.when(pl.program_id(1) % tiles_per_batch == 0)
    def _():
        acc_ref[0:HALO, :] = jnp.zeros((HALO, COL_TILE), F32)

    acc_ref[HALO:HALO + tm, :] = _dot(h_ref[...], w_ref[...])
    epilogue(acc_ref, o_ref, *extra, tm=tm)
    acc_ref[0:HALO, :] = acc_ref[tm:tm + HALO, :]


def _proj(hb, w, col_tiles, epilogue, extra, extra_specs, rows_per_batch, name):
    rows = hb.shape[0]
    tm = _row_tile(rows_per_batch, largest=PROJ_ROW_TILE)
    tpb = rows_per_batch // tm
    col_fn, ncol = col_tiles
    return pl.pallas_call(
        functools.partial(_proj_kernel, tm=tm, tiles_per_batch=tpb, n_extra=len(extra), epilogue=epilogue),
        grid=(ncol, rows // tm),
        in_specs=[
            pl.BlockSpec((tm, D_MODEL), lambda j, i: (i, 0)),
            pl.BlockSpec((None, D_MODEL, COL_TILE), lambda j, i: (0, 0, col_fn(j))),
        ] + [spec(tm, tpb) for spec in extra_specs],
        out_specs=pl.BlockSpec((tm, COL_TILE), lambda j, i: (i, j)),
        out_shape=jax.ShapeDtypeStruct((rows, ncol * COL_TILE), BF16),
        scratch_shapes=[pltpu.VMEM((tm + HALO, COL_TILE), F32)],
        compiler_params=_params(("arbitrary", "arbitrary")),
        name=name,
    )(hb, w, *extra)


def _per_col(width=COL_TILE, rows=1):
    return lambda tm, tpb: pl.BlockSpec((rows, width), lambda j, i: (0, j))


def _per_row_in_batch(width):
    return lambda tm, tpb: pl.BlockSpec((tm, width), lambda j, i: (i % tpb, 0))


def _gdn_gates(hb, w_ref, alog_ref, bias_ref):
    acc = _dot(hb, w_ref[...])
    lane = lax.broadcasted_iota(jnp.int32, acc.shape, 1)
    decay = -jnp.exp(alog_ref[...]) * _softplus(acc + bias_ref[...])
    return jnp.where(lane < GDN_HEADS, jax.nn.sigmoid(acc), decay)


def _assemble_kernel(x_ref, meta_ref, w_ref, alog_ref, bias_ref, h_ref, hb_ref, g_ref, gt_ref, *, tm):
    def emit(tile):
        hb = tile.astype(BF16)
        h_ref[...] = tile
        hb_ref[...] = hb
        gates = _gdn_gates(hb, w_ref, alog_ref, bias_ref)
        g_ref[...] = gates
        gt_ref[...] = gates.T

    first = pl.program_id(1) == 0

    @pl.when(first)
    def _():
        head = PAD_FRONT + N_META
        emit(jnp.concatenate([jnp.zeros((PAD_FRONT, D_MODEL), F32), meta_ref[...], x_ref[0:tm - head, :]], axis=0))

    @pl.when(jnp.logical_not(first))
    def _():
        emit(x_ref[...])


def _assemble(x, meta_tokens, w_gate, alog, bias, rows_per_batch):
    batch, seq, _ = x.shape
    rows = batch * rows_per_batch
    tm = _row_tile(rows_per_batch, largest=PROJ_ROW_TILE)
    tpb = rows_per_batch // tm
    head = PAD_FRONT + N_META
    vec = pl.BlockSpec((1, LANES), lambda b, i: (0, 0))
    tile = lambda width: pl.BlockSpec((tm, width), lambda b, i: (b * tpb + i, 0))
    return pl.pallas_call(
        functools.partial(_assemble_kernel, tm=tm),
        grid=(batch, tpb),
        in_specs=[
            pl.BlockSpec((pl.Element(tm), pl.Element(D_MODEL)),
                         lambda b, i: (pl.multiple_of(b * seq + jnp.maximum(i * tm - head, 0), LANES), 0)),
            pl.BlockSpec((N_META, D_MODEL), lambda b, i: (0, 0)),
            pl.BlockSpec((D_MODEL, LANES), lambda b, i: (0, 0)),
            vec, vec,
        ],
        out_specs=[tile(D_MODEL), tile(D_MODEL), tile(LANES), pl.BlockSpec((LANES, tm), lambda b, i: (0, b * tpb + i))],
        out_shape=[jax.ShapeDtypeStruct((rows, D_MODEL), F32), jax.ShapeDtypeStruct((rows, D_MODEL), BF16),
                   jax.ShapeDtypeStruct((rows, LANES), F32), jax.ShapeDtypeStruct((LANES, rows), F32)],
        compiler_params=_params(("arbitrary", "arbitrary")),
        name="assemble",
    )(x.reshape(batch * seq, D_MODEL), meta_tokens, w_gate, alog, bias)


def _ssd_gates(hb, live, w_ref, alog_ref, bias_ref):
    acc = _dot(hb, w_ref[...])
    lane = lax.broadcasted_iota(jnp.int32, acc.shape, 1)
    dt = jnp.where(live, _softplus(acc + bias_ref[...]), 0.0)
    return jnp.where(lane < SSD_HEADS, dt, dt * -jnp.exp(alog_ref[...]))


def _retention_kernel(q_ref, k_ref, v_ref, g_ref, gnw_ref, o_ref, s_ref, *, batch, subchunks):
    n = CHUNK

    @pl.when(pl.program_id(0) == 0)
    def _():
        s_ref[...] = jnp.zeros(s_ref.shape, F32)

    t_idx = lax.broadcasted_iota(jnp.int32, (n, n), 0)
    s_idx = lax.broadcasted_iota(jnp.int32, (n, n), 1)
    t_col = lax.broadcasted_iota(jnp.int32, (n, 1), 0).astype(F32)
    gap = (t_idx - s_idx).astype(F32)
    log_gamma = [jnp.log1p(-jnp.exp2(jnp.full((1, 1), -5.0 - hd, F32))) for hd in range(RET_HEADS)]
    decay = [jnp.exp(jnp.where(t_idx >= s_idx, gap * lg, -jnp.inf)) for lg in log_gamma]
    e_in = [jnp.exp((t_col + 1.0) * lg) for lg in log_gamma]
    e_out = [jnp.exp((n - 1.0 - t_col) * lg) for lg in log_gamma]
    e_all = [jnp.exp(n * lg) for lg in log_gamma]
    rows = lambda sub: slice(n * sub, n * (sub + 1))
    chains = [(b, sub, hd) for b in range(batch) for sub in range(subchunks) for hd in range(RET_HEADS)]
    cs = range(len(chains))
    q = [q_ref[b, rows(sub), RET_DK * hd:RET_DK * (hd + 1)] for b, sub, hd in chains]
    k = [k_ref[b, rows(sub), RET_DK * hd:RET_DK * (hd + 1)] for b, sub, hd in chains]
    v = [v_ref[b, rows(sub), RET_DV * hd:RET_DV * (hd + 1)] for b, sub, hd in chains]
    att = [(_dot_nt(q[c], k[c]) * decay[chains[c][2]]).astype(BF16) for c in cs]
    intra = [_dot(att[c], v[c]) for c in cs]
    q_in = [(q[c].astype(F32) * e_in[chains[c][2]]).astype(BF16) for c in cs]
    k_out = [(k[c].astype(F32) * e_out[chains[c][2]]).T.astype(BF16) for c in cs]

    lanes = [(b, hd) for b in range(batch) for hd in range(RET_HEADS)]
    states = [s_ref[i] for i in range(len(lanes))]
    for sub in range(subchunks):
        idx = [chains.index((b, sub, hd)) for b, hd in lanes]
        o = [intra[c] + _dot(q_in[c], states[i].astype(BF16)) for i, c in enumerate(idx)]
        states = [states[i] * e_all[hd] + _dot(k_out[c], v[c]) for (i, c), (b, hd) in zip(enumerate(idx), lanes)]
        for i, (b, hd) in enumerate(lanes):
            cols = slice(RET_DV * hd, RET_DV * (hd + 1))
            oc = o[i] - jnp.mean(o[i], axis=-1, keepdims=True)
            on = oc * lax.rsqrt(jnp.mean(oc * oc, axis=-1, keepdims=True) + GN_EPS)
            gate = _silu(g_ref[b, rows(sub), cols].astype(F32))
            o_ref[b, rows(sub), cols] = (on * gnw_ref[:, cols] * gate).astype(BF16)
    for i in range(len(lanes)):
        s_ref[i] = states[i]


def _retention(qk, vg, gn_w, batch, rows_per_batch):
    n = CHUNK
    nc = rows_per_batch // n
    subchunks = next(s for s in (5, 4, 2, 1) if nc % s == 0)
    step_rows = n * subchunks
    qk = qk.reshape(batch, rows_per_batch, qk.shape[1])
    vg = vg.reshape(batch, rows_per_batch, vg.shape[1])
    out = pl.pallas_call(
        functools.partial(_retention_kernel, batch=batch, subchunks=subchunks),
        grid=(nc // subchunks,),
        in_specs=[
            pl.BlockSpec((batch, step_rows, 512), lambda c: (0, c, 0)),
            pl.BlockSpec((batch, step_rows, 512), lambda c: (0, c, 1)),
            pl.BlockSpec((batch, step_rows, 1024), lambda c: (0, c, 0)),
            pl.BlockSpec((batch, step_rows, 1024), lambda c: (0, c, 1)),
            pl.BlockSpec((1, 1024), lambda c: (0, 0)),
        ],
        out_specs=pl.BlockSpec((batch, step_rows, 1024), lambda c: (0, c, 0)),
        out_shape=jax.ShapeDtypeStruct((batch, rows_per_batch, 1024), BF16),
        scratch_shapes=[pltpu.VMEM((batch * RET_HEADS, RET_DK, RET_DV), F32)],
        compiler_params=_params(("arbitrary",)),
        name="retention",
    )(qk, qk, vg, vg, gn_w)
    return out.reshape(batch * rows_per_batch, 1024)


def _gdn_kernel(*refs, batch):
    n = GDN_CHUNK
    q_ref, k_ref, v_ref, g_ref, gcol_ref = refs[:5]
    grow_refs = refs[5:5 + batch]
    nw_ref, o_ref, s_ref = refs[5 + batch:]

    @pl.when(pl.program_id(0) == 0)
    def _():
        s_ref[...] = jnp.zeros(s_ref.shape, F32)

    t_idx = lax.broadcasted_iota(jnp.int32, (n, n), 0)
    s_idx = lax.broadcasted_iota(jnp.int32, (n, n), 1)
    lower = (t_idx >= s_idx).astype(F32)
    upper = (t_idx <= s_idx).astype(F32)
    eye = (t_idx == s_idx).astype(F32)
    rows = lambda sub: slice(n * sub, n * (sub + 1))
    same_block = lambda size: (t_idx // size) == (s_idx // size)

    def state_free(chains):
        cs = range(len(chains))
        blocks = sorted({(b, sub) for b, sub, _ in chains})
        gcol = {bs: gcol_ref[bs[0], rows(bs[1])] for bs in blocks}
        gcum_col = {bs: _dot_hi(lower, gcol[bs]) for bs in blocks}
        gcum_row = {bs: _dot_hi(grow_refs[bs[0]][:, rows(bs[1])], upper) for bs in blocks}
        q = [q_ref[b, rows(sub), GDN_DK * hd:GDN_DK * (hd + 1)].astype(F32) for b, sub, hd in chains]
        k = [k_ref[b, rows(sub), GDN_DK * hd:GDN_DK * (hd + 1)].astype(F32) for b, sub, hd in chains]
        v = [v_ref[b, rows(sub), GDN_DV * hd:GDN_DV * (hd + 1)].astype(F32) for b, sub, hd in chains]
        beta = [gcol[b, sub][:, hd:hd + 1] for b, sub, hd in chains]
        gc = [gcum_col[b, sub][:, GDN_HEADS + hd:GDN_HEADS + hd + 1] for b, sub, hd in chains]
        gr = [gcum_row[b, sub][GDN_HEADS + hd:GDN_HEADS + hd + 1, :] for b, sub, hd in chains]
        g_last = [gc[c][n - 1:n, :] for c in cs]
        seg = [gc[c] - gr[c] for c in cs]
        dec_strict = [jnp.exp(jnp.where(t_idx > s_idx, seg[c], -jnp.inf)) for c in cs]
        dec_incl = [jnp.exp(jnp.where(t_idx >= s_idx, seg[c], -jnp.inf)) for c in cs]
        kb = [k[c] * beta[c] for c in cs]
        kq = [_dot_nt(jnp.concatenate([kb[c], q[c]], axis=0).astype(BF16), k[c].astype(BF16)) for c in cs]
        a = [kq[c][:n] * dec_strict[c] for c in cs]
        att = [(kq[c][n:] * dec_incl[c]).astype(BF16) for c in cs]
        diag = [jnp.where(same_block(GDN_BASE), a[c], 0.0) for c in cs]
        inv = [eye - diag[c] for c in cs]
        p = [_dot16(diag[c], diag[c]) for c in cs]
        span = 4
        while span < GDN_BASE:
            r = [_dot16(jnp.concatenate([p[c], inv[c]], axis=0), p[c]) for c in cs]
            inv = [inv[c] + r[c][n:] for c in cs]
            p = [r[c][:n] for c in cs]
            span *= 2
        inv = [inv[c] + _dot16(inv[c], p[c]) for c in cs]
        size = GDN_BASE
        while size < n:
            joins = jnp.logical_and(same_block(2 * size), jnp.logical_not(same_block(size)))
            off = [jnp.where(joins, a[c], 0.0) for c in cs]
            half = [_dot16(inv[c], off[c]) for c in cs]
            inv = [inv[c] - _dot16(half[c], inv[c]) for c in cs]
            size *= 2
        e_gc = [jnp.exp(gc[c]) for c in cs]
        rhs = [jnp.concatenate([v[c] * beta[c], kb[c] * e_gc[c]], axis=-1) for c in cs]
        u = [_dot16(inv[c], rhs[c]) for c in cs]
        lhs = [jnp.concatenate([u[c][:, GDN_DV:], q[c] * e_gc[c]], axis=0).astype(BF16) for c in cs]
        k_out = [(k[c] * jnp.exp(g_last[c] - gc[c])).T.astype(BF16) for c in cs]
        e_all = [jnp.exp(g_last[c]) for c in cs]
        return list(zip(att, u, lhs, k_out, e_all))

    lanes = [(b, hd) for b in range(batch) for hd in range(GDN_HEADS)]
    chains = [(b, sub, hd) for b in range(batch) for sub in range(GDN_SUBCHUNKS) for hd in range(GDN_HEADS)]
    prepared = dict(zip(chains, state_free(chains)))
    states = [s_ref[i] for i in range(len(lanes))]
    for sub in range(GDN_SUBCHUNKS):
        att, u, lhs, k_out, e_all = zip(*[prepared[b, sub, hd] for b, hd in lanes])
        ls = range(len(lanes))
        ws = [_dot(lhs[i], states[i].astype(BF16)) for i in ls]
        v_new = [(u[i][:, :GDN_DV] - ws[i][:n]).astype(BF16) for i in ls]
        o = [ws[i][n:] + _dot(att[i], v_new[i]) for i in ls]
        states = [states[i] * e_all[i] + _dot(k_out[i], v_new[i]) for i in ls]
        for i, (b, hd) in enumerate(lanes):
            cols = slice(GDN_DV * hd, GDN_DV * (hd + 1))
            on = o[i] * lax.rsqrt(jnp.mean(o[i] * o[i], axis=-1, keepdims=True) + RMS_EPS)
            gate = _silu(g_ref[b, rows(sub), cols].astype(F32))
            o_ref[b, rows(sub), cols] = (on * nw_ref[...] * gate).astype(BF16)
    for i in range(len(lanes)):
        s_ref[i] = states[i]


def _gdn(qk, v, gate, gates, gates_t, norm_w, batch, rows_per_batch):
    n = GDN_CHUNK
    step_rows = n * GDN_SUBCHUNKS
    assert rows_per_batch % step_rows == 0 and step_rows % LANES == 0
    steps = rows_per_batch // step_rows
    qk = qk.reshape(batch, rows_per_batch, qk.shape[1])
    v = v.reshape(batch, rows_per_batch, v.shape[1])
    gate = gate.reshape(batch, rows_per_batch, gate.shape[1])
    gates = gates.reshape(batch, rows_per_batch, LANES)
    out = pl.pallas_call(
        functools.partial(_gdn_kernel, batch=batch),
        grid=(steps,),
        in_specs=[
            pl.BlockSpec((batch, step_rows, 512), lambda c: (0, c, 0)),
            pl.BlockSpec((batch, step_rows, 512), lambda c: (0, c, 1)),
            pl.BlockSpec((batch, step_rows, 1024), lambda c: (0, c, 0)),
            pl.BlockSpec((batch, step_rows, 1024), lambda c: (0, c, 0)),
            pl.BlockSpec((batch, step_rows, LANES), lambda c: (0, c, 0)),
        ] + [pl.BlockSpec((8, step_rows), lambda c, b=b: (0, b * steps + c)) for b in range(batch)] + [
            pl.BlockSpec((1, GDN_DV), lambda c: (0, 0)),
        ],
        out_specs=pl.BlockSpec((batch, step_rows, 1024), lambda c: (0, c, 0)),
        out_shape=jax.ShapeDtypeStruct((batch, rows_per_batch, 1024), BF16),
        scratch_shapes=[pltpu.VMEM((batch * GDN_HEADS, GDN_DK, GDN_DV), F32)],
        compiler_params=_params(("arbitrary",)),
        name="gated_delta",
    )(qk, qk, v, gate, gates, *([gates_t] * batch), norm_w)
    return out.reshape(batch * rows_per_batch, 1024)


def _layer_norm(x, w, b):
    xc = x - jnp.mean(x, axis=-1, keepdims=True)
    return xc * lax.rsqrt(jnp.mean(xc * xc, axis=-1, keepdims=True) + LN_EPS) * w + b


def _outproj_ln_kernel(ya_ref, yb_ref, w_ref, h_ref, lw_ref, lb_ref, o_ref, acc_a, acc_b, *, n_tiles):
    half = ya_ref.shape[1]

    def produce(cur, prev):
        acc = _dot(ya_ref[...], w_ref[0:half, :]) + _dot(yb_ref[...], w_ref[half:2 * half, :])
        cur[...] = DN_ALPHA * h_ref[...] + acc

    def consume(prev):
        o_ref[...] = _layer_norm(prev[...], lw_ref[...], lb_ref[...])

    _lagged_steps(pl.program_id(0), n_tiles, acc_a, acc_b, produce, consume)


def _outproj_ln(ya, yb, col_a, col_b, w, h, lw, lb, rows_per_batch):
    rows = h.shape[0]
    tm = _row_tile(rows_per_batch)
    n_tiles = rows // tm
    half = w.shape[0] // 2
    cur = lambda i: jnp.minimum(i, n_tiles - 1)
    prev = lambda i: jnp.maximum(i - 1, 0)
    vec = pl.BlockSpec((1, D_MODEL), lambda i: (0, 0))
    return pl.pallas_call(
        functools.partial(_outproj_ln_kernel, n_tiles=n_tiles),
        grid=(n_tiles + 1,),
        in_specs=[
            pl.BlockSpec((tm, half), lambda i: (cur(i), col_a)),
            pl.BlockSpec((tm, half), lambda i: (cur(i), col_b)),
            pl.BlockSpec((2 * half, D_MODEL), lambda i: (0, 0), pipeline_mode=pl.Buffered(1)),
            pl.BlockSpec((tm, D_MODEL), lambda i: (cur(i), 0)),
            vec, vec,
        ],
        out_specs=pl.BlockSpec((tm, D_MODEL), lambda i: (prev(i), 0)),
        out_shape=jax.ShapeDtypeStruct((rows, D_MODEL), F32),
        scratch_shapes=[pltpu.VMEM((tm, D_MODEL), F32), pltpu.VMEM((tm, D_MODEL), F32)],
        compiler_params=_params(("arbitrary",)),
        name="outproj_ln",
    )(ya, yb, w, h, lw, lb)


def _mlp(h_ref, w1_ref, w2_ref, lw_ref, lb_ref):
    h = h_ref[...]
    hb = h.astype(BF16)
    acc = jnp.zeros(h.shape, F32)
    step = 1024
    for f in range(0, D_FF, step):
        a = _dot(hb, w1_ref[:, f:f + step])
        a = jnp.square(jnp.maximum(a, 0.0)).astype(BF16)
        acc = acc + _dot(a, w2_ref[f:f + step, :])
    return _layer_norm(DN_ALPHA * h + acc, lw_ref[...], lb_ref[...])


def _mlp_ln_kernel(h_ref, w1_ref, w2_ref, lw_ref, lb_ref, wg_ref, alog_ref, bias_ref, o_ref, ob_ref, g_ref, gt_ref,
                   *, tm, tiles_per_batch):
    out = _mlp(h_ref, w1_ref, w2_ref, lw_ref, lb_ref)
    o_ref[...] = out
    row = (pl.program_id(0) % tiles_per_batch) * tm + lax.broadcasted_iota(jnp.int32, (tm, 1), 0)
    live = row >= PAD_FRONT
    hb = jnp.where(live, out, 0.0).astype(BF16)
    ob_ref[...] = hb
    gates = _ssd_gates(hb, live, wg_ref, alog_ref, bias_ref)
    g_ref[...] = gates
    gt_ref[...] = gates.T


def _mlp_ln_final_kernel(h_ref, w1_ref, w2_ref, lw_ref, lb_ref, o_ref):
    o_ref[...] = _mlp(h_ref, w1_ref, w2_ref, lw_ref, lb_ref)


def _mlp_weight_specs(index):
    return [
        pl.BlockSpec((D_MODEL, D_FF), index, pipeline_mode=pl.Buffered(1)),
        pl.BlockSpec((D_FF, D_MODEL), index, pipeline_mode=pl.Buffered(1)),
        pl.BlockSpec((1, D_MODEL), index),
        pl.BlockSpec((1, D_MODEL), index),
    ]


def _mlp_ln(h, w1, w2, lw, lb, w_gate, alog, bias, rows_per_batch):
    rows = h.shape[0]
    tm = _row_tile(rows_per_batch)
    tile = pl.BlockSpec((tm, D_MODEL), lambda i: (i, 0))
    vec = pl.BlockSpec((1, LANES), lambda i: (0, 0))
    return pl.pallas_call(
        functools.partial(_mlp_ln_kernel, tm=tm, tiles_per_batch=rows_per_batch // tm),
        grid=(rows // tm,),
        in_specs=[tile] + _mlp_weight_specs(lambda i: (0, 0)) + [pl.BlockSpec((D_MODEL, LANES), lambda i: (0, 0)),
                                                                   vec, vec],
        out_specs=[tile, tile, pl.BlockSpec((tm, LANES), lambda i: (i, 0)), pl.BlockSpec((LANES, tm), lambda i: (0, i))],
        out_shape=[jax.ShapeDtypeStruct((rows, D_MODEL), F32), jax.ShapeDtypeStruct((rows, D_MODEL), BF16),
                   jax.ShapeDtypeStruct((rows, LANES), F32), jax.ShapeDtypeStruct((LANES, rows), F32)],
        compiler_params=_params(("arbitrary",)),
        name="mlp_ln",
    )(h, w1, w2, lw, lb, w_gate, alog, bias)


def _mlp_ln_final(h, w1, w2, lw, lb, batch, seq, rows_per_batch):
    tm = next(t for t in (1024, 512, 256, 128) if seq % t == 0)
    tiles = seq // tm
    first_row = PAD_FRONT + N_META
    return pl.pallas_call(
        _mlp_ln_final_kernel,
        grid=(batch, tiles),
        in_specs=[pl.BlockSpec((pl.Element(tm), pl.Element(D_MODEL)),
                               lambda b, i: (pl.multiple_of(b * rows_per_batch + first_row + i * tm, 128), 0))]
        + _mlp_weight_specs(lambda b, i: (0, 0)),
        out_specs=pl.BlockSpec((tm, D_MODEL), lambda b, i: (b * tiles + i, 0)),
        out_shape=jax.ShapeDtypeStruct((batch * seq, D_MODEL), F32),
        compiler_params=_params(("arbitrary", "arbitrary")),
        name="mlp_ln_final",
    )(h, w1, w2, lw, lb)


def _ssd_kernel(*refs, batch):
    n = CHUNK
    z_ref, x_ref, b_ref, c_ref, gcol_ref = refs[:5]
    grow_refs = refs[5:5 + batch]
    dskip_ref, nw_ref, o_ref, s_ref = refs[5 + batch:]
    pair_w = 2 * SSD_HEADDIM
    pairs_per_group = SSD_HPG // 2
    group_w = SSD_HPG * SSD_HEADDIM
    n_pairs = SSD_HEADS // 2

    @pl.when(pl.program_id(0) == 0)
    def _():
        s_ref[...] = jnp.zeros(s_ref.shape, F32)

    t_idx = lax.broadcasted_iota(jnp.int32, (n, n), 0)
    s_idx = lax.broadcasted_iota(jnp.int32, (n, n), 1)
    causal = t_idx >= s_idx
    lower = causal.astype(F32)
    upper = (t_idx <= s_idx).astype(F32)
    lane_v = lax.broadcasted_iota(jnp.int32, (n, pair_w), 1)
    lane_s = lax.broadcasted_iota(jnp.int32, (SSD_DSTATE, pair_w), 1)
    lane_1 = lax.broadcasted_iota(jnp.int32, (1, pair_w), 1)
    bs = range(batch)
    grow = [grow_refs[b][...] for b in bs]
    gcum_col = [_dot_hi(lower, gcol_ref[b]) for b in bs]
    gcum_row = [_dot_hi(grow[b], upper) for b in bs]
    for g in range(SSD_GROUPS):
        q = [c_ref[b, :, SSD_DSTATE * g:SSD_DSTATE * (g + 1)] for b in bs]
        k = [b_ref[b, :, SSD_DSTATE * g:SSD_DSTATE * (g + 1)] for b in bs]
        qf = [q[b].astype(F32) for b in bs]
        kt = [k[b].astype(F32).T for b in bs]
        cb = [_dot_nt(q[b], k[b]) for b in bs]
        ys = [[] for _ in bs]
        for p in range(pairs_per_group):
            pair = g * pairs_per_group + p
            col = group_w * g + pair_w * p
            for b in bs:
                xv = x_ref[b, :, col:col + pair_w]
                state = s_ref[b * n_pairs + pair]
                o = jnp.zeros((n, pair_w), F32)
                upd = jnp.zeros((SSD_DSTATE, pair_w), F32)
                last = []
                for side in range(2):
                    hd = 2 * pair + side
                    gc = gcum_col[b][:, SSD_HEADS + hd:SSD_HEADS + hd + 1]
                    gr = gcum_row[b][SSD_HEADS + hd:SSD_HEADS + hd + 1, :]
                    dt_row = grow[b][hd:hd + 1, :]
                    g_last = gr[:, n - 1:n]
                    dec = jnp.exp(jnp.where(causal, gc - gr, -jnp.inf))
                    att = cb[b] * dec * dt_row
                    q_in = qf[b] * jnp.exp(gc)
                    mine_v = (lane_v >= SSD_HEADDIM) if side else (lane_v < SSD_HEADDIM)
                    mine_s = (lane_s >= SSD_HEADDIM) if side else (lane_s < SSD_HEADDIM)
                    xm = jnp.where(mine_v, xv, jnp.zeros_like(xv))
                    sm = jnp.where(mine_s, state, 0.0).astype(BF16)
                    o = o + _dot(att.astype(BF16), xm) + _dot(q_in.astype(BF16), sm)
                    k_out = kt[b] * (jnp.exp(g_last - gr) * dt_row)
                    upd = upd + _dot(k_out.astype(BF16), xm)
                    last.append(jnp.exp(g_last))
                s_ref[b * n_pairs + pair] = state * jnp.where(lane_1 < SSD_HEADDIM, last[0], last[1]) + upd
                ys[b].append(o + xv.astype(F32) * dskip_ref[:, col:col + pair_w])
        cols = slice(group_w * g, group_w * (g + 1))
        for b in bs:
            y = jnp.concatenate(ys[b], axis=-1)
            y = y * _silu(z_ref[b, :, cols].astype(F32))
            y = y * lax.rsqrt(jnp.mean(y * y, axis=-1, keepdims=True) + RMS_EPS)
            o_ref[b, :, cols] = (y * nw_ref[:, cols]).astype(BF16)


def _ssd(z, xbc, gates, gates_t, dskip, norm_w, batch, rows_per_batch):
    n = CHUNK
    steps = rows_per_batch // n
    z = z.reshape(batch, rows_per_batch, z.shape[1])
    xbc = xbc.reshape(batch, rows_per_batch, xbc.shape[1])
    gates = gates.reshape(batch, rows_per_batch, LANES)
    out = pl.pallas_call(
        functools.partial(_ssd_kernel, batch=batch),
        grid=(steps,),
        in_specs=[
            pl.BlockSpec((batch, n, SSD_DINNER), lambda c: (0, c, 0)),
            pl.BlockSpec((batch, n, SSD_DINNER), lambda c: (0, c, 0)),
            pl.BlockSpec((batch, n, SSD_GN), lambda c: (0, c, 4)),
            pl.BlockSpec((batch, n, SSD_GN), lambda c: (0, c, 5)),
            pl.BlockSpec((batch, n, LANES), lambda c: (0, c, 0)),
        ] + [pl.BlockSpec((2 * SSD_HEADS, n), lambda c, b=b: (0, b * steps + c)) for b in range(batch)] + [
            pl.BlockSpec((1, SSD_DINNER), lambda c: (0, 0)),
            pl.BlockSpec((1, SSD_DINNER), lambda c: (0, 0)),
        ],
        out_specs=pl.BlockSpec((batch, n, SSD_DINNER), lambda c: (0, c, 0)),
        out_shape=jax.ShapeDtypeStruct((batch, rows_per_batch, SSD_DINNER), BF16),
        scratch_shapes=[pltpu.VMEM((batch * SSD_HEADS // 2, SSD_DSTATE, 2 * SSD_HEADDIM), F32)],
        compiler_params=_params(("arbitrary",)),
        name="ssd",
    )(z, xbc, xbc, xbc, gates, *([gates_t] * batch), dskip, norm_w)
    return out.reshape(batch * rows_per_batch, SSD_DINNER)


def _pad_lanes(v):
    v = v.reshape(1, -1).astype(F32)
    return jnp.pad(v, ((0, 0), (0, LANES - v.shape[1])))


def kernel(x, meta_tokens, ab_w_in, ab_ret_gn_w, ab_conv_q, ab_conv_k, ab_conv_v, ab_A_log, ab_dt_bias, ab_gdn_norm_w, ab_w_out, c_w_in, c_conv_w, c_conv_b, c_A_log, c_dt_bias, c_D, c_norm_w, c_w_out, mlp_w1, mlp_w2, ln1_w, ln1_b, ln2_w, ln2_b):
    batch, seq, d = x.shape
    assert d == D_MODEL and meta_tokens.shape == (N_META, D_MODEL)
    lp = PAD_FRONT + N_META + seq

    inv_freq = 1.0 / (ROPE_BASE ** jnp.linspace(0.0, 1.0, RET_DK // 2, dtype=F32))
    coarse = (jnp.arange(lp // CHUNK, dtype=F32) * CHUNK - PAD_FRONT)[:, None, None] * inv_freq
    fine = jnp.arange(CHUNK, dtype=F32)[None, :, None] * inv_freq
    cos = (jnp.cos(coarse) * jnp.cos(fine) - jnp.sin(coarse) * jnp.sin(fine)).reshape(lp, RET_DK // 2)
    sin = (jnp.sin(coarse) * jnp.cos(fine) + jnp.cos(coarse) * jnp.sin(fine)).reshape(lp, RET_DK // 2)
    cosf = jnp.concatenate([cos, cos], axis=-1)
    sinf = jnp.concatenate([-sin, sin], axis=-1)

    w_in = ab_w_in[:1].astype(BF16)
    w_gg = w_in[:, :, 5128:]
    w_gate = jnp.pad(w_in[0, :, 5120:5128], ((0, 0), (0, LANES - 8)))
    alog = _pad_lanes(jnp.concatenate([jnp.zeros((GDN_HEADS,), F32), ab_A_log[0]]))
    bias = _pad_lanes(jnp.concatenate([jnp.zeros((GDN_HEADS,), F32), ab_dt_bias[0]]))
    h, hb, g0, g0_t = _assemble(x, meta_tokens, w_gate, alog, bias, lp)
    ones = jnp.ones((1, COL_TILE // 2), F32)
    zeros = jnp.zeros((1, COL_TILE), F32)
    p_qk = _proj(hb, w_in, (lambda j: j, 1), _epilogue_rope,
                 [cosf, sinf, jnp.concatenate([ones * RET_DK ** -0.5, ones], axis=1)],
                 [_per_row_in_batch(LANES), _per_row_in_batch(LANES), _per_col()], lp, "proj_ret_qk")
    p_vg = _proj(hb, w_in, (lambda j: j + 1, 2), _epilogue_plain, [], [], lp, "proj_ret_vg")
    p_gg = _proj(hb, w_gg, (lambda j: j, 1), _epilogue_plain, [], [], lp, "proj_gdn_g")
    g_qk = _proj(hb, w_in, (lambda j: j + 3, 1), functools.partial(_epilogue_conv, l2norm=True),
                 [jnp.concatenate([ab_conv_q[0], ab_conv_k[0]], axis=1), zeros,
                  jnp.concatenate([ones * GDN_DK ** -0.5, ones], axis=1)],
                 [_per_col(rows=CONV_K), _per_col(), _per_col()], lp, "proj_gdn_qk")
    g_v = _proj(hb, w_in, (lambda j: j + 4, 1), functools.partial(_epilogue_conv, l2norm=False),
                [ab_conv_v[0], zeros, zeros], [_per_col(rows=CONV_K), _per_col(), _per_col()], lp, "proj_gdn_v")
    y_ret = _retention(p_qk, p_vg, ab_ret_gn_w[0].reshape(1, -1), batch, lp)
    y_gdn = _gdn(g_qk, g_v, p_gg, g0, g0_t, ab_gdn_norm_w[0].reshape(1, -1), batch, lp)
    h = _outproj_ln(y_ret, y_gdn, 0, 0, ab_w_out[0].astype(BF16), h,
                    ln1_w[0].reshape(1, -1), ln1_b[0].reshape(1, -1), lp)
    w_in = c_w_in[:1].astype(BF16)
    w_dt = w_in[0, :, 5120:]
    w_gate = jnp.pad(jnp.concatenate([w_dt, w_dt], axis=1), ((0, 0), (0, LANES - 2 * SSD_HEADS)))
    alog = _pad_lanes(jnp.concatenate([jnp.zeros((SSD_HEADS,), F32), c_A_log[0]]))
    bias = _pad_lanes(jnp.concatenate([c_dt_bias[0], c_dt_bias[0]]))
    h, hb, g1, g1_t = _mlp_ln(h, mlp_w1[0].astype(BF16), mlp_w2[0].astype(BF16),
                              ln2_w[0].reshape(1, -1), ln2_b[0].reshape(1, -1), w_gate, alog, bias, lp)

    p_z = _proj(hb, w_in, (lambda j: j, 2), _epilogue_plain, [], [], lp, "proj_ssd_z")
    p_xbc = _proj(hb, w_in, (lambda j: j + 2, 3), functools.partial(_epilogue_conv, l2norm=False),
                  [c_conv_w[0], c_conv_b[0].reshape(1, -1), jnp.zeros((1, 3 * COL_TILE), F32)],
                  [_per_col(rows=CONV_K), _per_col(), _per_col()], lp, "proj_ssd_xbc")
    dskip = jnp.repeat(c_D[0].astype(F32), SSD_HEADDIM).reshape(1, -1)
    y_ssd = _ssd(p_z, p_xbc, g1, g1_t, dskip, c_norm_w[0].reshape(1, -1), batch, lp)
    h = _outproj_ln(y_ssd, y_ssd, 0, 1, c_w_out[0].astype(BF16), h,
                    ln1_w[1].reshape(1, -1), ln1_b[1].reshape(1, -1), lp)
    out = _mlp_ln_final(h, mlp_w1[1].astype(BF16), mlp_w2[1].astype(BF16),
                        ln2_w[1].reshape(1, -1), ln2_b[1].reshape(1, -1), batch, seq, lp)
    return out.reshape(batch, seq, d)
```

```python
import functools

import jax
import jax.numpy as jnp
from jax import lax
from jax.experimental import pallas as pl
from jax.experimental.pallas import tpu as pltpu

F32 = jnp.float32
BF16 = jnp.bfloat16
HI = lax.Precision.HIGHEST

D_MODEL = 1024
DEPTH = 2
N_META = 16
CONV_K = 4
RET_HEADS = 4
RET_DK = 128
RET_DV = 256
ROPE_BASE = 10000.0
GDN_HEADS = 4
GDN_DK = 128
GDN_DV = 256
SSD_DINNER = 2048
SSD_HEADDIM = 64
SSD_HEADS = 32
SSD_GROUPS = 4
SSD_HPG = 8
SSD_DSTATE = 128
SSD_GN = 512
D_FF = 4096
DN_ALPHA = (2 * DEPTH) ** 0.25
LN_EPS = 1e-5
GN_EPS = 1e-5
RMS_EPS = 1e-6

LANES = 128
COL_TILE = 1024
CHUNK = 128
GDN_CHUNK = 64
GDN_BASE = 8
PAD_FRONT = CHUNK - N_META
HALO = 8
CONV_ROWS = 64
PROJ_ROW_TILE = 1664
VMEM_LIMIT = 56 * 1024 * 1024


def _row_tile(rows_per_batch, largest=640):
    for tm in (1664, 640, 512, 256, 128):
        if tm <= largest and rows_per_batch % tm == 0:
            return tm
    raise ValueError(f"unsupported padded sequence length {rows_per_batch}")


def _params(sem):
    return pltpu.CompilerParams(dimension_semantics=sem, vmem_limit_bytes=VMEM_LIMIT)


def _softplus(x):
    return jnp.maximum(x, 0.0) + jnp.log1p(jnp.exp(-jnp.abs(x)))


def _silu(x):
    return x * jax.nn.sigmoid(x)


def _dot(a, b):
    return jnp.dot(a, b, preferred_element_type=F32)


def _dot_nt(a, b):
    return lax.dot_general(a, b, (((1,), (1,)), ((), ())), preferred_element_type=F32)


def _dot_hi(a, b):
    return jnp.dot(a, b, preferred_element_type=F32, precision=HI)


def _dot16(a, b):
    return _dot(a.astype(BF16), b.astype(BF16))


def _epilogue_plain(acc_ref, o_ref, *, tm):
    o_ref[...] = acc_ref[HALO:HALO + tm, :].astype(BF16)


def _epilogue_rope(acc_ref, o_ref, cos_ref, sin_ref, scale_ref, *, tm):
    for hd in range(COL_TILE // RET_DK):
        cols = slice(RET_DK * hd, RET_DK * (hd + 1))
        x = acc_ref[HALO:HALO + tm, cols]
        y = x * cos_ref[...] + pltpu.roll(x, RET_DK // 2, 1) * sin_ref[...]
        o_ref[:, cols] = (y * scale_ref[:, cols]).astype(BF16)


def _epilogue_conv(acc_ref, o_ref, cw_ref, cb_ref, scale_ref, *, tm, l2norm):
    for r in range(0, tm, CONV_ROWS):
        rows = slice(r, r + CONV_ROWS)
        for c in range(0, COL_TILE, LANES):
            cols = slice(c, c + LANES)
            w = [cw_ref[tap:tap + 1, cols] for tap in range(CONV_K)]
            xw = acc_ref[r:r + CONV_ROWS + HALO, cols]
            s1 = pltpu.roll(xw, 1, 0)
            y = w[3] * xw + w[2] * s1 + pltpu.roll(w[1] * xw + w[0] * s1, 2, 0)
            y = _silu(y[HALO:] + cb_ref[:, cols])
            if l2norm:
                inv = lax.rsqrt(jnp.sum(y * y, axis=-1, keepdims=True) + 1e-6)
                y = y * (inv * scale_ref[:, cols])
            o_ref[rows, cols] = y.astype(BF16)


def _lagged_steps(i, n_tiles, acc_a, acc_b, produce, consume):
    last_prev = acc_b if n_tiles % 2 == 0 else acc_a

    @pl.when(i == 0)
    def _():
        acc_b[...] = jnp.zeros(acc_b.shape, F32)

    @pl.when(jnp.logical_and(i % 2 == 0, i < n_tiles))
    def _():
        produce(acc_a, acc_b)
        consume(acc_b)

    @pl.when(jnp.logical_and(i % 2 == 1, i < n_tiles))
    def _():
        produce(acc_b, acc_a)
        consume(acc_a)

    @pl.when(i == n_tiles)
    def _():
        consume(last_prev)


def _proj_kernel(*refs, tm, tiles_per_batch, n_extra, epilogue):
    h_ref, w_ref = refs[0], refs[1]
    extra = refs[2:2 + n_extra]
    o_ref, acc_ref = refs[2 + n_extra:]

    @pl.when(pl.program_id(1) % tiles_per_batch == 0)
    def _():
        acc_ref[0:HALO, :] = jnp.zeros((HALO, COL_TILE), F32)

    acc_ref[HALO:HALO + tm, :] = _dot(h_ref[...], w_ref[...])
    epilogue(acc_ref, o_ref, *extra, tm=tm)
    acc_ref[0:HALO, :] = acc_ref[tm:tm + HALO, :]


def _proj(hb, w, col_tiles, epilogue, extra, extra_specs, rows_per_batch, name):
    rows = hb.shape[0]
    tm = _row_tile(rows_per_batch, largest=PROJ_ROW_TILE)
    tpb = rows_per_batch // tm
    col_fn, ncol = col_tiles
    return pl.pallas_call(
        functools.partial(_proj_kernel, tm=tm, tiles_per_batch=tpb, n_extra=len(extra), epilogue=epilogue),
        grid=(ncol, rows // tm),
        in_specs=[
            pl.BlockSpec((tm, D_MODEL), lambda j, i: (i, 0)),
            pl.BlockSpec((None, D_MODEL, COL_TILE), lambda j, i: (0, 0, col_fn(j))),
        ] + [spec(tm, tpb) for spec in extra_specs],
        out_specs=pl.BlockSpec((tm, COL_TILE), lambda j, i: (i, j)),
        out_shape=jax.ShapeDtypeStruct((rows, ncol * COL_TILE), BF16),
        scratch_shapes=[pltpu.VMEM((tm + HALO, COL_TILE), F32)],
        compiler_params=_params(("arbitrary", "arbitrary")),
        name=name,
    )(hb, w, *extra)


def _per_col(width=COL_TILE, rows=1):
    return lambda tm, tpb: pl.BlockSpec((rows, width), lambda j, i: (0, j))


def _per_row_in_batch(width):
    return lambda tm, tpb: pl.BlockSpec((tm, width), lambda j, i: (i % tpb, 0))


def _gdn_gates(hb, w_ref, alog_ref, bias_ref):
    acc = _dot(hb, w_ref[...])
    lane = lax.broadcasted_iota(jnp.int32, acc.shape, 1)
    decay = -jnp.exp(alog_ref[...]) * _softplus(acc + bias_ref[...])
    return jnp.where(lane < GDN_HEADS, jax.nn.sigmoid(acc), decay)


def _assemble_kernel(x_ref, meta_ref, w_ref, alog_ref, bias_ref, h_ref, hb_ref, g_ref, gt_ref, *, tm):
    def emit(tile):
        hb = tile.astype(BF16)
        h_ref[...] = tile
        hb_ref[...] = hb
        gates = _gdn_gates(hb, w_ref, alog_ref, bias_ref)
        g_ref[...] = gates
        gt_ref[...] = gates.T

    first = pl.program_id(1) == 0

    @pl.when(first)
    def _():
        head = PAD_FRONT + N_META
        emit(jnp.concatenate([jnp.zeros((PAD_FRONT, D_MODEL), F32), meta_ref[...], x_ref[0:tm - head, :]], axis=0))

    @pl.when(jnp.logical_not(first))
    def _():
        emit(x_ref[...])


def _assemble(x, meta_tokens, w_gate, alog, bias, rows_per_batch):
    batch, seq, _ = x.shape
    rows = batch * rows_per_batch
    tm = _row_tile(rows_per_batch, largest=PROJ_ROW_TILE)
    tpb = rows_per_batch // tm
    head = PAD_FRONT + N_META
    vec = pl.BlockSpec((1, LANES), lambda b, i: (0, 0))
    tile = lambda width: pl.BlockSpec((tm, width), lambda b, i: (b * tpb + i, 0))
    return pl.pallas_call(
        functools.partial(_assemble_kernel, tm=tm),
        grid=(batch, tpb),
        in_specs=[
            pl.BlockSpec((pl.Element(tm), pl.Element(D_MODEL)),
                         lambda b, i: (pl.multiple_of(b * seq + jnp.maximum(i * tm - head, 0), LANES), 0)),
            pl.BlockSpec((N_META, D_MODEL), lambda b, i: (0, 0)),
            pl.BlockSpec((D_MODEL, LANES), lambda b, i: (0, 0)),
            vec, vec,
        ],
        out_specs=[tile(D_MODEL), tile(D_MODEL), tile(LANES), pl.BlockSpec((LANES, tm), lambda b, i: (0, b * tpb + i))],
        out_shape=[jax.ShapeDtypeStruct((rows, D_MODEL), F32), jax.ShapeDtypeStruct((rows, D_MODEL), BF16),
                   jax.ShapeDtypeStruct((rows, LANES), F32), jax.ShapeDtypeStruct((LANES, rows), F32)],
        compiler_params=_params(("arbitrary", "arbitrary")),
        name="assemble",
    )(x.reshape(batch * seq, D_MODEL), meta_tokens, w_gate, alog, bias)


def _ssd_gates(hb, live, w_ref, alog_ref, bias_ref):
    acc = _dot(hb, w_ref[...])
    lane = lax.broadcasted_iota(jnp.int32, acc.shape, 1)
    dt = jnp.where(live, _softplus(acc + bias_ref[...]), 0.0)
    return jnp.where(lane < SSD_HEADS, dt, dt * -jnp.exp(alog_ref[...]))


def _retention_kernel(q_ref, k_ref, v_ref, g_ref, gnw_ref, o_ref, s_ref, *, batch, subchunks):
    n = CHUNK

    @pl.when(pl.program_id(0) == 0)
    def _():
        s_ref[...] = jnp.zeros(s_ref.shape, F32)

    t_idx = lax.broadcasted_iota(jnp.int32, (n, n), 0)
    s_idx = lax.broadcasted_iota(jnp.int32, (n, n), 1)
    t_col = lax.broadcasted_iota(jnp.int32, (n, 1), 0).astype(F32)
    gap = (t_idx - s_idx).astype(F32)
    log_gamma = [jnp.log1p(-jnp.exp2(jnp.full((1, 1), -5.0 - hd, F32))) for hd in range(RET_HEADS)]
    decay = [jnp.exp(jnp.where(t_idx >= s_idx, gap * lg, -jnp.inf)) for lg in log_gamma]
    e_in = [jnp.exp((t_col + 1.0) * lg) for lg in log_gamma]
    e_out = [jnp.exp((n - 1.0 - t_col) * lg) for lg in log_gamma]
    e_all = [jnp.exp(n * lg) for lg in log_gamma]
    rows = lambda sub: slice(n * sub, n * (sub + 1))
    chains = [(b, sub, hd) for b in range(batch) for sub in range(subchunks) for hd in range(RET_HEADS)]
    cs = range(len(chains))
    q = [q_ref[b, rows(sub), RET_DK * hd:RET_DK * (hd + 1)] for b, sub, hd in chains]
    k = [k_ref[b, rows(sub), RET_DK * hd:RET_DK * (hd + 1)] for b, sub, hd in chains]
    v = [v_ref[b, rows(sub), RET_DV * hd:RET_DV * (hd + 1)] for b, sub, hd in chains]
    att = [(_dot_nt(q[c], k[c]) * decay[chains[c][2]]).astype(BF16) for c in cs]
    intra = [_dot(att[c], v[c]) for c in cs]
    q_in = [(q[c].astype(F32) * e_in[chains[c][2]]).astype(BF16) for c in cs]
    k_out = [(k[c].astype(F32) * e_out[chains[c][2]]).T.astype(BF16) for c in cs]

    lanes = [(b, hd) for b in range(batch) for hd in range(RET_HEADS)]
    states = [s_ref[i] for i in range(len(lanes))]
    for sub in range(subchunks):
        idx = [chains.index((b, sub, hd)) for b, hd in lanes]
        o = [intra[c] + _dot(q_in[c], states[i].astype(BF16)) for i, c in enumerate(idx)]
        states = [states[i] * e_all[hd] + _dot(k_out[c], v[c]) for (i, c), (b, hd) in zip(enumerate(idx), lanes)]
        for i, (b, hd) in enumerate(lanes):
            cols = slice(RET_DV * hd, RET_DV * (hd + 1))
            oc = o[i] - jnp.mean(o[i], axis=-1, keepdims=True)
            on = oc * lax.rsqrt(jnp.mean(oc * oc, axis=-1, keepdims=True) + GN_EPS)
            gate = _silu(g_ref[b, rows(sub), cols].astype(F32))
            o_ref[b, rows(sub), cols] = (on * gnw_ref[:, cols] * gate).astype(BF16)
    for i in range(len(lanes)):
        s_ref[i] = states[i]


def _retention(qk, vg, gn_w, batch, rows_per_batch):
    n = CHUNK
    nc = rows_per_batch // n
    subchunks = next(s for s in (5, 4, 2, 1) if nc % s == 0)
    step_rows = n * subchunks
    qk = qk.reshape(batch, rows_per_batch, qk.shape[1])
    vg = vg.reshape(batch, rows_per_batch, vg.shape[1])
    out = pl.pallas_call(
        functools.partial(_retention_kernel, batch=batch, subchunks=subchunks),
        grid=(nc // subchunks,),
        in_specs=[
            pl.BlockSpec((batch, step_rows, 512), lambda c: (0, c, 0)),
            pl.BlockSpec((batch, step_rows, 512), lambda c: (0, c, 1)),
            pl.BlockSpec((batch, step_rows, 1024), lambda c: (0, c, 0)),
            pl.BlockSpec((batch, step_rows, 1024), lambda c: (0, c, 1)),
            pl.BlockSpec((1, 1024), lambda c: (0, 0)),
        ],
        out_specs=pl.BlockSpec((batch, step_rows, 1024), lambda c: (0, c, 0)),
        out_shape=jax.ShapeDtypeStruct((batch, rows_per_batch, 1024), BF16),
        scratch_shapes=[pltpu.VMEM((batch * RET_HEADS, RET_DK, RET_DV), F32)],
        compiler_params=_params(("arbitrary",)),
        name="retention",
    )(qk, qk, vg, vg, gn_w)
    return out.reshape(batch * rows_per_batch, 1024)


def _gdn_kernel(*refs, batch, subchunks):
    n = GDN_CHUNK
    q_ref, k_ref, v_ref, g_ref, gcol_ref = refs[:5]
    grow_refs = refs[5:5 + batch]
    nw_ref, o_ref, s_ref = refs[5 + batch:]

    @pl.when(pl.program_id(0) == 0)
    def _():
        s_ref[...] = jnp.zeros(s_ref.shape, F32)

    t_idx = lax.broadcasted_iota(jnp.int32, (n, n), 0)
    s_idx = lax.broadcasted_iota(jnp.int32, (n, n), 1)
    lower = (t_idx >= s_idx).astype(F32)
    upper = (t_idx <= s_idx).astype(F32)
    eye = (t_idx == s_idx).astype(F32)
    rows = lambda sub: slice(n * sub, n * (sub + 1))
    same_block = lambda size: (t_idx // size) == (s_idx // size)

    def state_free(chains):
        cs = range(len(chains))
        blocks = sorted({(b, sub) for b, sub, _ in chains})
        gcol = {bs: gcol_ref[bs[0], rows(bs[1])] for bs in blocks}
        gcum_col = {bs: _dot_hi(lower, gcol[bs]) for bs in blocks}
        gcum_row = {bs: _dot_hi(grow_refs[bs[0]][:, rows(bs[1])], upper) for bs in blocks}
        q = [q_ref[b, rows(sub), GDN_DK * hd:GDN_DK * (hd + 1)].astype(F32) for b, sub, hd in chains]
        k = [k_ref[b, rows(sub), GDN_DK * hd:GDN_DK * (hd + 1)].astype(F32) for b, sub, hd in chains]
        v = [v_ref[b, rows(sub), GDN_DV * hd:GDN_DV * (hd + 1)].astype(F32) for b, sub, hd in chains]
        beta = [gcol[b, sub][:, hd:hd + 1] for b, sub, hd in chains]
        gc = [gcum_col[b, sub][:, GDN_HEADS + hd:GDN_HEADS + hd + 1] for b, sub, hd in chains]
        gr = [gcum_row[b, sub][GDN_HEADS + hd:GDN_HEADS + hd + 1, :] for b, sub, hd in chains]
        g_last = [gc[c][n - 1:n, :] for c in cs]
        seg = [gc[c] - gr[c] for c in cs]
        dec_strict = [jnp.exp(jnp.where(t_idx > s_idx, seg[c], -jnp.inf)) for c in cs]
        dec_incl = [jnp.exp(jnp.where(t_idx >= s_idx, seg[c], -jnp.inf)) for c in cs]
        kb = [k[c] * beta[c] for c in cs]
        kq = [_dot_nt(jnp.concatenate([kb[c], q[c]], axis=0).astype(BF16), k[c].astype(BF16)) for c in cs]
        a = [kq[c][:n] * dec_strict[c] for c in cs]
        att = [(kq[c][n:] * dec_incl[c]).astype(BF16) for c in cs]
        diag = [jnp.where(same_block(GDN_BASE), a[c], 0.0) for c in cs]
        inv = [eye - diag[c] for c in cs]
        p = [_dot16(diag[c], diag[c]) for c in cs]
        span = 4
        while span < GDN_BASE:
            r = [_dot16(jnp.concatenate([p[c], inv[c]], axis=0), p[c]) for c in cs]
            inv = [inv[c] + r[c][n:] for c in cs]
            p = [r[c][:n] for c in cs]
            span *= 2
        inv = [inv[c] + _dot16(inv[c], p[c]) for c in cs]
        size = GDN_BASE
        while size < n:
            joins = jnp.logical_and(same_block(2 * size), jnp.logical_not(same_block(size)))
            off = [jnp.where(joins, a[c], 0.0) for c in cs]
            half = [_dot16(inv[c], off[c]) for c in cs]
            inv = [inv[c] - _dot16(half[c], inv[c]) for c in cs]
            size *= 2
        e_gc = [jnp.exp(gc[c]) for c in cs]
        rhs = [jnp.concatenate([v[c] * beta[c], kb[c] * e_gc[c]], axis=-1) for c in cs]
        u = [_dot16(inv[c], rhs[c]) for c in cs]
        lhs = [jnp.concatenate([u[c][:, GDN_DV:], q[c] * e_gc[c]], axis=0).astype(BF16) for c in cs]
        k_out = [(k[c] * jnp.exp(g_last[c] - gc[c])).T.astype(BF16) for c in cs]
        e_all = [jnp.exp(g_last[c]) for c in cs]
        return list(zip(att, u, lhs, k_out, e_all))

    lanes = [(b, hd) for b in range(batch) for hd in range(GDN_HEADS)]
    chains = [(b, sub, hd) for b in range(batch) for sub in range(subchunks) for hd in range(GDN_HEADS)]
    prepared = dict(zip(chains, state_free(chains)))
    states = [s_ref[i] for i in range(len(lanes))]
    for sub in range(subchunks):
        att, u, lhs, k_out, e_all = zip(*[prepared[b, sub, hd] for b, hd in lanes])
        ls = range(len(lanes))
        ws = [_dot(lhs[i], states[i].astype(BF16)) for i in ls]
        v_new = [(u[i][:, :GDN_DV] - ws[i][:n]).astype(BF16) for i in ls]
        o = [ws[i][n:] + _dot(att[i], v_new[i]) for i in ls]
        states = [states[i] * e_all[i] + _dot(k_out[i], v_new[i]) for i in ls]
        for i, (b, hd) in enumerate(lanes):
            cols = slice(GDN_DV * hd, GDN_DV * (hd + 1))
            on = o[i] * lax.rsqrt(jnp.mean(o[i] * o[i], axis=-1, keepdims=True) + RMS_EPS)
            gate = _silu(g_ref[b, rows(sub), cols].astype(F32))
            o_ref[b, rows(sub), cols] = (on * nw_ref[...] * gate).astype(BF16)
    for i in range(len(lanes)):
        s_ref[i] = states[i]


def _gdn(qk, v, gate, gates, gates_t, norm_w, batch, rows_per_batch):
    n = GDN_CHUNK
    subchunks = next(s for s in (10, 2) if (rows_per_batch // n) % s == 0 and (n * s) % LANES == 0)
    step_rows = n * subchunks
    steps = rows_per_batch // step_rows
    qk = qk.reshape(batch, rows_per_batch, qk.shape[1])
    v = v.reshape(batch, rows_per_batch, v.shape[1])
    gate = gate.reshape(batch, rows_per_batch, gate.shape[1])
    gates = gates.reshape(batch, rows_per_batch, LANES)
    out = pl.pallas_call(
        functools.partial(_gdn_kernel, batch=batch, subchunks=subchunks),
        grid=(steps,),
        in_specs=[
            pl.BlockSpec((batch, step_rows, 512), lambda c: (0, c, 0)),
            pl.BlockSpec((batch, step_rows, 512), lambda c: (0, c, 1)),
            pl.BlockSpec((batch, step_rows, 1024), lambda c: (0, c, 0)),
            pl.BlockSpec((batch, step_rows, 1024), lambda c: (0, c, 0)),
            pl.BlockSpec((batch, step_rows, LANES), lambda c: (0, c, 0)),
        ] + [pl.BlockSpec((8, step_rows), lambda c, b=b: (0, b * steps + c)) for b in range(batch)] + [
            pl.BlockSpec((1, GDN_DV), lambda c: (0, 0)),
        ],
        out_specs=pl.BlockSpec((batch, step_rows, 1024), lambda c: (0, c, 0)),
        out_shape=jax.ShapeDtypeStruct((batch, rows_per_batch, 1024), BF16),
        scratch_shapes=[pltpu.VMEM((batch * GDN_HEADS, GDN_DK, GDN_DV), F32)],
        compiler_params=_params(("arbitrary",)),
        name="gated_delta",
    )(qk, qk, v, gate, gates, *([gates_t] * batch), norm_w)
    return out.reshape(batch * rows_per_batch, 1024)


def _layer_norm(x, w, b):
    xc = x - jnp.mean(x, axis=-1, keepdims=True)
    return xc * lax.rsqrt(jnp.mean(xc * xc, axis=-1, keepdims=True) + LN_EPS) * w + b


def _outproj_ln_kernel(ya_ref, yb_ref, w_ref, h_ref, lw_ref, lb_ref, o_ref, acc_a, acc_b, *, n_tiles):
    half = ya_ref.shape[1]

    def produce(cur, prev):
        acc = _dot(ya_ref[...], w_ref[0:half, :]) + _dot(yb_ref[...], w_ref[half:2 * half, :])
        cur[...] = DN_ALPHA * h_ref[...] + acc

    def consume(prev):
        o_ref[...] = _layer_norm(prev[...], lw_ref[...], lb_ref[...])

    _lagged_steps(pl.program_id(0), n_tiles, acc_a, acc_b, produce, consume)


def _outproj_ln(ya, yb, col_a, col_b, w, h, lw, lb, rows_per_batch):
    rows = h.shape[0]
    tm = _row_tile(rows_per_batch)
    n_tiles = rows // tm
    half = w.shape[0] // 2
    cur = lambda i: jnp.minimum(i, n_tiles - 1)
    prev = lambda i: jnp.maximum(i - 1, 0)
    vec = pl.BlockSpec((1, D_MODEL), lambda i: (0, 0))
    return pl.pallas_call(
        functools.partial(_outproj_ln_kernel, n_tiles=n_tiles),
        grid=(n_tiles + 1,),
        in_specs=[
            pl.BlockSpec((tm, half), lambda i: (cur(i), col_a)),
            pl.BlockSpec((tm, half), lambda i: (cur(i), col_b)),
            pl.BlockSpec((2 * half, D_MODEL), lambda i: (0, 0), pipeline_mode=pl.Buffered(1)),
            pl.BlockSpec((tm, D_MODEL), lambda i: (cur(i), 0)),
            vec, vec,
        ],
        out_specs=pl.BlockSpec((tm, D_MODEL), lambda i: (prev(i), 0)),
        out_shape=jax.ShapeDtypeStruct((rows, D_MODEL), F32),
        scratch_shapes=[pltpu.VMEM((tm, D_MODEL), F32), pltpu.VMEM((tm, D_MODEL), F32)],
        compiler_params=_params(("arbitrary",)),
        name="outproj_ln",
    )(ya, yb, w, h, lw, lb)


def _mlp(h_ref, w1_ref, w2_ref, lw_ref, lb_ref):
    h = h_ref[...]
    hb = h.astype(BF16)
    acc = jnp.zeros(h.shape, F32)
    step = 1024
    for f in range(0, D_FF, step):
        a = _dot(hb, w1_ref[:, f:f + step])
        a = jnp.square(jnp.maximum(a, 0.0)).astype(BF16)
        acc = acc + _dot(a, w2_ref[f:f + step, :])
    return _layer_norm(DN_ALPHA * h + acc, lw_ref[...], lb_ref[...])


def _mlp_ln_kernel(h_ref, w1_ref, w2_ref, lw_ref, lb_ref, wg_ref, alog_ref, bias_ref, o_ref, ob_ref, g_ref, gt_ref,
                   *, tm, tiles_per_batch):
    out = _mlp(h_ref, w1_ref, w2_ref, lw_ref, lb_ref)
    o_ref[...] = out
    row = (pl.program_id(0) % tiles_per_batch) * tm + lax.broadcasted_iota(jnp.int32, (tm, 1), 0)
    live = row >= PAD_FRONT
    hb = jnp.where(live, out, 0.0).astype(BF16)
    ob_ref[...] = hb
    gates = _ssd_gates(hb, live, wg_ref, alog_ref, bias_ref)
    g_ref[...] = gates
    gt_ref[...] = gates.T


def _mlp_ln_final_kernel(h_ref, w1_ref, w2_ref, lw_ref, lb_ref, o_ref):
    o_ref[...] = _mlp(h_ref, w1_ref, w2_ref, lw_ref, lb_ref)


def _mlp_weight_specs(index):
    return [
        pl.BlockSpec((D_MODEL, D_FF), index, pipeline_mode=pl.Buffered(1)),
        pl.BlockSpec((D_FF, D_MODEL), index, pipeline_mode=pl.Buffered(1)),
        pl.BlockSpec((1, D_MODEL), index),
        pl.BlockSpec((1, D_MODEL), index),
    ]


def _mlp_ln(h, w1, w2, lw, lb, w_gate, alog, bias, rows_per_batch):
    rows = h.shape[0]
    tm = _row_tile(rows_per_batch)
    tile = pl.BlockSpec((tm, D_MODEL), lambda i: (i, 0))
    vec = pl.BlockSpec((1, LANES), lambda i: (0, 0))
    return pl.pallas_call(
        functools.partial(_mlp_ln_kernel, tm=tm, tiles_per_batch=rows_per_batch // tm),
        grid=(rows // tm,),
        in_specs=[tile] + _mlp_weight_specs(lambda i: (0, 0)) + [pl.BlockSpec((D_MODEL, LANES), lambda i: (0, 0)),
                                                                   vec, vec],
        out_specs=[tile, tile, pl.BlockSpec((tm, LANES), lambda i: (i, 0)), pl.BlockSpec((LANES, tm), lambda i: (0, i))],
        out_shape=[jax.ShapeDtypeStruct((rows, D_MODEL), F32), jax.ShapeDtypeStruct((rows, D_MODEL), BF16),
                   jax.ShapeDtypeStruct((rows, LANES), F32), jax.ShapeDtypeStruct((LANES, rows), F32)],
        compiler_params=_params(("arbitrary",)),
        name="mlp_ln",
    )(h, w1, w2, lw, lb, w_gate, alog, bias)


def _mlp_ln_final(h, w1, w2, lw, lb, batch, seq, rows_per_batch):
    tm = next(t for t in (1024, 512, 256, 128) if seq % t == 0)
    tiles = seq // tm
    first_row = PAD_FRONT + N_META
    return pl.pallas_call(
        _mlp_ln_final_kernel,
        grid=(batch, tiles),
        in_specs=[pl.BlockSpec((pl.Element(tm), pl.Element(D_MODEL)),
                               lambda b, i: (pl.multiple_of(b * rows_per_batch + first_row + i * tm, 128), 0))]
        + _mlp_weight_specs(lambda b, i: (0, 0)),
        out_specs=pl.BlockSpec((tm, D_MODEL), lambda b, i: (b * tiles + i, 0)),
        out_shape=jax.ShapeDtypeStruct((batch * seq, D_MODEL), F32),
        compiler_params=_params(("arbitrary", "arbitrary")),
        name="mlp_ln_final",
    )(h, w1, w2, lw, lb)


def _ssd_kernel(*refs, batch, subchunks):
    n = CHUNK
    z_ref, x_ref, b_ref, c_ref, gcol_ref = refs[:5]
    grow_refs = refs[5:5 + batch]
    dskip_ref, nw_ref, o_ref, s_ref = refs[5 + batch:]
    pair_w = 2 * SSD_HEADDIM
    pairs_per_group = SSD_HPG // 2
    group_w = SSD_HPG * SSD_HEADDIM
    n_pairs = SSD_HEADS // 2

    @pl.when(pl.program_id(0) == 0)
    def _():
        s_ref[...] = jnp.zeros(s_ref.shape, F32)

    t_idx = lax.broadcasted_iota(jnp.int32, (n, n), 0)
    s_idx = lax.broadcasted_iota(jnp.int32, (n, n), 1)
    causal = t_idx >= s_idx
    lower = causal.astype(F32)
    upper = (t_idx <= s_idx).astype(F32)
    lane_v = lax.broadcasted_iota(jnp.int32, (n, pair_w), 1)
    lane_s = lax.broadcasted_iota(jnp.int32, (SSD_DSTATE, pair_w), 1)
    lane_1 = lax.broadcasted_iota(jnp.int32, (1, pair_w), 1)
    bs = range(batch)
    for sub in range(subchunks):
        rows = slice(n * sub, n * (sub + 1))
        grow = [grow_refs[b][:, rows] for b in bs]
        gcum_col = [_dot_hi(lower, gcol_ref[b, rows]) for b in bs]
        gcum_row = [_dot_hi(grow[b], upper) for b in bs]
        for g in range(SSD_GROUPS):
            q = [c_ref[b, rows, SSD_DSTATE * g:SSD_DSTATE * (g + 1)] for b in bs]
            k = [b_ref[b, rows, SSD_DSTATE * g:SSD_DSTATE * (g + 1)] for b in bs]
            qf = [q[b].astype(F32) for b in bs]
            kt = [k[b].astype(F32).T for b in bs]
            cb = [_dot_nt(q[b], k[b]) for b in bs]
            ys = [[] for _ in bs]
            for p in range(pairs_per_group):
                pair = g * pairs_per_group + p
                col = group_w * g + pair_w * p
                for b in bs:
                    xv = x_ref[b, rows, col:col + pair_w]
                    state = s_ref[b * n_pairs + pair]
                    o = jnp.zeros((n, pair_w), F32)
                    upd = jnp.zeros((SSD_DSTATE, pair_w), F32)
                    last = []
                    for side in range(2):
                        hd = 2 * pair + side
                        gc = gcum_col[b][:, SSD_HEADS + hd:SSD_HEADS + hd + 1]
                        gr = gcum_row[b][SSD_HEADS + hd:SSD_HEADS + hd + 1, :]
                        dt_row = grow[b][hd:hd + 1, :]
                        g_last = gr[:, n - 1:n]
                        dec = jnp.exp(jnp.where(causal, gc - gr, -jnp.inf))
                        att = cb[b] * dec * dt_row
                        q_in = qf[b] * jnp.exp(gc)
                        mine_v = (lane_v >= SSD_HEADDIM) if side else (lane_v < SSD_HEADDIM)
                        mine_s = (lane_s >= SSD_HEADDIM) if side else (lane_s < SSD_HEADDIM)
                        xm = jnp.where(mine_v, xv, jnp.zeros_like(xv))
                        sm = jnp.where(mine_s, state, 0.0).astype(BF16)
                        o = o + _dot(att.astype(BF16), xm) + _dot(q_in.astype(BF16), sm)
                        k_out = kt[b] * (jnp.exp(g_last - gr) * dt_row)
                        upd = upd + _dot(k_out.astype(BF16), xm)
                        last.append(jnp.exp(g_last))
                    s_ref[b * n_pairs + pair] = state * jnp.where(lane_1 < SSD_HEADDIM, last[0], last[1]) + upd
                    ys[b].append(o + xv.astype(F32) * dskip_ref[:, col:col + pair_w])
            cols = slice(group_w * g, group_w * (g + 1))
            for b in bs:
                y = jnp.concatenate(ys[b], axis=-1)
                y = y * _silu(z_ref[b, rows, cols].astype(F32))
                y = y * lax.rsqrt(jnp.mean(y * y, axis=-1, keepdims=True) + RMS_EPS)
                o_ref[b, rows, cols] = (y * nw_ref[:, cols]).astype(BF16)


def _ssd(z, xbc, gates, gates_t, dskip, norm_w, batch, rows_per_batch):
    n = CHUNK
    subchunks = next(s for s in (5, 4, 2, 1) if (rows_per_batch // n) % s == 0)
    step_rows = n * subchunks
    steps = rows_per_batch // step_rows
    z = z.reshape(batch, rows_per_batch, z.shape[1])
    xbc = xbc.reshape(batch, rows_per_batch, xbc.shape[1])
    gates = gates.reshape(batch, rows_per_batch, LANES)
    out = pl.pallas_call(
        functools.partial(_ssd_kernel, batch=batch, subchunks=subchunks),
        grid=(steps,),
        in_specs=[
            pl.BlockSpec((batch, step_rows, SSD_DINNER), lambda c: (0, c, 0)),
            pl.BlockSpec((batch, step_rows, SSD_DINNER), lambda c: (0, c, 0)),
            pl.BlockSpec((batch, step_rows, SSD_GN), lambda c: (0, c, 4)),
            pl.BlockSpec((batch, step_rows, SSD_GN), lambda c: (0, c, 5)),
            pl.BlockSpec((batch, step_rows, LANES), lambda c: (0, c, 0)),
        ] + [pl.BlockSpec((2 * SSD_HEADS, step_rows), lambda c, b=b: (0, b * steps + c)) for b in range(batch)] + [
            pl.BlockSpec((1, SSD_DINNER), lambda c: (0, 0)),
            pl.BlockSpec((1, SSD_DINNER), lambda c: (0, 0)),
        ],
        out_specs=pl.BlockSpec((batch, step_rows, SSD_DINNER), lambda c: (0, c, 0)),
        out_shape=jax.ShapeDtypeStruct((batch, rows_per_batch, SSD_DINNER), BF16),
        scratch_shapes=[pltpu.VMEM((batch * SSD_HEADS // 2, SSD_DSTATE, 2 * SSD_HEADDIM), F32)],
        compiler_params=_params(("arbitrary",)),
        name="ssd",
    )(z, xbc, xbc, xbc, gates, *([gates_t] * batch), dskip, norm_w)
    return out.reshape(batch * rows_per_batch, SSD_DINNER)


def _pad_lanes(v):
    v = v.reshape(1, -1).astype(F32)
    return jnp.pad(v, ((0, 0), (0, LANES - v.shape[1])))


def kernel(x, meta_tokens, ab_w_in, ab_ret_gn_w, ab_conv_q, ab_conv_k, ab_conv_v, ab_A_log, ab_dt_bias, ab_gdn_norm_w, ab_w_out, c_w_in, c_conv_w, c_conv_b, c_A_log, c_dt_bias, c_D, c_norm_w, c_w_out, mlp_w1, mlp_w2, ln1_w, ln1_b, ln2_w, ln2_b):
    batch, seq, d = x.shape
    assert d == D_MODEL and meta_tokens.shape == (N_META, D_MODEL)
    lp = PAD_FRONT + N_META + seq

    inv_freq = 1.0 / (ROPE_BASE ** jnp.linspace(0.0, 1.0, RET_DK // 2, dtype=F32))
    coarse = (jnp.arange(lp // CHUNK, dtype=F32) * CHUNK - PAD_FRONT)[:, None, None] * inv_freq
    fine = jnp.arange(CHUNK, dtype=F32)[None, :, None] * inv_freq
    cos = (jnp.cos(coarse) * jnp.cos(fine) - jnp.sin(coarse) * jnp.sin(fine)).reshape(lp, RET_DK // 2)
    sin = (jnp.sin(coarse) * jnp.cos(fine) + jnp.cos(coarse) * jnp.sin(fine)).reshape(lp, RET_DK // 2)
    cosf = jnp.concatenate([cos, cos], axis=-1)
    sinf = jnp.concatenate([-sin, sin], axis=-1)

    w_in = ab_w_in[:1].astype(BF16)
    w_gg = w_in[:, :, 5128:]
    w_gate = jnp.pad(w_in[0, :, 5120:5128], ((0, 0), (0, LANES - 8)))
    alog = _pad_lanes(jnp.concatenate([jnp.zeros((GDN_HEADS,), F32), ab_A_log[0]]))
    bias = _pad_lanes(jnp.concatenate([jnp.zeros((GDN_HEADS,), F32), ab_dt_bias[0]]))
    h, hb, g0, g0_t = _assemble(x, meta_tokens, w_gate, alog, bias, lp)
    ones = jnp.ones((1, COL_TILE // 2), F32)
    zeros = jnp.zeros((1, COL_TILE), F32)
    p_qk = _proj(hb, w_in, (lambda j: j, 1), _epilogue_rope,
                 [cosf, sinf, jnp.concatenate([ones * RET_DK ** -0.5, ones], axis=1)],
                 [_per_row_in_batch(LANES), _per_row_in_batch(LANES), _per_col()], lp, "proj_ret_qk")
    p_vg = _proj(hb, w_in, (lambda j: j + 1, 2), _epilogue_plain, [], [], lp, "proj_ret_vg")
    p_gg = _proj(hb, w_gg, (lambda j: j, 1), _epilogue_plain, [], [], lp, "proj_gdn_g")
    g_qk = _proj(hb, w_in, (lambda j: j + 3, 1), functools.partial(_epilogue_conv, l2norm=True),
                 [jnp.concatenate([ab_conv_q[0], ab_conv_k[0]], axis=1), zeros,
                  jnp.concatenate([ones * GDN_DK ** -0.5, ones], axis=1)],
                 [_per_col(rows=CONV_K), _per_col(), _per_col()], lp, "proj_gdn_qk")
    g_v = _proj(hb, w_in, (lambda j: j + 4, 1), functools.partial(_epilogue_conv, l2norm=False),
                [ab_conv_v[0], zeros, zeros], [_per_col(rows=CONV_K), _per_col(), _per_col()], lp, "proj_gdn_v")
    y_ret = _retention(p_qk, p_vg, ab_ret_gn_w[0].reshape(1, -1), batch, lp)
    y_gdn = _gdn(g_qk, g_v, p_gg, g0, g0_t, ab_gdn_norm_w[0].reshape(1, -1), batch, lp)
    h = _outproj_ln(y_ret, y_gdn, 0, 0, ab_w_out[0].astype(BF16), h,
                    ln1_w[0].reshape(1, -1), ln1_b[0].reshape(1, -1), lp)
    w_in = c_w_in[:1].astype(BF16)
    w_dt = w_in[0, :, 5120:]
    w_gate = jnp.pad(jnp.concatenate([w_dt, w_dt], axis=1), ((0, 0), (0, LANES - 2 * SSD_HEADS)))
    alog = _pad_lanes(jnp.concatenate([jnp.zeros((SSD_HEADS,), F32), c_A_log[0]]))
    bias = _pad_lanes(jnp.concatenate([c_dt_bias[0], c_dt_bias[0]]))
    h, hb, g1, g1_t = _mlp_ln(h, mlp_w1[0].astype(BF16), mlp_w2[0].astype(BF16),
                              ln2_w[0].reshape(1, -1), ln2_b[0].reshape(1, -1), w_gate, alog, bias, lp)

    p_z = _proj(hb, w_in, (lambda j: j, 2), _epilogue_plain, [], [], lp, "proj_ssd_z")
    p_xbc = _proj(hb, w_in, (lambda j: j + 2, 3), functools.partial(_epilogue_conv, l2norm=False),
                  [c_conv_w[0], c_conv_b[0].reshape(1, -1), jnp.zeros((1, 3 * COL_TILE), F32)],
                  [_per_col(rows=CONV_K), _per_col(), _per_col()], lp, "proj_ssd_xbc")
    dskip = jnp.repeat(c_D[0].astype(F32), SSD_HEADDIM).reshape(1, -1)
    y_ssd = _ssd(p_z, p_xbc, g1, g1_t, dskip, c_norm_w[0].reshape(1, -1), batch, lp)
    h = _outproj_ln(y_ssd, y_ssd, 0, 1, c_w_out[0].astype(BF16), h,
                    ln1_w[1].reshape(1, -1), ln1_b[1].reshape(1, -1), lp)
    out = _mlp_ln_final(h, mlp_w1[1].astype(BF16), mlp_w2[1].astype(BF16),
                        ln2_w[1].reshape(1, -1), ln2_b[1].reshape(1, -1), batch, seq, lp)
    return out.reshape(batch, seq, d)
```

```python
import functools

import jax
import jax.numpy as jnp
from jax import lax
from jax.experimental import pallas as pl
from jax.experimental.pallas import tpu as pltpu

F32 = jnp.float32
BF16 = jnp.bfloat16
HI = lax.Precision.HIGHEST

D_MODEL = 1024
DEPTH = 2
N_META = 16
CONV_K = 4
RET_HEADS = 4
RET_DK = 128
RET_DV = 256
ROPE_BASE = 10000.0
GDN_HEADS = 4
GDN_DK = 128
GDN_DV = 256
SSD_DINNER = 2048
SSD_HEADDIM = 64
SSD_HEADS = 32
SSD_GROUPS = 4
SSD_HPG = 8
SSD_DSTATE = 128
SSD_GN = 512
D_FF = 4096
DN_ALPHA = (2 * DEPTH) ** 0.25
LN_EPS = 1e-5
GN_EPS = 1e-5
RMS_EPS = 1e-6

LANES = 128
COL_TILE = 1024
CHUNK = 128
GDN_CHUNK = 64
GDN_BASE = 8
PAD_FRONT = CHUNK - N_META
HALO = 8
CONV_ROWS = 64
PROJ_ROW_TILE = 1664
VMEM_LIMIT = 56 * 1024 * 1024


def _row_tile(rows_per_batch, largest=640):
    for tm in (1664, 640, 512, 256, 128):
        if tm <= largest and rows_per_batch % tm == 0:
            return tm
    raise ValueError(f"unsupported padded sequence length {rows_per_batch}")


def _params(sem):
    return pltpu.CompilerParams(dimension_semantics=sem, vmem_limit_bytes=VMEM_LIMIT)


def _softplus(x):
    return jnp.maximum(x, 0.0) + jnp.log1p(jnp.exp(-jnp.abs(x)))


def _silu(x):
    return x * jax.nn.sigmoid(x)


def _dot(a, b):
    return jnp.dot(a, b, preferred_element_type=F32)


def _dot_nt(a, b):
    return lax.dot_general(a, b, (((1,), (1,)), ((), ())), preferred_element_type=F32)


def _dot_hi(a, b):
    return jnp.dot(a, b, preferred_element_type=F32, precision=HI)


def _dot16(a, b):
    return _dot(a.astype(BF16), b.astype(BF16))


def _epilogue_plain(acc_ref, o_ref, *, tm):
    o_ref[...] = acc_ref[HALO:HALO + tm, :].astype(BF16)


def _epilogue_rope(acc_ref, o_ref, cos_ref, sin_ref, scale_ref, *, tm):
    for hd in range(COL_TILE // RET_DK):
        cols = slice(RET_DK * hd, RET_DK * (hd + 1))
        x = acc_ref[HALO:HALO + tm, cols]
        y = x * cos_ref[...] + pltpu.roll(x, RET_DK // 2, 1) * sin_ref[...]
        o_ref[:, cols] = (y * scale_ref[:, cols]).astype(BF16)


def _epilogue_conv(acc_ref, o_ref, cw_ref, cb_ref, scale_ref, *, tm, l2norm):
    for r in range(0, tm, CONV_ROWS):
        rows = slice(r, r + CONV_ROWS)
        for c in range(0, COL_TILE, LANES):
            cols = slice(c, c + LANES)
            w = [cw_ref[tap:tap + 1, cols] for tap in range(CONV_K)]
            xw = acc_ref[r:r + CONV_ROWS + HALO, cols]
            s1 = pltpu.roll(xw, 1, 0)
            y = w[3] * xw + w[2] * s1 + pltpu.roll(w[1] * xw + w[0] * s1, 2, 0)
            y = _silu(y[HALO:] + cb_ref[:, cols])
            if l2norm:
                inv = lax.rsqrt(jnp.sum(y * y, axis=-1, keepdims=True) + 1e-6)
                y = y * (inv * scale_ref[:, cols])
            o_ref[rows, cols] = y.astype(BF16)


def _lagged_steps(i, n_tiles, acc_a, acc_b, produce, consume):
    last_prev = acc_b if n_tiles % 2 == 0 else acc_a

    @pl.when(i == 0)
    def _():
        acc_b[...] = jnp.zeros(acc_b.shape, F32)

    @pl.when(jnp.logical_and(i % 2 == 0, i < n_tiles))
    def _():
        produce(acc_a, acc_b)
        consume(acc_b)

    @pl.when(jnp.logical_and(i % 2 == 1, i < n_tiles))
    def _():
        produce(acc_b, acc_a)
        consume(acc_a)

    @pl.when(i == n_tiles)
    def _():
        consume(last_prev)


def _proj_kernel(*refs, tm, tiles_per_batch, n_extra, epilogue):
    h_ref, w_ref = refs[0], refs[1]
    extra = refs[2:2 + n_extra]
    o_ref, acc_ref = refs[2 + n_extra:]

    @pl.when(pl.program_id(1) % tiles_per_batch == 0)
    def _():
        acc_ref[0:HALO, :] = jnp.zeros((HALO, COL_TILE), F32)

    acc_ref[HALO:HALO + tm, :] = _dot(h_ref[...], w_ref[...])
    epilogue(acc_ref, o_ref, *extra, tm=tm)
    acc_ref[0:HALO, :] = acc_ref[tm:tm + HALO, :]


def _proj(hb, w, col_tiles, epilogue, extra, extra_specs, rows_per_batch, name):
    rows = hb.shape[0]
    tm = _row_tile(rows_per_batch, largest=PROJ_ROW_TILE)
    tpb = rows_per_batch // tm
    col_fn, ncol = col_tiles
    return pl.pallas_call(
        functools.partial(_proj_kernel, tm=tm, tiles_per_batch=tpb, n_extra=len(extra), epilogue=epilogue),
        grid=(ncol, rows // tm),
        in_specs=[
            pl.BlockSpec((tm, D_MODEL), lambda j, i: (i, 0)),
            pl.BlockSpec((None, D_MODEL, COL_TILE), lambda j, i: (0, 0, col_fn(j))),
        ] + [spec(tm, tpb) for spec in extra_specs],
        out_specs=pl.BlockSpec((tm, COL_TILE), lambda j, i: (i, j)),
        out_shape=jax.ShapeDtypeStruct((rows, ncol * COL_TILE), BF16),
        scratch_shapes=[pltpu.VMEM((tm + HALO, COL_TILE), F32)],
        compiler_params=_params(("arbitrary", "arbitrary")),
        name=name,
    )(hb, w, *extra)


def _per_col(width=COL_TILE, rows=1):
    return lambda tm, tpb: pl.BlockSpec((rows, width), lambda j, i: (0, j))


def _per_row_in_batch(width):
    return lambda tm, tpb: pl.BlockSpec((tm, width), lambda j, i: (i % tpb, 0))


def _gdn_gates(hb, w_ref, alog_ref, bias_ref):
    acc = _dot(hb, w_ref[...])
    lane = lax.broadcasted_iota(jnp.int32, acc.shape, 1)
    decay = -jnp.exp(alog_ref[...]) * _softplus(acc + bias_ref[...])
    return jnp.where(lane < GDN_HEADS, jax.nn.sigmoid(acc), decay)


def _assemble_kernel(x_ref, meta_ref, w_ref, alog_ref, bias_ref, h_ref, hb_ref, g_ref, gt_ref, *, tm):
    def emit(tile):
        hb = tile.astype(BF16)
        h_ref[...] = tile
        hb_ref[...] = hb
        gates = _gdn_gates(hb, w_ref, alog_ref, bias_ref)
        g_ref[...] = gates
        gt_ref[...] = gates.T

    first = pl.program_id(1) == 0

    @pl.when(first)
    def _():
        head = PAD_FRONT + N_META
        emit(jnp.concatenate([jnp.zeros((PAD_FRONT, D_MODEL), F32), meta_ref[...], x_ref[0:tm - head, :]], axis=0))

    @pl.when(jnp.logical_not(first))
    def _():
        emit(x_ref[...])


def _assemble(x, meta_tokens, w_gate, alog, bias, rows_per_batch):
    batch, seq, _ = x.shape
    rows = batch * rows_per_batch
    tm = _row_tile(rows_per_batch, largest=PROJ_ROW_TILE)
    tpb = rows_per_batch // tm
    head = PAD_FRONT + N_META
    vec = pl.BlockSpec((1, LANES), lambda b, i: (0, 0))
    tile = lambda width: pl.BlockSpec((tm, width), lambda b, i: (b * tpb + i, 0))
    return pl.pallas_call(
        functools.partial(_assemble_kernel, tm=tm),
        grid=(batch, tpb),
        in_specs=[
            pl.BlockSpec((pl.Element(tm), pl.Element(D_MODEL)),
                         lambda b, i: (pl.multiple_of(b * seq + jnp.maximum(i * tm - head, 0), LANES), 0)),
            pl.BlockSpec((N_META, D_MODEL), lambda b, i: (0, 0)),
            pl.BlockSpec((D_MODEL, LANES), lambda b, i: (0, 0)),
            vec, vec,
        ],
        out_specs=[tile(D_MODEL), tile(D_MODEL), tile(LANES), pl.BlockSpec((LANES, tm), lambda b, i: (0, b * tpb + i))],
        out_shape=[jax.ShapeDtypeStruct((rows, D_MODEL), F32), jax.ShapeDtypeStruct((rows, D_MODEL), BF16),
                   jax.ShapeDtypeStruct((rows, LANES), F32), jax.ShapeDtypeStruct((LANES, rows), F32)],
        compiler_params=_params(("arbitrary", "arbitrary")),
        name="assemble",
    )(x.reshape(batch * seq, D_MODEL), meta_tokens, w_gate, alog, bias)


def _ssd_gates(hb, live, w_ref, alog_ref, bias_ref):
    acc = _dot(hb, w_ref[...])
    lane = lax.broadcasted_iota(jnp.int32, acc.shape, 1)
    dt = jnp.where(live, _softplus(acc + bias_ref[...]), 0.0)
    return jnp.where(lane < SSD_HEADS, dt, dt * -jnp.exp(alog_ref[...]))


def _retention_kernel(q_ref, k_ref, v_ref, g_ref, gnw_ref, o_ref, s_ref, *, batch, subchunks):
    n = CHUNK

    @pl.when(pl.program_id(0) == 0)
    def _():
        s_ref[...] = jnp.zeros(s_ref.shape, F32)

    t_idx = lax.broadcasted_iota(jnp.int32, (n, n), 0)
    s_idx = lax.broadcasted_iota(jnp.int32, (n, n), 1)
    t_col = lax.broadcasted_iota(jnp.int32, (n, 1), 0).astype(F32)
    gap = (t_idx - s_idx).astype(F32)
    log_gamma = [jnp.log1p(-jnp.exp2(jnp.full((1, 1), -5.0 - hd, F32))) for hd in range(RET_HEADS)]
    decay = [jnp.exp(jnp.where(t_idx >= s_idx, gap * lg, -jnp.inf)) for lg in log_gamma]
    e_in = [jnp.exp((t_col + 1.0) * lg) for lg in log_gamma]
    e_out = [jnp.exp((n - 1.0 - t_col) * lg) for lg in log_gamma]
    e_all = [jnp.exp(n * lg) for lg in log_gamma]
    rows = lambda sub: slice(n * sub, n * (sub + 1))
    chains = [(b, sub, hd) for b in range(batch) for sub in range(subchunks) for hd in range(RET_HEADS)]
    cs = range(len(chains))
    q = [q_ref[b, rows(sub), RET_DK * hd:RET_DK * (hd + 1)] for b, sub, hd in chains]
    k = [k_ref[b, rows(sub), RET_DK * hd:RET_DK * (hd + 1)] for b, sub, hd in chains]
    v = [v_ref[b, rows(sub), RET_DV * hd:RET_DV * (hd + 1)] for b, sub, hd in chains]
    att = [(_dot_nt(q[c], k[c]) * decay[chains[c][2]]).astype(BF16) for c in cs]
    intra = [_dot(att[c], v[c]) for c in cs]
    q_in = [(q[c].astype(F32) * e_in[chains[c][2]]).astype(BF16) for c in cs]
    k_out = [(k[c].astype(F32) * e_out[chains[c][2]]).T.astype(BF16) for c in cs]

    lanes = [(b, hd) for b in range(batch) for hd in range(RET_HEADS)]
    states = [s_ref[i] for i in range(len(lanes))]
    for sub in range(subchunks):
        idx = [chains.index((b, sub, hd)) for b, hd in lanes]
        o = [intra[c] + _dot(q_in[c], states[i].astype(BF16)) for i, c in enumerate(idx)]
        states = [states[i] * e_all[hd] + _dot(k_out[c], v[c]) for (i, c), (b, hd) in zip(enumerate(idx), lanes)]
        for i, (b, hd) in enumerate(lanes):
            cols = slice(RET_DV * hd, RET_DV * (hd + 1))
            oc = o[i] - jnp.mean(o[i], axis=-1, keepdims=True)
            on = oc * lax.rsqrt(jnp.mean(oc * oc, axis=-1, keepdims=True) + GN_EPS)
            gate = _silu(g_ref[b, rows(sub), cols].astype(F32))
            o_ref[b, rows(sub), cols] = (on * gnw_ref[:, cols] * gate).astype(BF16)
    for i in range(len(lanes)):
        s_ref[i] = states[i]


def _retention(qk, vg, gn_w, batch, rows_per_batch):
    n = CHUNK
    nc = rows_per_batch // n
    subchunks = next(s for s in (5, 4, 2, 1) if nc % s == 0)
    step_rows = n * subchunks
    qk = qk.reshape(batch, rows_per_batch, qk.shape[1])
    vg = vg.reshape(batch, rows_per_batch, vg.shape[1])
    out = pl.pallas_call(
        functools.partial(_retention_kernel, batch=batch, subchunks=subchunks),
        grid=(nc // subchunks,),
        in_specs=[
            pl.BlockSpec((batch, step_rows, 512), lambda c: (0, c, 0)),
            pl.BlockSpec((batch, step_rows, 512), lambda c: (0, c, 1)),
            pl.BlockSpec((batch, step_rows, 1024), lambda c: (0, c, 0)),
            pl.BlockSpec((batch, step_rows, 1024), lambda c: (0, c, 1)),
            pl.BlockSpec((1, 1024), lambda c: (0, 0)),
        ],
        out_specs=pl.BlockSpec((batch, step_rows, 1024), lambda c: (0, c, 0)),
        out_shape=jax.ShapeDtypeStruct((batch, rows_per_batch, 1024), BF16),
        scratch_shapes=[pltpu.VMEM((batch * RET_HEADS, RET_DK, RET_DV), F32)],
        compiler_params=_params(("arbitrary",)),
        name="retention",
    )(qk, qk, vg, vg, gn_w)
    return out.reshape(batch * rows_per_batch, 1024)


def _gdn_kernel(*refs, batch, subchunks):
    n = GDN_CHUNK
    q_ref, k_ref, v_ref, g_ref, gcol_ref = refs[:5]
    grow_refs = refs[5:5 + batch]
    nw_ref, o_ref, s_ref = refs[5 + batch:]

    @pl.when(pl.program_id(0) == 0)
    def _():
        s_ref[...] = jnp.zeros(s_ref.shape, F32)

    t_idx = lax.broadcasted_iota(jnp.int32, (n, n), 0)
    s_idx = lax.broadcasted_iota(jnp.int32, (n, n), 1)
    lower = (t_idx >= s_idx).astype(F32)
    upper = (t_idx <= s_idx).astype(F32)
    eye = (t_idx == s_idx).astype(F32)
    rows = lambda sub: slice(n * sub, n * (sub + 1))
    same_block = lambda size: (t_idx // size) == (s_idx // size)

    def state_free(chains):
        cs = range(len(chains))
        blocks = sorted({(b, sub) for b, sub, _ in chains})
        gcol = {bs: gcol_ref[bs[0], rows(bs[1])] for bs in blocks}
        gcum_col = {bs: _dot_hi(lower, gcol[bs]) for bs in blocks}
        gcum_row = {bs: _dot_hi(grow_refs[bs[0]][:, rows(bs[1])], upper) for bs in blocks}
        q = [q_ref[b, rows(sub), GDN_DK * hd:GDN_DK * (hd + 1)].astype(F32) for b, sub, hd in chains]
        k = [k_ref[b, rows(sub), GDN_DK * hd:GDN_DK * (hd + 1)].astype(F32) for b, sub, hd in chains]
        v = [v_ref[b, rows(sub), GDN_DV * hd:GDN_DV * (hd + 1)].astype(F32) for b, sub, hd in chains]
        beta = [gcol[b, sub][:, hd:hd + 1] for b, sub, hd in chains]
        gc = [gcum_col[b, sub][:, GDN_HEADS + hd:GDN_HEADS + hd + 1] for b, sub, hd in chains]
        gr = [gcum_row[b, sub][GDN_HEADS + hd:GDN_HEADS + hd + 1, :] for b, sub, hd in chains]
        g_last = [gc[c][n - 1:n, :] for c in cs]
        seg = [gc[c] - gr[c] for c in cs]
        dec_strict = [jnp.exp(jnp.where(t_idx > s_idx, seg[c], -jnp.inf)) for c in cs]
        dec_incl = [jnp.exp(jnp.where(t_idx >= s_idx, seg[c], -jnp.inf)) for c in cs]
        kb = [k[c] * beta[c] for c in cs]
        kq = [_dot_nt(jnp.concatenate([kb[c], q[c]], axis=0).astype(BF16), k[c].astype(BF16)) for c in cs]
        a = [kq[c][:n] * dec_strict[c] for c in cs]
        att = [(kq[c][n:] * dec_incl[c]).astype(BF16) for c in cs]
        diag = [jnp.where(same_block(GDN_BASE), a[c], 0.0) for c in cs]
        inv = [eye - diag[c] for c in cs]
        p = [_dot16(diag[c], diag[c]) for c in cs]
        span = 4
        while span < GDN_BASE:
            r = [_dot16(jnp.concatenate([p[c], inv[c]], axis=0), p[c]) for c in cs]
            inv = [inv[c] + r[c][n:] for c in cs]
            p = [r[c][:n] for c in cs]
            span *= 2
        inv = [inv[c] + _dot16(inv[c], p[c]) for c in cs]
        size = GDN_BASE
        while size < n:
            joins = jnp.logical_and(same_block(2 * size), jnp.logical_not(same_block(size)))
            off = [jnp.where(joins, a[c], 0.0) for c in cs]
            half = [_dot16(inv[c], off[c]) for c in cs]
            inv = [inv[c] - _dot16(half[c], inv[c]) for c in cs]
            size *= 2
        e_gc = [jnp.exp(gc[c]) for c in cs]
        rhs = [jnp.concatenate([v[c] * beta[c], kb[c] * e_gc[c]], axis=-1) for c in cs]
        u = [_dot16(inv[c], rhs[c]) for c in cs]
        lhs = [jnp.concatenate([u[c][:, GDN_DV:], q[c] * e_gc[c]], axis=0).astype(BF16) for c in cs]
        k_out = [(k[c] * jnp.exp(g_last[c] - gc[c])).T.astype(BF16) for c in cs]
        e_all = [jnp.exp(g_last[c]) for c in cs]
        return list(zip(att, u, lhs, k_out, e_all))

    lanes = [(b, hd) for b in range(batch) for hd in range(GDN_HEADS)]
    chains = [(b, sub, hd) for b in range(batch) for sub in range(subchunks) for hd in range(GDN_HEADS)]
    prepared = dict(zip(chains, state_free(chains)))
    states = [s_ref[i] for i in range(len(lanes))]
    for sub in range(subchunks):
        att, u, lhs, k_out, e_all = zip(*[prepared[b, sub, hd] for b, hd in lanes])
        ls = range(len(lanes))
        ws = [_dot(lhs[i], states[i].astype(BF16)) for i in ls]
        v_new = [(u[i][:, :GDN_DV] - ws[i][:n]).astype(BF16) for i in ls]
        o = [ws[i][n:] + _dot(att[i], v_new[i]) for i in ls]
        states = [states[i] * e_all[i] + _dot(k_out[i], v_new[i]) for i in ls]
        for i, (b, hd) in enumerate(lanes):
            cols = slice(GDN_DV * hd, GDN_DV * (hd + 1))
            on = o[i] * lax.rsqrt(jnp.mean(o[i] * o[i], axis=-1, keepdims=True) + RMS_EPS)
            gate = _silu(g_ref[b, rows(sub), cols].astype(F32))
            o_ref[b, rows(sub), cols] = (on * nw_ref[...] * gate).astype(BF16)
    for i in range(len(lanes)):
        s_ref[i] = states[i]


def _gdn(qk, v, gate, gates, gates_t, norm_w, batch, rows_per_batch):
    n = GDN_CHUNK
    subchunks = next(s for s in (10, 2) if (rows_per_batch // n) % s == 0 and (n * s) % LANES == 0)
    step_rows = n * subchunks
    steps = rows_per_batch // step_rows
    qk = qk.reshape(batch, rows_per_batch, qk.shape[1])
    v = v.reshape(batch, rows_per_batch, v.shape[1])
    gate = gate.reshape(batch, rows_per_batch, gate.shape[1])
    gates = gates.reshape(batch, rows_per_batch, LANES)
    out = pl.pallas_call(
        functools.partial(_gdn_kernel, batch=batch, subchunks=subchunks),
        grid=(steps,),
        in_specs=[
            pl.BlockSpec((batch, step_rows, 512), lambda c: (0, c, 0)),
            pl.BlockSpec((batch, step_rows, 512), lambda c: (0, c, 1)),
            pl.BlockSpec((batch, step_rows, 1024), lambda c: (0, c, 0)),
            pl.BlockSpec((batch, step_rows, 1024), lambda c: (0, c, 0)),
            pl.BlockSpec((batch, step_rows, LANES), lambda c: (0, c, 0)),
        ] + [pl.BlockSpec((8, step_rows), lambda c, b=b: (0, b * steps + c)) for b in range(batch)] + [
            pl.BlockSpec((1, GDN_DV), lambda c: (0, 0)),
        ],
        out_specs=pl.BlockSpec((batch, step_rows, 1024), lambda c: (0, c, 0)),
        out_shape=jax.ShapeDtypeStruct((batch, rows_per_batch, 1024), BF16),
        scratch_shapes=[pltpu.VMEM((batch * GDN_HEADS, GDN_DK, GDN_DV), F32)],
        compiler_params=_params(("arbitrary",)),
        name="gated_delta",
    )(qk, qk, v, gate, gates, *([gates_t] * batch), norm_w)
    return out.reshape(batch * rows_per_batch, 1024)


def _layer_norm(x, w, b):
    xc = x - jnp.mean(x, axis=-1, keepdims=True)
    return xc * lax.rsqrt(jnp.mean(xc * xc, axis=-1, keepdims=True) + LN_EPS) * w + b


def _outproj_ln_kernel(ya_ref, yb_ref, w_ref, h_ref, lw_ref, lb_ref, o_ref, acc_a, acc_b, *, n_tiles):
    half = ya_ref.shape[1]

    def produce(cur, prev):
        acc = _dot(ya_ref[...], w_ref[0:half, :]) + _dot(yb_ref[...], w_ref[half:2 * half, :])
        cur[...] = DN_ALPHA * h_ref[...] + acc

    def consume(prev):
        o_ref[...] = _layer_norm(prev[...], lw_ref[...], lb_ref[...])

    _lagged_steps(pl.program_id(0), n_tiles, acc_a, acc_b, produce, consume)


def _outproj_ln(ya, yb, col_a, col_b, w, h, lw, lb, rows_per_batch):
    rows = h.shape[0]
    tm = _row_tile(rows_per_batch)
    n_tiles = rows // tm
    half = w.shape[0] // 2
    cur = lambda i: jnp.minimum(i, n_tiles - 1)
    prev = lambda i: jnp.maximum(i - 1, 0)
    vec = pl.BlockSpec((1, D_MODEL), lambda i: (0, 0))
    return pl.pallas_call(
        functools.partial(_outproj_ln_kernel, n_tiles=n_tiles),
        grid=(n_tiles + 1,),
        in_specs=[
            pl.BlockSpec((tm, half), lambda i: (cur(i), col_a)),
            pl.BlockSpec((tm, half), lambda i: (cur(i), col_b)),
            pl.BlockSpec((2 * half, D_MODEL), lambda i: (0, 0), pipeline_mode=pl.Buffered(1)),
            pl.BlockSpec((tm, D_MODEL), lambda i: (cur(i), 0)),
            vec, vec,
        ],
        out_specs=pl.BlockSpec((tm, D_MODEL), lambda i: (prev(i), 0)),
        out_shape=jax.ShapeDtypeStruct((rows, D_MODEL), F32),
        scratch_shapes=[pltpu.VMEM((tm, D_MODEL), F32), pltpu.VMEM((tm, D_MODEL), F32)],
        compiler_params=_params(("arbitrary",)),
        name="outproj_ln",
    )(ya, yb, w, h, lw, lb)


def _mlp(h_ref, w1_ref, w2_ref, lw_ref, lb_ref):
    h = h_ref[...]
    hb = h.astype(BF16)
    acc = jnp.zeros(h.shape, F32)
    step = 1024
    for f in range(0, D_FF, step):
        a = _dot(hb, w1_ref[:, f:f + step])
        a = jnp.square(jnp.maximum(a, 0.0)).astype(BF16)
        acc = acc + _dot(a, w2_ref[f:f + step, :])
    return _layer_norm(DN_ALPHA * h + acc, lw_ref[...], lb_ref[...])


def _mlp_ln_kernel(h_ref, w1_ref, w2_ref, lw_ref, lb_ref, wg_ref, alog_ref, bias_ref, o_ref, ob_ref, g_ref, gt_ref,
                   *, tm, tiles_per_batch):
    out = _mlp(h_ref, w1_ref, w2_ref, lw_ref, lb_ref)
    o_ref[...] = out
    row = (pl.program_id(0) % tiles_per_batch) * tm + lax.broadcasted_iota(jnp.int32, (tm, 1), 0)
    live = row >= PAD_FRONT
    hb = jnp.where(live, out, 0.0).astype(BF16)
    ob_ref[...] = hb
    gates = _ssd_gates(hb, live, wg_ref, alog_ref, bias_ref)
    g_ref[...] = gates
    gt_ref[...] = gates.T


def _mlp_ln_final_kernel(h_ref, w1_ref, w2_ref, lw_ref, lb_ref, o_ref):
    o_ref[...] = _mlp(h_ref, w1_ref, w2_ref, lw_ref, lb_ref)


def _mlp_weight_specs(index):
    return [
        pl.BlockSpec((D_MODEL, D_FF), index, pipeline_mode=pl.Buffered(1)),
        pl.BlockSpec((D_FF, D_MODEL), index, pipeline_mode=pl.Buffered(1)),
        pl.BlockSpec((1, D_MODEL), index),
        pl.BlockSpec((1, D_MODEL), index),
    ]


def _mlp_ln(h, w1, w2, lw, lb, w_gate, alog, bias, rows_per_batch):
    rows = h.shape[0]
    tm = _row_tile(rows_per_batch)
    tile = pl.BlockSpec((tm, D_MODEL), lambda i: (i, 0))
    vec = pl.BlockSpec((1, LANES), lambda i: (0, 0))
    return pl.pallas_call(
        functools.partial(_mlp_ln_kernel, tm=tm, tiles_per_batch=rows_per_batch // tm),
        grid=(rows // tm,),
        in_specs=[tile] + _mlp_weight_specs(lambda i: (0, 0)) + [pl.BlockSpec((D_MODEL, LANES), lambda i: (0, 0)),
                                                                   vec, vec],
        out_specs=[tile, tile, pl.BlockSpec((tm, LANES), lambda i: (i, 0)), pl.BlockSpec((LANES, tm), lambda i: (0, i))],
        out_shape=[jax.ShapeDtypeStruct((rows, D_MODEL), F32), jax.ShapeDtypeStruct((rows, D_MODEL), BF16),
                   jax.ShapeDtypeStruct((rows, LANES), F32), jax.ShapeDtypeStruct((LANES, rows), F32)],
        compiler_params=_params(("arbitrary",)),
        name="mlp_ln",
    )(h, w1, w2, lw, lb, w_gate, alog, bias)


def _mlp_ln_final(h, w1, w2, lw, lb, batch, seq, rows_per_batch):
    tm = next(t for t in (1024, 512, 256, 128) if seq % t == 0)
    tiles = seq // tm
    first_row = PAD_FRONT + N_META
    return pl.pallas_call(
        _mlp_ln_final_kernel,
        grid=(batch, tiles),
        in_specs=[pl.BlockSpec((pl.Element(tm), pl.Element(D_MODEL)),
                               lambda b, i: (pl.multiple_of(b * rows_per_batch + first_row + i * tm, 128), 0))]
        + _mlp_weight_specs(lambda b, i: (0, 0)),
        out_specs=pl.BlockSpec((tm, D_MODEL), lambda b, i: (b * tiles + i, 0)),
        out_shape=jax.ShapeDtypeStruct((batch * seq, D_MODEL), F32),
        compiler_params=_params(("arbitrary", "arbitrary")),
        name="mlp_ln_final",
    )(h, w1, w2, lw, lb)


def _ssd_kernel(*refs, batch, subchunks):
    n = CHUNK
    z_ref, x_ref, b_ref, c_ref, gcol_ref = refs[:5]
    grow_refs = refs[5:5 + batch]
    dskip_ref, nw_ref, o_ref, s_ref = refs[5 + batch:]
    pair_w = 2 * SSD_HEADDIM
    pairs_per_group = SSD_HPG // 2
    group_w = SSD_HPG * SSD_HEADDIM
    n_pairs = SSD_HEADS // 2

    @pl.when(pl.program_id(0) == 0)
    def _():
        s_ref[...] = jnp.zeros(s_ref.shape, F32)

    t_idx = lax.broadcasted_iota(jnp.int32, (n, n), 0)
    s_idx = lax.broadcasted_iota(jnp.int32, (n, n), 1)
    causal = t_idx >= s_idx
    lower = causal.astype(F32)
    upper = (t_idx <= s_idx).astype(F32)
    lane_v = lax.broadcasted_iota(jnp.int32, (n, pair_w), 1)
    lane_s = lax.broadcasted_iota(jnp.int32, (SSD_DSTATE, pair_w), 1)
    lane_1 = lax.broadcasted_iota(jnp.int32, (1, pair_w), 1)
    bs = range(batch)
    for sub in range(subchunks):
        rows = slice(n * sub, n * (sub + 1))
        grow = [grow_refs[b][:, rows] for b in bs]
        gcum_col = [_dot_hi(lower, gcol_ref[b, rows]) for b in bs]
        gcum_row = [_dot_hi(grow[b], upper) for b in bs]
        for g in range(SSD_GROUPS):
            q = [c_ref[b, rows, SSD_DSTATE * g:SSD_DSTATE * (g + 1)] for b in bs]
            k = [b_ref[b, rows, SSD_DSTATE * g:SSD_DSTATE * (g + 1)] for b in bs]
            qf = [q[b].astype(F32) for b in bs]
            kt = [k[b].astype(F32).T for b in bs]
            cb = [_dot_nt(q[b], k[b]) for b in bs]
            ys = [[] for _ in bs]
            for p in range(pairs_per_group):
                pair = g * pairs_per_group + p
                col = group_w * g + pair_w * p
                for b in bs:
                    xv = x_ref[b, rows, col:col + pair_w]
                    state = s_ref[b * n_pairs + pair]
                    o = jnp.zeros((n, pair_w), F32)
                    upd = jnp.zeros((SSD_DSTATE, pair_w), F32)
                    last = []
                    for side in range(2):
                        hd = 2 * pair + side
                        gc = gcum_col[b][:, SSD_HEADS + hd:SSD_HEADS + hd + 1]
                        gr = gcum_row[b][SSD_HEADS + hd:SSD_HEADS + hd + 1, :]
                        dt_row = grow[b][hd:hd + 1, :]
                        g_last = gr[:, n - 1:n]
                        dec = jnp.exp(jnp.where(causal, gc - gr, -jnp.inf))
                        att = cb[b] * dec * dt_row
                        q_in = qf[b] * jnp.exp(gc)
                        mine_v = (lane_v >= SSD_HEADDIM) if side else (lane_v < SSD_HEADDIM)
                        mine_s = (lane_s >= SSD_HEADDIM) if side else (lane_s < SSD_HEADDIM)
                        xm = jnp.where(mine_v, xv, jnp.zeros_like(xv))
                        sm = jnp.where(mine_s, state, 0.0).astype(BF16)
                        o = o + _dot(att.astype(BF16), xm) + _dot(q_in.astype(BF16), sm)
                        k_out = kt[b] * (jnp.exp(g_last - gr) * dt_row)
                        upd = upd + _dot(k_out.astype(BF16), xm)
                        last.append(jnp.exp(g_last))
                    s_ref[b * n_pairs + pair] = state * jnp.where(lane_1 < SSD_HEADDIM, last[0], last[1]) + upd
                    ys[b].append(o + xv.astype(F32) * dskip_ref[:, col:col + pair_w])
            cols = slice(group_w * g, group_w * (g + 1))
            for b in bs:
                y = jnp.concatenate(ys[b], axis=-1)
                y = y * _silu(z_ref[b, rows, cols].astype(F32))
                y = y * lax.rsqrt(jnp.mean(y * y, axis=-1, keepdims=True) + RMS_EPS)
                o_ref[b, rows, cols] = (y * nw_ref[:, cols]).astype(BF16)


def _ssd(z, xbc, gates, gates_t, dskip, norm_w, batch, rows_per_batch):
    n = CHUNK
    subchunks = 1
    step_rows = n * subchunks
    steps = rows_per_batch // step_rows
    z = z.reshape(batch, rows_per_batch, z.shape[1])
    xbc = xbc.reshape(batch, rows_per_batch, xbc.shape[1])
    gates = gates.reshape(batch, rows_per_batch, LANES)
    out = pl.pallas_call(
        functools.partial(_ssd_kernel, batch=batch, subchunks=subchunks),
        grid=(steps,),
        in_specs=[
            pl.BlockSpec((batch, step_rows, SSD_DINNER), lambda c: (0, c, 0)),
            pl.BlockSpec((batch, step_rows, SSD_DINNER), lambda c: (0, c, 0)),
            pl.BlockSpec((batch, step_rows, SSD_GN), lambda c: (0, c, 4)),
            pl.BlockSpec((batch, step_rows, SSD_GN), lambda c: (0, c, 5)),
            pl.BlockSpec((batch, step_rows, LANES), lambda c: (0, c, 0)),
        ] + [pl.BlockSpec((2 * SSD_HEADS, step_rows), lambda c, b=b: (0, b * steps + c)) for b in range(batch)] + [
            pl.BlockSpec((1, SSD_DINNER), lambda c: (0, 0)),
            pl.BlockSpec((1, SSD_DINNER), lambda c: (0, 0)),
        ],
        out_specs=pl.BlockSpec((batch, step_rows, SSD_DINNER), lambda c: (0, c, 0)),
        out_shape=jax.ShapeDtypeStruct((batch, rows_per_batch, SSD_DINNER), BF16),
        scratch_shapes=[pltpu.VMEM((batch * SSD_HEADS // 2, SSD_DSTATE, 2 * SSD_HEADDIM), F32)],
        compiler_params=_params(("arbitrary",)),
        name="ssd",
    )(z, xbc, xbc, xbc, gates, *([gates_t] * batch), dskip, norm_w)
    return out.reshape(batch * rows_per_batch, SSD_DINNER)


def _pad_lanes(v):
    v = v.reshape(1, -1).astype(F32)
    return jnp.pad(v, ((0, 0), (0, LANES - v.shape[1])))


def kernel(x, meta_tokens, ab_w_in, ab_ret_gn_w, ab_conv_q, ab_conv_k, ab_conv_v, ab_A_log, ab_dt_bias, ab_gdn_norm_w, ab_w_out, c_w_in, c_conv_w, c_conv_b, c_A_log, c_dt_bias, c_D, c_norm_w, c_w_out, mlp_w1, mlp_w2, ln1_w, ln1_b, ln2_w, ln2_b):
    batch, seq, d = x.shape
    assert d == D_MODEL and meta_tokens.shape == (N_META, D_MODEL)
    lp = PAD_FRONT + N_META + seq

    inv_freq = 1.0 / (ROPE_BASE ** jnp.linspace(0.0, 1.0, RET_DK // 2, dtype=F32))
    coarse = (jnp.arange(lp // CHUNK, dtype=F32) * CHUNK - PAD_FRONT)[:, None, None] * inv_freq
    fine = jnp.arange(CHUNK, dtype=F32)[None, :, None] * inv_freq
    cos = (jnp.cos(coarse) * jnp.cos(fine) - jnp.sin(coarse) * jnp.sin(fine)).reshape(lp, RET_DK // 2)
    sin = (jnp.sin(coarse) * jnp.cos(fine) + jnp.cos(coarse) * jnp.sin(fine)).reshape(lp, RET_DK // 2)
    cosf = jnp.concatenate([cos, cos], axis=-1)
    sinf = jnp.concatenate([-sin, sin], axis=-1)

    w_in = ab_w_in[:1].astype(BF16)
    w_gg = w_in[:, :, 5128:]
    w_gate = jnp.pad(w_in[0, :, 5120:5128], ((0, 0), (0, LANES - 8)))
    alog = _pad_lanes(jnp.concatenate([jnp.zeros((GDN_HEADS,), F32), ab_A_log[0]]))
    bias = _pad_lanes(jnp.concatenate([jnp.zeros((GDN_HEADS,), F32), ab_dt_bias[0]]))
    h, hb, g0, g0_t = _assemble(x, meta_tokens, w_gate, alog, bias, lp)
    ones = jnp.ones((1, COL_TILE // 2), F32)
    zeros = jnp.zeros((1, COL_TILE), F32)
    p_qk = _proj(hb, w_in, (lambda j: j, 1), _epilogue_rope,
                 [cosf, sinf, jnp.concatenate([ones * RET_DK ** -0.5, ones], axis=1)],
                 [_per_row_in_batch(LANES), _per_row_in_batch(LANES), _per_col()], lp, "proj_ret_qk")
    p_vg = _proj(hb, w_in, (lambda j: j + 1, 2), _epilogue_plain, [], [], lp, "proj_ret_vg")
    p_gg = _proj(hb, w_gg, (lambda j: j, 1), _epilogue_plain, [], [], lp, "proj_gdn_g")
    g_qk = _proj(hb, w_in, (lambda j: j + 3, 1), functools.partial(_epilogue_conv, l2norm=True),
                 [jnp.concatenate([ab_conv_q[0], ab_conv_k[0]], axis=1), zeros,
                  jnp.concatenate([ones * GDN_DK ** -0.5, ones], axis=1)],
                 [_per_col(rows=CONV_K), _per_col(), _per_col()], lp, "proj_gdn_qk")
    g_v = _proj(hb, w_in, (lambda j: j + 4, 1), functools.partial(_epilogue_conv, l2norm=False),
                [ab_conv_v[0], zeros, zeros], [_per_col(rows=CONV_K), _per_col(), _per_col()], lp, "proj_gdn_v")
    y_ret = _retention(p_qk, p_vg, ab_ret_gn_w[0].reshape(1, -1), batch, lp)
    y_gdn = _gdn(g_qk, g_v, p_gg, g0, g0_t, ab_gdn_norm_w[0].reshape(1, -1), batch, lp)
    h = _outproj_ln(y_ret, y_gdn, 0, 0, ab_w_out[0].astype(BF16), h,
                    ln1_w[0].reshape(1, -1), ln1_b[0].reshape(1, -1), lp)
    w_in = c_w_in[:1].astype(BF16)
    w_dt = w_in[0, :, 5120:]
    w_gate = jnp.pad(jnp.concatenate([w_dt, w_dt], axis=1), ((0, 0), (0, LANES - 2 * SSD_HEADS)))
    alog = _pad_lanes(jnp.concatenate([jnp.zeros((SSD_HEADS,), F32), c_A_log[0]]))
    bias = _pad_lanes(jnp.concatenate([c_dt_bias[0], c_dt_bias[0]]))
    h, hb, g1, g1_t = _mlp_ln(h, mlp_w1[0].astype(BF16), mlp_w2[0].astype(BF16),
                              ln2_w[0].reshape(1, -1), ln2_b[0].reshape(1, -1), w_gate, alog, bias, lp)

    p_z = _proj(hb, w_in, (lambda j: j, 2), _epilogue_plain, [], [], lp, "proj_ssd_z")
    p_xbc = _proj(hb, w_in, (lambda j: j + 2, 3), functools.partial(_epilogue_conv, l2norm=False),
                  [c_conv_w[0], c_conv_b[0].reshape(1, -1), jnp.zeros((1, 3 * COL_TILE), F32)],
                  [_per_col(rows=CONV_K), _per_col(), _per_col()], lp, "proj_ssd_xbc")
    dskip = jnp.repeat(c_D[0].astype(F32), SSD_HEADDIM).reshape(1, -1)
    y_ssd = _ssd(p_z, p_xbc, g1, g1_t, dskip, c_norm_w[0].reshape(1, -1), batch, lp)
    h = _outproj_ln(y_ssd, y_ssd, 0, 1, c_w_out[0].astype(BF16), h,
                    ln1_w[1].reshape(1, -1), ln1_b[1].reshape(1, -1), lp)
    out = _mlp_ln_final(h, mlp_w1[1].astype(BF16), mlp_w2[1].astype(BF16),
                        ln2_w[1].reshape(1, -1), ln2_b[1].reshape(1, -1), batch, seq, lp)
    return out.reshape(batch, seq, d)
```

```python
import functools

import jax
import jax.numpy as jnp
from jax import lax
from jax.experimental import pallas as pl
from jax.experimental.pallas import tpu as pltpu

F32 = jnp.float32
BF16 = jnp.bfloat16
HI = lax.Precision.HIGHEST

D_MODEL = 1024
DEPTH = 2
N_META = 16
CONV_K = 4
RET_HEADS = 4
RET_DK = 128
RET_DV = 256
ROPE_BASE = 10000.0
GDN_HEADS = 4
GDN_DK = 128
GDN_DV = 256
SSD_DINNER = 2048
SSD_HEADDIM = 64
SSD_HEADS = 32
SSD_GROUPS = 4
SSD_HPG = 8
SSD_DSTATE = 128
SSD_GN = 512
D_FF = 4096
DN_ALPHA = (2 * DEPTH) ** 0.25
LN_EPS = 1e-5
GN_EPS = 1e-5
RMS_EPS = 1e-6

LANES = 128
COL_TILE = 1024
CHUNK = 128
GDN_CHUNK = 64
GDN_BASE = 8
PAD_FRONT = CHUNK - N_META
HALO = 8
CONV_ROWS = 64
PROJ_ROW_TILE = 1664
VMEM_LIMIT = 56 * 1024 * 1024


def _row_tile(rows_per_batch, largest=640):
    for tm in (1664, 640, 512, 256, 128):
        if tm <= largest and rows_per_batch % tm == 0:
            return tm
    raise ValueError(f"unsupported padded sequence length {rows_per_batch}")


def _params(sem):
    return pltpu.CompilerParams(dimension_semantics=sem, vmem_limit_bytes=VMEM_LIMIT)


def _softplus(x):
    return jnp.maximum(x, 0.0) + jnp.log1p(jnp.exp(-jnp.abs(x)))


def _silu(x):
    return x * jax.nn.sigmoid(x)


def _dot(a, b):
    return jnp.dot(a, b, preferred_element_type=F32)


def _dot_nt(a, b):
    return lax.dot_general(a, b, (((1,), (1,)), ((), ())), preferred_element_type=F32)


def _dot_hi(a, b):
    return jnp.dot(a, b, preferred_element_type=F32, precision=HI)


def _dot16(a, b):
    return _dot(a.astype(BF16), b.astype(BF16))


def _epilogue_plain(acc_ref, o_ref, *, tm):
    o_ref[...] = acc_ref[HALO:HALO + tm, :].astype(BF16)


def _epilogue_rope(acc_ref, o_ref, cos_ref, sin_ref, scale_ref, *, tm):
    for hd in range(COL_TILE // RET_DK):
        cols = slice(RET_DK * hd, RET_DK * (hd + 1))
        x = acc_ref[HALO:HALO + tm, cols]
        y = x * cos_ref[...] + pltpu.roll(x, RET_DK // 2, 1) * sin_ref[...]
        o_ref[:, cols] = (y * scale_ref[:, cols]).astype(BF16)


def _epilogue_conv(acc_ref, o_ref, cw_ref, cb_ref, scale_ref, *, tm, l2norm):
    for r in range(0, tm, CONV_ROWS):
        rows = slice(r, r + CONV_ROWS)
        for c in range(0, COL_TILE, LANES):
            cols = slice(c, c + LANES)
            w = [cw_ref[tap:tap + 1, cols] for tap in range(CONV_K)]
            xw = acc_ref[r:r + CONV_ROWS + HALO, cols]
            s1 = pltpu.roll(xw, 1, 0)
            y = w[3] * xw + w[2] * s1 + pltpu.roll(w[1] * xw + w[0] * s1, 2, 0)
            y = _silu(y[HALO:] + cb_ref[:, cols])
            if l2norm:
                inv = lax.rsqrt(jnp.sum(y * y, axis=-1, keepdims=True) + 1e-6)
                y = y * (inv * scale_ref[:, cols])
            o_ref[rows, cols] = y.astype(BF16)


def _lagged_steps(i, n_tiles, acc_a, acc_b, produce, consume):
    last_prev = acc_b if n_tiles % 2 == 0 else acc_a

    @pl.when(i == 0)
    def _():
        acc_b[...] = jnp.zeros(acc_b.shape, F32)

    @pl.when(jnp.logical_and(i % 2 == 0, i < n_tiles))
    def _():
        produce(acc_a, acc_b)
        consume(acc_b)

    @pl.when(jnp.logical_and(i % 2 == 1, i < n_tiles))
    def _():
        produce(acc_b, acc_a)
        consume(acc_a)

    @pl.when(i == n_tiles)
    def _():
        consume(last_prev)


def _proj_kernel(*refs, tm, tiles_per_batch, n_extra, epilogue):
    h_ref, w_ref = refs[0], refs[1]
    extra = refs[2:2 + n_extra]
    o_ref, acc_ref = refs[2 + n_extra:]

    @pl.when(pl.program_id(1) % tiles_per_batch == 0)
    def _():
        acc_ref[0:HALO, :] = jnp.zeros((HALO, COL_TILE), F32)

    acc_ref[HALO:HALO + tm, :] = _dot(h_ref[...], w_ref[...])
    epilogue(acc_ref, o_ref, *extra, tm=tm)
    acc_ref[0:HALO, :] = acc_ref[tm:tm + HALO, :]


def _proj(hb, w, col_tiles, epilogue, extra, extra_specs, rows_per_batch, name):
    rows = hb.shape[0]
    tm = _row_tile(rows_per_batch, largest=PROJ_ROW_TILE)
    tpb = rows_per_batch // tm
    col_fn, ncol = col_tiles
    return pl.pallas_call(
        functools.partial(_proj_kernel, tm=tm, tiles_per_batch=tpb, n_extra=len(extra), epilogue=epilogue),
        grid=(ncol, rows // tm),
        in_specs=[
            pl.BlockSpec((tm, D_MODEL), lambda j, i: (i, 0)),
            pl.BlockSpec((None, D_MODEL, COL_TILE), lambda j, i: (0, 0, col_fn(j))),
        ] + [spec(tm, tpb) for spec in extra_specs],
        out_specs=pl.BlockSpec((tm, COL_TILE), lambda j, i: (i, j)),
        out_shape=jax.ShapeDtypeStruct((rows, ncol * COL_TILE), BF16),
        scratch_shapes=[pltpu.VMEM((tm + HALO, COL_TILE), F32)],
        compiler_params=_params(("arbitrary", "arbitrary")),
        name=name,
    )(hb, w, *extra)


def _per_col(width=COL_TILE, rows=1):
    return lambda tm, tpb: pl.BlockSpec((rows, width), lambda j, i: (0, j))


def _per_row_in_batch(width):
    return lambda tm, tpb: pl.BlockSpec((tm, width), lambda j, i: (i % tpb, 0))


def _gdn_gates(hb, w_ref, alog_ref, bias_ref):
    acc = _dot(hb, w_ref[...])
    lane = lax.broadcasted_iota(jnp.int32, acc.shape, 1)
    decay = -jnp.exp(alog_ref[...]) * _softplus(acc + bias_ref[...])
    return jnp.where(lane < GDN_HEADS, jax.nn.sigmoid(acc), decay)


def _assemble_kernel(x_ref, meta_ref, w_ref, alog_ref, bias_ref, h_ref, hb_ref, g_ref, gt_ref, *, tm):
    def emit(tile):
        hb = tile.astype(BF16)
        h_ref[...] = tile
        hb_ref[...] = hb
        gates = _gdn_gates(hb, w_ref, alog_ref, bias_ref)
        g_ref[...] = gates
        gt_ref[...] = gates.T

    first = pl.program_id(1) == 0

    @pl.when(first)
    def _():
        head = PAD_FRONT + N_META
        emit(jnp.concatenate([jnp.zeros((PAD_FRONT, D_MODEL), F32), meta_ref[...], x_ref[0:tm - head, :]], axis=0))

    @pl.when(jnp.logical_not(first))
    def _():
        emit(x_ref[...])


def _assemble(x, meta_tokens, w_gate, alog, bias, rows_per_batch):
    batch, seq, _ = x.shape
    rows = batch * rows_per_batch
    tm = _row_tile(rows_per_batch, largest=PROJ_ROW_TILE)
    tpb = rows_per_batch // tm
    head = PAD_FRONT + N_META
    vec = pl.BlockSpec((1, LANES), lambda b, i: (0, 0))
    tile = lambda width: pl.BlockSpec((tm, width), lambda b, i: (b * tpb + i, 0))
    return pl.pallas_call(
        functools.partial(_assemble_kernel, tm=tm),
        grid=(batch, tpb),
        in_specs=[
            pl.BlockSpec((pl.Element(tm), pl.Element(D_MODEL)),
                         lambda b, i: (pl.multiple_of(b * seq + jnp.maximum(i * tm - head, 0), LANES), 0)),
            pl.BlockSpec((N_META, D_MODEL), lambda b, i: (0, 0)),
            pl.BlockSpec((D_MODEL, LANES), lambda b, i: (0, 0)),
            vec, vec,
        ],
        out_specs=[tile(D_MODEL), tile(D_MODEL), tile(LANES), pl.BlockSpec((LANES, tm), lambda b, i: (0, b * tpb + i))],
        out_shape=[jax.ShapeDtypeStruct((rows, D_MODEL), F32), jax.ShapeDtypeStruct((rows, D_MODEL), BF16),
                   jax.ShapeDtypeStruct((rows, LANES), F32), jax.ShapeDtypeStruct((LANES, rows), F32)],
        compiler_params=_params(("arbitrary", "arbitrary")),
        name="assemble",
    )(x.reshape(batch * seq, D_MODEL), meta_tokens, w_gate, alog, bias)


def _ssd_gates(hb, live, w_ref, alog_ref, bias_ref):
    acc = _dot(hb, w_ref[...])
    lane = lax.broadcasted_iota(jnp.int32, acc.shape, 1)
    dt = jnp.where(live, _softplus(acc + bias_ref[...]), 0.0)
    return jnp.where(lane < SSD_HEADS, dt, dt * -jnp.exp(alog_ref[...]))


def _retention_kernel(q_ref, k_ref, v_ref, g_ref, gnw_ref, o_ref, s_ref, *, batch, subchunks):
    n = CHUNK

    @pl.when(pl.program_id(0) == 0)
    def _():
        s_ref[...] = jnp.zeros(s_ref.shape, F32)

    t_idx = lax.broadcasted_iota(jnp.int32, (n, n), 0)
    s_idx = lax.broadcasted_iota(jnp.int32, (n, n), 1)
    t_col = lax.broadcasted_iota(jnp.int32, (n, 1), 0).astype(F32)
    gap = (t_idx - s_idx).astype(F32)
    log_gamma = [jnp.log1p(-jnp.exp2(jnp.full((1, 1), -5.0 - hd, F32))) for hd in range(RET_HEADS)]
    decay = [jnp.exp(jnp.where(t_idx >= s_idx, gap * lg, -jnp.inf)) for lg in log_gamma]
    e_in = [jnp.exp((t_col + 1.0) * lg) for lg in log_gamma]
    e_out = [jnp.exp((n - 1.0 - t_col) * lg) for lg in log_gamma]
    e_all = [jnp.exp(n * lg) for lg in log_gamma]
    rows = lambda sub: slice(n * sub, n * (sub + 1))
    chains = [(b, sub, hd) for b in range(batch) for sub in range(subchunks) for hd in range(RET_HEADS)]
    cs = range(len(chains))
    q = [q_ref[b, rows(sub), RET_DK * hd:RET_DK * (hd + 1)] for b, sub, hd in chains]
    k = [k_ref[b, rows(sub), RET_DK * hd:RET_DK * (hd + 1)] for b, sub, hd in chains]
    v = [v_ref[b, rows(sub), RET_DV * hd:RET_DV * (hd + 1)] for b, sub, hd in chains]
    att = [(_dot_nt(q[c], k[c]) * decay[chains[c][2]]).astype(BF16) for c in cs]
    intra = [_dot(att[c], v[c]) for c in cs]
    q_in = [(q[c].astype(F32) * e_in[chains[c][2]]).astype(BF16) for c in cs]
    k_out = [(k[c].astype(F32) * e_out[chains[c][2]]).T.astype(BF16) for c in cs]

    lanes = [(b, hd) for b in range(batch) for hd in range(RET_HEADS)]
    states = [s_ref[i] for i in range(len(lanes))]
    for sub in range(subchunks):
        idx = [chains.index((b, sub, hd)) for b, hd in lanes]
        o = [intra[c] + _dot(q_in[c], states[i].astype(BF16)) for i, c in enumerate(idx)]
        states = [states[i] * e_all[hd] + _dot(k_out[c], v[c]) for (i, c), (b, hd) in zip(enumerate(idx), lanes)]
        for i, (b, hd) in enumerate(lanes):
            cols = slice(RET_DV * hd, RET_DV * (hd + 1))
            oc = o[i] - jnp.mean(o[i], axis=-1, keepdims=True)
            on = oc * lax.rsqrt(jnp.mean(oc * oc, axis=-1, keepdims=True) + GN_EPS)
            gate = _silu(g_ref[b, rows(sub), cols].astype(F32))
            o_ref[b, rows(sub), cols] = (on * gnw_ref[:, cols] * gate).astype(BF16)
    for i in range(len(lanes)):
        s_ref[i] = states[i]


def _retention(qk, vg, gn_w, batch, rows_per_batch):
    n = CHUNK
    nc = rows_per_batch // n
    subchunks = next(s for s in (5, 4, 2, 1) if nc % s == 0)
    step_rows = n * subchunks
    qk = qk.reshape(batch, rows_per_batch, qk.shape[1])
    vg = vg.reshape(batch, rows_per_batch, vg.shape[1])
    out = pl.pallas_call(
        functools.partial(_retention_kernel, batch=batch, subchunks=subchunks),
        grid=(nc // subchunks,),
        in_specs=[
            pl.BlockSpec((batch, step_rows, 512), lambda c: (0, c, 0)),
            pl.BlockSpec((batch, step_rows, 512), lambda c: (0, c, 1)),
            pl.BlockSpec((batch, step_rows, 1024), lambda c: (0, c, 0)),
            pl.BlockSpec((batch, step_rows, 1024), lambda c: (0, c, 1)),
            pl.BlockSpec((1, 1024), lambda c: (0, 0)),
        ],
        out_specs=pl.BlockSpec((batch, step_rows, 1024), lambda c: (0, c, 0)),
        out_shape=jax.ShapeDtypeStruct((batch, rows_per_batch, 1024), BF16),
        scratch_shapes=[pltpu.VMEM((batch * RET_HEADS, RET_DK, RET_DV), F32)],
        compiler_params=_params(("arbitrary",)),
        name="retention",
    )(qk, qk, vg, vg, gn_w)
    return out.reshape(batch * rows_per_batch, 1024)


def _gdn_kernel(*refs, batch, subchunks):
    n = GDN_CHUNK
    q_ref, k_ref, v_ref, g_ref, gcol_ref = refs[:5]
    grow_refs = refs[5:5 + batch]
    nw_ref, o_ref, s_ref = refs[5 + batch:]

    @pl.when(pl.program_id(0) == 0)
    def _():
        s_ref[...] = jnp.zeros(s_ref.shape, F32)

    t_idx = lax.broadcasted_iota(jnp.int32, (n, n), 0)
    s_idx = lax.broadcasted_iota(jnp.int32, (n, n), 1)
    lower = (t_idx >= s_idx).astype(F32)
    upper = (t_idx <= s_idx).astype(F32)
    eye = (t_idx == s_idx).astype(F32)
    rows = lambda sub: slice(n * sub, n * (sub + 1))
    same_block = lambda size: (t_idx // size) == (s_idx // size)

    def state_free(chains):
        cs = range(len(chains))
        blocks = sorted({(b, sub) for b, sub, _ in chains})
        gcol = {bs: gcol_ref[bs[0], rows(bs[1])] for bs in blocks}
        gcum_col = {bs: _dot_hi(lower, gcol[bs]) for bs in blocks}
        gcum_row = {bs: _dot_hi(grow_refs[bs[0]][:, rows(bs[1])], upper) for bs in blocks}
        q = [q_ref[b, rows(sub), GDN_DK * hd:GDN_DK * (hd + 1)].astype(F32) for b, sub, hd in chains]
        k = [k_ref[b, rows(sub), GDN_DK * hd:GDN_DK * (hd + 1)].astype(F32) for b, sub, hd in chains]
        v = [v_ref[b, rows(sub), GDN_DV * hd:GDN_DV * (hd + 1)].astype(F32) for b, sub, hd in chains]
        beta = [gcol[b, sub][:, hd:hd + 1] for b, sub, hd in chains]
        gc = [gcum_col[b, sub][:, GDN_HEADS + hd:GDN_HEADS + hd + 1] for b, sub, hd in chains]
        gr = [gcum_row[b, sub][GDN_HEADS + hd:GDN_HEADS + hd + 1, :] for b, sub, hd in chains]
        g_last = [gc[c][n - 1:n, :] for c in cs]
        seg = [gc[c] - gr[c] for c in cs]
        dec_strict = [jnp.exp(jnp.where(t_idx > s_idx, seg[c], -jnp.inf)) for c in cs]
        dec_incl = [jnp.exp(jnp.where(t_idx >= s_idx, seg[c], -jnp.inf)) for c in cs]
        kb = [k[c] * beta[c] for c in cs]
        kq = [_dot_nt(jnp.concatenate([kb[c], q[c]], axis=0).astype(BF16), k[c].astype(BF16)) for c in cs]
        a = [kq[c][:n] * dec_strict[c] for c in cs]
        att = [(kq[c][n:] * dec_incl[c]).astype(BF16) for c in cs]
        diag = [jnp.where(same_block(GDN_BASE), a[c], 0.0) for c in cs]
        inv = [eye - diag[c] for c in cs]
        p = [_dot16(diag[c], diag[c]) for c in cs]
        span = 4
        while span < GDN_BASE:
            r = [_dot16(jnp.concatenate([p[c], inv[c]], axis=0), p[c]) for c in cs]
            inv = [inv[c] + r[c][n:] for c in cs]
            p = [r[c][:n] for c in cs]
            span *= 2
        inv = [inv[c] + _dot16(inv[c], p[c]) for c in cs]
        size = GDN_BASE
        while size < n:
            joins = jnp.logical_and(same_block(2 * size), jnp.logical_not(same_block(size)))
            off = [jnp.where(joins, a[c], 0.0) for c in cs]
            half = [_dot16(inv[c], off[c]) for c in cs]
            inv = [inv[c] - _dot16(half[c], inv[c]) for c in cs]
            size *= 2
        e_gc = [jnp.exp(gc[c]) for c in cs]
        rhs = [jnp.concatenate([v[c] * beta[c], kb[c] * e_gc[c]], axis=-1) for c in cs]
        u = [_dot16(inv[c], rhs[c]) for c in cs]
        lhs = [jnp.concatenate([u[c][:, GDN_DV:], q[c] * e_gc[c]], axis=0).astype(BF16) for c in cs]
        k_out = [(k[c] * jnp.exp(g_last[c] - gc[c])).T.astype(BF16) for c in cs]
        e_all = [jnp.exp(g_last[c]) for c in cs]
        return list(zip(att, u, lhs, k_out, e_all))

    lanes = [(b, hd) for b in range(batch) for hd in range(GDN_HEADS)]
    chains = [(b, sub, hd) for b in range(batch) for sub in range(subchunks) for hd in range(GDN_HEADS)]
    prepared = dict(zip(chains, state_free(chains)))
    states = [s_ref[i] for i in range(len(lanes))]
    for sub in range(subchunks):
        att, u, lhs, k_out, e_all = zip(*[prepared[b, sub, hd] for b, hd in lanes])
        ls = range(len(lanes))
        ws = [_dot(lhs[i], states[i].astype(BF16)) for i in ls]
        v_new = [(u[i][:, :GDN_DV] - ws[i][:n]).astype(BF16) for i in ls]
        o = [ws[i][n:] + _dot(att[i], v_new[i]) for i in ls]
        states = [states[i] * e_all[i] + _dot(k_out[i], v_new[i]) for i in ls]
        for i, (b, hd) in enumerate(lanes):
            cols = slice(GDN_DV * hd, GDN_DV * (hd + 1))
            on = o[i] * lax.rsqrt(jnp.mean(o[i] * o[i], axis=-1, keepdims=True) + RMS_EPS)
            gate = _silu(g_ref[b, rows(sub), cols].astype(F32))
            o_ref[b, rows(sub), cols] = (on * nw_ref[...] * gate).astype(BF16)
    for i in range(len(lanes)):
        s_ref[i] = states[i]


def _gdn(qk, v, gate, gates, gates_t, norm_w, batch, rows_per_batch):
    n = GDN_CHUNK
    subchunks = next(s for s in (10, 2) if (rows_per_batch // n) % s == 0 and (n * s) % LANES == 0)
    step_rows = n * subchunks
    steps = rows_per_batch // step_rows
    qk = qk.reshape(batch, rows_per_batch, qk.shape[1])
    v = v.reshape(batch, rows_per_batch, v.shape[1])
    gate = gate.reshape(batch, rows_per_batch, gate.shape[1])
    gates = gates.reshape(batch, rows_per_batch, LANES)
    out = pl.pallas_call(
        functools.partial(_gdn_kernel, batch=batch, subchunks=subchunks),
        grid=(steps,),
        in_specs=[
            pl.BlockSpec((batch, step_rows, 512), lambda c: (0, c, 0)),
            pl.BlockSpec((batch, step_rows, 512), lambda c: (0, c, 1)),
            pl.BlockSpec((batch, step_rows, 1024), lambda c: (0, c, 0)),
            pl.BlockSpec((batch, step_rows, 1024), lambda c: (0, c, 0)),
            pl.BlockSpec((batch, step_rows, LANES), lambda c: (0, c, 0)),
        ] + [pl.BlockSpec((8, step_rows), lambda c, b=b: (0, b * steps + c)) for b in range(batch)] + [
            pl.BlockSpec((1, GDN_DV), lambda c: (0, 0)),
        ],
        out_specs=pl.BlockSpec((batch, step_rows, 1024), lambda c: (0, c, 0)),
        out_shape=jax.ShapeDtypeStruct((batch, rows_per_batch, 1024), BF16),
        scratch_shapes=[pltpu.VMEM((batch * GDN_HEADS, GDN_DK, GDN_DV), F32)],
        compiler_params=_params(("arbitrary",)),
        name="gated_delta",
    )(qk, qk, v, gate, gates, *([gates_t] * batch), norm_w)
    return out.reshape(batch * rows_per_batch, 1024)


def _layer_norm(x, w, b):
    xc = x - jnp.mean(x, axis=-1, keepdims=True)
    return xc * lax.rsqrt(jnp.mean(xc * xc, axis=-1, keepdims=True) + LN_EPS) * w + b


def _outproj_ln_kernel(ya_ref, yb_ref, w_ref, h_ref, lw_ref, lb_ref, o_ref, acc_a, acc_b, wb_ref, *, n_tiles):
    half = ya_ref.shape[1]

    @pl.when(pl.program_id(0) == 0)
    def _():
        wb_ref[...] = w_ref[...].astype(BF16)

    def produce(cur, prev):
        acc = _dot(ya_ref[...], wb_ref[0:half, :]) + _dot(yb_ref[...], wb_ref[half:2 * half, :])
        cur[...] = DN_ALPHA * h_ref[...] + acc

    def consume(prev):
        o_ref[...] = _layer_norm(prev[...], lw_ref[...], lb_ref[...])

    _lagged_steps(pl.program_id(0), n_tiles, acc_a, acc_b, produce, consume)


def _outproj_ln(ya, yb, col_a, col_b, w, h, lw, lb, rows_per_batch):
    rows = h.shape[0]
    tm = _row_tile(rows_per_batch)
    n_tiles = rows // tm
    half = w.shape[1] // 2
    cur = lambda i: jnp.minimum(i, n_tiles - 1)
    prev = lambda i: jnp.maximum(i - 1, 0)
    vec = pl.BlockSpec((1, D_MODEL), lambda i: (0, 0))
    return pl.pallas_call(
        functools.partial(_outproj_ln_kernel, n_tiles=n_tiles),
        grid=(n_tiles + 1,),
        in_specs=[
            pl.BlockSpec((tm, half), lambda i: (cur(i), col_a)),
            pl.BlockSpec((tm, half), lambda i: (cur(i), col_b)),
            pl.BlockSpec((None, 2 * half, D_MODEL), lambda i: (0, 0, 0), pipeline_mode=pl.Buffered(1)),
            pl.BlockSpec((tm, D_MODEL), lambda i: (cur(i), 0)),
            vec, vec,
        ],
        out_specs=pl.BlockSpec((tm, D_MODEL), lambda i: (prev(i), 0)),
        out_shape=jax.ShapeDtypeStruct((rows, D_MODEL), F32),
        scratch_shapes=[pltpu.VMEM((tm, D_MODEL), F32), pltpu.VMEM((tm, D_MODEL), F32),
                        pltpu.VMEM((2 * half, D_MODEL), BF16)],
        compiler_params=_params(("arbitrary",)),
        name="outproj_ln",
    )(ya, yb, w, h, lw, lb)


def _mlp(h_ref, w1_ref, w2_ref, lw_ref, lb_ref):
    h = h_ref[...]
    hb = h.astype(BF16)
    acc = jnp.zeros(h.shape, F32)
    step = 1024
    for f in range(0, D_FF, step):
        a = _dot(hb, w1_ref[:, f:f + step])
        a = jnp.square(jnp.maximum(a, 0.0)).astype(BF16)
        acc = acc + _dot(a, w2_ref[f:f + step, :])
    return _layer_norm(DN_ALPHA * h + acc, lw_ref[...], lb_ref[...])


def _mlp_ln_kernel(h_ref, w1_ref, w2_ref, lw_ref, lb_ref, wg_ref, alog_ref, bias_ref, o_ref, ob_ref, g_ref, gt_ref,
                   *, tm, tiles_per_batch):
    out = _mlp(h_ref, w1_ref, w2_ref, lw_ref, lb_ref)
    o_ref[...] = out
    row = (pl.program_id(0) % tiles_per_batch) * tm + lax.broadcasted_iota(jnp.int32, (tm, 1), 0)
    live = row >= PAD_FRONT
    hb = jnp.where(live, out, 0.0).astype(BF16)
    ob_ref[...] = hb
    gates = _ssd_gates(hb, live, wg_ref, alog_ref, bias_ref)
    g_ref[...] = gates
    gt_ref[...] = gates.T


def _mlp_ln_final_kernel(h_ref, w1_ref, w2_ref, lw_ref, lb_ref, o_ref):
    o_ref[...] = _mlp(h_ref, w1_ref, w2_ref, lw_ref, lb_ref)


def _mlp_weight_specs(index):
    return [
        pl.BlockSpec((D_MODEL, D_FF), index, pipeline_mode=pl.Buffered(1)),
        pl.BlockSpec((D_FF, D_MODEL), index, pipeline_mode=pl.Buffered(1)),
        pl.BlockSpec((1, D_MODEL), index),
        pl.BlockSpec((1, D_MODEL), index),
    ]


def _mlp_ln(h, w1, w2, lw, lb, w_gate, alog, bias, rows_per_batch):
    rows = h.shape[0]
    tm = _row_tile(rows_per_batch)
    tile = pl.BlockSpec((tm, D_MODEL), lambda i: (i, 0))
    vec = pl.BlockSpec((1, LANES), lambda i: (0, 0))
    return pl.pallas_call(
        functools.partial(_mlp_ln_kernel, tm=tm, tiles_per_batch=rows_per_batch // tm),
        grid=(rows // tm,),
        in_specs=[tile] + _mlp_weight_specs(lambda i: (0, 0)) + [pl.BlockSpec((D_MODEL, LANES), lambda i: (0, 0)),
                                                                   vec, vec],
        out_specs=[tile, tile, pl.BlockSpec((tm, LANES), lambda i: (i, 0)), pl.BlockSpec((LANES, tm), lambda i: (0, i))],
        out_shape=[jax.ShapeDtypeStruct((rows, D_MODEL), F32), jax.ShapeDtypeStruct((rows, D_MODEL), BF16),
                   jax.ShapeDtypeStruct((rows, LANES), F32), jax.ShapeDtypeStruct((LANES, rows), F32)],
        compiler_params=_params(("arbitrary",)),
        name="mlp_ln",
    )(h, w1, w2, lw, lb, w_gate, alog, bias)


def _mlp_ln_final(h, w1, w2, lw, lb, batch, seq, rows_per_batch):
    tm = next(t for t in (1024, 512, 256, 128) if seq % t == 0)
    tiles = seq // tm
    first_row = PAD_FRONT + N_META
    return pl.pallas_call(
        _mlp_ln_final_kernel,
        grid=(batch, tiles),
        in_specs=[pl.BlockSpec((pl.Element(tm), pl.Element(D_MODEL)),
                               lambda b, i: (pl.multiple_of(b * rows_per_batch + first_row + i * tm, 128), 0))]
        + _mlp_weight_specs(lambda b, i: (0, 0)),
        out_specs=pl.BlockSpec((tm, D_MODEL), lambda b, i: (b * tiles + i, 0)),
        out_shape=jax.ShapeDtypeStruct((batch * seq, D_MODEL), F32),
        compiler_params=_params(("arbitrary", "arbitrary")),
        name="mlp_ln_final",
    )(h, w1, w2, lw, lb)


def _ssd_kernel(*refs, batch, subchunks):
    n = CHUNK
    z_ref, x_ref, b_ref, c_ref, gcol_ref = refs[:5]
    grow_refs = refs[5:5 + batch]
    dskip_ref, nw_ref, o_ref, s_ref = refs[5 + batch:]
    pair_w = 2 * SSD_HEADDIM
    pairs_per_group = SSD_HPG // 2
    group_w = SSD_HPG * SSD_HEADDIM
    n_pairs = SSD_HEADS // 2

    @pl.when(pl.program_id(0) == 0)
    def _():
        s_ref[...] = jnp.zeros(s_ref.shape, F32)

    t_idx = lax.broadcasted_iota(jnp.int32, (n, n), 0)
    s_idx = lax.broadcasted_iota(jnp.int32, (n, n), 1)
    causal = t_idx >= s_idx
    lower = causal.astype(F32)
    upper = (t_idx <= s_idx).astype(F32)
    lane_v = lax.broadcasted_iota(jnp.int32, (n, pair_w), 1)
    lane_s = lax.broadcasted_iota(jnp.int32, (SSD_DSTATE, pair_w), 1)
    lane_1 = lax.broadcasted_iota(jnp.int32, (1, pair_w), 1)
    bs = range(batch)
    for sub in range(subchunks):
        rows = slice(n * sub, n * (sub + 1))
        grow = [grow_refs[b][:, rows] for b in bs]
        gcum_col = [_dot_hi(lower, gcol_ref[b, rows]) for b in bs]
        gcum_row = [_dot_hi(grow[b], upper) for b in bs]
        for g in range(SSD_GROUPS):
            q = [c_ref[b, rows, SSD_DSTATE * g:SSD_DSTATE * (g + 1)] for b in bs]
            k = [b_ref[b, rows, SSD_DSTATE * g:SSD_DSTATE * (g + 1)] for b in bs]
            qf = [q[b].astype(F32) for b in bs]
            kt = [k[b].astype(F32).T for b in bs]
            cb = [_dot_nt(q[b], k[b]) for b in bs]
            ys = [[] for _ in bs]
            for p in range(pairs_per_group):
                pair = g * pairs_per_group + p
                col = group_w * g + pair_w * p
                for b in bs:
                    xv = x_ref[b, rows, col:col + pair_w]
                    state = s_ref[b * n_pairs + pair]
                    o = jnp.zeros((n, pair_w), F32)
                    upd = jnp.zeros((SSD_DSTATE, pair_w), F32)
                    last = []
                    for side in range(2):
                        hd = 2 * pair + side
                        gc = gcum_col[b][:, SSD_HEADS + hd:SSD_HEADS + hd + 1]
                        gr = gcum_row[b][SSD_HEADS + hd:SSD_HEADS + hd + 1, :]
                        dt_row = grow[b][hd:hd + 1, :]
                        g_last = gr[:, n - 1:n]
                        dec = jnp.exp(jnp.where(causal, gc - gr, -jnp.inf))
                        att = cb[b] * dec * dt_row
                        q_in = qf[b] * jnp.exp(gc)
                        mine_v = (lane_v >= SSD_HEADDIM) if side else (lane_v < SSD_HEADDIM)
                        mine_s = (lane_s >= SSD_HEADDIM) if side else (lane_s < SSD_HEADDIM)
                        xm = jnp.where(mine_v, xv, jnp.zeros_like(xv))
                        sm = jnp.where(mine_s, state, 0.0).astype(BF16)
                        o = o + _dot(att.astype(BF16), xm) + _dot(q_in.astype(BF16), sm)
                        k_out = kt[b] * (jnp.exp(g_last - gr) * dt_row)
                        upd = upd + _dot(k_out.astype(BF16), xm)
                        last.append(jnp.exp(g_last))
                    s_ref[b * n_pairs + pair] = state * jnp.where(lane_1 < SSD_HEADDIM, last[0], last[1]) + upd
                    ys[b].append(o + xv.astype(F32) * dskip_ref[:, col:col + pair_w])
            cols = slice(group_w * g, group_w * (g + 1))
            for b in bs:
                y = jnp.concatenate(ys[b], axis=-1)
                y = y * _silu(z_ref[b, rows, cols].astype(F32))
                y = y * lax.rsqrt(jnp.mean(y * y, axis=-1, keepdims=True) + RMS_EPS)
                o_ref[b, rows, cols] = (y * nw_ref[:, cols]).astype(BF16)


def _ssd(z, xbc, gates, gates_t, dskip, norm_w, batch, rows_per_batch):
    n = CHUNK
    subchunks = 1
    step_rows = n * subchunks
    steps = rows_per_batch // step_rows
    z = z.reshape(batch, rows_per_batch, z.shape[1])
    xbc = xbc.reshape(batch, rows_per_batch, xbc.shape[1])
    gates = gates.reshape(batch, rows_per_batch, LANES)
    out = pl.pallas_call(
        functools.partial(_ssd_kernel, batch=batch, subchunks=subchunks),
        grid=(steps,),
        in_specs=[
            pl.BlockSpec((batch, step_rows, SSD_DINNER), lambda c: (0, c, 0)),
            pl.BlockSpec((batch, step_rows, SSD_DINNER), lambda c: (0, c, 0)),
            pl.BlockSpec((batch, step_rows, SSD_GN), lambda c: (0, c, 4)),
            pl.BlockSpec((batch, step_rows, SSD_GN), lambda c: (0, c, 5)),
            pl.BlockSpec((batch, step_rows, LANES), lambda c: (0, c, 0)),
        ] + [pl.BlockSpec((2 * SSD_HEADS, step_rows), lambda c, b=b: (0, b * steps + c)) for b in range(batch)] + [
            pl.BlockSpec((1, SSD_DINNER), lambda c: (0, 0)),
            pl.BlockSpec((1, SSD_DINNER), lambda c: (0, 0)),
        ],
        out_specs=pl.BlockSpec((batch, step_rows, SSD_DINNER), lambda c: (0, c, 0)),
        out_shape=jax.ShapeDtypeStruct((batch, rows_per_batch, SSD_DINNER), BF16),
        scratch_shapes=[pltpu.VMEM((batch * SSD_HEADS // 2, SSD_DSTATE, 2 * SSD_HEADDIM), F32)],
        compiler_params=_params(("arbitrary",)),
        name="ssd",
    )(z, xbc, xbc, xbc, gates, *([gates_t] * batch), dskip, norm_w)
    return out.reshape(batch * rows_per_batch, SSD_DINNER)


def _pad_lanes(v):
    v = v.reshape(1, -1).astype(F32)
    return jnp.pad(v, ((0, 0), (0, LANES - v.shape[1])))


def kernel(x, meta_tokens, ab_w_in, ab_ret_gn_w, ab_conv_q, ab_conv_k, ab_conv_v, ab_A_log, ab_dt_bias, ab_gdn_norm_w, ab_w_out, c_w_in, c_conv_w, c_conv_b, c_A_log, c_dt_bias, c_D, c_norm_w, c_w_out, mlp_w1, mlp_w2, ln1_w, ln1_b, ln2_w, ln2_b):
    batch, seq, d = x.shape
    assert d == D_MODEL and meta_tokens.shape == (N_META, D_MODEL)
    lp = PAD_FRONT + N_META + seq

    inv_freq = 1.0 / (ROPE_BASE ** jnp.linspace(0.0, 1.0, RET_DK // 2, dtype=F32))
    coarse = (jnp.arange(lp // CHUNK, dtype=F32) * CHUNK - PAD_FRONT)[:, None, None] * inv_freq
    fine = jnp.arange(CHUNK, dtype=F32)[None, :, None] * inv_freq
    cos = (jnp.cos(coarse) * jnp.cos(fine) - jnp.sin(coarse) * jnp.sin(fine)).reshape(lp, RET_DK // 2)
    sin = (jnp.sin(coarse) * jnp.cos(fine) + jnp.cos(coarse) * jnp.sin(fine)).reshape(lp, RET_DK // 2)
    cosf = jnp.concatenate([cos, cos], axis=-1)
    sinf = jnp.concatenate([-sin, sin], axis=-1)

    w_in = ab_w_in[:1].astype(BF16)
    w_gg = w_in[:, :, 5128:]
    w_gate = jnp.pad(w_in[0, :, 5120:5128], ((0, 0), (0, LANES - 8)))
    alog = _pad_lanes(jnp.concatenate([jnp.zeros((GDN_HEADS,), F32), ab_A_log[0]]))
    bias = _pad_lanes(jnp.concatenate([jnp.zeros((GDN_HEADS,), F32), ab_dt_bias[0]]))
    h, hb, g0, g0_t = _assemble(x, meta_tokens, w_gate, alog, bias, lp)
    ones = jnp.ones((1, COL_TILE // 2), F32)
    zeros = jnp.zeros((1, COL_TILE), F32)
    p_qk = _proj(hb, w_in, (lambda j: j, 1), _epilogue_rope,
                 [cosf, sinf, jnp.concatenate([ones * RET_DK ** -0.5, ones], axis=1)],
                 [_per_row_in_batch(LANES), _per_row_in_batch(LANES), _per_col()], lp, "proj_ret_qk")
    p_vg = _proj(hb, w_in, (lambda j: j + 1, 2), _epilogue_plain, [], [], lp, "proj_ret_vg")
    p_gg = _proj(hb, w_gg, (lambda j: j, 1), _epilogue_plain, [], [], lp, "proj_gdn_g")
    g_qk = _proj(hb, w_in, (lambda j: j + 3, 1), functools.partial(_epilogue_conv, l2norm=True),
                 [jnp.concatenate([ab_conv_q[0], ab_conv_k[0]], axis=1), zeros,
                  jnp.concatenate([ones * GDN_DK ** -0.5, ones], axis=1)],
                 [_per_col(rows=CONV_K), _per_col(), _per_col()], lp, "proj_gdn_qk")
    g_v = _proj(hb, w_in, (lambda j: j + 4, 1), functools.partial(_epilogue_conv, l2norm=False),
                [ab_conv_v[0], zeros, zeros], [_per_col(rows=CONV_K), _per_col(), _per_col()], lp, "proj_gdn_v")
    y_ret = _retention(p_qk, p_vg, ab_ret_gn_w[0].reshape(1, -1), batch, lp)
    y_gdn = _gdn(g_qk, g_v, p_gg, g0, g0_t, ab_gdn_norm_w[0].reshape(1, -1), batch, lp)
    h = _outproj_ln(y_ret, y_gdn, 0, 0, ab_w_out[:1], h,
                    ln1_w[0].reshape(1, -1), ln1_b[0].reshape(1, -1), lp)
    w_in = c_w_in[:1].astype(BF16)
    w_dt = w_in[0, :, 5120:]
    w_gate = jnp.pad(jnp.concatenate([w_dt, w_dt], axis=1), ((0, 0), (0, LANES - 2 * SSD_HEADS)))
    alog = _pad_lanes(jnp.concatenate([jnp.zeros((SSD_HEADS,), F32), c_A_log[0]]))
    bias = _pad_lanes(jnp.concatenate([c_dt_bias[0], c_dt_bias[0]]))
    h, hb, g1, g1_t = _mlp_ln(h, mlp_w1[0].astype(BF16), mlp_w2[0].astype(BF16),
                              ln2_w[0].reshape(1, -1), ln2_b[0].reshape(1, -1), w_gate, alog, bias, lp)

    p_z = _proj(hb, w_in, (lambda j: j, 2), _epilogue_plain, [], [], lp, "proj_ssd_z")
    p_xbc = _proj(hb, w_in, (lambda j: j + 2, 3), functools.partial(_epilogue_conv, l2norm=False),
                  [c_conv_w[0], c_conv_b[0].reshape(1, -1), jnp.zeros((1, 3 * COL_TILE), F32)],
                  [_per_col(rows=CONV_K), _per_col(), _per_col()], lp, "proj_ssd_xbc")
    dskip = jnp.repeat(c_D[0].astype(F32), SSD_HEADDIM).reshape(1, -1)
    y_ssd = _ssd(p_z, p_xbc, g1, g1_t, dskip, c_norm_w[0].reshape(1, -1), batch, lp)
    h = _outproj_ln(y_ssd, y_ssd, 0, 1, c_w_out[:1], h,
                    ln1_w[1].reshape(1, -1), ln1_b[1].reshape(1, -1), lp)
    out = _mlp_ln_final(h, mlp_w1[1].astype(BF16), mlp_w2[1].astype(BF16),
                        ln2_w[1].reshape(1, -1), ln2_b[1].reshape(1, -1), batch, seq, lp)
    return out.reshape(batch, seq, d)
```

```python
import functools

import jax
import jax.numpy as jnp
from jax import lax
from jax.experimental import pallas as pl
from jax.experimental.pallas import tpu as pltpu

F32 = jnp.float32
BF16 = jnp.bfloat16
HI = lax.Precision.HIGHEST

D_MODEL = 1024
DEPTH = 2
N_META = 16
CONV_K = 4
RET_HEADS = 4
RET_DK = 128
RET_DV = 256
ROPE_BASE = 10000.0
GDN_HEADS = 4
GDN_DK = 128
GDN_DV = 256
SSD_DINNER = 2048
SSD_HEADDIM = 64
SSD_HEADS = 32
SSD_GROUPS = 4
SSD_HPG = 8
SSD_DSTATE = 128
SSD_GN = 512
D_FF = 4096
DN_ALPHA = (2 * DEPTH) ** 0.25
LN_EPS = 1e-5
GN_EPS = 1e-5
RMS_EPS = 1e-6

LANES = 128
COL_TILE = 1024
CHUNK = 128
GDN_CHUNK = 64
GDN_BASE = 8
PAD_FRONT = CHUNK - N_META
HALO = 8
CONV_ROWS = 64
PROJ_ROW_TILE = 1664
VMEM_LIMIT = 56 * 1024 * 1024


def _row_tile(rows_per_batch, largest=640):
    for tm in (1664, 640, 512, 256, 128):
        if tm <= largest and rows_per_batch % tm == 0:
            return tm
    raise ValueError(f"unsupported padded sequence length {rows_per_batch}")


def _params(sem):
    return pltpu.CompilerParams(dimension_semantics=sem, vmem_limit_bytes=VMEM_LIMIT)


def _softplus(x):
    return jnp.maximum(x, 0.0) + jnp.log1p(jnp.exp(-jnp.abs(x)))


def _silu(x):
    return x * jax.nn.sigmoid(x)


def _dot(a, b):
    return jnp.dot(a, b, preferred_element_type=F32)


def _dot_nt(a, b):
    return lax.dot_general(a, b, (((1,), (1,)), ((), ())), preferred_element_type=F32)


def _dot_hi(a, b):
    return jnp.dot(a, b, preferred_element_type=F32, precision=HI)


def _dot16(a, b):
    return _dot(a.astype(BF16), b.astype(BF16))


def _epilogue_plain(acc_ref, o_ref, *, tm):
    o_ref[...] = acc_ref[HALO:HALO + tm, :].astype(BF16)


def _epilogue_conv(acc_ref, o_ref, cw_ref, cb_ref, scale_ref, *, tm, l2norm):
    for r in range(0, tm, CONV_ROWS):
        rows = slice(r, r + CONV_ROWS)
        for c in range(0, COL_TILE, LANES):
            cols = slice(c, c + LANES)
            w = [cw_ref[tap:tap + 1, cols] for tap in range(CONV_K)]
            xw = acc_ref[r:r + CONV_ROWS + HALO, cols]
            s1 = pltpu.roll(xw, 1, 0)
            y = w[3] * xw + w[2] * s1 + pltpu.roll(w[1] * xw + w[0] * s1, 2, 0)
            y = _silu(y[HALO:] + cb_ref[:, cols])
            if l2norm:
                inv = lax.rsqrt(jnp.sum(y * y, axis=-1, keepdims=True) + 1e-6)
                y = y * (inv * scale_ref[:, cols])
            o_ref[rows, cols] = y.astype(BF16)


def _lagged_steps(i, n_tiles, acc_a, acc_b, produce, consume):
    last_prev = acc_b if n_tiles % 2 == 0 else acc_a

    @pl.when(i == 0)
    def _():
        acc_b[...] = jnp.zeros(acc_b.shape, F32)

    @pl.when(jnp.logical_and(i % 2 == 0, i < n_tiles))
    def _():
        produce(acc_a, acc_b)
        consume(acc_b)

    @pl.when(jnp.logical_and(i % 2 == 1, i < n_tiles))
    def _():
        produce(acc_b, acc_a)
        consume(acc_a)

    @pl.when(i == n_tiles)
    def _():
        consume(last_prev)


def _proj_kernel(*refs, tm, tiles_per_batch, n_extra, epilogue):
    h_ref, w_ref = refs[0], refs[1]
    extra = refs[2:2 + n_extra]
    o_ref, acc_ref = refs[2 + n_extra:]

    @pl.when(pl.program_id(1) % tiles_per_batch == 0)
    def _():
        acc_ref[0:HALO, :] = jnp.zeros((HALO, COL_TILE), F32)

    acc_ref[HALO:HALO + tm, :] = _dot(h_ref[...], w_ref[...])
    epilogue(acc_ref, o_ref, *extra, tm=tm)
    acc_ref[0:HALO, :] = acc_ref[tm:tm + HALO, :]


def _proj(hb, w, col_tiles, epilogue, extra, extra_specs, rows_per_batch, name):
    rows = hb.shape[0]
    tm = _row_tile(rows_per_batch, largest=PROJ_ROW_TILE)
    tpb = rows_per_batch // tm
    col_fn, ncol = col_tiles
    return pl.pallas_call(
        functools.partial(_proj_kernel, tm=tm, tiles_per_batch=tpb, n_extra=len(extra), epilogue=epilogue),
        grid=(ncol, rows // tm),
        in_specs=[
            pl.BlockSpec((tm, D_MODEL), lambda j, i: (i, 0)),
            pl.BlockSpec((None, D_MODEL, COL_TILE), lambda j, i: (0, 0, col_fn(j))),
        ] + [spec(tm, tpb) for spec in extra_specs],
        out_specs=pl.BlockSpec((tm, COL_TILE), lambda j, i: (i, j)),
        out_shape=jax.ShapeDtypeStruct((rows, ncol * COL_TILE), BF16),
        scratch_shapes=[pltpu.VMEM((tm + HALO, COL_TILE), F32)],
        compiler_params=_params(("arbitrary", "arbitrary")),
        name=name,
    )(hb, w, *extra)


def _per_col(width=COL_TILE, rows=1):
    return lambda tm, tpb: pl.BlockSpec((rows, width), lambda j, i: (0, j))


def _gdn_gates(hb, w_ref, alog_ref, bias_ref):
    acc = _dot(hb, w_ref[...])
    lane = lax.broadcasted_iota(jnp.int32, acc.shape, 1)
    decay = -jnp.exp(alog_ref[...]) * _softplus(acc + bias_ref[...])
    return jnp.where(lane < GDN_HEADS, jax.nn.sigmoid(acc), decay)


def _assemble_kernel(x_ref, meta_ref, w_ref, alog_ref, bias_ref, wqk_ref, cos_ref, sin_ref, scale_ref,
                     h_ref, hb_ref, g_ref, gt_ref, qk_ref, *, tm):
    def emit(tile):
        hb = tile.astype(BF16)
        h_ref[...] = tile
        hb_ref[...] = hb
        gates = _gdn_gates(hb, w_ref, alog_ref, bias_ref)
        g_ref[...] = gates
        gt_ref[...] = gates.T
        acc = _dot(hb, wqk_ref[...])
        for hd in range(COL_TILE // RET_DK):
            cols = slice(RET_DK * hd, RET_DK * (hd + 1))
            xh = acc[:, cols]
            y = xh * cos_ref[...] + pltpu.roll(xh, RET_DK // 2, 1) * sin_ref[...]
            qk_ref[:, cols] = (y * scale_ref[:, cols]).astype(BF16)

    first = pl.program_id(1) == 0

    @pl.when(first)
    def _():
        head = PAD_FRONT + N_META
        emit(jnp.concatenate([jnp.zeros((PAD_FRONT, D_MODEL), F32), meta_ref[...], x_ref[0:tm - head, :]], axis=0))

    @pl.when(jnp.logical_not(first))
    def _():
        emit(x_ref[...])


def _assemble(x, meta_tokens, w_gate, alog, bias, w_in, cosf, sinf, qk_scale, rows_per_batch):
    batch, seq, _ = x.shape
    rows = batch * rows_per_batch
    tm = _row_tile(rows_per_batch)
    tpb = rows_per_batch // tm
    head = PAD_FRONT + N_META
    vec = lambda width: pl.BlockSpec((1, width), lambda b, i: (0, 0))
    tile = lambda width: pl.BlockSpec((tm, width), lambda b, i: (b * tpb + i, 0))
    table = pl.BlockSpec((tm, LANES), lambda b, i: (i, 0))
    return pl.pallas_call(
        functools.partial(_assemble_kernel, tm=tm),
        grid=(batch, tpb),
        in_specs=[
            pl.BlockSpec((pl.Element(tm), pl.Element(D_MODEL)),
                         lambda b, i: (pl.multiple_of(b * seq + jnp.maximum(i * tm - head, 0), LANES), 0)),
            pl.BlockSpec((N_META, D_MODEL), lambda b, i: (0, 0)),
            pl.BlockSpec((D_MODEL, LANES), lambda b, i: (0, 0)),
            vec(LANES), vec(LANES),
            pl.BlockSpec((None, D_MODEL, COL_TILE), lambda b, i: (0, 0, 0)),
            table, table, vec(COL_TILE),
        ],
        out_specs=[tile(D_MODEL), tile(D_MODEL), tile(LANES), pl.BlockSpec((LANES, tm), lambda b, i: (0, b * tpb + i)),
                   tile(COL_TILE)],
        out_shape=[jax.ShapeDtypeStruct((rows, D_MODEL), F32), jax.ShapeDtypeStruct((rows, D_MODEL), BF16),
                   jax.ShapeDtypeStruct((rows, LANES), F32), jax.ShapeDtypeStruct((LANES, rows), F32),
                   jax.ShapeDtypeStruct((rows, COL_TILE), BF16)],
        compiler_params=_params(("arbitrary", "arbitrary")),
        name="assemble",
    )(x.reshape(batch * seq, D_MODEL), meta_tokens, w_gate, alog, bias, w_in, cosf, sinf, qk_scale)


def _ssd_gates(hb, live, w_ref, alog_ref, bias_ref):
    acc = _dot(hb, w_ref[...])
    lane = lax.broadcasted_iota(jnp.int32, acc.shape, 1)
    dt = jnp.where(live, _softplus(acc + bias_ref[...]), 0.0)
    return jnp.where(lane < SSD_HEADS, dt, dt * -jnp.exp(alog_ref[...]))


def _retention_kernel(q_ref, k_ref, v_ref, g_ref, gnw_ref, o_ref, s_ref, *, batch, subchunks):
    n = CHUNK

    @pl.when(pl.program_id(0) == 0)
    def _():
        s_ref[...] = jnp.zeros(s_ref.shape, F32)

    t_idx = lax.broadcasted_iota(jnp.int32, (n, n), 0)
    s_idx = lax.broadcasted_iota(jnp.int32, (n, n), 1)
    t_col = lax.broadcasted_iota(jnp.int32, (n, 1), 0).astype(F32)
    gap = (t_idx - s_idx).astype(F32)
    log_gamma = [jnp.log1p(-jnp.exp2(jnp.full((1, 1), -5.0 - hd, F32))) for hd in range(RET_HEADS)]
    decay = [jnp.exp(jnp.where(t_idx >= s_idx, gap * lg, -jnp.inf)) for lg in log_gamma]
    e_in = [jnp.exp((t_col + 1.0) * lg) for lg in log_gamma]
    e_out = [jnp.exp((n - 1.0 - t_col) * lg) for lg in log_gamma]
    e_all = [jnp.exp(n * lg) for lg in log_gamma]
    rows = lambda sub: slice(n * sub, n * (sub + 1))
    chains = [(b, sub, hd) for b in range(batch) for sub in range(subchunks) for hd in range(RET_HEADS)]
    cs = range(len(chains))
    q = [q_ref[b, rows(sub), RET_DK * hd:RET_DK * (hd + 1)] for b, sub, hd in chains]
    k = [k_ref[b, rows(sub), RET_DK * hd:RET_DK * (hd + 1)] for b, sub, hd in chains]
    v = [v_ref[b, rows(sub), RET_DV * hd:RET_DV * (hd + 1)] for b, sub, hd in chains]
    att = [(_dot_nt(q[c], k[c]) * decay[chains[c][2]]).astype(BF16) for c in cs]
    intra = [_dot(att[c], v[c]) for c in cs]
    q_in = [(q[c].astype(F32) * e_in[chains[c][2]]).astype(BF16) for c in cs]
    k_out = [(k[c].astype(F32) * e_out[chains[c][2]]).T.astype(BF16) for c in cs]

    lanes = [(b, hd) for b in range(batch) for hd in range(RET_HEADS)]
    states = [s_ref[i] for i in range(len(lanes))]
    for sub in range(subchunks):
        idx = [chains.index((b, sub, hd)) for b, hd in lanes]
        o = [intra[c] + _dot(q_in[c], states[i].astype(BF16)) for i, c in enumerate(idx)]
        states = [states[i] * e_all[hd] + _dot(k_out[c], v[c]) for (i, c), (b, hd) in zip(enumerate(idx), lanes)]
        for i, (b, hd) in enumerate(lanes):
            cols = slice(RET_DV * hd, RET_DV * (hd + 1))
            oc = o[i] - jnp.mean(o[i], axis=-1, keepdims=True)
            on = oc * lax.rsqrt(jnp.mean(oc * oc, axis=-1, keepdims=True) + GN_EPS)
            gate = _silu(g_ref[b, rows(sub), cols].astype(F32))
            o_ref[b, rows(sub), cols] = (on * gnw_ref[:, cols] * gate).astype(BF16)
    for i in range(len(lanes)):
        s_ref[i] = states[i]


def _retention(qk, vg, gn_w, batch, rows_per_batch):
    n = CHUNK
    nc = rows_per_batch // n
    subchunks = next(s for s in (5, 4, 2, 1) if nc % s == 0)
    step_rows = n * subchunks
    qk = qk.reshape(batch, rows_per_batch, qk.shape[1])
    vg = vg.reshape(batch, rows_per_batch, vg.shape[1])
    out = pl.pallas_call(
        functools.partial(_retention_kernel, batch=batch, subchunks=subchunks),
        grid=(nc // subchunks,),
        in_specs=[
            pl.BlockSpec((batch, step_rows, 512), lambda c: (0, c, 0)),
            pl.BlockSpec((batch, step_rows, 512), lambda c: (0, c, 1)),
            pl.BlockSpec((batch, step_rows, 1024), lambda c: (0, c, 0)),
            pl.BlockSpec((batch, step_rows, 1024), lambda c: (0, c, 1)),
            pl.BlockSpec((1, 1024), lambda c: (0, 0)),
        ],
        out_specs=pl.BlockSpec((batch, step_rows, 1024), lambda c: (0, c, 0)),
        out_shape=jax.ShapeDtypeStruct((batch, rows_per_batch, 1024), BF16),
        scratch_shapes=[pltpu.VMEM((batch * RET_HEADS, RET_DK, RET_DV), F32)],
        compiler_params=_params(("arbitrary",)),
        name="retention",
    )(qk, qk, vg, vg, gn_w)
    return out.reshape(batch * rows_per_batch, 1024)


def _gdn_kernel(*refs, batch, subchunks):
    n = GDN_CHUNK
    q_ref, k_ref, v_ref, g_ref, gcol_ref = refs[:5]
    grow_refs = refs[5:5 + batch]
    nw_ref, o_ref, s_ref = refs[5 + batch:]

    @pl.when(pl.program_id(0) == 0)
    def _():
        s_ref[...] = jnp.zeros(s_ref.shape, F32)

    t_idx = lax.broadcasted_iota(jnp.int32, (n, n), 0)
    s_idx = lax.broadcasted_iota(jnp.int32, (n, n), 1)
    lower = (t_idx >= s_idx).astype(F32)
    upper = (t_idx <= s_idx).astype(F32)
    eye = (t_idx == s_idx).astype(F32)
    rows = lambda sub: slice(n * sub, n * (sub + 1))
    same_block = lambda size: (t_idx // size) == (s_idx // size)

    def state_free(chains):
        cs = range(len(chains))
        blocks = sorted({(b, sub) for b, sub, _ in chains})
        gcol = {bs: gcol_ref[bs[0], rows(bs[1])] for bs in blocks}
        gcum_col = {bs: _dot_hi(lower, gcol[bs]) for bs in blocks}
        gcum_row = {bs: _dot_hi(grow_refs[bs[0]][:, rows(bs[1])], upper) for bs in blocks}
        q = [q_ref[b, rows(sub), GDN_DK * hd:GDN_DK * (hd + 1)].astype(F32) for b, sub, hd in chains]
        k = [k_ref[b, rows(sub), GDN_DK * hd:GDN_DK * (hd + 1)].astype(F32) for b, sub, hd in chains]
        v = [v_ref[b, rows(sub), GDN_DV * hd:GDN_DV * (hd + 1)].astype(F32) for b, sub, hd in chains]
        beta = [gcol[b, sub][:, hd:hd + 1] for b, sub, hd in chains]
        gc = [gcum_col[b, sub][:, GDN_HEADS + hd:GDN_HEADS + hd + 1] for b, sub, hd in chains]
        gr = [gcum_row[b, sub][GDN_HEADS + hd:GDN_HEADS + hd + 1, :] for b, sub, hd in chains]
        g_last = [gc[c][n - 1:n, :] for c in cs]
        seg = [gc[c] - gr[c] for c in cs]
        dec_strict = [jnp.exp(jnp.where(t_idx > s_idx, seg[c], -jnp.inf)) for c in cs]
        dec_incl = [jnp.exp(jnp.where(t_idx >= s_idx, seg[c], -jnp.inf)) for c in cs]
        kb = [k[c] * beta[c] for c in cs]
        kq = [_dot_nt(jnp.concatenate([kb[c], q[c]], axis=0).astype(BF16), k[c].astype(BF16)) for c in cs]
        a = [kq[c][:n] * dec_strict[c] for c in cs]
        att = [(kq[c][n:] * dec_incl[c]).astype(BF16) for c in cs]
        diag = [jnp.where(same_block(GDN_BASE), a[c], 0.0) for c in cs]
        inv = [eye - diag[c] for c in cs]
        p = [_dot16(diag[c], diag[c]) for c in cs]
        span = 4
        while span < GDN_BASE:
            r = [_dot16(jnp.concatenate([p[c], inv[c]], axis=0), p[c]) for c in cs]
            inv = [inv[c] + r[c][n:] for c in cs]
            p = [r[c][:n] for c in cs]
            span *= 2
        inv = [inv[c] + _dot16(inv[c], p[c]) for c in cs]
        size = GDN_BASE
        while size < n:
            joins = jnp.logical_and(same_block(2 * size), jnp.logical_not(same_block(size)))
            off = [jnp.where(joins, a[c], 0.0) for c in cs]
            half = [_dot16(inv[c], off[c]) for c in cs]
            inv = [inv[c] - _dot16(half[c], inv[c]) for c in cs]
            size *= 2
        e_gc = [jnp.exp(gc[c]) for c in cs]
        rhs = [jnp.concatenate([v[c] * beta[c], kb[c] * e_gc[c]], axis=-1) for c in cs]
        u = [_dot16(inv[c], rhs[c]) for c in cs]
        lhs = [jnp.concatenate([u[c][:, GDN_DV:], q[c] * e_gc[c]], axis=0).astype(BF16) for c in cs]
        k_out = [(k[c] * jnp.exp(g_last[c] - gc[c])).T.astype(BF16) for c in cs]
        e_all = [jnp.exp(g_last[c]) for c in cs]
        return list(zip(att, u, lhs, k_out, e_all))

    lanes = [(b, hd) for b in range(batch) for hd in range(GDN_HEADS)]
    chains = [(b, sub, hd) for b in range(batch) for sub in range(subchunks) for hd in range(GDN_HEADS)]
    prepared = dict(zip(chains, state_free(chains)))
    states = [s_ref[i] for i in range(len(lanes))]
    for sub in range(subchunks):
        att, u, lhs, k_out, e_all = zip(*[prepared[b, sub, hd] for b, hd in lanes])
        ls = range(len(lanes))
        ws = [_dot(lhs[i], states[i].astype(BF16)) for i in ls]
        v_new = [(u[i][:, :GDN_DV] - ws[i][:n]).astype(BF16) for i in ls]
        o = [ws[i][n:] + _dot(att[i], v_new[i]) for i in ls]
        states = [states[i] * e_all[i] + _dot(k_out[i], v_new[i]) for i in ls]
        for i, (b, hd) in enumerate(lanes):
            cols = slice(GDN_DV * hd, GDN_DV * (hd + 1))
            on = o[i] * lax.rsqrt(jnp.mean(o[i] * o[i], axis=-1, keepdims=True) + RMS_EPS)
            gate = _silu(g_ref[b, rows(sub), cols].astype(F32))
            o_ref[b, rows(sub), cols] = (on * nw_ref[...] * gate).astype(BF16)
    for i in range(len(lanes)):
        s_ref[i] = states[i]


def _gdn(qk, v, gate, gates, gates_t, norm_w, batch, rows_per_batch):
    n = GDN_CHUNK
    subchunks = next(s for s in (10, 2) if (rows_per_batch // n) % s == 0 and (n * s) % LANES == 0)
    step_rows = n * subchunks
    steps = rows_per_batch // step_rows
    qk = qk.reshape(batch, rows_per_batch, qk.shape[1])
    v = v.reshape(batch, rows_per_batch, v.shape[1])
    gate = gate.reshape(batch, rows_per_batch, gate.shape[1])
    gates = gates.reshape(batch, rows_per_batch, LANES)
    out = pl.pallas_call(
        functools.partial(_gdn_kernel, batch=batch, subchunks=subchunks),
        grid=(steps,),
        in_specs=[
            pl.BlockSpec((batch, step_rows, 512), lambda c: (0, c, 0)),
            pl.BlockSpec((batch, step_rows, 512), lambda c: (0, c, 1)),
            pl.BlockSpec((batch, step_rows, 1024), lambda c: (0, c, 0)),
            pl.BlockSpec((batch, step_rows, 1024), lambda c: (0, c, 0)),
            pl.BlockSpec((batch, step_rows, LANES), lambda c: (0, c, 0)),
        ] + [pl.BlockSpec((8, step_rows), lambda c, b=b: (0, b * steps + c)) for b in range(batch)] + [
            pl.BlockSpec((1, GDN_DV), lambda c: (0, 0)),
        ],
        out_specs=pl.BlockSpec((batch, step_rows, 1024), lambda c: (0, c, 0)),
        out_shape=jax.ShapeDtypeStruct((batch, rows_per_batch, 1024), BF16),
        scratch_shapes=[pltpu.VMEM((batch * GDN_HEADS, GDN_DK, GDN_DV), F32)],
        compiler_params=_params(("arbitrary",)),
        name="gated_delta",
    )(qk, qk, v, gate, gates, *([gates_t] * batch), norm_w)
    return out.reshape(batch * rows_per_batch, 1024)


def _layer_norm(x, w, b):
    xc = x - jnp.mean(x, axis=-1, keepdims=True)
    return xc * lax.rsqrt(jnp.mean(xc * xc, axis=-1, keepdims=True) + LN_EPS) * w + b


def _outproj_ln_kernel(ya_ref, yb_ref, w_ref, h_ref, lw_ref, lb_ref, o_ref, acc_a, acc_b, wb_ref, *, n_tiles):
    half = ya_ref.shape[1]

    @pl.when(pl.program_id(0) == 0)
    def _():
        wb_ref[...] = w_ref[...].astype(BF16)

    def produce(cur, prev):
        acc = _dot(ya_ref[...], wb_ref[0:half, :]) + _dot(yb_ref[...], wb_ref[half:2 * half, :])
        cur[...] = DN_ALPHA * h_ref[...] + acc

    def consume(prev):
        o_ref[...] = _layer_norm(prev[...], lw_ref[...], lb_ref[...])

    _lagged_steps(pl.program_id(0), n_tiles, acc_a, acc_b, produce, consume)


def _outproj_ln(ya, yb, col_a, col_b, w, h, lw, lb, rows_per_batch):
    rows = h.shape[0]
    tm = _row_tile(rows_per_batch)
    n_tiles = rows // tm
    half = w.shape[1] // 2
    cur = lambda i: jnp.minimum(i, n_tiles - 1)
    prev = lambda i: jnp.maximum(i - 1, 0)
    vec = pl.BlockSpec((1, D_MODEL), lambda i: (0, 0))
    return pl.pallas_call(
        functools.partial(_outproj_ln_kernel, n_tiles=n_tiles),
        grid=(n_tiles + 1,),
        in_specs=[
            pl.BlockSpec((tm, half), lambda i: (cur(i), col_a)),
            pl.BlockSpec((tm, half), lambda i: (cur(i), col_b)),
            pl.BlockSpec((None, 2 * half, D_MODEL), lambda i: (0, 0, 0), pipeline_mode=pl.Buffered(1)),
            pl.BlockSpec((tm, D_MODEL), lambda i: (cur(i), 0)),
            vec, vec,
        ],
        out_specs=pl.BlockSpec((tm, D_MODEL), lambda i: (prev(i), 0)),
        out_shape=jax.ShapeDtypeStruct((rows, D_MODEL), F32),
        scratch_shapes=[pltpu.VMEM((tm, D_MODEL), F32), pltpu.VMEM((tm, D_MODEL), F32),
                        pltpu.VMEM((2 * half, D_MODEL), BF16)],
        compiler_params=_params(("arbitrary",)),
        name="outproj_ln",
    )(ya, yb, w, h, lw, lb)


def _mlp(h_ref, w1_ref, w2_ref, lw_ref, lb_ref):
    h = h_ref[...]
    hb = h.astype(BF16)
    acc = jnp.zeros(h.shape, F32)
    step = 1024
    for f in range(0, D_FF, step):
        a = _dot(hb, w1_ref[:, f:f + step])
        a = jnp.square(jnp.maximum(a, 0.0)).astype(BF16)
        acc = acc + _dot(a, w2_ref[f:f + step, :])
    return _layer_norm(DN_ALPHA * h + acc, lw_ref[...], lb_ref[...])


def _mlp_ln_kernel(h_ref, w1_ref, w2_ref, lw_ref, lb_ref, wg_ref, alog_ref, bias_ref, o_ref, ob_ref, g_ref, gt_ref,
                   *, tm, tiles_per_batch):
    out = _mlp(h_ref, w1_ref, w2_ref, lw_ref, lb_ref)
    o_ref[...] = out
    row = (pl.program_id(0) % tiles_per_batch) * tm + lax.broadcasted_iota(jnp.int32, (tm, 1), 0)
    live = row >= PAD_FRONT
    hb = jnp.where(live, out, 0.0).astype(BF16)
    ob_ref[...] = hb
    gates = _ssd_gates(hb, live, wg_ref, alog_ref, bias_ref)
    g_ref[...] = gates
    gt_ref[...] = gates.T


def _mlp_ln_final_kernel(h_ref, w1_ref, w2_ref, lw_ref, lb_ref, o_ref):
    o_ref[...] = _mlp(h_ref, w1_ref, w2_ref, lw_ref, lb_ref)


def _mlp_weight_specs(index):
    return [
        pl.BlockSpec((D_MODEL, D_FF), index, pipeline_mode=pl.Buffered(1)),
        pl.BlockSpec((D_FF, D_MODEL), index, pipeline_mode=pl.Buffered(1)),
        pl.BlockSpec((1, D_MODEL), index),
        pl.BlockSpec((1, D_MODEL), index),
    ]


def _mlp_ln(h, w1, w2, lw, lb, w_gate, alog, bias, rows_per_batch):
    rows = h.shape[0]
    tm = _row_tile(rows_per_batch)
    tile = pl.BlockSpec((tm, D_MODEL), lambda i: (i, 0))
    vec = pl.BlockSpec((1, LANES), lambda i: (0, 0))
    return pl.pallas_call(
        functools.partial(_mlp_ln_kernel, tm=tm, tiles_per_batch=rows_per_batch // tm),
        grid=(rows // tm,),
        in_specs=[tile] + _mlp_weight_specs(lambda i: (0, 0)) + [pl.BlockSpec((D_MODEL, LANES), lambda i: (0, 0)),
                                                                   vec, vec],
        out_specs=[tile, tile, pl.BlockSpec((tm, LANES), lambda i: (i, 0)), pl.BlockSpec((LANES, tm), lambda i: (0, i))],
        out_shape=[jax.ShapeDtypeStruct((rows, D_MODEL), F32), jax.ShapeDtypeStruct((rows, D_MODEL), BF16),
                   jax.ShapeDtypeStruct((rows, LANES), F32), jax.ShapeDtypeStruct((LANES, rows), F32)],
        compiler_params=_params(("arbitrary",)),
        name="mlp_ln",
    )(h, w1, w2, lw, lb, w_gate, alog, bias)


def _mlp_ln_final(h, w1, w2, lw, lb, batch, seq, rows_per_batch):
    tm = next(t for t in (1024, 512, 256, 128) if seq % t == 0)
    tiles = seq // tm
    first_row = PAD_FRONT + N_META
    return pl.pallas_call(
        _mlp_ln_final_kernel,
        grid=(batch, tiles),
        in_specs=[pl.BlockSpec((pl.Element(tm), pl.Element(D_MODEL)),
                               lambda b, i: (pl.multiple_of(b * rows_per_batch + first_row + i * tm, 128), 0))]
        + _mlp_weight_specs(lambda b, i: (0, 0)),
        out_specs=pl.BlockSpec((tm, D_MODEL), lambda b, i: (b * tiles + i, 0)),
        out_shape=jax.ShapeDtypeStruct((batch * seq, D_MODEL), F32),
        compiler_params=_params(("arbitrary", "arbitrary")),
        name="mlp_ln_final",
    )(h, w1, w2, lw, lb)


def _ssd_kernel(*refs, batch, subchunks):
    n = CHUNK
    z_ref, x_ref, b_ref, c_ref, gcol_ref = refs[:5]
    grow_refs = refs[5:5 + batch]
    dskip_ref, nw_ref, o_ref, s_ref = refs[5 + batch:]
    pair_w = 2 * SSD_HEADDIM
    pairs_per_group = SSD_HPG // 2
    group_w = SSD_HPG * SSD_HEADDIM
    n_pairs = SSD_HEADS // 2

    @pl.when(pl.program_id(0) == 0)
    def _():
        s_ref[...] = jnp.zeros(s_ref.shape, F32)

    t_idx = lax.broadcasted_iota(jnp.int32, (n, n), 0)
    s_idx = lax.broadcasted_iota(jnp.int32, (n, n), 1)
    causal = t_idx >= s_idx
    lower = causal.astype(F32)
    upper = (t_idx <= s_idx).astype(F32)
    lane_v = lax.broadcasted_iota(jnp.int32, (n, pair_w), 1)
    lane_s = lax.broadcasted_iota(jnp.int32, (SSD_DSTATE, pair_w), 1)
    lane_1 = lax.broadcasted_iota(jnp.int32, (1, pair_w), 1)
    bs = range(batch)
    for sub in range(subchunks):
        rows = slice(n * sub, n * (sub + 1))
        grow = [grow_refs[b][:, rows] for b in bs]
        gcum_col = [_dot_hi(lower, gcol_ref[b, rows]) for b in bs]
        gcum_row = [_dot_hi(grow[b], upper) for b in bs]
        for g in range(SSD_GROUPS):
            q = [c_ref[b, rows, SSD_DSTATE * g:SSD_DSTATE * (g + 1)] for b in bs]
            k = [b_ref[b, rows, SSD_DSTATE * g:SSD_DSTATE * (g + 1)] for b in bs]
            qf = [q[b].astype(F32) for b in bs]
            kt = [k[b].astype(F32).T for b in bs]
            cb = [_dot_nt(q[b], k[b]) for b in bs]
            ys = [[] for _ in bs]
            for p in range(pairs_per_group):
                pair = g * pairs_per_group + p
                col = group_w * g + pair_w * p
                for b in bs:
                    xv = x_ref[b, rows, col:col + pair_w]
                    state = s_ref[b * n_pairs + pair]
                    o = jnp.zeros((n, pair_w), F32)
                    upd = jnp.zeros((SSD_DSTATE, pair_w), F32)
                    last = []
                    for side in range(2):
                        hd = 2 * pair + side
                        gc = gcum_col[b][:, SSD_HEADS + hd:SSD_HEADS + hd + 1]
                        gr = gcum_row[b][SSD_HEADS + hd:SSD_HEADS + hd + 1, :]
                        dt_row = grow[b][hd:hd + 1, :]
                        g_last = gr[:, n - 1:n]
                        dec = jnp.exp(jnp.where(causal, gc - gr, -jnp.inf))
                        att = cb[b] * dec * dt_row
                        q_in = qf[b] * jnp.exp(gc)
                        mine_v = (lane_v >= SSD_HEADDIM) if side else (lane_v < SSD_HEADDIM)
                        mine_s = (lane_s >= SSD_HEADDIM) if side else (lane_s < SSD_HEADDIM)
                        xm = jnp.where(mine_v, xv, jnp.zeros_like(xv))
                        sm = jnp.where(mine_s, state, 0.0).astype(BF16)
                        o = o + _dot(att.astype(BF16), xm) + _dot(q_in.astype(BF16), sm)
                        k_out = kt[b] * (jnp.exp(g_last - gr) * dt_row)
                        upd = upd + _dot(k_out.astype(BF16), xm)
                        last.append(jnp.exp(g_last))
                    s_ref[b * n_pairs + pair] = state * jnp.where(lane_1 < SSD_HEADDIM, last[0], last[1]) + upd
                    ys[b].append(o + xv.astype(F32) * dskip_ref[:, col:col + pair_w])
            cols = slice(group_w * g, group_w * (g + 1))
            for b in bs:
                y = jnp.concatenate(ys[b], axis=-1)
                y = y * _silu(z_ref[b, rows, cols].astype(F32))
                y = y * lax.rsqrt(jnp.mean(y * y, axis=-1, keepdims=True) + RMS_EPS)
                o_ref[b, rows, cols] = (y * nw_ref[:, cols]).astype(BF16)


def _ssd(z, xbc, gates, gates_t, dskip, norm_w, batch, rows_per_batch):
    n = CHUNK
    subchunks = 1
    step_rows = n * subchunks
    steps = rows_per_batch // step_rows
    z = z.reshape(batch, rows_per_batch, z.shape[1])
    xbc = xbc.reshape(batch, rows_per_batch, xbc.shape[1])
    gates = gates.reshape(batch, rows_per_batch, LANES)
    out = pl.pallas_call(
        functools.partial(_ssd_kernel, batch=batch, subchunks=subchunks),
        grid=(steps,),
        in_specs=[
            pl.BlockSpec((batch, step_rows, SSD_DINNER), lambda c: (0, c, 0)),
            pl.BlockSpec((batch, step_rows, SSD_DINNER), lambda c: (0, c, 0)),
            pl.BlockSpec((batch, step_rows, SSD_GN), lambda c: (0, c, 4)),
            pl.BlockSpec((batch, step_rows, SSD_GN), lambda c: (0, c, 5)),
            pl.BlockSpec((batch, step_rows, LANES), lambda c: (0, c, 0)),
        ] + [pl.BlockSpec((2 * SSD_HEADS, step_rows), lambda c, b=b: (0, b * steps + c)) for b in range(batch)] + [
            pl.BlockSpec((1, SSD_DINNER), lambda c: (0, 0)),
            pl.BlockSpec((1, SSD_DINNER), lambda c: (0, 0)),
        ],
        out_specs=pl.BlockSpec((batch, step_rows, SSD_DINNER), lambda c: (0, c, 0)),
        out_shape=jax.ShapeDtypeStruct((batch, rows_per_batch, SSD_DINNER), BF16),
        scratch_shapes=[pltpu.VMEM((batch * SSD_HEADS // 2, SSD_DSTATE, 2 * SSD_HEADDIM), F32)],
        compiler_params=_params(("arbitrary",)),
        name="ssd",
    )(z, xbc, xbc, xbc, gates, *([gates_t] * batch), dskip, norm_w)
    return out.reshape(batch * rows_per_batch, SSD_DINNER)


def _pad_lanes(v):
    v = v.reshape(1, -1).astype(F32)
    return jnp.pad(v, ((0, 0), (0, LANES - v.shape[1])))


def kernel(x, meta_tokens, ab_w_in, ab_ret_gn_w, ab_conv_q, ab_conv_k, ab_conv_v, ab_A_log, ab_dt_bias, ab_gdn_norm_w, ab_w_out, c_w_in, c_conv_w, c_conv_b, c_A_log, c_dt_bias, c_D, c_norm_w, c_w_out, mlp_w1, mlp_w2, ln1_w, ln1_b, ln2_w, ln2_b):
    batch, seq, d = x.shape
    assert d == D_MODEL and meta_tokens.shape == (N_META, D_MODEL)
    lp = PAD_FRONT + N_META + seq

    inv_freq = 1.0 / (ROPE_BASE ** jnp.linspace(0.0, 1.0, RET_DK // 2, dtype=F32))
    coarse = (jnp.arange(lp // CHUNK, dtype=F32) * CHUNK - PAD_FRONT)[:, None, None] * inv_freq
    fine = jnp.arange(CHUNK, dtype=F32)[None, :, None] * inv_freq
    cos = (jnp.cos(coarse) * jnp.cos(fine) - jnp.sin(coarse) * jnp.sin(fine)).reshape(lp, RET_DK // 2)
    sin = (jnp.sin(coarse) * jnp.cos(fine) + jnp.cos(coarse) * jnp.sin(fine)).reshape(lp, RET_DK // 2)
    cosf = jnp.concatenate([cos, cos], axis=-1)
    sinf = jnp.concatenate([-sin, sin], axis=-1)

    w_in = ab_w_in[:1].astype(BF16)
    w_gg = w_in[:, :, 5128:]
    w_gate = jnp.pad(w_in[0, :, 5120:5128], ((0, 0), (0, LANES - 8)))
    alog = _pad_lanes(jnp.concatenate([jnp.zeros((GDN_HEADS,), F32), ab_A_log[0]]))
    bias = _pad_lanes(jnp.concatenate([jnp.zeros((GDN_HEADS,), F32), ab_dt_bias[0]]))
    ones = jnp.ones((1, COL_TILE // 2), F32)
    zeros = jnp.zeros((1, COL_TILE), F32)
    h, hb, g0, g0_t, p_qk = _assemble(x, meta_tokens, w_gate, alog, bias, w_in, cosf, sinf,
                                      jnp.concatenate([ones * RET_DK ** -0.5, ones], axis=1), lp)
    p_vg = _proj(hb, w_in, (lambda j: j + 1, 2), _epilogue_plain, [], [], lp, "proj_ret_vg")
    p_gg = _proj(hb, w_gg, (lambda j: j, 1), _epilogue_plain, [], [], lp, "proj_gdn_g")
    g_qk = _proj(hb, w_in, (lambda j: j + 3, 1), functools.partial(_epilogue_conv, l2norm=True),
                 [jnp.concatenate([ab_conv_q[0], ab_conv_k[0]], axis=1), zeros,
                  jnp.concatenate([ones * GDN_DK ** -0.5, ones], axis=1)],
                 [_per_col(rows=CONV_K), _per_col(), _per_col()], lp, "proj_gdn_qk")
    g_v = _proj(hb, w_in, (lambda j: j + 4, 1), functools.partial(_epilogue_conv, l2norm=False),
                [ab_conv_v[0], zeros, zeros], [_per_col(rows=CONV_K), _per_col(), _per_col()], lp, "proj_gdn_v")
    y_ret = _retention(p_qk, p_vg, ab_ret_gn_w[0].reshape(1, -1), batch, lp)
    y_gdn = _gdn(g_qk, g_v, p_gg, g0, g0_t, ab_gdn_norm_w[0].reshape(1, -1), batch, lp)
    h = _outproj_ln(y_ret, y_gdn, 0, 0, ab_w_out[:1], h,
                    ln1_w[0].reshape(1, -1), ln1_b[0].reshape(1, -1), lp)
    w_in = c_w_in[:1].astype(BF16)
    w_dt = w_in[0, :, 5120:]
    w_gate = jnp.pad(jnp.concatenate([w_dt, w_dt], axis=1), ((0, 0), (0, LANES - 2 * SSD_HEADS)))
    alog = _pad_lanes(jnp.concatenate([jnp.zeros((SSD_HEADS,), F32), c_A_log[0]]))
    bias = _pad_lanes(jnp.concatenate([c_dt_bias[0], c_dt_bias[0]]))
    h, hb, g1, g1_t = _mlp_ln(h, mlp_w1[0].astype(BF16), mlp_w2[0].astype(BF16),
                              ln2_w[0].reshape(1, -1), ln2_b[0].reshape(1, -1), w_gate, alog, bias, lp)

    p_z = _proj(hb, w_in, (lambda j: j, 2), _epilogue_plain, [], [], lp, "proj_ssd_z")
    p_xbc = _proj(hb, w_in, (lambda j: j + 2, 3), functools.partial(_epilogue_conv, l2norm=False),
                  [c_conv_w[0], c_conv_b[0].reshape(1, -1), jnp.zeros((1, 3 * COL_TILE), F32)],
                  [_per_col(rows=CONV_K), _per_col(), _per_col()], lp, "proj_ssd_xbc")
    dskip = jnp.repeat(c_D[0].astype(F32), SSD_HEADDIM).reshape(1, -1)
    y_ssd = _ssd(p_z, p_xbc, g1, g1_t, dskip, c_norm_w[0].reshape(1, -1), batch, lp)
    h = _outproj_ln(y_ssd, y_ssd, 0, 1, c_w_out[:1], h,
                    ln1_w[1].reshape(1, -1), ln1_b[1].reshape(1, -1), lp)
    out = _mlp_ln_final(h, mlp_w1[1].astype(BF16), mlp_w2[1].astype(BF16),
                        ln2_w[1].reshape(1, -1), ln2_b[1].reshape(1, -1), batch, seq, lp)
    return out.reshape(batch, seq, d)
```

```python
import functools

import jax
import jax.numpy as jnp
from jax import lax
from jax.experimental import pallas as pl
from jax.experimental.pallas import tpu as pltpu

F32 = jnp.float32
BF16 = jnp.bfloat16
HI = lax.Precision.HIGHEST

D_MODEL = 1024
DEPTH = 2
N_META = 16
CONV_K = 4
RET_HEADS = 4
RET_DK = 128
RET_DV = 256
ROPE_BASE = 10000.0
GDN_HEADS = 4
GDN_DK = 128
GDN_DV = 256
SSD_DINNER = 2048
SSD_HEADDIM = 64
SSD_HEADS = 32
SSD_GROUPS = 4
SSD_HPG = 8
SSD_DSTATE = 128
SSD_GN = 512
D_FF = 4096
DN_ALPHA = (2 * DEPTH) ** 0.25
LN_EPS = 1e-5
GN_EPS = 1e-5
RMS_EPS = 1e-6

LANES = 128
COL_TILE = 1024
CHUNK = 128
GDN_CHUNK = 64
GDN_BASE = 8
PAD_FRONT = CHUNK - N_META
HALO = 8
CONV_ROWS = 64
PROJ_ROW_TILE = 1664
VMEM_LIMIT = 56 * 1024 * 1024


def _row_tile(rows_per_batch, largest=640):
    for tm in (1664, 640, 512, 256, 128):
        if tm <= largest and rows_per_batch % tm == 0:
            return tm
    raise ValueError(f"unsupported padded sequence length {rows_per_batch}")


def _params(sem):
    return pltpu.CompilerParams(dimension_semantics=sem, vmem_limit_bytes=VMEM_LIMIT)


def _softplus(x):
    return jnp.maximum(x, 0.0) + jnp.log1p(jnp.exp(-jnp.abs(x)))


def _silu(x):
    return x * jax.nn.sigmoid(x)


def _dot(a, b):
    return jnp.dot(a, b, preferred_element_type=F32)


def _dot_nt(a, b):
    return lax.dot_general(a, b, (((1,), (1,)), ((), ())), preferred_element_type=F32)


def _dot_hi(a, b):
    return jnp.dot(a, b, preferred_element_type=F32, precision=HI)


def _dot16(a, b):
    return _dot(a.astype(BF16), b.astype(BF16))


def _epilogue_plain(acc_ref, o_ref, *, tm):
    o_ref[...] = acc_ref[HALO:HALO + tm, :].astype(BF16)


def _epilogue_conv(acc_ref, o_ref, cw_ref, cb_ref, scale_ref, *, tm, l2norm):
    for r in range(0, tm, CONV_ROWS):
        rows = slice(r, r + CONV_ROWS)
        for c in range(0, COL_TILE, LANES):
            cols = slice(c, c + LANES)
            w = [cw_ref[tap:tap + 1, cols] for tap in range(CONV_K)]
            xw = acc_ref[r:r + CONV_ROWS + HALO, cols]
            s1 = pltpu.roll(xw, 1, 0)
            y = w[3] * xw + w[2] * s1 + pltpu.roll(w[1] * xw + w[0] * s1, 2, 0)
            y = _silu(y[HALO:] + cb_ref[:, cols])
            if l2norm:
                inv = lax.rsqrt(jnp.sum(y * y, axis=-1, keepdims=True) + 1e-6)
                y = y * (inv * scale_ref[:, cols])
            o_ref[rows, cols] = y.astype(BF16)


def _lagged_steps(i, n_tiles, acc_a, acc_b, produce, consume):
    last_prev = acc_b if n_tiles % 2 == 0 else acc_a

    @pl.when(i == 0)
    def _():
        acc_b[...] = jnp.zeros(acc_b.shape, F32)

    @pl.when(jnp.logical_and(i % 2 == 0, i < n_tiles))
    def _():
        produce(acc_a, acc_b)
        consume(acc_b)

    @pl.when(jnp.logical_and(i % 2 == 1, i < n_tiles))
    def _():
        produce(acc_b, acc_a)
        consume(acc_a)

    @pl.when(i == n_tiles)
    def _():
        consume(last_prev)


def _proj_kernel(*refs, tm, tiles_per_batch, n_extra, epilogue):
    h_ref, w_ref = refs[0], refs[1]
    extra = refs[2:2 + n_extra]
    o_ref, acc_ref = refs[2 + n_extra:]

    @pl.when(pl.program_id(1) % tiles_per_batch == 0)
    def _():
        acc_ref[0:HALO, :] = jnp.zeros((HALO, COL_TILE), F32)

    acc_ref[HALO:HALO + tm, :] = _dot(h_ref[...], w_ref[...])
    epilogue(acc_ref, o_ref, *extra, tm=tm)
    acc_ref[0:HALO, :] = acc_ref[tm:tm + HALO, :]


def _proj(hb, w, col_tiles, epilogue, extra, extra_specs, rows_per_batch, name):
    rows = hb.shape[0]
    tm = _row_tile(rows_per_batch, largest=PROJ_ROW_TILE)
    tpb = rows_per_batch // tm
    col_fn, ncol = col_tiles
    return pl.pallas_call(
        functools.partial(_proj_kernel, tm=tm, tiles_per_batch=tpb, n_extra=len(extra), epilogue=epilogue),
        grid=(ncol, rows // tm),
        in_specs=[
            pl.BlockSpec((tm, D_MODEL), lambda j, i: (i, 0)),
            pl.BlockSpec((None, D_MODEL, COL_TILE), lambda j, i: (0, 0, col_fn(j))),
        ] + [spec(tm, tpb) for spec in extra_specs],
        out_specs=pl.BlockSpec((tm, COL_TILE), lambda j, i: (i, j)),
        out_shape=jax.ShapeDtypeStruct((rows, ncol * COL_TILE), BF16),
        scratch_shapes=[pltpu.VMEM((tm + HALO, COL_TILE), F32)],
        compiler_params=_params(("arbitrary", "arbitrary")),
        name=name,
    )(hb, w, *extra)


def _per_col(width=COL_TILE, rows=1):
    return lambda tm, tpb: pl.BlockSpec((rows, width), lambda j, i: (0, j))


def _gdn_gates(hb, w_ref, alog_ref, bias_ref):
    acc = _dot(hb, w_ref[...])
    lane = lax.broadcasted_iota(jnp.int32, acc.shape, 1)
    decay = -jnp.exp(alog_ref[...]) * _softplus(acc + bias_ref[...])
    return jnp.where(lane < GDN_HEADS, jax.nn.sigmoid(acc), decay)


def _assemble_kernel(x_ref, meta_ref, w_ref, alog_ref, bias_ref, wqk_ref, cos_ref, sin_ref, scale_ref,
                     h_ref, hb_ref, g_ref, gt_ref, qk_ref, *, tm):
    def emit(tile):
        hb = tile.astype(BF16)
        h_ref[...] = tile
        hb_ref[...] = hb
        gates = _gdn_gates(hb, w_ref, alog_ref, bias_ref)
        g_ref[...] = gates
        gt_ref[...] = gates.T
        acc = _dot(hb, wqk_ref[...])
        for hd in range(COL_TILE // RET_DK):
            cols = slice(RET_DK * hd, RET_DK * (hd + 1))
            xh = acc[:, cols]
            y = xh * cos_ref[...] + pltpu.roll(xh, RET_DK // 2, 1) * sin_ref[...]
            qk_ref[:, cols] = (y * scale_ref[:, cols]).astype(BF16)

    first = pl.program_id(1) == 0

    @pl.when(first)
    def _():
        head = PAD_FRONT + N_META
        emit(jnp.concatenate([jnp.zeros((PAD_FRONT, D_MODEL), F32), meta_ref[...], x_ref[0:tm - head, :]], axis=0))

    @pl.when(jnp.logical_not(first))
    def _():
        emit(x_ref[...])


def _assemble(x, meta_tokens, w_gate, alog, bias, w_in, cosf, sinf, qk_scale, rows_per_batch):
    batch, seq, _ = x.shape
    rows = batch * rows_per_batch
    tm = _row_tile(rows_per_batch)
    tpb = rows_per_batch // tm
    head = PAD_FRONT + N_META
    vec = lambda width: pl.BlockSpec((1, width), lambda b, i: (0, 0))
    tile = lambda width: pl.BlockSpec((tm, width), lambda b, i: (b * tpb + i, 0))
    table = pl.BlockSpec((tm, LANES), lambda b, i: (i, 0))
    return pl.pallas_call(
        functools.partial(_assemble_kernel, tm=tm),
        grid=(batch, tpb),
        in_specs=[
            pl.BlockSpec((pl.Element(tm), pl.Element(D_MODEL)),
                         lambda b, i: (pl.multiple_of(b * seq + jnp.maximum(i * tm - head, 0), LANES), 0)),
            pl.BlockSpec((N_META, D_MODEL), lambda b, i: (0, 0)),
            pl.BlockSpec((D_MODEL, LANES), lambda b, i: (0, 0)),
            vec(LANES), vec(LANES),
            pl.BlockSpec((None, D_MODEL, COL_TILE), lambda b, i: (0, 0, 0)),
            table, table, vec(COL_TILE),
        ],
        out_specs=[tile(D_MODEL), tile(D_MODEL), tile(LANES), pl.BlockSpec((LANES, tm), lambda b, i: (0, b * tpb + i)),
                   tile(COL_TILE)],
        out_shape=[jax.ShapeDtypeStruct((rows, D_MODEL), F32), jax.ShapeDtypeStruct((rows, D_MODEL), BF16),
                   jax.ShapeDtypeStruct((rows, LANES), F32), jax.ShapeDtypeStruct((LANES, rows), F32),
                   jax.ShapeDtypeStruct((rows, COL_TILE), BF16)],
        compiler_params=_params(("arbitrary", "arbitrary")),
        name="assemble",
    )(x.reshape(batch * seq, D_MODEL), meta_tokens, w_gate, alog, bias, w_in, cosf, sinf, qk_scale)


def _ssd_gates(hb, live, w_ref, alog_ref, bias_ref):
    acc = _dot(hb, w_ref[...])
    lane = lax.broadcasted_iota(jnp.int32, acc.shape, 1)
    dt = jnp.where(live, _softplus(acc + bias_ref[...]), 0.0)
    return jnp.where(lane < SSD_HEADS, dt, dt * -jnp.exp(alog_ref[...]))


def _retention_kernel(q_ref, k_ref, v_ref, g_ref, gnw_ref, o_ref, s_ref, *, batch, subchunks):
    n = CHUNK

    @pl.when(pl.program_id(0) == 0)
    def _():
        s_ref[...] = jnp.zeros(s_ref.shape, F32)

    t_idx = lax.broadcasted_iota(jnp.int32, (n, n), 0)
    s_idx = lax.broadcasted_iota(jnp.int32, (n, n), 1)
    t_col = lax.broadcasted_iota(jnp.int32, (n, 1), 0).astype(F32)
    gap = (t_idx - s_idx).astype(F32)
    log_gamma = [jnp.log1p(-jnp.exp2(jnp.full((1, 1), -5.0 - hd, F32))) for hd in range(RET_HEADS)]
    decay = [jnp.exp(jnp.where(t_idx >= s_idx, gap * lg, -jnp.inf)) for lg in log_gamma]
    e_in = [jnp.exp((t_col + 1.0) * lg) for lg in log_gamma]
    e_out = [jnp.exp((n - 1.0 - t_col) * lg) for lg in log_gamma]
    e_all = [jnp.exp(n * lg) for lg in log_gamma]
    rows = lambda sub: slice(n * sub, n * (sub + 1))
    chains = [(b, sub, hd) for b in range(batch) for sub in range(subchunks) for hd in range(RET_HEADS)]
    cs = range(len(chains))
    q = [q_ref[b, rows(sub), RET_DK * hd:RET_DK * (hd + 1)] for b, sub, hd in chains]
    k = [k_ref[b, rows(sub), RET_DK * hd:RET_DK * (hd + 1)] for b, sub, hd in chains]
    v = [v_ref[b, rows(sub), RET_DV * hd:RET_DV * (hd + 1)] for b, sub, hd in chains]
    att = [(_dot_nt(q[c], k[c]) * decay[chains[c][2]]).astype(BF16) for c in cs]
    intra = [_dot(att[c], v[c]) for c in cs]
    q_in = [(q[c].astype(F32) * e_in[chains[c][2]]).astype(BF16) for c in cs]
    k_out = [(k[c].astype(F32) * e_out[chains[c][2]]).T.astype(BF16) for c in cs]

    lanes = [(b, hd) for b in range(batch) for hd in range(RET_HEADS)]
    states = [s_ref[i] for i in range(len(lanes))]
    for sub in range(subchunks):
        idx = [chains.index((b, sub, hd)) for b, hd in lanes]
        o = [intra[c] + _dot(q_in[c], states[i].astype(BF16)) for i, c in enumerate(idx)]
        states = [states[i] * e_all[hd] + _dot(k_out[c], v[c]) for (i, c), (b, hd) in zip(enumerate(idx), lanes)]
        for i, (b, hd) in enumerate(lanes):
            cols = slice(RET_DV * hd, RET_DV * (hd + 1))
            oc = o[i] - jnp.mean(o[i], axis=-1, keepdims=True)
            on = oc * lax.rsqrt(jnp.mean(oc * oc, axis=-1, keepdims=True) + GN_EPS)
            gate = _silu(g_ref[b, rows(sub), cols].astype(F32))
            o_ref[b, rows(sub), cols] = (on * gnw_ref[:, cols] * gate).astype(BF16)
    for i in range(len(lanes)):
        s_ref[i] = states[i]


def _retention(qk, vg, gn_w, batch, rows_per_batch):
    n = CHUNK
    nc = rows_per_batch // n
    subchunks = next(s for s in (5, 4, 2, 1) if nc % s == 0)
    step_rows = n * subchunks
    qk = qk.reshape(batch, rows_per_batch, qk.shape[1])
    vg = vg.reshape(batch, rows_per_batch, vg.shape[1])
    out = pl.pallas_call(
        functools.partial(_retention_kernel, batch=batch, subchunks=subchunks),
        grid=(nc // subchunks,),
        in_specs=[
            pl.BlockSpec((batch, step_rows, 512), lambda c: (0, c, 0)),
            pl.BlockSpec((batch, step_rows, 512), lambda c: (0, c, 1)),
            pl.BlockSpec((batch, step_rows, 1024), lambda c: (0, c, 0)),
            pl.BlockSpec((batch, step_rows, 1024), lambda c: (0, c, 1)),
            pl.BlockSpec((1, 1024), lambda c: (0, 0)),
        ],
        out_specs=pl.BlockSpec((batch, step_rows, 1024), lambda c: (0, c, 0)),
        out_shape=jax.ShapeDtypeStruct((batch, rows_per_batch, 1024), BF16),
        scratch_shapes=[pltpu.VMEM((batch * RET_HEADS, RET_DK, RET_DV), F32)],
        compiler_params=_params(("arbitrary",)),
        name="retention",
    )(qk, qk, vg, vg, gn_w)
    return out.reshape(batch * rows_per_batch, 1024)


def _gdn_kernel(*refs, batch, subchunks):
    n = GDN_CHUNK
    q_ref, k_ref, v_ref, g_ref, gcol_ref = refs[:5]
    grow_refs = refs[5:5 + batch]
    nw_ref, o_ref, s_ref = refs[5 + batch:]

    @pl.when(pl.program_id(0) == 0)
    def _():
        s_ref[...] = jnp.zeros(s_ref.shape, F32)

    t_idx = lax.broadcasted_iota(jnp.int32, (n, n), 0)
    s_idx = lax.broadcasted_iota(jnp.int32, (n, n), 1)
    lower = (t_idx >= s_idx).astype(F32)
    upper = (t_idx <= s_idx).astype(F32)
    eye = (t_idx == s_idx).astype(F32)
    rows = lambda sub: slice(n * sub, n * (sub + 1))
    same_block = lambda size: (t_idx // size) == (s_idx // size)

    def state_free(chains):
        cs = range(len(chains))
        blocks = sorted({(b, sub) for b, sub, _ in chains})
        gcol = {bs: gcol_ref[bs[0], rows(bs[1])] for bs in blocks}
        gcum_col = {bs: _dot_hi(lower, gcol[bs]) for bs in blocks}
        gcum_row = {bs: _dot_hi(grow_refs[bs[0]][:, rows(bs[1])], upper) for bs in blocks}
        q = [q_ref[b, rows(sub), GDN_DK * hd:GDN_DK * (hd + 1)].astype(F32) for b, sub, hd in chains]
        k = [k_ref[b, rows(sub), GDN_DK * hd:GDN_DK * (hd + 1)].astype(F32) for b, sub, hd in chains]
        v = [v_ref[b, rows(sub), GDN_DV * hd:GDN_DV * (hd + 1)].astype(F32) for b, sub, hd in chains]
        beta = [gcol[b, sub][:, hd:hd + 1] for b, sub, hd in chains]
        gc = [gcum_col[b, sub][:, GDN_HEADS + hd:GDN_HEADS + hd + 1] for b, sub, hd in chains]
        gr = [gcum_row[b, sub][GDN_HEADS + hd:GDN_HEADS + hd + 1, :] for b, sub, hd in chains]
        g_last = [gc[c][n - 1:n, :] for c in cs]
        seg = [gc[c] - gr[c] for c in cs]
        dec_strict = [jnp.exp(jnp.where(t_idx > s_idx, seg[c], -jnp.inf)) for c in cs]
        dec_incl = [jnp.exp(jnp.where(t_idx >= s_idx, seg[c], -jnp.inf)) for c in cs]
        kb = [k[c] * beta[c] for c in cs]
        kq = [_dot_nt(jnp.concatenate([kb[c], q[c]], axis=0).astype(BF16), k[c].astype(BF16)) for c in cs]
        a = [kq[c][:n] * dec_strict[c] for c in cs]
        att = [(kq[c][n:] * dec_incl[c]).astype(BF16) for c in cs]
        diag = [jnp.where(same_block(GDN_BASE), a[c], 0.0) for c in cs]
        inv = [eye - diag[c] for c in cs]
        p = [_dot16(diag[c], diag[c]) for c in cs]
        span = 4
        while span < GDN_BASE:
            r = [_dot16(jnp.concatenate([p[c], inv[c]], axis=0), p[c]) for c in cs]
            inv = [inv[c] + r[c][n:] for c in cs]
            p = [r[c][:n] for c in cs]
            span *= 2
        inv = [inv[c] + _dot16(inv[c], p[c]) for c in cs]
        size = GDN_BASE
        while size < n:
            joins = jnp.logical_and(same_block(2 * size), jnp.logical_not(same_block(size)))
            off = [jnp.where(joins, a[c], 0.0) for c in cs]
            half = [_dot16(inv[c], off[c]) for c in cs]
            inv = [inv[c] - _dot16(half[c], inv[c]) for c in cs]
            size *= 2
        e_gc = [jnp.exp(gc[c]) for c in cs]
        rhs = [jnp.concatenate([v[c] * beta[c], kb[c] * e_gc[c]], axis=-1) for c in cs]
        u = [_dot16(inv[c], rhs[c]) for c in cs]
        lhs = [jnp.concatenate([u[c][:, GDN_DV:], q[c] * e_gc[c]], axis=0).astype(BF16) for c in cs]
        k_out = [(k[c] * jnp.exp(g_last[c] - gc[c])).T.astype(BF16) for c in cs]
        e_all = [jnp.exp(g_last[c]) for c in cs]
        return list(zip(att, u, lhs, k_out, e_all))

    lanes = [(b, hd) for b in range(batch) for hd in range(GDN_HEADS)]
    chains = [(b, sub, hd) for b in range(batch) for sub in range(subchunks) for hd in range(GDN_HEADS)]
    prepared = dict(zip(chains, state_free(chains)))
    states = [s_ref[i] for i in range(len(lanes))]
    for sub in range(subchunks):
        att, u, lhs, k_out, e_all = zip(*[prepared[b, sub, hd] for b, hd in lanes])
        ls = range(len(lanes))
        ws = [_dot(lhs[i], states[i].astype(BF16)) for i in ls]
        v_new = [(u[i][:, :GDN_DV] - ws[i][:n]).astype(BF16) for i in ls]
        o = [ws[i][n:] + _dot(att[i], v_new[i]) for i in ls]
        states = [states[i] * e_all[i] + _dot(k_out[i], v_new[i]) for i in ls]
        for i, (b, hd) in enumerate(lanes):
            cols = slice(GDN_DV * hd, GDN_DV * (hd + 1))
            on = o[i] * lax.rsqrt(jnp.mean(o[i] * o[i], axis=-1, keepdims=True) + RMS_EPS)
            gate = _silu(g_ref[b, rows(sub), cols].astype(F32))
            o_ref[b, rows(sub), cols] = (on * nw_ref[...] * gate).astype(BF16)
    for i in range(len(lanes)):
        s_ref[i] = states[i]


def _gdn(qk, v, gate, gates, gates_t, norm_w, batch, rows_per_batch):
    n = GDN_CHUNK
    subchunks = next(s for s in (10, 2) if (rows_per_batch // n) % s == 0 and (n * s) % LANES == 0)
    step_rows = n * subchunks
    steps = rows_per_batch // step_rows
    qk = qk.reshape(batch, rows_per_batch, qk.shape[1])
    v = v.reshape(batch, rows_per_batch, v.shape[1])
    gate = gate.reshape(batch, rows_per_batch, gate.shape[1])
    gates = gates.reshape(batch, rows_per_batch, LANES)
    out = pl.pallas_call(
        functools.partial(_gdn_kernel, batch=batch, subchunks=subchunks),
        grid=(steps,),
        in_specs=[
            pl.BlockSpec((batch, step_rows, 512), lambda c: (0, c, 0)),
            pl.BlockSpec((batch, step_rows, 512), lambda c: (0, c, 1)),
            pl.BlockSpec((batch, step_rows, 1024), lambda c: (0, c, 0)),
            pl.BlockSpec((batch, step_rows, 1024), lambda c: (0, c, 0)),
            pl.BlockSpec((batch, step_rows, LANES), lambda c: (0, c, 0)),
        ] + [pl.BlockSpec((8, step_rows), lambda c, b=b: (0, b * steps + c)) for b in range(batch)] + [
            pl.BlockSpec((1, GDN_DV), lambda c: (0, 0)),
        ],
        out_specs=pl.BlockSpec((batch, step_rows, 1024), lambda c: (0, c, 0)),
        out_shape=jax.ShapeDtypeStruct((batch, rows_per_batch, 1024), BF16),
        scratch_shapes=[pltpu.VMEM((batch * GDN_HEADS, GDN_DK, GDN_DV), F32)],
        compiler_params=_params(("arbitrary",)),
        name="gated_delta",
    )(qk, qk, v, gate, gates, *([gates_t] * batch), norm_w)
    return out.reshape(batch * rows_per_batch, 1024)


def _layer_norm(x, w, b):
    xc = x - jnp.mean(x, axis=-1, keepdims=True)
    return xc * lax.rsqrt(jnp.mean(xc * xc, axis=-1, keepdims=True) + LN_EPS) * w + b


def _outproj_ln_kernel(ya_ref, yb_ref, w_ref, h_ref, lw_ref, lb_ref, o_ref, acc_a, acc_b, wb_ref, *, n_tiles):
    half = ya_ref.shape[1]

    @pl.when(pl.program_id(0) == 0)
    def _():
        wb_ref[...] = w_ref[...].astype(BF16)

    def produce(cur, prev):
        acc = _dot(ya_ref[...], wb_ref[0:half, :]) + _dot(yb_ref[...], wb_ref[half:2 * half, :])
        cur[...] = DN_ALPHA * h_ref[...] + acc

    def consume(prev):
        o_ref[...] = _layer_norm(prev[...], lw_ref[...], lb_ref[...])

    _lagged_steps(pl.program_id(0), n_tiles, acc_a, acc_b, produce, consume)


def _outproj_ln(ya, yb, col_a, col_b, w, h, lw, lb, rows_per_batch):
    rows = h.shape[0]
    tm = _row_tile(rows_per_batch)
    n_tiles = rows // tm
    half = w.shape[1] // 2
    cur = lambda i: jnp.minimum(i, n_tiles - 1)
    prev = lambda i: jnp.maximum(i - 1, 0)
    vec = pl.BlockSpec((1, D_MODEL), lambda i: (0, 0))
    return pl.pallas_call(
        functools.partial(_outproj_ln_kernel, n_tiles=n_tiles),
        grid=(n_tiles + 1,),
        in_specs=[
            pl.BlockSpec((tm, half), lambda i: (cur(i), col_a)),
            pl.BlockSpec((tm, half), lambda i: (cur(i), col_b)),
            pl.BlockSpec((None, 2 * half, D_MODEL), lambda i: (0, 0, 0), pipeline_mode=pl.Buffered(1)),
            pl.BlockSpec((tm, D_MODEL), lambda i: (cur(i), 0)),
            vec, vec,
        ],
        out_specs=pl.BlockSpec((tm, D_MODEL), lambda i: (prev(i), 0)),
        out_shape=jax.ShapeDtypeStruct((rows, D_MODEL), F32),
        scratch_shapes=[pltpu.VMEM((tm, D_MODEL), F32), pltpu.VMEM((tm, D_MODEL), F32),
                        pltpu.VMEM((2 * half, D_MODEL), BF16)],
        compiler_params=_params(("arbitrary",)),
        name="outproj_ln",
    )(ya, yb, w, h, lw, lb)


def _mlp(h, w1_ref, w2_ref, lw_ref, lb_ref):
    hb = h.astype(BF16)
    acc = jnp.zeros(h.shape, F32)
    step = 1024
    for f in range(0, D_FF, step):
        a = _dot(hb, w1_ref[:, f:f + step])
        a = jnp.square(jnp.maximum(a, 0.0)).astype(BF16)
        acc = acc + _dot(a, w2_ref[f:f + step, :])
    return _layer_norm(DN_ALPHA * h + acc, lw_ref[...], lb_ref[...])


def _mlp_ln_kernel(ya_ref, yb_ref, wo_ref, h_ref, lw1_ref, lb1_ref, w1_ref, w2_ref, lw_ref, lb_ref,
                   wg_ref, alog_ref, bias_ref, o_ref, ob_ref, g_ref, gt_ref, wb_ref, *, tm, tiles_per_batch):
    half = ya_ref.shape[1]

    @pl.when(pl.program_id(0) == 0)
    def _():
        wb_ref[...] = wo_ref[...].astype(BF16)

    acc = _dot(ya_ref[...], wb_ref[0:half, :]) + _dot(yb_ref[...], wb_ref[half:2 * half, :])
    h = _layer_norm(DN_ALPHA * h_ref[...] + acc, lw1_ref[...], lb1_ref[...])
    out = _mlp(h, w1_ref, w2_ref, lw_ref, lb_ref)
    o_ref[...] = out
    row = (pl.program_id(0) % tiles_per_batch) * tm + lax.broadcasted_iota(jnp.int32, (tm, 1), 0)
    live = row >= PAD_FRONT
    hb = jnp.where(live, out, 0.0).astype(BF16)
    ob_ref[...] = hb
    gates = _ssd_gates(hb, live, wg_ref, alog_ref, bias_ref)
    g_ref[...] = gates
    gt_ref[...] = gates.T


def _mlp_ln_final_kernel(h_ref, w1_ref, w2_ref, lw_ref, lb_ref, o_ref):
    o_ref[...] = _mlp(h_ref[...], w1_ref, w2_ref, lw_ref, lb_ref)


def _mlp_weight_specs(index):
    return [
        pl.BlockSpec((D_MODEL, D_FF), index, pipeline_mode=pl.Buffered(1)),
        pl.BlockSpec((D_FF, D_MODEL), index, pipeline_mode=pl.Buffered(1)),
        pl.BlockSpec((1, D_MODEL), index),
        pl.BlockSpec((1, D_MODEL), index),
    ]


def _mlp_ln(ya, yb, w_out, h, lw1, lb1, w1, w2, lw, lb, w_gate, alog, bias, rows_per_batch):
    rows = h.shape[0]
    tm = _row_tile(rows_per_batch)
    half = w_out.shape[1] // 2
    tile = pl.BlockSpec((tm, D_MODEL), lambda i: (i, 0))
    ytile = pl.BlockSpec((tm, half), lambda i: (i, 0))
    dvec = pl.BlockSpec((1, D_MODEL), lambda i: (0, 0))
    vec = pl.BlockSpec((1, LANES), lambda i: (0, 0))
    return pl.pallas_call(
        functools.partial(_mlp_ln_kernel, tm=tm, tiles_per_batch=rows_per_batch // tm),
        grid=(rows // tm,),
        in_specs=[ytile, ytile,
                  pl.BlockSpec((None, 2 * half, D_MODEL), lambda i: (0, 0, 0), pipeline_mode=pl.Buffered(1)),
                  tile, dvec, dvec]
        + _mlp_weight_specs(lambda i: (0, 0)) + [pl.BlockSpec((D_MODEL, LANES), lambda i: (0, 0)), vec, vec],
        out_specs=[tile, tile, pl.BlockSpec((tm, LANES), lambda i: (i, 0)), pl.BlockSpec((LANES, tm), lambda i: (0, i))],
        out_shape=[jax.ShapeDtypeStruct((rows, D_MODEL), F32), jax.ShapeDtypeStruct((rows, D_MODEL), BF16),
                   jax.ShapeDtypeStruct((rows, LANES), F32), jax.ShapeDtypeStruct((LANES, rows), F32)],
        scratch_shapes=[pltpu.VMEM((2 * half, D_MODEL), BF16)],
        compiler_params=_params(("arbitrary",)),
        name="mlp_ln",
    )(ya, yb, w_out, h, lw1, lb1, w1, w2, lw, lb, w_gate, alog, bias)


def _mlp_ln_final(h, w1, w2, lw, lb, batch, seq, rows_per_batch):
    tm = next(t for t in (1024, 512, 256, 128) if seq % t == 0)
    tiles = seq // tm
    first_row = PAD_FRONT + N_META
    return pl.pallas_call(
        _mlp_ln_final_kernel,
        grid=(batch, tiles),
        in_specs=[pl.BlockSpec((pl.Element(tm), pl.Element(D_MODEL)),
                               lambda b, i: (pl.multiple_of(b * rows_per_batch + first_row + i * tm, 128), 0))]
        + _mlp_weight_specs(lambda b, i: (0, 0)),
        out_specs=pl.BlockSpec((tm, D_MODEL), lambda b, i: (b * tiles + i, 0)),
        out_shape=jax.ShapeDtypeStruct((batch * seq, D_MODEL), F32),
        compiler_params=_params(("arbitrary", "arbitrary")),
        name="mlp_ln_final",
    )(h, w1, w2, lw, lb)


def _ssd_kernel(*refs, batch, subchunks):
    n = CHUNK
    z_ref, x_ref, b_ref, c_ref, gcol_ref = refs[:5]
    grow_refs = refs[5:5 + batch]
    dskip_ref, nw_ref, o_ref, s_ref = refs[5 + batch:]
    pair_w = 2 * SSD_HEADDIM
    pairs_per_group = SSD_HPG // 2
    group_w = SSD_HPG * SSD_HEADDIM
    n_pairs = SSD_HEADS // 2

    @pl.when(pl.program_id(0) == 0)
    def _():
        s_ref[...] = jnp.zeros(s_ref.shape, F32)

    t_idx = lax.broadcasted_iota(jnp.int32, (n, n), 0)
    s_idx = lax.broadcasted_iota(jnp.int32, (n, n), 1)
    causal = t_idx >= s_idx
    lower = causal.astype(F32)
    upper = (t_idx <= s_idx).astype(F32)
    lane_v = lax.broadcasted_iota(jnp.int32, (n, pair_w), 1)
    lane_s = lax.broadcasted_iota(jnp.int32, (SSD_DSTATE, pair_w), 1)
    lane_1 = lax.broadcasted_iota(jnp.int32, (1, pair_w), 1)
    bs = range(batch)
    for sub in range(subchunks):
        rows = slice(n * sub, n * (sub + 1))
        grow = [grow_refs[b][:, rows] for b in bs]
        gcum_col = [_dot_hi(lower, gcol_ref[b, rows]) for b in bs]
        gcum_row = [_dot_hi(grow[b], upper) for b in bs]
        for g in range(SSD_GROUPS):
            q = [c_ref[b, rows, SSD_DSTATE * g:SSD_DSTATE * (g + 1)] for b in bs]
            k = [b_ref[b, rows, SSD_DSTATE * g:SSD_DSTATE * (g + 1)] for b in bs]
            qf = [q[b].astype(F32) for b in bs]
            kt = [k[b].astype(F32).T for b in bs]
            cb = [_dot_nt(q[b], k[b]) for b in bs]
            ys = [[] for _ in bs]
            for p in range(pairs_per_group):
                pair = g * pairs_per_group + p
                col = group_w * g + pair_w * p
                for b in bs:
                    xv = x_ref[b, rows, col:col + pair_w]
                    state = s_ref[b * n_pairs + pair]
                    o = jnp.zeros((n, pair_w), F32)
                    upd = jnp.zeros((SSD_DSTATE, pair_w), F32)
                    last = []
                    for side in range(2):
                        hd = 2 * pair + side
                        gc = gcum_col[b][:, SSD_HEADS + hd:SSD_HEADS + hd + 1]
                        gr = gcum_row[b][SSD_HEADS + hd:SSD_HEADS + hd + 1, :]
                        dt_row = grow[b][hd:hd + 1, :]
                        g_last = gr[:, n - 1:n]
                        dec = jnp.exp(jnp.where(causal, gc - gr, -jnp.inf))
                        att = cb[b] * dec * dt_row
                        q_in = qf[b] * jnp.exp(gc)
                        mine_v = (lane_v >= SSD_HEADDIM) if side else (lane_v < SSD_HEADDIM)
                        mine_s = (lane_s >= SSD_HEADDIM) if side else (lane_s < SSD_HEADDIM)
                        xm = jnp.where(mine_v, xv, jnp.zeros_like(xv))
                        sm = jnp.where(mine_s, state, 0.0).astype(BF16)
                        o = o + _dot(att.astype(BF16), xm) + _dot(q_in.astype(BF16), sm)
                        k_out = kt[b] * (jnp.exp(g_last - gr) * dt_row)
                        upd = upd + _dot(k_out.astype(BF16), xm)
                        last.append(jnp.exp(g_last))
                    s_ref[b * n_pairs + pair] = state * jnp.where(lane_1 < SSD_HEADDIM, last[0], last[1]) + upd
                    ys[b].append(o + xv.astype(F32) * dskip_ref[:, col:col + pair_w])
            cols = slice(group_w * g, group_w * (g + 1))
            for b in bs:
                y = jnp.concatenate(ys[b], axis=-1)
                y = y * _silu(z_ref[b, rows, cols].astype(F32))
                y = y * lax.rsqrt(jnp.mean(y * y, axis=-1, keepdims=True) + RMS_EPS)
                o_ref[b, rows, cols] = (y * nw_ref[:, cols]).astype(BF16)


def _ssd(z, xbc, gates, gates_t, dskip, norm_w, batch, rows_per_batch):
    n = CHUNK
    subchunks = 1
    step_rows = n * subchunks
    steps = rows_per_batch // step_rows
    z = z.reshape(batch, rows_per_batch, z.shape[1])
    xbc = xbc.reshape(batch, rows_per_batch, xbc.shape[1])
    gates = gates.reshape(batch, rows_per_batch, LANES)
    out = pl.pallas_call(
        functools.partial(_ssd_kernel, batch=batch, subchunks=subchunks),
        grid=(steps,),
        in_specs=[
            pl.BlockSpec((batch, step_rows, SSD_DINNER), lambda c: (0, c, 0)),
            pl.BlockSpec((batch, step_rows, SSD_DINNER), lambda c: (0, c, 0)),
            pl.BlockSpec((batch, step_rows, SSD_GN), lambda c: (0, c, 4)),
            pl.BlockSpec((batch, step_rows, SSD_GN), lambda c: (0, c, 5)),
            pl.BlockSpec((batch, step_rows, LANES), lambda c: (0, c, 0)),
        ] + [pl.BlockSpec((2 * SSD_HEADS, step_rows), lambda c, b=b: (0, b * steps + c)) for b in range(batch)] + [
            pl.BlockSpec((1, SSD_DINNER), lambda c: (0, 0)),
            pl.BlockSpec((1, SSD_DINNER), lambda c: (0, 0)),
        ],
        out_specs=pl.BlockSpec((batch, step_rows, SSD_DINNER), lambda c: (0, c, 0)),
        out_shape=jax.ShapeDtypeStruct((batch, rows_per_batch, SSD_DINNER), BF16),
        scratch_shapes=[pltpu.VMEM((batch * SSD_HEADS // 2, SSD_DSTATE, 2 * SSD_HEADDIM), F32)],
        compiler_params=_params(("arbitrary",)),
        name="ssd",
    )(z, xbc, xbc, xbc, gates, *([gates_t] * batch), dskip, norm_w)
    return out.reshape(batch * rows_per_batch, SSD_DINNER)


def _pad_lanes(v):
    v = v.reshape(1, -1).astype(F32)
    return jnp.pad(v, ((0, 0), (0, LANES - v.shape[1])))


def kernel(x, meta_tokens, ab_w_in, ab_ret_gn_w, ab_conv_q, ab_conv_k, ab_conv_v, ab_A_log, ab_dt_bias, ab_gdn_norm_w, ab_w_out, c_w_in, c_conv_w, c_conv_b, c_A_log, c_dt_bias, c_D, c_norm_w, c_w_out, mlp_w1, mlp_w2, ln1_w, ln1_b, ln2_w, ln2_b):
    batch, seq, d = x.shape
    assert d == D_MODEL and meta_tokens.shape == (N_META, D_MODEL)
    lp = PAD_FRONT + N_META + seq

    inv_freq = 1.0 / (ROPE_BASE ** jnp.linspace(0.0, 1.0, RET_DK // 2, dtype=F32))
    coarse = (jnp.arange(lp // CHUNK, dtype=F32) * CHUNK - PAD_FRONT)[:, None, None] * inv_freq
    fine = jnp.arange(CHUNK, dtype=F32)[None, :, None] * inv_freq
    cos = (jnp.cos(coarse) * jnp.cos(fine) - jnp.sin(coarse) * jnp.sin(fine)).reshape(lp, RET_DK // 2)
    sin = (jnp.sin(coarse) * jnp.cos(fine) + jnp.cos(coarse) * jnp.sin(fine)).reshape(lp, RET_DK // 2)
    cosf = jnp.concatenate([cos, cos], axis=-1)
    sinf = jnp.concatenate([-sin, sin], axis=-1)

    w_in = ab_w_in[:1].astype(BF16)
    w_gg = w_in[:, :, 5128:]
    w_gate = jnp.pad(w_in[0, :, 5120:5128], ((0, 0), (0, LANES - 8)))
    alog = _pad_lanes(jnp.concatenate([jnp.zeros((GDN_HEADS,), F32), ab_A_log[0]]))
    bias = _pad_lanes(jnp.concatenate([jnp.zeros((GDN_HEADS,), F32), ab_dt_bias[0]]))
    ones = jnp.ones((1, COL_TILE // 2), F32)
    zeros = jnp.zeros((1, COL_TILE), F32)
    h, hb, g0, g0_t, p_qk = _assemble(x, meta_tokens, w_gate, alog, bias, w_in, cosf, sinf,
                                      jnp.concatenate([ones * RET_DK ** -0.5, ones], axis=1), lp)
    p_vg = _proj(hb, w_in, (lambda j: j + 1, 2), _epilogue_plain, [], [], lp, "proj_ret_vg")
    p_gg = _proj(hb, w_gg, (lambda j: j, 1), _epilogue_plain, [], [], lp, "proj_gdn_g")
    g_qk = _proj(hb, w_in, (lambda j: j + 3, 1), functools.partial(_epilogue_conv, l2norm=True),
                 [jnp.concatenate([ab_conv_q[0], ab_conv_k[0]], axis=1), zeros,
                  jnp.concatenate([ones * GDN_DK ** -0.5, ones], axis=1)],
                 [_per_col(rows=CONV_K), _per_col(), _per_col()], lp, "proj_gdn_qk")
    g_v = _proj(hb, w_in, (lambda j: j + 4, 1), functools.partial(_epilogue_conv, l2norm=False),
                [ab_conv_v[0], zeros, zeros], [_per_col(rows=CONV_K), _per_col(), _per_col()], lp, "proj_gdn_v")
    y_ret = _retention(p_qk, p_vg, ab_ret_gn_w[0].reshape(1, -1), batch, lp)
    y_gdn = _gdn(g_qk, g_v, p_gg, g0, g0_t, ab_gdn_norm_w[0].reshape(1, -1), batch, lp)
    w_in = c_w_in[:1].astype(BF16)
    w_dt = w_in[0, :, 5120:]
    w_gate = jnp.pad(jnp.concatenate([w_dt, w_dt], axis=1), ((0, 0), (0, LANES - 2 * SSD_HEADS)))
    alog = _pad_lanes(jnp.concatenate([jnp.zeros((SSD_HEADS,), F32), c_A_log[0]]))
    bias = _pad_lanes(jnp.concatenate([c_dt_bias[0], c_dt_bias[0]]))
    h, hb, g1, g1_t = _mlp_ln(y_ret, y_gdn, ab_w_out[:1], h, ln1_w[0].reshape(1, -1), ln1_b[0].reshape(1, -1),
                              mlp_w1[0].astype(BF16), mlp_w2[0].astype(BF16),
                              ln2_w[0].reshape(1, -1), ln2_b[0].reshape(1, -1), w_gate, alog, bias, lp)

    p_z = _proj(hb, w_in, (lambda j: j, 2), _epilogue_plain, [], [], lp, "proj_ssd_z")
    p_xbc = _proj(hb, w_in, (lambda j: j + 2, 3), functools.partial(_epilogue_conv, l2norm=False),
                  [c_conv_w[0], c_conv_b[0].reshape(1, -1), jnp.zeros((1, 3 * COL_TILE), F32)],
                  [_per_col(rows=CONV_K), _per_col(), _per_col()], lp, "proj_ssd_xbc")
    dskip = jnp.repeat(c_D[0].astype(F32), SSD_HEADDIM).reshape(1, -1)
    y_ssd = _ssd(p_z, p_xbc, g1, g1_t, dskip, c_norm_w[0].reshape(1, -1), batch, lp)
    h = _outproj_ln(y_ssd, y_ssd, 0, 1, c_w_out[:1], h,
                    ln1_w[1].reshape(1, -1), ln1_b[1].reshape(1, -1), lp)
    out = _mlp_ln_final(h, mlp_w1[1].astype(BF16), mlp_w2[1].astype(BF16),
                        ln2_w[1].reshape(1, -1), ln2_b[1].reshape(1, -1), batch, seq, lp)
    return out.reshape(batch, seq, d)
```
